```python
import jax, jax.numpy as jnp
from jax import lax
import numpy as np

D_MODEL = 2048
BATCH = 8
SEQ = 4096
DEPTH = 4

N_EVEN = (DEPTH + 1) // 2
N_ODD = DEPTH // 2
EPS = 1e-6
D_FF = 4 * D_MODEL
GROUP_DIM = 128
D_A = D_MODEL // 2
D_B = D_MODEL - D_A
N_A_GROUPS = D_A // GROUP_DIM
CHUNK = 128
CONV_WIDTH = 31
D_AB_IN = 2 * D_A + 2 * D_B
N_HEADS = 16
Q_RANK = 512
KV_RANK = 512
NOPE_DIM = 128
ROPE_DIM = 64
V_DIM = 128
QK_DIM = NOPE_DIM + ROPE_DIM
D_MLA_IN = Q_RANK + KV_RANK + ROPE_DIM
ROPE_THETA = 10000.0
Q_BLOCK = 128
ATTN_SCALE = QK_DIM ** -0.5

kernel_name = "hybrid_sgu_conv_mla_adaln_trunk"


def _rms(x, g):
    xf = x.astype(jnp.float32)
    y = xf * lax.rsqrt(jnp.mean(xf * xf, axis=-1, keepdims=True) + EPS)
    return (y * g.astype(jnp.float32)).astype(x.dtype)


def _layernorm(x, g, b):
    xf = x.astype(jnp.float32)
    mu = jnp.mean(xf, axis=-1, keepdims=True)
    var = jnp.mean(jnp.square(xf - mu), axis=-1, keepdims=True)
    y = (xf - mu) * lax.rsqrt(var + EPS)
    return (y * g.astype(jnp.float32) + b.astype(jnp.float32)).astype(x.dtype)


def _modulate(h, shift, scale):
    return h * (1 + scale[:, None, :]) + shift[:, None, :]


def _spatial_gating(u, v, v_norm_g, w_s, b_s):
    bsz, s, _ = u.shape
    shp = (bsz, s // CHUNK, CHUNK, N_A_GROUPS, GROUP_DIM)
    v = _rms(v.reshape(shp), v_norm_g)
    mask = jnp.tril(jnp.ones((CHUNK, CHUNK), dtype=w_s.dtype))
    mixed = jnp.einsum('gts,bnsgd->bntgd', w_s * mask, v) + b_s.T[None, None, :, :, None]
    return (u.reshape(shp) * mixed).reshape(bsz, s, D_A)


def _conformer_conv(a, g, conv_w, conv_b, ln_g, ln_b):
    y = a * jax.nn.sigmoid(g)
    y = lax.conv_general_dilated(y, conv_w[:, None, :], window_strides=(1,),
                                 padding=[(CONV_WIDTH - 1, 0)],
                                 dimension_numbers=('NWC', 'WIO', 'NWC'),
                                 feature_group_count=D_B) + conv_b
    return jax.nn.silu(_layernorm(y, ln_g, ln_b))


def _even_mixer(h, w_in, sgu_norm_g, sgu_w, sgu_b, conv_w, conv_b, ln_g, ln_b, w_out):
    proj = h @ w_in
    u, v, a, g = jnp.split(proj, [D_A, 2 * D_A, 2 * D_A + D_B], axis=-1)
    out_a = _spatial_gating(jax.nn.gelu(u), jax.nn.gelu(v), sgu_norm_g, sgu_w, sgu_b)
    out_b = _conformer_conv(a, g, conv_w, conv_b, ln_g, ln_b)
    return jnp.concatenate([out_a, out_b], axis=-1) @ w_out


def _rope_tables(s):
    pos = jnp.arange(s, dtype=jnp.float32)
    inv = ROPE_THETA ** (-jnp.arange(0, ROPE_DIM, 2, dtype=jnp.float32) / ROPE_DIM)
    ang = pos[:, None] * inv[None, :]
    return jnp.cos(ang), jnp.sin(ang)


def _apply_rope(x, cos, sin):
    xf = x.astype(jnp.float32)
    x1, x2 = jnp.split(xf, 2, axis=-1)
    return jnp.concatenate([x1 * cos - x2 * sin, x1 * sin + x2 * cos], axis=-1).astype(x.dtype)


def _segment_head_norm(t, g):
    return jnp.concatenate([_rms(t[..., :NOPE_DIM], g[:NOPE_DIM]),
                            _rms(t[..., NOPE_DIM:], g[NOPE_DIM:])], axis=-1)


def _block_causal_attention(q_nope, q_rope, k_nope, k_rope, v):
    s = q_nope.shape[1]
    outs = []
    for i in range(s // Q_BLOCK):
        q0, q1 = i * Q_BLOCK, (i + 1) * Q_BLOCK
        sc = (jnp.einsum('bqhd,bkhd->bhqk', q_nope[:, q0:q1], k_nope[:, :q1])
              + jnp.einsum('bqhr,bkr->bhqk', q_rope[:, q0:q1], k_rope[:, :q1]))
        sc = sc.astype(jnp.float32) * ATTN_SCALE
        qi = q0 + jnp.arange(Q_BLOCK)
        ki = jnp.arange(q1)
        sc = jnp.where(ki[None, :] <= qi[:, None], sc, -jnp.inf)
        p = jax.nn.softmax(sc, axis=-1).astype(v.dtype)
        outs.append(jnp.einsum('bhqk,bkhd->bqhd', p, v[:, :q1]))
    return jnp.concatenate(outs, axis=1)


def _mla_mixer(h, w_in, q_norm_g, kv_norm_g, w_uq, w_ukv, q_head_g, k_head_g, w_out):
    bsz, s, _ = h.shape
    proj = h @ w_in
    c_q, c_kv, k_rope = jnp.split(proj, [Q_RANK, Q_RANK + KV_RANK], axis=-1)
    q = (_rms(c_q, q_norm_g) @ w_uq).reshape(bsz, s, N_HEADS, QK_DIM)
    kv = (_rms(c_kv, kv_norm_g) @ w_ukv).reshape(bsz, s, N_HEADS, NOPE_DIM + V_DIM)
    k_nope, v = jnp.split(kv, [NOPE_DIM], axis=-1)
    q = _segment_head_norm(q, q_head_g)
    k_nope = _rms(k_nope, k_head_g[:NOPE_DIM])
    k_rope = _rms(k_rope, k_head_g[NOPE_DIM:])
    cos, sin = _rope_tables(s)
    q_rope = _apply_rope(q[..., NOPE_DIM:], cos[:, None, :], sin[:, None, :])
    k_rope = _apply_rope(k_rope, cos, sin)
    o = _block_causal_attention(q[..., :NOPE_DIM], q_rope, k_nope, k_rope, v)
    return o.reshape(bsz, s, N_HEADS * V_DIM) @ w_out


def _fwd_setup_inputs(seed: int = 0) -> dict:
    key = jax.random.key(seed)
    ks = jax.random.split(key, 32)
    f = jnp.float32
    nrm = lambda k, shp, sc: jax.random.normal(k, shp, f) * sc
    d = D_MODEL
    return {
        "x": nrm(ks[0], (BATCH, SEQ, d), 1.0),
        "c": nrm(ks[1], (BATCH, d), 1.0),
        "norm1_g": 1.0 + nrm(ks[2], (DEPTH, d), 0.02),
        "norm2_g": 1.0 + nrm(ks[3], (DEPTH, d), 0.02),
        "ada_w": nrm(ks[4], (DEPTH, d, 6 * d), 0.5 * d ** -0.5),
        "ada_b": nrm(ks[5], (DEPTH, 6 * d), 0.01),
        "mlp_w1": nrm(ks[6], (DEPTH, d, D_FF), d ** -0.5),
        "mlp_w2": nrm(ks[7], (DEPTH, D_FF, d), D_FF ** -0.5),
        "ab_w_in": nrm(ks[8], (N_EVEN, d, D_AB_IN), d ** -0.5),
        "sgu_norm_g": 1.0 + nrm(ks[9], (N_EVEN, N_A_GROUPS, GROUP_DIM), 0.02),
        "sgu_w": nrm(ks[10], (N_EVEN, N_A_GROUPS, CHUNK, CHUNK), CHUNK ** -0.5),
        "sgu_b": 1.0 + nrm(ks[11], (N_EVEN, N_A_GROUPS, CHUNK), 0.02),
        "conv_w": nrm(ks[12], (N_EVEN, CONV_WIDTH, D_B), CONV_WIDTH ** -0.5),
        "conv_b": nrm(ks[13], (N_EVEN, D_B), 0.01),
        "conv_ln_g": 1.0 + nrm(ks[14], (N_EVEN, D_B), 0.02),
        "conv_ln_b": nrm(ks[15], (N_EVEN, D_B), 0.01),
        "ab_w_out": nrm(ks[16], (N_EVEN, D_A + D_B, d), (D_A + D_B) ** -0.5),
        "mla_w_in": nrm(ks[17], (N_ODD, d, D_MLA_IN), d ** -0.5),
        "mla_q_norm_g": 1.0 + nrm(ks[18], (N_ODD, Q_RANK), 0.02),
        "mla_kv_norm_g": 1.0 + nrm(ks[19], (N_ODD, KV_RANK), 0.02),
        "mla_w_uq": nrm(ks[20], (N_ODD, Q_RANK, N_HEADS * QK_DIM), Q_RANK ** -0.5),
        "mla_w_ukv": nrm(ks[21], (N_ODD, KV_RANK, N_HEADS * (NOPE_DIM + V_DIM)), KV_RANK ** -0.5),
        "mla_q_head_g": 1.0 + nrm(ks[22], (N_ODD, QK_DIM), 0.02),
        "mla_k_head_g": 1.0 + nrm(ks[23], (N_ODD, QK_DIM), 0.02),
        "mla_w_out": nrm(ks[24], (N_ODD, N_HEADS * V_DIM, d), (N_HEADS * V_DIM) ** -0.5),
    }


def _fwd_reference(x, c, norm1_g, norm2_g, ada_w, ada_b, mlp_w1, mlp_w2,
              ab_w_in, sgu_norm_g, sgu_w, sgu_b, conv_w, conv_b, conv_ln_g, conv_ln_b, ab_w_out,
              mla_w_in, mla_q_norm_g, mla_kv_norm_g, mla_w_uq, mla_w_ukv,
              mla_q_head_g, mla_k_head_g, mla_w_out):
    c_act = jax.nn.silu(c.astype(x.dtype))
    for l in range(DEPTH):
        mod = c_act @ ada_w[l] + ada_b[l]
        shift1, scale1, gate1, shift2, scale2, gate2 = jnp.split(mod, 6, axis=-1)
        h = _modulate(_rms(x, norm1_g[l]), shift1, scale1)
        if l % 2 == 0:
            e = l // 2
            mix = _even_mixer(h, ab_w_in[e], sgu_norm_g[e], sgu_w[e], sgu_b[e], conv_w[e],
                              conv_b[e], conv_ln_g[e], conv_ln_b[e], ab_w_out[e])
        else:
            o = l // 2
            mix = _mla_mixer(h, mla_w_in[o], mla_q_norm_g[o], mla_kv_norm_g[o], mla_w_uq[o],
                             mla_w_ukv[o], mla_q_head_g[o], mla_k_head_g[o], mla_w_out[o])
        x = x + gate1[:, None, :] * mix
        h = _modulate(_rms(x, norm2_g[l]), shift2, scale2)
        x = x + gate2[:, None, :] * (jnp.square(jax.nn.relu(h @ mlp_w1[l])) @ mlp_w2[l])
    return x


import jax as _jax
import jax.numpy as _jnp

TWIN_FORMAT = 'train_step'
FWD_PARAMS = ['x', 'c', 'norm1_g', 'norm2_g', 'ada_w', 'ada_b', 'mlp_w1', 'mlp_w2', 'ab_w_in', 'sgu_norm_g', 'sgu_w', 'sgu_b', 'conv_w', 'conv_b', 'conv_ln_g', 'conv_ln_b', 'ab_w_out', 'mla_w_in', 'mla_q_norm_g', 'mla_kv_norm_g', 'mla_w_uq', 'mla_w_ukv', 'mla_q_head_g', 'mla_k_head_g', 'mla_w_out']
TWIN_WEIGHTS = ['norm1_g', 'norm2_g', 'ada_w', 'ada_b', 'mlp_w1', 'mlp_w2', 'ab_w_in', 'sgu_norm_g', 'sgu_w', 'sgu_b', 'conv_w', 'conv_b', 'conv_ln_g', 'conv_ln_b', 'ab_w_out', 'mla_w_in', 'mla_q_norm_g', 'mla_kv_norm_g', 'mla_w_uq', 'mla_w_ukv', 'mla_q_head_g', 'mla_k_head_g', 'mla_w_out']
TWIN_DIFF_INPUT = 'x'
TWIN_INPUTS = ['x', 'c', 'norm1_g', 'norm2_g', 'ada_w', 'ada_b', 'mlp_w1', 'mlp_w2', 'ab_w_in', 'sgu_norm_g', 'sgu_w', 'sgu_b', 'conv_w', 'conv_b', 'conv_ln_g', 'conv_ln_b', 'ab_w_out', 'mla_w_in', 'mla_q_norm_g', 'mla_kv_norm_g', 'mla_w_uq', 'mla_w_ukv', 'mla_q_head_g', 'mla_k_head_g', 'mla_w_out', 'loss_target', 'm_norm1_g', 'm_norm2_g', 'm_ada_w', 'm_ada_b', 'm_mlp_w1', 'm_mlp_w2', 'm_ab_w_in', 'm_sgu_norm_g', 'm_sgu_w', 'm_sgu_b', 'm_conv_w', 'm_conv_b', 'm_conv_ln_g', 'm_conv_ln_b', 'm_ab_w_out', 'm_mla_w_in', 'm_mla_q_norm_g', 'm_mla_kv_norm_g', 'm_mla_w_uq', 'm_mla_w_ukv', 'm_mla_q_head_g', 'm_mla_k_head_g', 'm_mla_w_out', 'v_norm1_g', 'v_norm2_g', 'v_ada_w', 'v_ada_b', 'v_mlp_w1', 'v_mlp_w2', 'v_ab_w_in', 'v_sgu_norm_g', 'v_sgu_w', 'v_sgu_b', 'v_conv_w', 'v_conv_b', 'v_conv_ln_g', 'v_conv_ln_b', 'v_ab_w_out', 'v_mla_w_in', 'v_mla_q_norm_g', 'v_mla_kv_norm_g', 'v_mla_w_uq', 'v_mla_w_ukv', 'v_mla_q_head_g', 'v_mla_k_head_g', 'v_mla_w_out']
TWIN_OUTPUTS = ['loss', 'grad_x', 'grad_norm1_g', 'grad_norm2_g', 'grad_ada_w', 'grad_ada_b', 'grad_mlp_w1', 'grad_mlp_w2', 'grad_ab_w_in', 'grad_sgu_norm_g', 'grad_sgu_w', 'grad_sgu_b', 'grad_conv_w', 'grad_conv_b', 'grad_conv_ln_g', 'grad_conv_ln_b', 'grad_ab_w_out', 'grad_mla_w_in', 'grad_mla_q_norm_g', 'grad_mla_kv_norm_g', 'grad_mla_w_uq', 'grad_mla_w_ukv', 'grad_mla_q_head_g', 'grad_mla_k_head_g', 'grad_mla_w_out', 'delta_norm1_g', 'delta_norm2_g', 'delta_ada_w', 'delta_ada_b', 'delta_mlp_w1', 'delta_mlp_w2', 'delta_ab_w_in', 'delta_sgu_norm_g', 'delta_sgu_w', 'delta_sgu_b', 'delta_conv_w', 'delta_conv_b', 'delta_conv_ln_g', 'delta_conv_ln_b', 'delta_ab_w_out', 'delta_mla_w_in', 'delta_mla_q_norm_g', 'delta_mla_kv_norm_g', 'delta_mla_w_uq', 'delta_mla_w_ukv', 'delta_mla_q_head_g', 'delta_mla_k_head_g', 'delta_mla_w_out', 'new_m_norm1_g', 'new_m_norm2_g', 'new_m_ada_w', 'new_m_ada_b', 'new_m_mlp_w1', 'new_m_mlp_w2', 'new_m_ab_w_in', 'new_m_sgu_norm_g', 'new_m_sgu_w', 'new_m_sgu_b', 'new_m_conv_w', 'new_m_conv_b', 'new_m_conv_ln_g', 'new_m_conv_ln_b', 'new_m_ab_w_out', 'new_m_mla_w_in', 'new_m_mla_q_norm_g', 'new_m_mla_kv_norm_g', 'new_m_mla_w_uq', 'new_m_mla_w_ukv', 'new_m_mla_q_head_g', 'new_m_mla_k_head_g', 'new_m_mla_w_out', 'new_v_norm1_g', 'new_v_norm2_g', 'new_v_ada_w', 'new_v_ada_b', 'new_v_mlp_w1', 'new_v_mlp_w2', 'new_v_ab_w_in', 'new_v_sgu_norm_g', 'new_v_sgu_w', 'new_v_sgu_b', 'new_v_conv_w', 'new_v_conv_b', 'new_v_conv_ln_g', 'new_v_conv_ln_b', 'new_v_ab_w_out', 'new_v_mla_w_in', 'new_v_mla_q_norm_g', 'new_v_mla_kv_norm_g', 'new_v_mla_w_uq', 'new_v_mla_w_ukv', 'new_v_mla_q_head_g', 'new_v_mla_k_head_g', 'new_v_mla_w_out']
TWIN_LEAF_KINDS = {'loss': 'loss', 'grad_x': 'grad_x', 'grad_norm1_g': 'grad_w', 'grad_norm2_g': 'grad_w', 'grad_ada_w': 'grad_w', 'grad_ada_b': 'grad_w', 'grad_mlp_w1': 'grad_w', 'grad_mlp_w2': 'grad_w', 'grad_ab_w_in': 'grad_w', 'grad_sgu_norm_g': 'grad_w', 'grad_sgu_w': 'grad_w', 'grad_sgu_b': 'grad_w', 'grad_conv_w': 'grad_w', 'grad_conv_b': 'grad_w', 'grad_conv_ln_g': 'grad_w', 'grad_conv_ln_b': 'grad_w', 'grad_ab_w_out': 'grad_w', 'grad_mla_w_in': 'grad_w', 'grad_mla_q_norm_g': 'grad_w', 'grad_mla_kv_norm_g': 'grad_w', 'grad_mla_w_uq': 'grad_w', 'grad_mla_w_ukv': 'grad_w', 'grad_mla_q_head_g': 'grad_w', 'grad_mla_k_head_g': 'grad_w', 'grad_mla_w_out': 'grad_w', 'delta_norm1_g': 'delta_w', 'delta_norm2_g': 'delta_w', 'delta_ada_w': 'delta_w', 'delta_ada_b': 'delta_w', 'delta_mlp_w1': 'delta_w', 'delta_mlp_w2': 'delta_w', 'delta_ab_w_in': 'delta_w', 'delta_sgu_norm_g': 'delta_w', 'delta_sgu_w': 'delta_w', 'delta_sgu_b': 'delta_w', 'delta_conv_w': 'delta_w', 'delta_conv_b': 'delta_w', 'delta_conv_ln_g': 'delta_w', 'delta_conv_ln_b': 'delta_w', 'delta_ab_w_out': 'delta_w', 'delta_mla_w_in': 'delta_w', 'delta_mla_q_norm_g': 'delta_w', 'delta_mla_kv_norm_g': 'delta_w', 'delta_mla_w_uq': 'delta_w', 'delta_mla_w_ukv': 'delta_w', 'delta_mla_q_head_g': 'delta_w', 'delta_mla_k_head_g': 'delta_w', 'delta_mla_w_out': 'delta_w', 'new_m_norm1_g': 'new_m', 'new_m_norm2_g': 'new_m', 'new_m_ada_w': 'new_m', 'new_m_ada_b': 'new_m', 'new_m_mlp_w1': 'new_m', 'new_m_mlp_w2': 'new_m', 'new_m_ab_w_in': 'new_m', 'new_m_sgu_norm_g': 'new_m', 'new_m_sgu_w': 'new_m', 'new_m_sgu_b': 'new_m', 'new_m_conv_w': 'new_m', 'new_m_conv_b': 'new_m', 'new_m_conv_ln_g': 'new_m', 'new_m_conv_ln_b': 'new_m', 'new_m_ab_w_out': 'new_m', 'new_m_mla_w_in': 'new_m', 'new_m_mla_q_norm_g': 'new_m', 'new_m_mla_kv_norm_g': 'new_m', 'new_m_mla_w_uq': 'new_m', 'new_m_mla_w_ukv': 'new_m', 'new_m_mla_q_head_g': 'new_m', 'new_m_mla_k_head_g': 'new_m', 'new_m_mla_w_out': 'new_m', 'new_v_norm1_g': 'new_v', 'new_v_norm2_g': 'new_v', 'new_v_ada_w': 'new_v', 'new_v_ada_b': 'new_v', 'new_v_mlp_w1': 'new_v', 'new_v_mlp_w2': 'new_v', 'new_v_ab_w_in': 'new_v', 'new_v_sgu_norm_g': 'new_v', 'new_v_sgu_w': 'new_v', 'new_v_sgu_b': 'new_v', 'new_v_conv_w': 'new_v', 'new_v_conv_b': 'new_v', 'new_v_conv_ln_g': 'new_v', 'new_v_conv_ln_b': 'new_v', 'new_v_ab_w_out': 'new_v', 'new_v_mla_w_in': 'new_v', 'new_v_mla_q_norm_g': 'new_v', 'new_v_mla_kv_norm_g': 'new_v', 'new_v_mla_w_uq': 'new_v', 'new_v_mla_w_ukv': 'new_v', 'new_v_mla_q_head_g': 'new_v', 'new_v_mla_k_head_g': 'new_v', 'new_v_mla_w_out': 'new_v'}


def _forward(args):
    return _fwd_reference(*[args[k] for k in FWD_PARAMS])


def _output_shape():
    def fwd():
        inp = _fwd_setup_inputs(0)
        return _fwd_reference(*[inp[k] for k in FWD_PARAMS])
    out = _jax.eval_shape(fwd)
    return out.shape, out.dtype

N_MICROBATCH = 1
ADAM_LR = 0.001
ADAM_B1 = 0.9
ADAM_B2 = 0.999
ADAM_EPS = 1e-08
ADAM_WD = 0.01
ADAM_STEP = 10
PER_EXAMPLE_BATCH_AXIS = {'x': 0, 'c': 0, 'loss_target': 0}
SHARED_INPUTS = []
_WEIGHT_DTYPES = {'norm1_g': _jnp.float32, 'norm2_g': _jnp.float32, 'ada_w': _jnp.float32, 'ada_b': _jnp.float32, 'mlp_w1': _jnp.float32, 'mlp_w2': _jnp.float32, 'ab_w_in': _jnp.float32, 'sgu_norm_g': _jnp.float32, 'sgu_w': _jnp.float32, 'sgu_b': _jnp.float32, 'conv_w': _jnp.float32, 'conv_b': _jnp.float32, 'conv_ln_g': _jnp.float32, 'conv_ln_b': _jnp.float32, 'ab_w_out': _jnp.float32, 'mla_w_in': _jnp.float32, 'mla_q_norm_g': _jnp.float32, 'mla_kv_norm_g': _jnp.float32, 'mla_w_uq': _jnp.float32, 'mla_w_ukv': _jnp.float32, 'mla_q_head_g': _jnp.float32, 'mla_k_head_g': _jnp.float32, 'mla_w_out': _jnp.float32}
MOMENT_SCALE = {'norm1_g': 4.607332e-01, 'norm2_g': 5.748698e+00, 'ada_w': 1.346717e+00, 'ada_b': 3.324342e+00, 'mlp_w1': 2.341491e-01, 'mlp_w2': 8.762179e-01, 'ab_w_in': 1.430109e-01, 'sgu_norm_g': 3.859082e-01, 'sgu_w': 2.544000e-01, 'sgu_b': 7.825616e-01, 'conv_w': 1.767398e-01, 'conv_b': 9.978042e-01, 'conv_ln_g': 8.017992e-01, 'conv_ln_b': 7.480025e-01, 'ab_w_out': 3.650007e-01, 'mla_w_in': 5.132305e-01, 'mla_q_norm_g': 2.257079e-02, 'mla_kv_norm_g': 1.094675e+00, 'mla_w_uq': 9.167822e-03, 'mla_w_ukv': 2.537664e-01, 'mla_q_head_g': 8.302210e-02, 'mla_k_head_g': 8.284936e-02, 'mla_w_out': 3.548073e-01}


def _to_microbatches(a, axis):
    t = _jnp.moveaxis(a, axis, 0)
    t = t.reshape((N_MICROBATCH, t.shape[0] // N_MICROBATCH) + t.shape[1:])
    return _jnp.moveaxis(t, 1, axis + 1)


def setup_inputs(seed: int = 0) -> dict:
    inp = _fwd_setup_inputs(seed)
    key = _jax.random.fold_in(_jax.random.key(seed), 7919)
    shape, _ = _output_shape()
    out = dict(inp)
    out["loss_target"] = _jax.random.normal(_jax.random.fold_in(key, 0), shape, _jnp.float32)
    for i, name in enumerate(TWIN_WEIGHTS):
        w = inp[name].astype(_jnp.float32)
        if MOMENT_SCALE is None:
            s = _jnp.sqrt(_jnp.mean(_jnp.square(w)) + 1e-30)
        else:
            s = MOMENT_SCALE[name]
        km, kv = _jax.random.split(_jax.random.fold_in(key, i + 1))
        out[name] = w
        out["m_" + name] = s * _jax.random.normal(km, w.shape, _jnp.float32)
        out["v_" + name] = (s * s) * _jax.random.uniform(kv, w.shape, _jnp.float32, 0.5, 1.5)
    if N_MICROBATCH > 1:
        for name, axis in PER_EXAMPLE_BATCH_AXIS.items():
            out[name] = _to_microbatches(out[name], axis)
    return {'x': out['x'], 'c': out['c'], 'norm1_g': out['norm1_g'], 'norm2_g': out['norm2_g'], 'ada_w': out['ada_w'], 'ada_b': out['ada_b'], 'mlp_w1': out['mlp_w1'], 'mlp_w2': out['mlp_w2'], 'ab_w_in': out['ab_w_in'], 'sgu_norm_g': out['sgu_norm_g'], 'sgu_w': out['sgu_w'], 'sgu_b': out['sgu_b'], 'conv_w': out['conv_w'], 'conv_b': out['conv_b'], 'conv_ln_g': out['conv_ln_g'], 'conv_ln_b': out['conv_ln_b'], 'ab_w_out': out['ab_w_out'], 'mla_w_in': out['mla_w_in'], 'mla_q_norm_g': out['mla_q_norm_g'], 'mla_kv_norm_g': out['mla_kv_norm_g'], 'mla_w_uq': out['mla_w_uq'], 'mla_w_ukv': out['mla_w_ukv'], 'mla_q_head_g': out['mla_q_head_g'], 'mla_k_head_g': out['mla_k_head_g'], 'mla_w_out': out['mla_w_out'], 'loss_target': out['loss_target'], 'm_norm1_g': out['m_norm1_g'], 'm_norm2_g': out['m_norm2_g'], 'm_ada_w': out['m_ada_w'], 'm_ada_b': out['m_ada_b'], 'm_mlp_w1': out['m_mlp_w1'], 'm_mlp_w2': out['m_mlp_w2'], 'm_ab_w_in': out['m_ab_w_in'], 'm_sgu_norm_g': out['m_sgu_norm_g'], 'm_sgu_w': out['m_sgu_w'], 'm_sgu_b': out['m_sgu_b'], 'm_conv_w': out['m_conv_w'], 'm_conv_b': out['m_conv_b'], 'm_conv_ln_g': out['m_conv_ln_g'], 'm_conv_ln_b': out['m_conv_ln_b'], 'm_ab_w_out': out['m_ab_w_out'], 'm_mla_w_in': out['m_mla_w_in'], 'm_mla_q_norm_g': out['m_mla_q_norm_g'], 'm_mla_kv_norm_g': out['m_mla_kv_norm_g'], 'm_mla_w_uq': out['m_mla_w_uq'], 'm_mla_w_ukv': out['m_mla_w_ukv'], 'm_mla_q_head_g': out['m_mla_q_head_g'], 'm_mla_k_head_g': out['m_mla_k_head_g'], 'm_mla_w_out': out['m_mla_w_out'], 'v_norm1_g': out['v_norm1_g'], 'v_norm2_g': out['v_norm2_g'], 'v_ada_w': out['v_ada_w'], 'v_ada_b': out['v_ada_b'], 'v_mlp_w1': out['v_mlp_w1'], 'v_mlp_w2': out['v_mlp_w2'], 'v_ab_w_in': out['v_ab_w_in'], 'v_sgu_norm_g': out['v_sgu_norm_g'], 'v_sgu_w': out['v_sgu_w'], 'v_sgu_b': out['v_sgu_b'], 'v_conv_w': out['v_conv_w'], 'v_conv_b': out['v_conv_b'], 'v_conv_ln_g': out['v_conv_ln_g'], 'v_conv_ln_b': out['v_conv_ln_b'], 'v_ab_w_out': out['v_ab_w_out'], 'v_mla_w_in': out['v_mla_w_in'], 'v_mla_q_norm_g': out['v_mla_q_norm_g'], 'v_mla_kv_norm_g': out['v_mla_kv_norm_g'], 'v_mla_w_uq': out['v_mla_w_uq'], 'v_mla_w_ukv': out['v_mla_w_ukv'], 'v_mla_q_head_g': out['v_mla_q_head_g'], 'v_mla_k_head_g': out['v_mla_k_head_g'], 'v_mla_w_out': out['v_mla_w_out']}


def _loss(weights, diff, rest, loss_target):
    with _jax.named_scope("forward"):
        args = {**rest, TWIN_DIFF_INPUT: diff, **{k: w.astype(_WEIGHT_DTYPES[k]) for k, w in weights.items()}}
        y = _forward(args)
    with _jax.named_scope("loss_head"):
        err = _jnp.square(y.astype(_jnp.float32) - loss_target)
        return 0.5 * _jnp.sum(_jnp.mean(err, axis=-1)) if err.ndim else 0.5 * err


def _adamw(w, g, m, v):
    m = ADAM_B1 * m + (1.0 - ADAM_B1) * g
    v = ADAM_B2 * v + (1.0 - ADAM_B2) * _jnp.square(g)
    m_hat = m / (1.0 - ADAM_B1 ** ADAM_STEP)
    v_hat = v / (1.0 - ADAM_B2 ** ADAM_STEP)
    delta = -ADAM_LR * (m_hat / (_jnp.sqrt(v_hat) + ADAM_EPS) + ADAM_WD * w)
    return delta, m, v


def reference(x, c, norm1_g, norm2_g, ada_w, ada_b, mlp_w1, mlp_w2, ab_w_in, sgu_norm_g, sgu_w, sgu_b, conv_w, conv_b, conv_ln_g, conv_ln_b, ab_w_out, mla_w_in, mla_q_norm_g, mla_kv_norm_g, mla_w_uq, mla_w_ukv, mla_q_head_g, mla_k_head_g, mla_w_out, loss_target, m_norm1_g, m_norm2_g, m_ada_w, m_ada_b, m_mlp_w1, m_mlp_w2, m_ab_w_in, m_sgu_norm_g, m_sgu_w, m_sgu_b, m_conv_w, m_conv_b, m_conv_ln_g, m_conv_ln_b, m_ab_w_out, m_mla_w_in, m_mla_q_norm_g, m_mla_kv_norm_g, m_mla_w_uq, m_mla_w_ukv, m_mla_q_head_g, m_mla_k_head_g, m_mla_w_out, v_norm1_g, v_norm2_g, v_ada_w, v_ada_b, v_mlp_w1, v_mlp_w2, v_ab_w_in, v_sgu_norm_g, v_sgu_w, v_sgu_b, v_conv_w, v_conv_b, v_conv_ln_g, v_conv_ln_b, v_ab_w_out, v_mla_w_in, v_mla_q_norm_g, v_mla_kv_norm_g, v_mla_w_uq, v_mla_w_ukv, v_mla_q_head_g, v_mla_k_head_g, v_mla_w_out):
    given = dict(x=x, c=c, norm1_g=norm1_g, norm2_g=norm2_g, ada_w=ada_w, ada_b=ada_b, mlp_w1=mlp_w1, mlp_w2=mlp_w2, ab_w_in=ab_w_in, sgu_norm_g=sgu_norm_g, sgu_w=sgu_w, sgu_b=sgu_b, conv_w=conv_w, conv_b=conv_b, conv_ln_g=conv_ln_g, conv_ln_b=conv_ln_b, ab_w_out=ab_w_out, mla_w_in=mla_w_in, mla_q_norm_g=mla_q_norm_g, mla_kv_norm_g=mla_kv_norm_g, mla_w_uq=mla_w_uq, mla_w_ukv=mla_w_ukv, mla_q_head_g=mla_q_head_g, mla_k_head_g=mla_k_head_g, mla_w_out=mla_w_out, loss_target=loss_target, m_norm1_g=m_norm1_g, m_norm2_g=m_norm2_g, m_ada_w=m_ada_w, m_ada_b=m_ada_b, m_mlp_w1=m_mlp_w1, m_mlp_w2=m_mlp_w2, m_ab_w_in=m_ab_w_in, m_sgu_norm_g=m_sgu_norm_g, m_sgu_w=m_sgu_w, m_sgu_b=m_sgu_b, m_conv_w=m_conv_w, m_conv_b=m_conv_b, m_conv_ln_g=m_conv_ln_g, m_conv_ln_b=m_conv_ln_b, m_ab_w_out=m_ab_w_out, m_mla_w_in=m_mla_w_in, m_mla_q_norm_g=m_mla_q_norm_g, m_mla_kv_norm_g=m_mla_kv_norm_g, m_mla_w_uq=m_mla_w_uq, m_mla_w_ukv=m_mla_w_ukv, m_mla_q_head_g=m_mla_q_head_g, m_mla_k_head_g=m_mla_k_head_g, m_mla_w_out=m_mla_w_out, v_norm1_g=v_norm1_g, v_norm2_g=v_norm2_g, v_ada_w=v_ada_w, v_ada_b=v_ada_b, v_mlp_w1=v_mlp_w1, v_mlp_w2=v_mlp_w2, v_ab_w_in=v_ab_w_in, v_sgu_norm_g=v_sgu_norm_g, v_sgu_w=v_sgu_w, v_sgu_b=v_sgu_b, v_conv_w=v_conv_w, v_conv_b=v_conv_b, v_conv_ln_g=v_conv_ln_g, v_conv_ln_b=v_conv_ln_b, v_ab_w_out=v_ab_w_out, v_mla_w_in=v_mla_w_in, v_mla_q_norm_g=v_mla_q_norm_g, v_mla_kv_norm_g=v_mla_kv_norm_g, v_mla_w_uq=v_mla_w_uq, v_mla_w_ukv=v_mla_w_ukv, v_mla_q_head_g=v_mla_q_head_g, v_mla_k_head_g=v_mla_k_head_g, v_mla_w_out=v_mla_w_out)
    weights = {n: given[n] for n in TWIN_WEIGHTS}
    shared = {n: given[n] for n in SHARED_INPUTS}
    per_example = {n: given[n] for n in ['x', 'c']}
    grad_fn = _jax.value_and_grad(_loss, argnums=(0, 1))

    def one_microbatch(ex, loss_target):
        ex = dict(ex)
        diff = ex.pop(TWIN_DIFF_INPUT)
        return grad_fn(weights, diff, {**shared, **ex}, loss_target)

    if N_MICROBATCH == 1:
        loss, (grad_w, grad_x) = one_microbatch(per_example, given["loss_target"])
    else:
        def body(carry, xs):
            loss_sum, grad_sum = carry
            l_k, (gw_k, gx_k) = one_microbatch(xs[0], xs[1])
            with _jax.named_scope("update"):
                return (loss_sum + l_k, _jax.tree.map(_jnp.add, grad_sum, gw_k)), gx_k

        init = (_jnp.zeros((), _jnp.float32), _jax.tree.map(_jnp.zeros_like, weights))
        (loss, grad_w), grad_x = _jax.lax.scan(body, init, (per_example, given["loss_target"]))
    with _jax.named_scope("update"):
        delta_w, new_m, new_v = {}, {}, {}
        for n in TWIN_WEIGHTS:
            delta_w[n], new_m[n], new_v[n] = _adamw(weights[n], grad_w[n], given["m_" + n], given["v_" + n])
    return (loss, grad_x, *[grad_w[n] for n in TWIN_WEIGHTS], *[delta_w[n] for n in TWIN_WEIGHTS],
            *[new_m[n] for n in TWIN_WEIGHTS], *[new_v[n] for n in TWIN_WEIGHTS])
```

```python
import functools
import numpy as np
import jax
import jax.numpy as jnp
from jax import lax
from jax.experimental import pallas as pl
from jax.experimental.pallas import tpu as pltpu

F32 = jnp.float32
BF16 = jnp.bfloat16
MESH = pl.DeviceIdType.MESH

EPS = 1e-6
N_HEADS = 16
NOPE = 128
ROPE = 64
VDIM = 128
QK_DIM = NOPE + ROPE
HEAD_W = 256
CHUNK = 128
GROUP = 128
CONV_W = 31
CONV_PAD = 32
ROPE_THETA = 10000.0
ATTN_SCALE = QK_DIM ** -0.5
ADAM_LR, ADAM_B1, ADAM_B2, ADAM_EPS, ADAM_WD, ADAM_STEP = 0.001, 0.9, 0.999, 1e-08, 0.01, 10
N_CHIPS = 4
N_DEV = 8
VMEM_LIMIT = 56 * 1024 * 1024
SMALL_COLS = 1024
CHIP_RELS = ((1, 0), (0, 1), (1, 1))


def _pcall(body, **kw):
    return pl.pallas_call(body, **kw)


def _cp(sem=None, **kw):
    return pltpu.CompilerParams(dimension_semantics=sem, vmem_limit_bytes=VMEM_LIMIT, **kw)


def _pick(n, cands):
    for c in cands:
        if n % c == 0:
            return c
    return n


def _matmul(a, b, *, ta=False, tb=False, out_dtype=F32, name):
    if ta:
        K, M = a.shape
    else:
        M, K = a.shape
    if tb:
        N, K2 = b.shape
    else:
        K2, N = b.shape
    assert K == K2, (a.shape, b.shape, ta, tb)
    tm = _pick(M, (1024, 512, 256, 128))
    tn = _pick(N, (1024, 512, 256, 128))
    tk = _pick(K, (2048, 1024, 512, 256, 128))
    nk = K // tk
    dn = (((0 if ta else 1,), (1 if tb else 0,)), ((), ()))

    def body(a_ref, b_ref, o_ref, acc_ref):
        k = pl.program_id(2)

        @pl.when(k == 0)
        def _():
            acc_ref[...] = jnp.zeros_like(acc_ref)

        acc_ref[...] += lax.dot_general(a_ref[...].astype(BF16), b_ref[...].astype(BF16), dn,
                                        preferred_element_type=F32)

        @pl.when(k == nk - 1)
        def _():
            o_ref[...] = acc_ref[...].astype(o_ref.dtype)

    a_spec = pl.BlockSpec((tk, tm), lambda i, j, k: (k, i)) if ta else pl.BlockSpec((tm, tk), lambda i, j, k: (i, k))
    b_spec = pl.BlockSpec((tn, tk), lambda i, j, k: (j, k)) if tb else pl.BlockSpec((tk, tn), lambda i, j, k: (k, j))
    return _pcall(
        body, name=name, grid=(M // tm, N // tn, nk), in_specs=[a_spec, b_spec],
        out_specs=pl.BlockSpec((tm, tn), lambda i, j, k: (i, j)),
        out_shape=jax.ShapeDtypeStruct((M, N), out_dtype),
        scratch_shapes=[pltpu.VMEM((tm, tn), F32)],
        compiler_params=_cp(("parallel", "parallel", "arbitrary")),
    )(a, b)


def _linear(name):
    @jax.custom_vjp
    def mm(a, w):
        return _matmul(a, w, name=name + "_fwd")

    def fwd(a, w):
        return mm(a, w), (a, w)

    def bwd(res, dy):
        a, w = res
        da = _matmul(dy, w, tb=True, out_dtype=a.dtype, name=name + "_dgrad")
        dw = _matmul(a, dy, ta=True, out_dtype=w.dtype, name=name + "_wgrad")
        return da, dw

    mm.defvjp(fwd, bwd)
    return mm


def _rowop(name, f, *, n_x, n_nd, n_p, x_w, nd_w, nd_shared, p_per_group, out_w, out_dtypes, tile, groups=1):
    G = groups

    def specs(S):
        t = min(tile, S)
        xs = [pl.BlockSpec((t, w), lambda g, r: (r, g)) for w in x_w]
        nds = [pl.BlockSpec((t, w), (lambda g, r: (r, 0)) if sh else (lambda g, r: (r, g))) for w, sh in zip(nd_w, nd_shared)]
        outs = [pl.BlockSpec((t, w), lambda g, r: (r, g)) for w in out_w]
        return t, xs, nds, outs

    def pspecs(ps):
        return [pl.BlockSpec((None,) + p.shape[1:], (lambda g, r: (g, 0, 0)) if pg else (lambda g, r: (0, 0, 0)))
                for p, pg in zip(ps, p_per_group)]

    def fwd_call(xs, nds, ps):
        S = xs[0].shape[0]
        t, xsp, ndsp, osp = specs(S)

        def body(*refs):
            ins, outs = refs[:n_x + n_nd + n_p], refs[n_x + n_nd + n_p:]
            vals = [r[...].astype(F32) for r in ins]
            res = f(*vals)
            for o, r in zip(res, outs):
                r[...] = o.astype(r.dtype)

        return _pcall(
            body, name=name + "_fwd", grid=(G, S // t), in_specs=xsp + ndsp + pspecs(ps), out_specs=osp,
            out_shape=[jax.ShapeDtypeStruct((S, G * w), d) for w, d in zip(out_w, out_dtypes)],
            compiler_params=_cp(("parallel", "parallel")),
        )(*xs, *nds, *ps)

    def bwd_call(xs, nds, ps, douts):
        S = xs[0].shape[0]
        t, xsp, ndsp, osp = specs(S)
        n_in = n_x + n_nd + n_p + len(out_w)

        def body(*refs):
            ins, outs = refs[:n_in], refs[n_in:]
            xv = [r[...].astype(F32) for r in ins[:n_x]]
            ndv = [r[...].astype(F32) for r in ins[n_x:n_x + n_nd]]
            pv = [r[...].astype(F32) for r in ins[n_x + n_nd:n_x + n_nd + n_p]]
            dov = tuple(r[...].astype(F32) for r in ins[n_x + n_nd + n_p:])
            _, vjp = jax.vjp(lambda *a: tuple(f(*a[:n_x], *ndv, *a[n_x:])), *xv, *pv)
            cts = vjp(dov)
            for i in range(n_x):
                outs[i][...] = cts[i].astype(outs[i].dtype)
            g, r = pl.program_id(0), pl.program_id(1)
            for i in range(n_p):
                first = (r == 0) if p_per_group[i] else jnp.logical_and(g == 0, r == 0)
                ref, ct = outs[n_x + i], cts[n_x + i]

                @pl.when(first)
                def _(ref=ref, ct=ct):
                    ref[...] = ct

                @pl.when(jnp.logical_not(first))
                def _(ref=ref, ct=ct):
                    ref[...] += ct

        return _pcall(
            body, name=name + "_bwd", grid=(G, S // t), in_specs=xsp + ndsp + pspecs(ps) + osp,
            out_specs=xsp + pspecs(ps),
            out_shape=[jax.ShapeDtypeStruct(x.shape, x.dtype) for x in xs] + [jax.ShapeDtypeStruct(p.shape, F32) for p in ps],
            compiler_params=_cp(("arbitrary", "arbitrary")),
        )(*xs, *nds, *ps, *douts)

    @jax.custom_vjp
    def op(xs, nds, ps):
        return tuple(fwd_call(xs, nds, ps))

    def op_fwd(xs, nds, ps):
        return op(xs, nds, ps), (xs, nds, ps)

    def op_bwd(res, douts):
        xs, nds, ps = res
        out = bwd_call(xs, nds, ps, douts)
        return tuple(out[:n_x]), tuple(jnp.zeros_like(n) for n in nds), tuple(out[n_x:])

    op.defvjp(op_fwd, op_bwd)
    return op


def _rms_rows(x):
    return x * lax.rsqrt(jnp.mean(x * x, axis=-1, keepdims=True) + EPS)


def _f_norm_mod(x, g, shift, scale):
    return ((_rms_rows(x) * g) * (1.0 + scale) + shift,)


def _f_resid_norm_mod(x, mix, gate, g, shift, scale):
    xn = x + gate * mix
    return xn, (_rms_rows(xn) * g) * (1.0 + scale) + shift


def _f_resid(x, mix, gate):
    return (x + gate * mix,)


def _f_relu2(h):
    return (jnp.square(jnp.maximum(h, 0.0)),)


def _f_rms(x, g):
    return (_rms_rows(x) * g,)


def _f_glu(a, g):
    return (a * jax.nn.sigmoid(g),)


def _f_ln_silu(z, g, b):
    mu = jnp.mean(z, axis=-1, keepdims=True)
    zc = z - mu
    var = jnp.mean(zc * zc, axis=-1, keepdims=True)
    y = zc * lax.rsqrt(var + EPS) * g + b
    return (y * jax.nn.sigmoid(y),)


def _f_silu(x):
    return (x * jax.nn.sigmoid(x),)


@jax.custom_vjp
def _bdot(a, b):
    return jnp.dot(a.astype(BF16), b.astype(BF16), preferred_element_type=F32)


def _bdot_fwd(a, b):
    return _bdot(a, b), (a, b)


def _bdot_bwd(res, ct):
    a, b = res
    c16 = ct.astype(BF16)
    da = lax.dot_general(c16, b.astype(BF16), (((1,), (1,)), ((), ())), preferred_element_type=F32)
    db = lax.dot_general(a.astype(BF16), c16, (((0,), (0,)), ((), ())), preferred_element_type=F32)
    return da, db


_bdot.defvjp(_bdot_fwd, _bdot_bwd)


def _f_sgu(u, v, ng, w, bexp):
    vn = _rms_rows(jax.nn.gelu(v)) * ng
    row = lax.broadcasted_iota(jnp.int32, w.shape, 0)
    col = lax.broadcasted_iota(jnp.int32, w.shape, 1)
    mixed = _bdot(jnp.where(row >= col, w, 0.0), vn) + bexp
    return (jax.nn.gelu(u) * mixed,)


def _lo_mask():
    return lax.broadcasted_iota(jnp.int32, (1, HEAD_W), 1) < NOPE


def _f_qhead(x, tab, g):
    lo = _lo_mask()
    x2 = x * x
    ms_lo = jnp.sum(jnp.where(lo, x2, 0.0), axis=-1, keepdims=True) * (1.0 / NOPE)
    ms_hi = jnp.sum(jnp.where(lo, 0.0, x2), axis=-1, keepdims=True) * (1.0 / (HEAD_W - NOPE))
    r = jnp.where(lo, lax.rsqrt(ms_lo + EPS), lax.rsqrt(ms_hi + EPS))
    return (((x * r) * g) * tab,)


def _f_kvhead(x, g):
    lo = _lo_mask()
    ms = jnp.sum(jnp.where(lo, x * x, 0.0), axis=-1, keepdims=True) * (1.0 / NOPE)
    return (jnp.where(lo, (x * lax.rsqrt(ms + EPS)) * g, x),)


def _f_krope(a, b, ta, tb, ga, gb):
    r = lax.rsqrt(jnp.mean(a * a, axis=-1, keepdims=True) + EPS)
    return (((a * r) * ga) * ta + ((b * r) * gb) * tb,)


def _conv_fwd_call(y, w, b, name):
    S, C = y.shape
    cw = 128
    rt = _pick(S, (128,))
    w = jnp.pad(w, ((0, CONV_PAD - CONV_W), (0, 0)))

    def body(y_ref, w_ref, b_ref, z_ref, pad_ref):
        pad_ref[pl.ds(0, CONV_PAD), :] = jnp.zeros((CONV_PAD, cw), F32)
        pad_ref[pl.ds(CONV_PAD, S), :] = y_ref[...]
        wv = w_ref[...]
        bv = b_ref[...]

        def chunk(ci, carry):
            r0 = pl.multiple_of(ci * rt, rt)
            win = pad_ref[pl.ds(r0, rt + CONV_PAD), :]
            acc = jnp.broadcast_to(bv, (rt, cw))
            for k in range(CONV_W):
                off = CONV_PAD - (CONV_W - 1) + k
                sh = win if off == 0 else pltpu.roll(win, rt + CONV_PAD - off, axis=0)
                acc = acc + wv[k:k + 1, :] * sh[:rt, :]
            z_ref[pl.ds(r0, rt), :] = acc
            return carry

        lax.fori_loop(0, S // rt, chunk, 0)

    return _pcall(
        body, name=name, grid=(C // cw,),
        in_specs=[pl.BlockSpec((S, cw), lambda j: (0, j)), pl.BlockSpec((CONV_PAD, cw), lambda j: (0, j)),
                  pl.BlockSpec((1, cw), lambda j: (0, j))],
        out_specs=pl.BlockSpec((S, cw), lambda j: (0, j)),
        out_shape=jax.ShapeDtypeStruct((S, C), F32),
        scratch_shapes=[pltpu.VMEM((S + CONV_PAD, cw), F32)],
        compiler_params=_cp(("parallel",)),
    )(y, w, b)


def _conv_bwd_call(y, w, dz, name):
    S, C = y.shape
    cw = 128
    rt = _pick(S, (128,))
    w = jnp.pad(w, ((0, CONV_PAD - CONV_W), (0, 0)))

    def body(y_ref, w_ref, dz_ref, dy_ref, dw_ref, db_ref, ypad_ref, zpad_ref):
        ypad_ref[pl.ds(0, CONV_PAD), :] = jnp.zeros((CONV_PAD, cw), F32)
        ypad_ref[pl.ds(CONV_PAD, S), :] = y_ref[...]
        zpad_ref[pl.ds(0, S), :] = dz_ref[...]
        zpad_ref[pl.ds(S, CONV_PAD), :] = jnp.zeros((CONV_PAD, cw), F32)
        dw_ref[...] = jnp.zeros_like(dw_ref)
        wv = w_ref[...]

        def chunk(ci, dbacc):
            r0 = pl.multiple_of(ci * rt, rt)
            ywin = ypad_ref[pl.ds(r0, rt + CONV_PAD), :]
            zwin = zpad_ref[pl.ds(r0, rt + CONV_PAD), :]
            dzc = zwin[:rt, :]
            acc = jnp.zeros((rt, cw), F32)
            for k in range(CONV_W):
                off_z = (CONV_W - 1) - k
                zs = zwin if off_z == 0 else pltpu.roll(zwin, rt + CONV_PAD - off_z, axis=0)
                acc = acc + wv[k:k + 1, :] * zs[:rt, :]
                off_y = CONV_PAD - (CONV_W - 1) + k
                ys = pltpu.roll(ywin, rt + CONV_PAD - off_y, axis=0)
                dw_ref[k:k + 1, :] += jnp.sum(dzc * ys[:rt, :], axis=0, keepdims=True)
            dy_ref[pl.ds(r0, rt), :] = acc
            return dbacc + jnp.sum(dzc, axis=0, keepdims=True)

        db_ref[...] = lax.fori_loop(0, S // rt, chunk, jnp.zeros((1, cw), F32))

    dy, dw, db = _pcall(
        body, name=name, grid=(C // cw,),
        in_specs=[pl.BlockSpec((S, cw), lambda j: (0, j)), pl.BlockSpec((CONV_PAD, cw), lambda j: (0, j)),
                  pl.BlockSpec((S, cw), lambda j: (0, j))],
        out_specs=[pl.BlockSpec((S, cw), lambda j: (0, j)), pl.BlockSpec((CONV_PAD, cw), lambda j: (0, j)),
                   pl.BlockSpec((1, cw), lambda j: (0, j))],
        out_shape=[jax.ShapeDtypeStruct((S, C), F32), jax.ShapeDtypeStruct((CONV_PAD, C), F32),
                   jax.ShapeDtypeStruct((1, C), F32)],
        scratch_shapes=[pltpu.VMEM((S + CONV_PAD, cw), F32), pltpu.VMEM((S + CONV_PAD, cw), F32)],
        compiler_params=_cp(("parallel",)),
    )(y, w, dz)
    return dy, dw[:CONV_W], db


def _conv_op(name):
    @jax.custom_vjp
    def conv(y, w, b):
        return _conv_fwd_call(y, w, b, name + "_fwd")

    def fwd(y, w, b):
        return conv(y, w, b), (y, w)

    def bwd(res, dz):
        y, w = res
        return _conv_bwd_call(y, w, dz, name + "_bwd")

    conv.defvjp(fwd, bwd)
    return conv


def _attn_tile(S):
    return _pick(S, (256, 128))


def _scores(qv, kc, q0, k0, t):
    s = lax.dot_general(qv, kc, (((1,), (1,)), ((), ())), preferred_element_type=F32) * ATTN_SCALE
    qi = q0 + lax.broadcasted_iota(jnp.int32, (t, t), 0)
    ki = k0 + lax.broadcasted_iota(jnp.int32, (t, t), 1)
    return jnp.where(ki <= qi, s, -jnp.inf)


def _attn_fwd_call(q, kv, kr, name):
    S = q.shape[0]
    t = _attn_tile(S)

    def body(q_ref, kv_ref, kr_ref, o_ref, lse_ref):
        i = pl.program_id(1)
        qv = q_ref[...]

        def step(j, carry):
            m, l, acc = carry
            off = pl.multiple_of(j * t, t)
            kc = jnp.concatenate([kv_ref[pl.ds(off, t), pl.ds(0, NOPE)], kr_ref[pl.ds(off, t), :]], axis=-1)
            vv = kv_ref[pl.ds(off, t), pl.ds(NOPE, VDIM)]
            s = _scores(qv, kc, i * t, off, t)
            mn = jnp.maximum(m, jnp.max(s, axis=-1, keepdims=True))
            p = jnp.exp(s - mn)
            al = jnp.exp(m - mn)
            l = al * l + jnp.sum(p, axis=-1, keepdims=True)
            acc = al * acc + jnp.dot(p.astype(BF16), vv, preferred_element_type=F32)
            return mn, l, acc

        init = (jnp.full((t, 1), -jnp.inf, F32), jnp.zeros((t, 1), F32), jnp.zeros((t, VDIM), F32))
        m, l, acc = lax.fori_loop(0, i + 1, step, init)
        o_ref[...] = (acc / l).astype(o_ref.dtype)
        lse_ref[...] = m + jnp.log(l)

    return _pcall(
        body, name=name, grid=(N_HEADS, S // t),
        in_specs=[pl.BlockSpec((t, HEAD_W), lambda h, i: (i, h)), pl.BlockSpec((S, HEAD_W), lambda h, i: (0, h)),
                  pl.BlockSpec((S, 128), lambda h, i: (0, 0))],
        out_specs=[pl.BlockSpec((t, VDIM), lambda h, i: (i, h)), pl.BlockSpec((None, t, 1), lambda h, i: (h, i, 0))],
        out_shape=[jax.ShapeDtypeStruct((S, N_HEADS * VDIM), BF16), jax.ShapeDtypeStruct((N_HEADS, S, 1), F32)],
        compiler_params=_cp(("parallel", "parallel")),
    )(q, kv, kr)


def _attn_dq_call(q, kv, kr, o, do, lse, name):
    S = q.shape[0]
    t = _attn_tile(S)

    def body(q_ref, kv_ref, kr_ref, o_ref, do_ref, lse_ref, dq_ref, dd_ref):
        i = pl.program_id(1)
        qv = q_ref[...]
        dov = do_ref[...]
        dd = jnp.sum(dov.astype(F32) * o_ref[...].astype(F32), axis=-1, keepdims=True)
        lse = lse_ref[...]

        def step(j, dq):
            off = pl.multiple_of(j * t, t)
            kc = jnp.concatenate([kv_ref[pl.ds(off, t), pl.ds(0, NOPE)], kr_ref[pl.ds(off, t), :]], axis=-1)
            vv = kv_ref[pl.ds(off, t), pl.ds(NOPE, VDIM)]
            p = jnp.exp(_scores(qv, kc, i * t, off, t) - lse)
            dp = lax.dot_general(dov, vv, (((1,), (1,)), ((), ())), preferred_element_type=F32)
            ds = (p * (dp - dd) * ATTN_SCALE).astype(BF16)
            return dq + jnp.dot(ds, kc, preferred_element_type=F32)

        dq = lax.fori_loop(0, i + 1, step, jnp.zeros((t, HEAD_W), F32))
        dq_ref[...] = dq.astype(dq_ref.dtype)
        dd_ref[...] = dd

    return _pcall(
        body, name=name, grid=(N_HEADS, S // t),
        in_specs=[pl.BlockSpec((t, HEAD_W), lambda h, i: (i, h)), pl.BlockSpec((S, HEAD_W), lambda h, i: (0, h)),
                  pl.BlockSpec((S, 128), lambda h, i: (0, 0)), pl.BlockSpec((t, VDIM), lambda h, i: (i, h)),
                  pl.BlockSpec((t, VDIM), lambda h, i: (i, h)), pl.BlockSpec((None, t, 1), lambda h, i: (h, i, 0))],
        out_specs=[pl.BlockSpec((t, HEAD_W), lambda h, i: (i, h)), pl.BlockSpec((None, t, 1), lambda h, i: (h, i, 0))],
        out_shape=[jax.ShapeDtypeStruct(q.shape, q.dtype), jax.ShapeDtypeStruct((N_HEADS, S, 1), F32)],
        compiler_params=_cp(("parallel", "parallel")),
    )(q, kv, kr, o, do, lse)


def _attn_dkv_call(q, kv, kr, do, lse, dd, name):
    S = q.shape[0]
    t = _attn_tile(S)
    nq = S // t

    def body(q_ref, kv_ref, kr_ref, do_ref, lse_ref, dd_ref, dkv_ref, dkr_ref):
        j = pl.program_id(1)
        kc = jnp.concatenate([kv_ref[:, pl.ds(0, NOPE)], kr_ref[...]], axis=-1)
        vv = kv_ref[:, pl.ds(NOPE, VDIM)]
        ki = j * t + lax.broadcasted_iota(jnp.int32, (t, t), 0)

        def step(i, carry):
            dkc, dv = carry
            off = pl.multiple_of(i * t, t)
            qv = q_ref[pl.ds(off, t), :]
            dov = do_ref[pl.ds(off, t), :]
            st = lax.dot_general(kc, qv, (((1,), (1,)), ((), ())), preferred_element_type=F32) * ATTN_SCALE
            qi = off + lax.broadcasted_iota(jnp.int32, (t, t), 1)
            pt = jnp.exp(jnp.where(ki <= qi, st, -jnp.inf) - lse_ref[pl.ds(i, 1), :])
            dv = dv + jnp.dot(pt.astype(BF16), dov, preferred_element_type=F32)
            dpt = lax.dot_general(vv, dov, (((1,), (1,)), ((), ())), preferred_element_type=F32)
            dst = (pt * (dpt - dd_ref[pl.ds(i, 1), :]) * ATTN_SCALE).astype(BF16)
            dkc = dkc + jnp.dot(dst, qv, preferred_element_type=F32)
            return dkc, dv

        dkc, dv = lax.fori_loop(j, nq, step, (jnp.zeros((t, HEAD_W), F32), jnp.zeros((t, VDIM), F32)))
        dkv_ref[:, pl.ds(0, NOPE)] = dkc[:, :NOPE].astype(dkv_ref.dtype)
        dkv_ref[:, pl.ds(NOPE, VDIM)] = dv.astype(dkv_ref.dtype)
        dkr_ref[...] = dkc[:, NOPE:]

    dkv, dkr_heads = _pcall(
        body, name=name, grid=(N_HEADS, nq),
        in_specs=[pl.BlockSpec((S, HEAD_W), lambda h, j: (0, h)), pl.BlockSpec((t, HEAD_W), lambda h, j: (j, h)),
                  pl.BlockSpec((t, 128), lambda h, j: (j, 0)), pl.BlockSpec((S, VDIM), lambda h, j: (0, h)),
                  pl.BlockSpec((None, nq, t), lambda h, j: (h, 0, 0)), pl.BlockSpec((None, nq, t), lambda h, j: (h, 0, 0))],
        out_specs=[pl.BlockSpec((t, HEAD_W), lambda h, j: (j, h)), pl.BlockSpec((None, t, 128), lambda h, j: (h, j, 0))],
        out_shape=[jax.ShapeDtypeStruct(kv.shape, kv.dtype), jax.ShapeDtypeStruct((N_HEADS, S, 128), F32)],
        compiler_params=_cp(("parallel", "parallel")),
    )(q, kv, kr, do, lse.reshape(N_HEADS, nq, t), dd.reshape(N_HEADS, nq, t))

    def sum_body(p_ref, o_ref):
        acc = p_ref[0]
        for h in range(1, N_HEADS):
            acc = acc + p_ref[h]
        o_ref[...] = acc.astype(o_ref.dtype)

    dkr = _pcall(
        sum_body, name=name + "_rope_sum", grid=(nq,),
        in_specs=[pl.BlockSpec((N_HEADS, t, 128), lambda i: (0, i, 0))], out_specs=pl.BlockSpec((t, 128), lambda i: (i, 0)),
        out_shape=jax.ShapeDtypeStruct((S, 128), kr.dtype), compiler_params=_cp(("parallel",)),
    )(dkr_heads)
    return dkv, dkr


def _attn_op(name):
    @jax.custom_vjp
    def attn(q, kv, kr):
        return _attn_fwd_call(q, kv, kr, name + "_fwd")[0]

    def fwd(q, kv, kr):
        o, lse = _attn_fwd_call(q, kv, kr, name + "_fwd")
        return o, (q, kv, kr, o, lse)

    def bwd(res, do):
        q, kv, kr, o, lse = res
        dq, dd = _attn_dq_call(q, kv, kr, o, do, lse, name + "_dq")
        dkv, dkr = _attn_dkv_call(q, kv, kr, do, lse, dd, name + "_dkv")
        return dq, dkv, dkr

    attn.defvjp(fwd, bwd)
    return attn


def _loss_call(y, target):
    S, D = y.shape
    t = _pick(S, (256, 128))

    def body(y_ref, t_ref, dy_ref, loss_ref):
        @pl.when(pl.program_id(0) == 0)
        def _():
            loss_ref[...] = jnp.zeros_like(loss_ref)

        e = y_ref[...] - t_ref[...]
        dy_ref[...] = e * (1.0 / D)
        loss_ref[...] += 0.5 * jnp.sum(jnp.mean(e * e, axis=-1, keepdims=True), axis=0, keepdims=True)

    return _pcall(
        body, name="loss_head", grid=(S // t,),
        in_specs=[pl.BlockSpec((t, D), lambda i: (i, 0)), pl.BlockSpec((t, D), lambda i: (i, 0))],
        out_specs=[pl.BlockSpec((t, D), lambda i: (i, 0)), pl.BlockSpec((1, 1), lambda i: (0, 0))],
        out_shape=[jax.ShapeDtypeStruct((S, D), F32), jax.ShapeDtypeStruct((1, 1), F32)],
        compiler_params=_cp(("arbitrary",)),
    )(y, target)


def _adamw_call(w, g, m, v, name):
    shape = w.shape
    C = shape[-1]
    R = int(np.prod(shape[:-1]))
    tr = R
    for cand in (512, 256, 128, 64, 32, 16, 8):
        if R % cand == 0 and cand * C * 4 <= 2 * 1024 * 1024:
            tr = cand
            break
    c1 = 1.0 - ADAM_B1 ** ADAM_STEP
    c2 = 1.0 - ADAM_B2 ** ADAM_STEP

    def body(w_ref, g_ref, m_ref, v_ref, d_ref, mo_ref, vo_ref):
        gv = g_ref[...]
        mn = ADAM_B1 * m_ref[...] + (1.0 - ADAM_B1) * gv
        vn = ADAM_B2 * v_ref[...] + (1.0 - ADAM_B2) * (gv * gv)
        d_ref[...] = -ADAM_LR * ((mn / c1) / (jnp.sqrt(vn / c2) + ADAM_EPS) + ADAM_WD * w_ref[...])
        mo_ref[...] = mn
        vo_ref[...] = vn

    spec = pl.BlockSpec((tr, C), lambda i: (i, 0))
    outs = _pcall(
        body, name=name, grid=(R // tr,), in_specs=[spec] * 4, out_specs=[spec] * 3,
        out_shape=[jax.ShapeDtypeStruct((R, C), F32)] * 3, compiler_params=_cp(("parallel",)),
    )(*[a.reshape(R, C) for a in (w, g, m, v)])
    return [o.reshape(shape) for o in outs]


def _gather_small(x2d, *, reduce, name):
    R, C = x2d.shape

    def body(x_ref, out_ref, *scratch):
        if reduce:
            buf_ref, send_sems, recv_sems, local_sem = scratch
        else:
            buf_ref = out_ref
            send_sems, recv_sems, local_sem = scratch
        x, y, c = lax.axis_index("x"), lax.axis_index("y"), lax.axis_index("c")
        me, sibling = (x, y, c), (x, y, 1 - c)
        chips = [(1 - x, y), (x, 1 - y), (1 - x, 1 - y)]

        def rows(px, py, pc):
            return buf_ref.at[pl.ds((4 * px + 2 * py + pc) * R, R), :]

        def copy(k, block, to, src=None):
            return pltpu.make_async_remote_copy(
                src_ref=rows(*block) if src is None else src, dst_ref=rows(*block),
                send_sem=send_sems.at[k], recv_sem=recv_sems.at[k], device_id=to, device_id_type=MESH)

        mine = pltpu.make_async_copy(x_ref, rows(*me), local_sem)
        mine.start()
        first = [copy(0, me, sibling, src=x_ref)]
        first += [copy(1 + j, me, (*chip, c), src=x_ref) for j, chip in enumerate(chips)]
        for cp in first:
            cp.start()
        passed = [copy(4 + j, (*chip, c), sibling) for j, chip in enumerate(chips)]
        for j, chip in enumerate(chips):
            copy(1 + j, (*chip, c), me).wait_recv()
            passed[j].start()
        copy(0, sibling, me).wait_recv()
        for j, chip in enumerate(chips):
            copy(4 + j, (*chip, 1 - c), me).wait_recv()
        for cp in first + passed:
            cp.wait_send()
        mine.wait()
        if reduce:
            acc = buf_ref[pl.ds(0, R), :]
            for d in range(1, N_DEV):
                acc = acc + buf_ref[pl.ds(d * R, R), :]
            out_ref[...] = acc

    scratch = [pltpu.SemaphoreType.DMA((7,)), pltpu.SemaphoreType.DMA((7,)), pltpu.SemaphoreType.DMA]
    if reduce:
        scratch = [pltpu.VMEM((N_DEV * R, C), F32)] + scratch
    return _pcall(
        body, name=name, out_shape=jax.ShapeDtypeStruct((R if reduce else N_DEV * R, C), F32),
        in_specs=[pl.BlockSpec(memory_space=pltpu.VMEM)], out_specs=pl.BlockSpec(memory_space=pltpu.VMEM),
        scratch_shapes=scratch, compiler_params=pltpu.CompilerParams(vmem_limit_bytes=VMEM_LIMIT),
    )(x2d)


def _pack(arrs):
    flat = jnp.concatenate([a.reshape(-1).astype(F32) for a in arrs])
    n = flat.shape[0]
    unit = 8 * SMALL_COLS
    flat = jnp.pad(flat, (0, (-n) % unit))
    return flat.reshape(-1, SMALL_COLS)


def _unpack(flat, shapes):
    out, o = [], 0
    for s in shapes:
        n = int(np.prod(s))
        out.append(flat[o:o + n].reshape(s))
        o += n
    return out


def _all_gather_small(arrs, name):
    p = _pack(arrs)
    g = _gather_small(p, reduce=False, name=name).reshape(N_DEV, -1)
    out, o = [], 0
    for a in arrs:
        n = int(np.prod(a.shape))
        out.append(g[:, o:o + n].reshape((N_DEV,) + a.shape))
        o += n
    return out


def _all_reduce_small(arrs, name):
    p = _pack(arrs)
    return _unpack(_gather_small(p, reduce=True, name=name).reshape(-1), [a.shape for a in arrs])


def _half_rows(shard_shape):
    return shard_shape[0] // 2


def _half_slot(ref, kind, rh, cols, half, slot):
    if kind == "col":
        return ref.at[pl.ds(half * rh, rh), pl.ds(slot * cols, cols)]
    return ref.at[pl.ds((slot * 2 + half) * rh, rh), :]


def _full_shape(kind, shard_shape):
    r, c = shard_shape
    return (r, c * N_CHIPS) if kind == "col" else (r * N_CHIPS, c)


def _mesh_pos():
    x, y, c = lax.axis_index("x"), lax.axis_index("y"), lax.axis_index("c")
    return x, y, c


def _gather_big(shards, kinds):
    n = len(shards)
    HBM = pl.BlockSpec(memory_space=pl.ANY)

    def body(*refs):
        srcs, outs = refs[:n], refs[n:2 * n]
        ici_send, ici_recv, d2d_send, d2d_recv, local_sems = refs[2 * n:]
        x, y, c = _mesh_pos()
        my = 2 * x + y
        geo = [(_half_rows(s.shape), s.shape[1]) for s in shards]

        def region(a, half, slot):
            return _half_slot(outs[a], kinds[a], geo[a][0], geo[a][1], half, slot)

        def slot_of(a, slot):
            rh, cols = geo[a]
            if kinds[a] == "col":
                return outs[a].at[:, pl.ds(slot * cols, cols)]
            return outs[a].at[pl.ds(slot * 2 * rh, 2 * rh), :]

        local = [pltpu.make_async_copy(srcs[a], slot_of(a, my), local_sems.at[a]) for a in range(n)]
        for cp in local:
            cp.start()
        sends = []
        for a in range(n):
            rh = geo[a][0]
            for r, (fx, fy) in enumerate(CHIP_RELS):
                cp = pltpu.make_async_remote_copy(
                    src_ref=srcs[a].at[pl.ds(c * rh, rh), :], dst_ref=region(a, c, my),
                    send_sem=ici_send.at[3 * a + r], recv_sem=ici_recv.at[3 * a + r],
                    device_id=(x ^ fx, y ^ fy, c), device_id_type=MESH)
                cp.start()
                sends.append(cp)
        passed = []
        for a in range(n):
            for r, (fx, fy) in enumerate(CHIP_RELS):
                frm = 2 * (x ^ fx) + (y ^ fy)
                landed = region(a, c, frm)
                pltpu.make_async_remote_copy(
                    src_ref=landed, dst_ref=landed, send_sem=ici_send.at[3 * a + r], recv_sem=ici_recv.at[3 * a + r],
                    device_id=(x, y, c), device_id_type=MESH).wait_recv()
                cp = pltpu.make_async_remote_copy(
                    src_ref=landed, dst_ref=landed, send_sem=d2d_send.at[3 * a + r], recv_sem=d2d_recv.at[3 * a + r],
                    device_id=(x, y, 1 - c), device_id_type=MESH)
                cp.start()
                passed.append(cp)
        for a in range(n):
            for r, (fx, fy) in enumerate(CHIP_RELS):
                frm = 2 * (x ^ fx) + (y ^ fy)
                other = region(a, 1 - c, frm)
                pltpu.make_async_remote_copy(
                    src_ref=other, dst_ref=other, send_sem=d2d_send.at[3 * a + r], recv_sem=d2d_recv.at[3 * a + r],
                    device_id=(x, y, c), device_id_type=MESH).wait_recv()
        for cp in sends + passed:
            cp.wait_send()
        for cp in local:
            cp.wait()

    return _pcall(
        body, name="gather_weights",
        out_shape=[jax.ShapeDtypeStruct(_full_shape(k, s.shape), s.dtype) for s, k in zip(shards, kinds)],
        in_specs=[HBM] * n, out_specs=[HBM] * n,
        scratch_shapes=[pltpu.SemaphoreType.DMA((3 * n,))] * 4 + [pltpu.SemaphoreType.DMA((n,))],
        compiler_params=pltpu.CompilerParams(vmem_limit_bytes=VMEM_LIMIT),
    )(*shards)


def _swap_halves(fulls, kinds, shard_shapes):
    n = len(fulls)
    HBM = pl.BlockSpec(memory_space=pl.ANY)
    geo = [(_half_rows(s), s[1]) for s in shard_shapes]

    def body(*refs):
        srcs, outs = refs[:n], refs[n:2 * n]
        send_sems, recv_sems = refs[2 * n:]
        x, y, c = _mesh_pos()
        cps = []
        for a in range(n):
            for s in range(N_CHIPS):
                cp = pltpu.make_async_remote_copy(
                    src_ref=_half_slot(srcs[a], kinds[a], geo[a][0], geo[a][1], 1 - c, s), dst_ref=outs[a].at[s],
                    send_sem=send_sems.at[N_CHIPS * a + s], recv_sem=recv_sems.at[N_CHIPS * a + s],
                    device_id=(x, y, 1 - c), device_id_type=MESH)
                cp.start()
                cps.append(cp)
        for cp in cps:
            cp.wait()

    return _pcall(
        body, name="grad_swap_halves",
        out_shape=[jax.ShapeDtypeStruct((N_CHIPS,) + g, f.dtype) for g, f in zip(geo, fulls)],
        in_specs=[HBM] * n, out_specs=[HBM] * n,
        scratch_shapes=[pltpu.SemaphoreType.DMA((N_CHIPS * n,))] * 2,
        compiler_params=pltpu.CompilerParams(vmem_limit_bytes=VMEM_LIMIT),
    )(*fulls)


def _add_own_half(full, recv, kind, shard_shape, cidx, name):
    rh, cols = _half_rows(shard_shape), shard_shape[1]
    tr = _pick(rh, (256, 128, 64, 32, 16))
    nb = rh // tr

    def body(c_ref, f_ref, r_ref, o_ref):
        o_ref[...] = (f_ref[...].astype(F32) + r_ref[...].astype(F32)).astype(o_ref.dtype)

    if kind == "col":
        f_spec = pl.BlockSpec((tr, cols), lambda s, i, c_ref: (c_ref[0] * nb + i, s))
    else:
        f_spec = pl.BlockSpec((tr, cols), lambda s, i, c_ref: ((s * 2 + c_ref[0]) * nb + i, 0))
    blk = pl.BlockSpec((None, tr, cols), lambda s, i, c_ref: (s, i, 0))
    return _pcall(
        body, name=name,
        grid_spec=pltpu.PrefetchScalarGridSpec(num_scalar_prefetch=1, grid=(N_CHIPS, nb), in_specs=[f_spec, blk], out_specs=blk),
        out_shape=jax.ShapeDtypeStruct((N_CHIPS, rh, cols), BF16),
        compiler_params=_cp(("arbitrary", "arbitrary")),
    )(cidx, full, recv)


def _scatter_partials(partials):
    n = len(partials)
    HBM = pl.BlockSpec(memory_space=pl.ANY)

    def body(*refs):
        srcs, outs = refs[:n], refs[n:2 * n]
        send_sems, recv_sems, local_sems = refs[2 * n:]
        x, y, c = _mesh_pos()
        my = 2 * x + y
        local = [pltpu.make_async_copy(srcs[a].at[my], outs[a].at[my], local_sems.at[a]) for a in range(n)]
        for cp in local:
            cp.start()
        cps = []
        for a in range(n):
            for r, (fx, fy) in enumerate(CHIP_RELS):
                to = 2 * (x ^ fx) + (y ^ fy)
                cp = pltpu.make_async_remote_copy(
                    src_ref=srcs[a].at[to], dst_ref=outs[a].at[my],
                    send_sem=send_sems.at[3 * a + r], recv_sem=recv_sems.at[3 * a + r],
                    device_id=(x ^ fx, y ^ fy, c), device_id_type=MESH)
                cp.start()
                cps.append(cp)
        for a in range(n):
            for r, (fx, fy) in enumerate(CHIP_RELS):
                frm = 2 * (x ^ fx) + (y ^ fy)
                pltpu.make_async_remote_copy(
                    src_ref=outs[a].at[frm], dst_ref=outs[a].at[frm],
                    send_sem=send_sems.at[3 * a + r], recv_sem=recv_sems.at[3 * a + r],
                    device_id=(x, y, c), device_id_type=MESH).wait_recv()
        for cp in cps:
            cp.wait_send()
        for cp in local:
            cp.wait()

    return _pcall(
        body, name="grad_scatter_partials",
        out_shape=[jax.ShapeDtypeStruct(p.shape, p.dtype) for p in partials],
        in_specs=[HBM] * n, out_specs=[HBM] * n,
        scratch_shapes=[pltpu.SemaphoreType.DMA((3 * n,))] * 2 + [pltpu.SemaphoreType.DMA((n,))],
        compiler_params=pltpu.CompilerParams(vmem_limit_bytes=VMEM_LIMIT),
    )(*partials)


def _sum_chips(parts, name):
    _, rh, cols = parts.shape
    tr = _pick(rh, (256, 128, 64, 32, 16))

    def body(p_ref, o_ref):
        acc = p_ref[0].astype(F32)
        for j in range(1, N_CHIPS):
            acc = acc + p_ref[j].astype(F32)
        o_ref[...] = acc

    return _pcall(
        body, name=name, grid=(rh // tr,),
        in_specs=[pl.BlockSpec((N_CHIPS, tr, cols), lambda i: (0, i, 0))],
        out_specs=pl.BlockSpec((tr, cols), lambda i: (i, 0)),
        out_shape=jax.ShapeDtypeStruct((rh, cols), F32), compiler_params=_cp(("parallel",)),
    )(parts)


def _exchange_final(halves, groups):
    n = len(halves)
    HBM = pl.BlockSpec(memory_space=pl.ANY)
    where = {}
    for wi, idxs in enumerate(groups):
        for l, a in enumerate(idxs):
            where[a] = (wi, l)

    def body(*refs):
        srcs, outs = refs[:n], refs[n:n + len(groups)]
        send_sems, recv_sems, local_sems = refs[n + len(groups):]
        x, y, c = _mesh_pos()
        local, cps = [], []
        for a in range(n):
            wi, l = where[a]
            rh = halves[a].shape[0]
            mine = outs[wi].at[l, pl.ds(c * rh, rh), :]
            lc = pltpu.make_async_copy(srcs[a], mine, local_sems.at[a])
            lc.start()
            local.append(lc)
            cp = pltpu.make_async_remote_copy(
                src_ref=srcs[a], dst_ref=mine, send_sem=send_sems.at[a], recv_sem=recv_sems.at[a],
                device_id=(x, y, 1 - c), device_id_type=MESH)
            cp.start()
            cps.append(cp)
        for a in range(n):
            wi, l = where[a]
            rh = halves[a].shape[0]
            other = outs[wi].at[l, pl.ds((1 - c) * rh, rh), :]
            pltpu.make_async_remote_copy(
                src_ref=other, dst_ref=other, send_sem=send_sems.at[a], recv_sem=recv_sems.at[a],
                device_id=(x, y, c), device_id_type=MESH).wait_recv()
        for cp in cps:
            cp.wait_send()
        for lc in local:
            lc.wait()

    out_shape = [jax.ShapeDtypeStruct((len(idxs), 2 * halves[idxs[0]].shape[0], halves[idxs[0]].shape[1]), F32) for idxs in groups]
    return _pcall(
        body, name="grad_exchange_final", out_shape=out_shape,
        in_specs=[HBM] * n, out_specs=[HBM] * len(groups),
        scratch_shapes=[pltpu.SemaphoreType.DMA((n,))] * 3,
        compiler_params=pltpu.CompilerParams(vmem_limit_bytes=VMEM_LIMIT),
    )(*halves)


def _reduce_scatter_big(fulls, kinds, shard_shapes, groups, cidx):
    recv = _swap_halves(fulls, kinds, shard_shapes)
    partials = [_add_own_half(f, r, k, s, cidx, f"grad_add_sibling_{a}")
                for a, (f, r, k, s) in enumerate(zip(fulls, recv, kinds, shard_shapes))]
    landed = _scatter_partials(partials)
    halves = [_sum_chips(p, f"grad_sum_chips_{a}") for a, p in enumerate(landed)]
    return _exchange_final(halves, groups)


def _swap64():
    return np.concatenate([np.arange(ROPE // 2, ROPE), np.arange(ROPE // 2)])


def _uq_ext_cols(n_heads):
    sw = _swap64()
    cols = []
    for h in range(n_heads):
        b = h * QK_DIM
        cols += [b + np.arange(NOPE), b + NOPE + np.arange(ROPE), b + NOPE + sw]
    return np.concatenate(cols)


def _win_ext_cols(rank2):
    sw = _swap64()
    r = np.arange(ROPE)
    return np.concatenate([np.arange(rank2), rank2 + r, rank2 + sw, rank2 + sw, rank2 + r])


def _fold_cols(d_ext, cols, n):
    return jnp.zeros(d_ext.shape[:-1] + (n,), d_ext.dtype).at[..., cols].add(d_ext)


def _forward(x, wts, tabs):
    S, D = x.shape
    depth = len(wts["mlp_w1"])
    tile = 256
    tab_q, tab_ka, tab_kb = tabs
    first = _rowop("norm_mod_0", _f_norm_mod, n_x=1, n_nd=0, n_p=3, x_w=[D], nd_w=[], nd_shared=[], p_per_group=[False] * 3,
                   out_w=[D], out_dtypes=[BF16], tile=tile)
    xcur = x
    mix = gate = None
    for l in range(depth):
        sh1, sc1, g1, sh2, sc2, g2 = [wts["mod"][l][i].reshape(1, 1, D) for i in range(6)]
        n1 = wts["norm1_g"][l].reshape(1, 1, D)
        n2 = wts["norm2_g"][l].reshape(1, 1, D)
        if l == 0:
            (h,) = first((xcur,), (), (n1, sh1, sc1))
        else:
            op = _rowop(f"resid_norm_mod_a{l}", _f_resid_norm_mod, n_x=2, n_nd=0, n_p=4, x_w=[D, D], nd_w=[], nd_shared=[],
                        p_per_group=[False] * 4, out_w=[D, D], out_dtypes=[F32, BF16], tile=tile)
            xcur, h = op((xcur, mix), (), (gate, n1, sh1, sc1))
        if l % 2 == 0:
            mix = _even_mixer(h, wts, l // 2, tile)
        else:
            mix = _mla_mixer(h, wts, l // 2, tile, tab_q, tab_ka, tab_kb)
        op = _rowop(f"resid_norm_mod_b{l}", _f_resid_norm_mod, n_x=2, n_nd=0, n_p=4, x_w=[D, D], nd_w=[], nd_shared=[],
                    p_per_group=[False] * 4, out_w=[D, D], out_dtypes=[F32, BF16], tile=tile)
        xcur, h = op((xcur, mix), (), (g1, n2, sh2, sc2))
        h1 = _linear(f"mlp_up_{l}")(h, wts["mlp_w1"][l])
        dff = h1.shape[1]
        gw = _pick(dff, (2048, 1024, 512, 256, 128))
        (act,) = _rowop(f"relu2_{l}", _f_relu2, n_x=1, n_nd=0, n_p=0, x_w=[gw], nd_w=[], nd_shared=[], p_per_group=[],
                        out_w=[gw], out_dtypes=[BF16], tile=tile, groups=dff // gw)((h1,), (), ())
        mix = _linear(f"mlp_down_{l}")(act, wts["mlp_w2"][l])
        gate = g2
    (y,) = _rowop("resid_last", _f_resid, n_x=2, n_nd=0, n_p=1, x_w=[D, D], nd_w=[], nd_shared=[], p_per_group=[False],
                  out_w=[D], out_dtypes=[F32], tile=tile)((xcur, mix), (), (gate,))
    return y


def _even_mixer(h, wts, e, tile):
    proj = _linear(f"ab_in_{e}")(h, wts["ab_w_in"][e])
    da = proj.shape[1] // 4
    u, v, a, g = [proj[:, i * da:(i + 1) * da] for i in range(4)]
    ng = da // GROUP
    sgu = _rowop(f"sgu_{e}", _f_sgu, n_x=2, n_nd=0, n_p=3, x_w=[GROUP, GROUP], nd_w=[], nd_shared=[], p_per_group=[True] * 3,
                 out_w=[GROUP], out_dtypes=[BF16], tile=CHUNK, groups=ng)
    bexp = jnp.broadcast_to(wts["sgu_b"][e][:, :, None], (ng, CHUNK, GROUP))
    (out_a,) = sgu((u, v), (), (wts["sgu_norm_g"][e].reshape(ng, 1, GROUP), wts["sgu_w"][e], bexp))
    (yglu,) = _rowop(f"glu_{e}", _f_glu, n_x=2, n_nd=0, n_p=0, x_w=[da, da], nd_w=[], nd_shared=[], p_per_group=[],
                     out_w=[da], out_dtypes=[F32], tile=tile)((a, g), (), ())
    z = _conv_op(f"conv_{e}")(yglu, wts["conv_w"][e], wts["conv_b"][e].reshape(1, da))
    (out_b,) = _rowop(f"ln_silu_{e}", _f_ln_silu, n_x=1, n_nd=0, n_p=2, x_w=[da], nd_w=[], nd_shared=[], p_per_group=[False] * 2,
                      out_w=[da], out_dtypes=[BF16], tile=tile)(
        (z,), (), (wts["conv_ln_g"][e].reshape(1, 1, da), wts["conv_ln_b"][e].reshape(1, 1, da)))
    return _linear(f"ab_out_{e}")(jnp.concatenate([out_a, out_b], axis=-1), wts["ab_w_out"][e])


def _mla_mixer(h, wts, o, tile, tab_q, tab_ka, tab_kb):
    proj = _linear(f"mla_in_{o}")(h, wts["mla_w_in"][o])
    rank = (proj.shape[1] - 4 * ROPE) // 2
    c_q, c_kv = proj[:, :rank], proj[:, rank:2 * rank]
    kr_a, kr_b = proj[:, 2 * rank:2 * rank + 2 * ROPE], proj[:, 2 * rank + 2 * ROPE:]

    def rms(name, xx, gg):
        return _rowop(name, _f_rms, n_x=1, n_nd=0, n_p=1, x_w=[rank], nd_w=[], nd_shared=[], p_per_group=[False],
                      out_w=[rank], out_dtypes=[BF16], tile=tile)((xx,), (), (gg.reshape(1, 1, rank),))[0]

    q_raw = _linear(f"mla_uq_{o}")(rms(f"rms_q_{o}", c_q, wts["mla_q_norm_g"][o]), wts["mla_w_uq"][o])
    kv_raw = _linear(f"mla_ukv_{o}")(rms(f"rms_kv_{o}", c_kv, wts["mla_kv_norm_g"][o]), wts["mla_w_ukv"][o])
    sw = _swap64()
    gq, gk = wts["mla_q_head_g"][o], wts["mla_k_head_g"][o]
    gq_ext = jnp.concatenate([gq, gq[NOPE:][sw]]).reshape(1, 1, HEAD_W)
    gk_ext = jnp.concatenate([gk[:NOPE], jnp.ones((HEAD_W - NOPE,), F32)]).reshape(1, 1, HEAD_W)
    gk_a = jnp.concatenate([gk[NOPE:], gk[NOPE:][sw]]).reshape(1, 1, 2 * ROPE)
    gk_b = jnp.concatenate([gk[NOPE:][sw], gk[NOPE:]]).reshape(1, 1, 2 * ROPE)
    (q,) = _rowop(f"q_head_{o}", _f_qhead, n_x=1, n_nd=1, n_p=1, x_w=[HEAD_W], nd_w=[HEAD_W], nd_shared=[True], p_per_group=[False],
                  out_w=[HEAD_W], out_dtypes=[BF16], tile=tile, groups=N_HEADS)((q_raw,), (tab_q,), (gq_ext,))
    (kv,) = _rowop(f"kv_head_{o}", _f_kvhead, n_x=1, n_nd=0, n_p=1, x_w=[HEAD_W], nd_w=[], nd_shared=[], p_per_group=[False],
                   out_w=[HEAD_W], out_dtypes=[BF16], tile=tile, groups=N_HEADS)((kv_raw,), (), (gk_ext,))
    (kr,) = _rowop(f"k_rope_{o}", _f_krope, n_x=2, n_nd=2, n_p=2, x_w=[2 * ROPE] * 2, nd_w=[2 * ROPE] * 2, nd_shared=[True] * 2,
                   p_per_group=[False] * 2, out_w=[2 * ROPE], out_dtypes=[BF16], tile=tile)((kr_a, kr_b), (tab_ka, tab_kb), (gk_a, gk_b))
    att = _attn_op(f"attn_{o}")(q, kv, kr)
    return _linear(f"mla_out_{o}")(att, wts["mla_w_out"][o])


def _rope_tabs(S):
    pos = jnp.arange(S, dtype=F32)
    inv = ROPE_THETA ** (-jnp.arange(0, ROPE, 2, dtype=F32) / ROPE)
    ang = pos[:, None] * inv[None, :]
    cos, sin = jnp.cos(ang), jnp.sin(ang)
    cc = jnp.concatenate([cos, cos], axis=-1)
    sg = jnp.concatenate([-sin, sin], axis=-1)
    tab_q = jnp.concatenate([jnp.ones((S, NOPE), F32), cc, sg], axis=-1)
    return tab_q, jnp.concatenate([cc, sg], axis=-1), jnp.concatenate([sg, cc], axis=-1)


def kernel(x, c, norm1_g, norm2_g, ada_w, ada_b, mlp_w1, mlp_w2, ab_w_in, sgu_norm_g, sgu_w, sgu_b, conv_w, conv_b, conv_ln_g, conv_ln_b, ab_w_out, mla_w_in, mla_q_norm_g, mla_kv_norm_g, mla_w_uq, mla_w_ukv, mla_q_head_g, mla_k_head_g, mla_w_out, loss_target, m_norm1_g, m_norm2_g, m_ada_w, m_ada_b, m_mlp_w1, m_mlp_w2, m_ab_w_in, m_sgu_norm_g, m_sgu_w, m_sgu_b, m_conv_w, m_conv_b, m_conv_ln_g, m_conv_ln_b, m_ab_w_out, m_mla_w_in, m_mla_q_norm_g, m_mla_kv_norm_g, m_mla_w_uq, m_mla_w_ukv, m_mla_q_head_g, m_mla_k_head_g, m_mla_w_out, v_norm1_g, v_norm2_g, v_ada_w, v_ada_b, v_mlp_w1, v_mlp_w2, v_ab_w_in, v_sgu_norm_g, v_sgu_w, v_sgu_b, v_conv_w, v_conv_b, v_conv_ln_g, v_conv_ln_b, v_ab_w_out, v_mla_w_in, v_mla_q_norm_g, v_mla_kv_norm_g, v_mla_w_uq, v_mla_w_ukv, v_mla_q_head_g, v_mla_k_head_g, v_mla_w_out):
    names = ["norm1_g", "norm2_g", "ada_w", "ada_b", "mlp_w1", "mlp_w2", "ab_w_in", "sgu_norm_g", "sgu_w", "sgu_b", "conv_w",
             "conv_b", "conv_ln_g", "conv_ln_b", "ab_w_out", "mla_w_in", "mla_q_norm_g", "mla_kv_norm_g", "mla_w_uq", "mla_w_ukv",
             "mla_q_head_g", "mla_k_head_g", "mla_w_out"]
    W = dict(zip(names, [norm1_g, norm2_g, ada_w, ada_b, mlp_w1, mlp_w2, ab_w_in, sgu_norm_g, sgu_w, sgu_b, conv_w, conv_b, conv_ln_g,
                         conv_ln_b, ab_w_out, mla_w_in, mla_q_norm_g, mla_kv_norm_g, mla_w_uq, mla_w_ukv, mla_q_head_g, mla_k_head_g,
                         mla_w_out]))
    M = dict(zip(names, [m_norm1_g, m_norm2_g, m_ada_w, m_ada_b, m_mlp_w1, m_mlp_w2, m_ab_w_in, m_sgu_norm_g, m_sgu_w, m_sgu_b, m_conv_w,
                         m_conv_b, m_conv_ln_g, m_conv_ln_b, m_ab_w_out, m_mla_w_in, m_mla_q_norm_g, m_mla_kv_norm_g, m_mla_w_uq,
                         m_mla_w_ukv, m_mla_q_head_g, m_mla_k_head_g, m_mla_w_out]))
    V = dict(zip(names, [v_norm1_g, v_norm2_g, v_ada_w, v_ada_b, v_mlp_w1, v_mlp_w2, v_ab_w_in, v_sgu_norm_g, v_sgu_w, v_sgu_b, v_conv_w,
                         v_conv_b, v_conv_ln_g, v_conv_ln_b, v_ab_w_out, v_mla_w_in, v_mla_q_norm_g, v_mla_kv_norm_g, v_mla_w_uq,
                         v_mla_w_ukv, v_mla_q_head_g, v_mla_k_head_g, v_mla_w_out]))
    xi, yi, ci = lax.axis_index("x"), lax.axis_index("y"), lax.axis_index("c")
    chip = 2 * xi + yi
    dev = 2 * chip + ci
    cidx = ci.astype(jnp.int32).reshape(1)
    S, D = x.shape[1], x.shape[2]
    depth, n_even, n_odd = norm1_g.shape[0], ab_w_in.shape[0], mla_w_in.shape[0]
    rank = mla_w_uq.shape[1]
    heads_per_chip = mla_w_uq.shape[2] // QK_DIM
    uq_cols = _uq_ext_cols(heads_per_chip)
    win_cols = _win_ext_cols(2 * rank)

    c_all, conv_w_all, qn_all, kvn_all = _all_gather_small([c, conv_w, mla_q_norm_g, mla_kv_norm_g], "gather_small_inputs")
    c_all = c_all.reshape(N_DEV, D)
    by_chip = lambda a: jnp.concatenate([a[2 * j] for j in range(N_CHIPS)], axis=-1)
    conv_w_full, qn_full, kvn_full = by_chip(conv_w_all), by_chip(qn_all), by_chip(kvn_all)

    (c_act,) = _rowop("silu_c", _f_silu, n_x=1, n_nd=0, n_p=0, x_w=[D], nd_w=[], nd_shared=[], p_per_group=[], out_w=[D],
                      out_dtypes=[F32], tile=N_DEV)((c_all,), (), ())
    c_act_pad = jnp.pad(c_act, ((0, 128 - N_DEV), (0, 0)))
    mod_cols = jnp.stack([_matmul(c_act_pad, ada_w[l], name=f"ada_fwd_{l}")[:N_DEV] for l in range(depth)])
    (mod_all,) = _all_gather_small([mod_cols], "gather_mod")
    mod_mine = jnp.concatenate([lax.dynamic_index_in_dim(mod_all[2 * j], dev, axis=1, keepdims=False) for j in range(N_CHIPS)], axis=-1)
    mod_mine = (mod_mine + ada_b).reshape(depth, 6, D)

    big = {"mlp_w1": "col", "mlp_w2": "row", "ab_w_in": "col", "ab_w_out": "row", "mla_w_in": "row", "mla_w_uq": "col",
           "mla_w_ukv": "col", "mla_w_out": "row"}
    src = dict(W)
    src["mla_w_in"] = mla_w_in[:, :, win_cols]
    src["mla_w_uq"] = mla_w_uq[:, :, uq_cols]
    shards, kinds, owner = [], [], []
    for nme, kind in big.items():
        for l in range(src[nme].shape[0]):
            shards.append(src[nme][l].astype(BF16))
            kinds.append(kind)
            owner.append((nme, l))
    fulls = _gather_big(shards, kinds)
    wts = {nme: [None] * src[nme].shape[0] for nme in big}
    for (nme, l), f in zip(owner, fulls):
        wts[nme][l] = f
    wts.update(norm1_g=norm1_g, norm2_g=norm2_g, mod=mod_mine, sgu_norm_g=sgu_norm_g, sgu_w=sgu_w, sgu_b=sgu_b, conv_w=conv_w_full,
               conv_b=conv_b, conv_ln_g=conv_ln_g, conv_ln_b=conv_ln_b, mla_q_norm_g=qn_full, mla_kv_norm_g=kvn_full,
               mla_q_head_g=mla_q_head_g, mla_k_head_g=mla_k_head_g)

    tabs = _rope_tabs(S)
    y, vjp = jax.vjp(lambda xx, ww: _forward(xx, ww, tabs), x[0], wts)
    dy, loss_mine = _loss_call(y, loss_target[0])
    dx, dw = vjp(dy)
    loss = lax.psum(loss_mine[0, 0], ("x", "y", "c"))

    dmod = dw["mod"].reshape(depth, 6 * D)
    small_names = ["norm1_g", "norm2_g", "sgu_norm_g", "sgu_w", "sgu_b", "conv_w", "conv_b", "conv_ln_g", "conv_ln_b", "mla_q_norm_g",
                   "mla_kv_norm_g", "mla_q_head_g", "mla_k_head_g"]
    red = _all_reduce_small([dmod] + [dw[nme] for nme in small_names], "reduce_small_grads")
    G = dict(zip(["ada_b"] + small_names, red))
    own_cols = lambda a: lax.dynamic_slice_in_dim(a, chip * (a.shape[-1] // N_CHIPS), a.shape[-1] // N_CHIPS, axis=-1)
    for nme in ("conv_w", "mla_q_norm_g", "mla_kv_norm_g"):
        G[nme] = own_cols(G[nme])

    (dmod_all,) = _all_gather_small([dmod], "gather_dmod")
    dmod_cols = own_cols(dmod_all)
    dmod_pad = jnp.pad(dmod_cols, ((0, 128 - N_DEV), (0, 0), (0, 0)))
    G["ada_w"] = jnp.stack([_matmul(c_act_pad, dmod_pad[:, l], ta=True, name=f"ada_wgrad_{l}") for l in range(depth)])

    gfull = [dw[nme][l] for nme, l in owner]
    groups, a0 = [], 0
    for nme in big:
        groups.append(list(range(a0, a0 + src[nme].shape[0])))
        a0 += src[nme].shape[0]
    gsh = _reduce_scatter_big(gfull, kinds, [s.shape for s in shards], groups, cidx)
    for nme, g in zip(big, gsh):
        G[nme] = g
    G["mla_w_in"] = _fold_cols(G["mla_w_in"], win_cols, mla_w_in.shape[-1])
    G["mla_w_uq"] = _fold_cols(G["mla_w_uq"], uq_cols, mla_w_uq.shape[-1])

    deltas, new_m, new_v = [], [], []
    for nme in names:
        d, mn, vn = _adamw_call(W[nme], G[nme], M[nme], V[nme], f"adamw_{nme}")
        deltas.append(d)
        new_m.append(mn)
        new_v.append(vn)
    return (loss, dx[None], *[G[nme] for nme in names], *deltas, *new_m, *new_v)
```

```python
import functools
import numpy as np
import jax
import jax.numpy as jnp
from jax import lax
from jax.experimental import pallas as pl
from jax.experimental.pallas import tpu as pltpu

F32 = jnp.float32
BF16 = jnp.bfloat16
MESH = pl.DeviceIdType.MESH

EPS = 1e-6
N_HEADS = 16
NOPE = 128
ROPE = 64
VDIM = 128
QK_DIM = NOPE + ROPE
HEAD_W = 256
CHUNK = 128
GROUP = 128
CONV_W = 31
CONV_PAD = 32
ROPE_THETA = 10000.0
ATTN_SCALE = QK_DIM ** -0.5
ADAM_LR, ADAM_B1, ADAM_B2, ADAM_EPS, ADAM_WD, ADAM_STEP = 0.001, 0.9, 0.999, 1e-08, 0.01, 10
N_CHIPS = 4
N_DEV = 8
VMEM_LIMIT = 56 * 1024 * 1024
SMALL_COLS = 1024
CHIP_RELS = ((1, 0), (0, 1), (1, 1))


def _pcall(body, **kw):
    return pl.pallas_call(body, **kw)


def _cp(sem=None, **kw):
    return pltpu.CompilerParams(dimension_semantics=sem, vmem_limit_bytes=VMEM_LIMIT, **kw)


def _pick(n, cands):
    for c in cands:
        if n % c == 0:
            return c
    return n


def _matmul(a, b, *, ta=False, tb=False, out_dtype=F32, name, extra=(), epilogue=None, out_dtypes=None):
    if ta:
        K, M = a.shape
    else:
        M, K = a.shape
    if tb:
        N, K2 = b.shape
    else:
        K2, N = b.shape
    assert K == K2, (a.shape, b.shape, ta, tb)
    tm = _pick(M, (1024, 512, 256, 128))
    tn = _pick(N, (1024, 512, 256, 128))
    tk = _pick(K, (2048, 1024, 512, 256, 128))
    nk = K // tk
    dn = (((0 if ta else 1,), (1 if tb else 0,)), ((), ()))

    n_extra = len(extra)
    single = epilogue is None
    if single:
        out_dtypes = [out_dtype]

    def body(a_ref, b_ref, *rest):
        extra_refs, o_refs, acc_ref = rest[:n_extra], rest[n_extra:-1], rest[-1]
        k = pl.program_id(2)

        @pl.when(k == 0)
        def _():
            acc_ref[...] = jnp.zeros_like(acc_ref)

        acc_ref[...] += lax.dot_general(a_ref[...].astype(BF16), b_ref[...].astype(BF16), dn,
                                        preferred_element_type=F32)

        @pl.when(k == nk - 1)
        def _():
            acc = acc_ref[...]
            res = (acc,) if single else epilogue(acc, *[r[...] for r in extra_refs])
            for o_ref, val in zip(o_refs, res):
                o_ref[...] = val.astype(o_ref.dtype)

    a_spec = pl.BlockSpec((tk, tm), lambda i, j, k: (k, i)) if ta else pl.BlockSpec((tm, tk), lambda i, j, k: (i, k))
    b_spec = pl.BlockSpec((tn, tk), lambda i, j, k: (j, k)) if tb else pl.BlockSpec((tk, tn), lambda i, j, k: (k, j))
    mn_spec = pl.BlockSpec((tm, tn), lambda i, j, k: (i, j))
    outs = _pcall(
        body, name=name, grid=(M // tm, N // tn, nk), in_specs=[a_spec, b_spec] + [mn_spec] * n_extra,
        out_specs=[mn_spec] * len(out_dtypes),
        out_shape=[jax.ShapeDtypeStruct((M, N), d) for d in out_dtypes],
        scratch_shapes=[pltpu.VMEM((tm, tn), F32)],
        compiler_params=_cp(("parallel", "parallel", "arbitrary")),
    )(a, b, *extra)
    return outs[0] if single else outs


def _linear(name):
    @jax.custom_vjp
    def mm(a, w):
        return _matmul(a, w, name=name + "_fwd")

    def fwd(a, w):
        return mm(a, w), (a, w)

    def bwd(res, dy):
        a, w = res
        da = _matmul(dy, w, tb=True, out_dtype=a.dtype, name=name + "_dgrad")
        dw = _matmul(a, dy, ta=True, out_dtype=w.dtype, name=name + "_wgrad")
        return da, dw

    mm.defvjp(fwd, bwd)
    return mm


def _relu2_epilogue(acc):
    return acc, jnp.square(jnp.maximum(acc, 0.0))


def _relu2_grad_epilogue(acc, h1):
    return (acc * (2.0 * jnp.maximum(h1.astype(F32), 0.0)),)


def _mlp(name):
    def run(h, w1, w2):
        h1, act = _matmul(h, w1, name=name + "_up_fwd", epilogue=_relu2_epilogue, out_dtypes=[BF16, BF16])
        return _matmul(act, w2, name=name + "_down_fwd"), h1, act

    @jax.custom_vjp
    def mlp(h, w1, w2):
        return run(h, w1, w2)[0]

    def fwd(h, w1, w2):
        y, h1, act = run(h, w1, w2)
        return y, (h, w1, w2, h1, act)

    def bwd(res, dy):
        h, w1, w2, h1, act = res
        dw2 = _matmul(act, dy, ta=True, out_dtype=w2.dtype, name=name + "_down_wgrad")
        (dh1,) = _matmul(dy, w2, tb=True, name=name + "_down_dgrad", extra=(h1,), epilogue=_relu2_grad_epilogue, out_dtypes=[BF16])
        dw1 = _matmul(h, dh1, ta=True, out_dtype=w1.dtype, name=name + "_up_wgrad")
        dh = _matmul(dh1, w1, tb=True, out_dtype=h.dtype, name=name + "_up_dgrad")
        return dh, dw1, dw2

    mlp.defvjp(fwd, bwd)
    return mlp


def _rowop(name, f, *, n_x, n_nd, n_p, x_w, nd_w, nd_shared, p_per_group, out_w, out_dtypes, tile, groups=1):
    G = groups

    def specs(S):
        t = min(tile, S)
        xs = [pl.BlockSpec((t, w), lambda g, r: (r, g)) for w in x_w]
        nds = [pl.BlockSpec((t, w), (lambda g, r: (r, 0)) if sh else (lambda g, r: (r, g))) for w, sh in zip(nd_w, nd_shared)]
        outs = [pl.BlockSpec((t, w), lambda g, r: (r, g)) for w in out_w]
        return t, xs, nds, outs

    def pspecs(ps):
        return [pl.BlockSpec((None,) + p.shape[1:], (lambda g, r: (g, 0, 0)) if pg else (lambda g, r: (0, 0, 0)))
                for p, pg in zip(ps, p_per_group)]

    def fwd_call(xs, nds, ps):
        S = xs[0].shape[0]
        t, xsp, ndsp, osp = specs(S)

        def body(*refs):
            ins, outs = refs[:n_x + n_nd + n_p], refs[n_x + n_nd + n_p:]
            vals = [r[...].astype(F32) for r in ins]
            res = f(*vals)
            for o, r in zip(res, outs):
                r[...] = o.astype(r.dtype)

        return _pcall(
            body, name=name + "_fwd", grid=(G, S // t), in_specs=xsp + ndsp + pspecs(ps), out_specs=osp,
            out_shape=[jax.ShapeDtypeStruct((S, G * w), d) for w, d in zip(out_w, out_dtypes)],
            compiler_params=_cp(("parallel", "parallel")),
        )(*xs, *nds, *ps)

    def bwd_call(xs, nds, ps, douts):
        S = xs[0].shape[0]
        t, xsp, ndsp, osp = specs(S)
        n_in = n_x + n_nd + n_p + len(out_w)

        def body(*refs):
            ins, outs = refs[:n_in], refs[n_in:]
            xv = [r[...].astype(F32) for r in ins[:n_x]]
            ndv = [r[...].astype(F32) for r in ins[n_x:n_x + n_nd]]
            pv = [r[...].astype(F32) for r in ins[n_x + n_nd:n_x + n_nd + n_p]]
            dov = tuple(r[...].astype(F32) for r in ins[n_x + n_nd + n_p:])
            _, vjp = jax.vjp(lambda *a: tuple(f(*a[:n_x], *ndv, *a[n_x:])), *xv, *pv)
            cts = vjp(dov)
            for i in range(n_x):
                outs[i][...] = cts[i].astype(outs[i].dtype)
            g, r = pl.program_id(0), pl.program_id(1)
            for i in range(n_p):
                first = (r == 0) if p_per_group[i] else jnp.logical_and(g == 0, r == 0)
                ref, ct = outs[n_x + i], cts[n_x + i]

                @pl.when(first)
                def _(ref=ref, ct=ct):
                    ref[...] = ct

                @pl.when(jnp.logical_not(first))
                def _(ref=ref, ct=ct):
                    ref[...] += ct

        return _pcall(
            body, name=name + "_bwd", grid=(G, S // t), in_specs=xsp + ndsp + pspecs(ps) + osp,
            out_specs=xsp + pspecs(ps),
            out_shape=[jax.ShapeDtypeStruct(x.shape, x.dtype) for x in xs] + [jax.ShapeDtypeStruct(p.shape, F32) for p in ps],
            compiler_params=_cp(("arbitrary", "arbitrary")),
        )(*xs, *nds, *ps, *douts)

    @jax.custom_vjp
    def op(xs, nds, ps):
        return tuple(fwd_call(xs, nds, ps))

    def op_fwd(xs, nds, ps):
        return op(xs, nds, ps), (xs, nds, ps)

    def op_bwd(res, douts):
        xs, nds, ps = res
        out = bwd_call(xs, nds, ps, douts)
        return tuple(out[:n_x]), tuple(jnp.zeros_like(n) for n in nds), tuple(out[n_x:])

    op.defvjp(op_fwd, op_bwd)
    return op


def _rms_rows(x):
    return x * lax.rsqrt(jnp.mean(x * x, axis=-1, keepdims=True) + EPS)


def _f_norm_mod(x, g, shift, scale):
    return ((_rms_rows(x) * g) * (1.0 + scale) + shift,)


def _f_resid_norm_mod(x, mix, gate, g, shift, scale):
    xn = x + gate * mix
    return xn, (_rms_rows(xn) * g) * (1.0 + scale) + shift


def _f_resid(x, mix, gate):
    return (x + gate * mix,)


def _f_rms(x, g):
    return (_rms_rows(x) * g,)


def _f_glu(a, g):
    return (a * jax.nn.sigmoid(g),)


def _f_ln_silu(z, g, b):
    mu = jnp.mean(z, axis=-1, keepdims=True)
    zc = z - mu
    var = jnp.mean(zc * zc, axis=-1, keepdims=True)
    y = zc * lax.rsqrt(var + EPS) * g + b
    return (y * jax.nn.sigmoid(y),)


def _f_silu(x):
    return (x * jax.nn.sigmoid(x),)


@jax.custom_vjp
def _bdot(a, b):
    return jnp.dot(a.astype(BF16), b.astype(BF16), preferred_element_type=F32)


def _bdot_fwd(a, b):
    return _bdot(a, b), (a, b)


def _bdot_bwd(res, ct):
    a, b = res
    c16 = ct.astype(BF16)
    da = lax.dot_general(c16, b.astype(BF16), (((1,), (1,)), ((), ())), preferred_element_type=F32)
    db = lax.dot_general(a.astype(BF16), c16, (((0,), (0,)), ((), ())), preferred_element_type=F32)
    return da, db


_bdot.defvjp(_bdot_fwd, _bdot_bwd)


def _f_sgu(u, v, ng, w, bexp):
    vn = _rms_rows(jax.nn.gelu(v)) * ng
    row = lax.broadcasted_iota(jnp.int32, w.shape, 0)
    col = lax.broadcasted_iota(jnp.int32, w.shape, 1)
    mixed = _bdot(jnp.where(row >= col, w, 0.0), vn) + bexp
    return (jax.nn.gelu(u) * mixed,)


def _lo_mask():
    return lax.broadcasted_iota(jnp.int32, (1, HEAD_W), 1) < NOPE


def _f_qhead(x, tab, g):
    lo = _lo_mask()
    x2 = x * x
    ms_lo = jnp.sum(jnp.where(lo, x2, 0.0), axis=-1, keepdims=True) * (1.0 / NOPE)
    ms_hi = jnp.sum(jnp.where(lo, 0.0, x2), axis=-1, keepdims=True) * (1.0 / (HEAD_W - NOPE))
    r = jnp.where(lo, lax.rsqrt(ms_lo + EPS), lax.rsqrt(ms_hi + EPS))
    return (((x * r) * g) * tab,)


def _f_kvhead(x, g):
    lo = _lo_mask()
    ms = jnp.sum(jnp.where(lo, x * x, 0.0), axis=-1, keepdims=True) * (1.0 / NOPE)
    return (jnp.where(lo, (x * lax.rsqrt(ms + EPS)) * g, x),)


def _f_krope(a, b, ta, tb, ga, gb):
    r = lax.rsqrt(jnp.mean(a * a, axis=-1, keepdims=True) + EPS)
    return (((a * r) * ga) * ta + ((b * r) * gb) * tb,)


def _conv_fwd_call(y, w, b, name):
    S, C = y.shape
    cw = 128
    rt = _pick(S, (128,))
    w = jnp.pad(w, ((0, CONV_PAD - CONV_W), (0, 0)))

    def body(y_ref, w_ref, b_ref, z_ref, pad_ref):
        pad_ref[pl.ds(0, CONV_PAD), :] = jnp.zeros((CONV_PAD, cw), F32)
        pad_ref[pl.ds(CONV_PAD, S), :] = y_ref[...]
        wv = w_ref[...]
        bv = b_ref[...]

        def chunk(ci, carry):
            r0 = pl.multiple_of(ci * rt, rt)
            win = pad_ref[pl.ds(r0, rt + CONV_PAD), :]
            acc = jnp.broadcast_to(bv, (rt, cw))
            for k in range(CONV_W):
                off = CONV_PAD - (CONV_W - 1) + k
                sh = win if off == 0 else pltpu.roll(win, rt + CONV_PAD - off, axis=0)
                acc = acc + wv[k:k + 1, :] * sh[:rt, :]
            z_ref[pl.ds(r0, rt), :] = acc
            return carry

        lax.fori_loop(0, S // rt, chunk, 0)

    return _pcall(
        body, name=name, grid=(C // cw,),
        in_specs=[pl.BlockSpec((S, cw), lambda j: (0, j)), pl.BlockSpec((CONV_PAD, cw), lambda j: (0, j)),
                  pl.BlockSpec((1, cw), lambda j: (0, j))],
        out_specs=pl.BlockSpec((S, cw), lambda j: (0, j)),
        out_shape=jax.ShapeDtypeStruct((S, C), F32),
        scratch_shapes=[pltpu.VMEM((S + CONV_PAD, cw), F32)],
        compiler_params=_cp(("parallel",)),
    )(y, w, b)


def _conv_bwd_call(y, w, dz, name):
    S, C = y.shape
    cw = 128
    rt = _pick(S, (128,))
    w = jnp.pad(w, ((0, CONV_PAD - CONV_W), (0, 0)))

    def body(y_ref, w_ref, dz_ref, dy_ref, dw_ref, db_ref, ypad_ref, zpad_ref):
        ypad_ref[pl.ds(0, CONV_PAD), :] = jnp.zeros((CONV_PAD, cw), F32)
        ypad_ref[pl.ds(CONV_PAD, S), :] = y_ref[...]
        zpad_ref[pl.ds(0, S), :] = dz_ref[...]
        zpad_ref[pl.ds(S, CONV_PAD), :] = jnp.zeros((CONV_PAD, cw), F32)
        dw_ref[...] = jnp.zeros_like(dw_ref)
        wv = w_ref[...]

        def chunk(ci, dbacc):
            r0 = pl.multiple_of(ci * rt, rt)
            ywin = ypad_ref[pl.ds(r0, rt + CONV_PAD), :]
            zwin = zpad_ref[pl.ds(r0, rt + CONV_PAD), :]
            dzc = zwin[:rt, :]
            acc = jnp.zeros((rt, cw), F32)
            for k in range(CONV_W):
                off_z = (CONV_W - 1) - k
                zs = zwin if off_z == 0 else pltpu.roll(zwin, rt + CONV_PAD - off_z, axis=0)
                acc = acc + wv[k:k + 1, :] * zs[:rt, :]
                off_y = CONV_PAD - (CONV_W - 1) + k
                ys = pltpu.roll(ywin, rt + CONV_PAD - off_y, axis=0)
                dw_ref[k:k + 1, :] += jnp.sum(dzc * ys[:rt, :], axis=0, keepdims=True)
            dy_ref[pl.ds(r0, rt), :] = acc
            return dbacc + jnp.sum(dzc, axis=0, keepdims=True)

        db_ref[...] = lax.fori_loop(0, S // rt, chunk, jnp.zeros((1, cw), F32))

    dy, dw, db = _pcall(
        body, name=name, grid=(C // cw,),
        in_specs=[pl.BlockSpec((S, cw), lambda j: (0, j)), pl.BlockSpec((CONV_PAD, cw), lambda j: (0, j)),
                  pl.BlockSpec((S, cw), lambda j: (0, j))],
        out_specs=[pl.BlockSpec((S, cw), lambda j: (0, j)), pl.BlockSpec((CONV_PAD, cw), lambda j: (0, j)),
                   pl.BlockSpec((1, cw), lambda j: (0, j))],
        out_shape=[jax.ShapeDtypeStruct((S, C), F32), jax.ShapeDtypeStruct((CONV_PAD, C), F32),
                   jax.ShapeDtypeStruct((1, C), F32)],
        scratch_shapes=[pltpu.VMEM((S + CONV_PAD, cw), F32), pltpu.VMEM((S + CONV_PAD, cw), F32)],
        compiler_params=_cp(("parallel",)),
    )(y, w, dz)
    return dy, dw[:CONV_W], db


def _conv_op(name):
    @jax.custom_vjp
    def conv(y, w, b):
        return _conv_fwd_call(y, w, b, name + "_fwd")

    def fwd(y, w, b):
        return conv(y, w, b), (y, w)

    def bwd(res, dz):
        y, w = res
        return _conv_bwd_call(y, w, dz, name + "_bwd")

    conv.defvjp(fwd, bwd)
    return conv


def _attn_tile(S):
    return _pick(S, (512, 256, 128))


def _attn_fwd_call(q, kv, kr, name):
    S = q.shape[0]
    t = _attn_tile(S)

    def body(q_ref, kv_ref, kr_ref, o_ref, lse_ref):
        i = pl.program_id(1)
        qv = q_ref[...]

        def update(jb, carry, diagonal):
            m, l, acc = carry
            off = pl.multiple_of(jb * t, t)
            kc = jnp.concatenate([kv_ref[pl.ds(off, t), pl.ds(0, NOPE)], kr_ref[pl.ds(off, t), :]], axis=-1)
            vv = kv_ref[pl.ds(off, t), pl.ds(NOPE, VDIM)]
            s = lax.dot_general(qv, kc, (((1,), (1,)), ((), ())), preferred_element_type=F32) * ATTN_SCALE
            if diagonal:
                row = lax.broadcasted_iota(jnp.int32, (t, t), 0)
                col = lax.broadcasted_iota(jnp.int32, (t, t), 1)
                s = jnp.where(col <= row, s, -jnp.inf)
            mn = jnp.maximum(m, jnp.max(s, axis=-1, keepdims=True))
            p = jnp.exp(s - mn)
            al = jnp.exp(m - mn)
            l = al * l + jnp.sum(p, axis=-1, keepdims=True)
            acc = al * acc + jnp.dot(p.astype(BF16), vv, preferred_element_type=F32)
            return mn, l, acc

        init = (jnp.full((t, 1), -jnp.inf, F32), jnp.zeros((t, 1), F32), jnp.zeros((t, VDIM), F32))
        carry = lax.fori_loop(0, i, lambda jb, cr: update(jb, cr, False), init)
        m, l, acc = update(i, carry, True)
        o_ref[...] = (acc / l).astype(o_ref.dtype)
        lse_ref[...] = m + jnp.log(l)

    return _pcall(
        body, name=name, grid=(N_HEADS, S // t),
        in_specs=[pl.BlockSpec((t, HEAD_W), lambda h, i: (i, h)), pl.BlockSpec((S, HEAD_W), lambda h, i: (0, h)),
                  pl.BlockSpec((S, 128), lambda h, i: (0, 0))],
        out_specs=[pl.BlockSpec((t, VDIM), lambda h, i: (i, h)), pl.BlockSpec((None, t, 1), lambda h, i: (h, i, 0))],
        out_shape=[jax.ShapeDtypeStruct((S, N_HEADS * VDIM), BF16), jax.ShapeDtypeStruct((N_HEADS, S, 1), F32)],
        compiler_params=_cp(("parallel", "parallel")),
    )(q, kv, kr)


def _attn_dd_call(o, do, name):
    S = o.shape[0]
    t = _attn_tile(S)

    def body(o_ref, do_ref, dd_ref):
        dd_ref[...] = jnp.sum(do_ref[...].astype(F32) * o_ref[...].astype(F32), axis=-1, keepdims=True)

    return _pcall(
        body, name=name, grid=(N_HEADS, S // t),
        in_specs=[pl.BlockSpec((t, VDIM), lambda h, i: (i, h)), pl.BlockSpec((t, VDIM), lambda h, i: (i, h))],
        out_specs=pl.BlockSpec((None, t, 1), lambda h, i: (h, i, 0)),
        out_shape=jax.ShapeDtypeStruct((N_HEADS, S, 1), F32), compiler_params=_cp(("parallel", "parallel")),
    )(o, do)


def _attn_bwd_call(q, kv, kr, do, lse, dd, name):
    S = q.shape[0]
    t = _attn_tile(S)
    nq = S // t

    def body(q_ref, kv_ref, kr_ref, do_ref, lse_ref, dd_ref, dq_ref, dkv_ref, dkr_ref, dq_acc):
        j = pl.program_id(1)

        @pl.when(j == 0)
        def _():
            dq_acc[...] = jnp.zeros_like(dq_acc)

        kc = jnp.concatenate([kv_ref[:, pl.ds(0, NOPE)], kr_ref[...]], axis=-1)
        vv = kv_ref[:, pl.ds(NOPE, VDIM)]

        def update(ib, carry, diagonal):
            dkc, dv = carry
            off = pl.multiple_of(ib * t, t)
            qv = q_ref[pl.ds(off, t), :]
            dov = do_ref[pl.ds(off, t), :]
            st = lax.dot_general(kc, qv, (((1,), (1,)), ((), ())), preferred_element_type=F32) * ATTN_SCALE
            if diagonal:
                key = lax.broadcasted_iota(jnp.int32, (t, t), 0)
                qry = lax.broadcasted_iota(jnp.int32, (t, t), 1)
                st = jnp.where(key <= qry, st, -jnp.inf)
            pt = jnp.exp(st - lse_ref[pl.ds(ib, 1), :])
            dv = dv + jnp.dot(pt.astype(BF16), dov, preferred_element_type=F32)
            dpt = lax.dot_general(vv, dov, (((1,), (1,)), ((), ())), preferred_element_type=F32)
            dst = (pt * (dpt - dd_ref[pl.ds(ib, 1), :]) * ATTN_SCALE).astype(BF16)
            dkc = dkc + jnp.dot(dst, qv, preferred_element_type=F32)
            dq_acc[pl.ds(off, t), :] += lax.dot_general(dst, kc, (((0,), (0,)), ((), ())), preferred_element_type=F32)
            return dkc, dv

        carry = update(j, (jnp.zeros((t, HEAD_W), F32), jnp.zeros((t, VDIM), F32)), True)
        dkc, dv = lax.fori_loop(j + 1, nq, lambda ib, cr: update(ib, cr, False), carry)
        dkv_ref[:, pl.ds(0, NOPE)] = dkc[:, :NOPE].astype(dkv_ref.dtype)
        dkv_ref[:, pl.ds(NOPE, VDIM)] = dv.astype(dkv_ref.dtype)
        dkr_ref[...] = dkc[:, NOPE:]

        @pl.when(j == nq - 1)
        def _():
            dq_ref[...] = dq_acc[...].astype(dq_ref.dtype)

    dq, dkv, dkr_heads = _pcall(
        body, name=name, grid=(N_HEADS, nq),
        in_specs=[pl.BlockSpec((S, HEAD_W), lambda h, j: (0, h)), pl.BlockSpec((t, HEAD_W), lambda h, j: (j, h)),
                  pl.BlockSpec((t, 128), lambda h, j: (j, 0)), pl.BlockSpec((S, VDIM), lambda h, j: (0, h)),
                  pl.BlockSpec((None, nq, t), lambda h, j: (h, 0, 0)), pl.BlockSpec((None, nq, t), lambda h, j: (h, 0, 0))],
        out_specs=[pl.BlockSpec((S, HEAD_W), lambda h, j: (0, h)), pl.BlockSpec((t, HEAD_W), lambda h, j: (j, h)),
                   pl.BlockSpec((None, t, 128), lambda h, j: (h, j, 0))],
        out_shape=[jax.ShapeDtypeStruct(q.shape, q.dtype), jax.ShapeDtypeStruct(kv.shape, kv.dtype),
                   jax.ShapeDtypeStruct((N_HEADS, S, 128), F32)],
        scratch_shapes=[pltpu.VMEM((S, HEAD_W), F32)],
        compiler_params=_cp(("parallel", "arbitrary")),
    )(q, kv, kr, do, lse.reshape(N_HEADS, nq, t), dd.reshape(N_HEADS, nq, t))

    def sum_body(p_ref, o_ref):
        acc = p_ref[0]
        for h in range(1, N_HEADS):
            acc = acc + p_ref[h]
        o_ref[...] = acc.astype(o_ref.dtype)

    dkr = _pcall(
        sum_body, name=name + "_rope_sum", grid=(nq,),
        in_specs=[pl.BlockSpec((N_HEADS, t, 128), lambda i: (0, i, 0))], out_specs=pl.BlockSpec((t, 128), lambda i: (i, 0)),
        out_shape=jax.ShapeDtypeStruct((S, 128), kr.dtype), compiler_params=_cp(("parallel",)),
    )(dkr_heads)
    return dq, dkv, dkr


def _attn_op(name):
    @jax.custom_vjp
    def attn(q, kv, kr):
        return _attn_fwd_call(q, kv, kr, name + "_fwd")[0]

    def fwd(q, kv, kr):
        o, lse = _attn_fwd_call(q, kv, kr, name + "_fwd")
        return o, (q, kv, kr, o, lse)

    def bwd(res, do):
        q, kv, kr, o, lse = res
        dd = _attn_dd_call(o, do, name + "_dd")
        return _attn_bwd_call(q, kv, kr, do, lse, dd, name + "_bwd")

    attn.defvjp(fwd, bwd)
    return attn


def _loss_call(y, target):
    S, D = y.shape
    t = _pick(S, (256, 128))

    def body(y_ref, t_ref, dy_ref, loss_ref):
        @pl.when(pl.program_id(0) == 0)
        def _():
            loss_ref[...] = jnp.zeros_like(loss_ref)

        e = y_ref[...] - t_ref[...]
        dy_ref[...] = e * (1.0 / D)
        loss_ref[...] += 0.5 * jnp.sum(jnp.mean(e * e, axis=-1, keepdims=True), axis=0, keepdims=True)

    return _pcall(
        body, name="loss_head", grid=(S // t,),
        in_specs=[pl.BlockSpec((t, D), lambda i: (i, 0)), pl.BlockSpec((t, D), lambda i: (i, 0))],
        out_specs=[pl.BlockSpec((t, D), lambda i: (i, 0)), pl.BlockSpec((1, 1), lambda i: (0, 0))],
        out_shape=[jax.ShapeDtypeStruct((S, D), F32), jax.ShapeDtypeStruct((1, 1), F32)],
        compiler_params=_cp(("arbitrary",)),
    )(y, target)


def _adamw_call(w, g, m, v, name):
    shape = w.shape
    C = shape[-1]
    R = int(np.prod(shape[:-1]))
    tr = R
    for cand in (512, 256, 128, 64, 32, 16, 8):
        if R % cand == 0 and cand * C * 4 <= 2 * 1024 * 1024:
            tr = cand
            break
    c1 = 1.0 - ADAM_B1 ** ADAM_STEP
    c2 = 1.0 - ADAM_B2 ** ADAM_STEP

    def body(w_ref, g_ref, m_ref, v_ref, d_ref, mo_ref, vo_ref):
        gv = g_ref[...]
        mn = ADAM_B1 * m_ref[...] + (1.0 - ADAM_B1) * gv
        vn = ADAM_B2 * v_ref[...] + (1.0 - ADAM_B2) * (gv * gv)
        d_ref[...] = -ADAM_LR * ((mn / c1) / (jnp.sqrt(vn / c2) + ADAM_EPS) + ADAM_WD * w_ref[...])
        mo_ref[...] = mn
        vo_ref[...] = vn

    spec = pl.BlockSpec((tr, C), lambda i: (i, 0))
    outs = _pcall(
        body, name=name, grid=(R // tr,), in_specs=[spec] * 4, out_specs=[spec] * 3,
        out_shape=[jax.ShapeDtypeStruct((R, C), F32)] * 3, compiler_params=_cp(("parallel",)),
    )(*[a.reshape(R, C) for a in (w, g, m, v)])
    return [o.reshape(shape) for o in outs]


def _gather_small(x2d, *, reduce, name):
    R, C = x2d.shape

    def body(x_ref, out_ref, *scratch):
        if reduce:
            buf_ref, send_sems, recv_sems, local_sem = scratch
        else:
            buf_ref = out_ref
            send_sems, recv_sems, local_sem = scratch
        x, y, c = lax.axis_index("x"), lax.axis_index("y"), lax.axis_index("c")
        me, sibling = (x, y, c), (x, y, 1 - c)
        chips = [(1 - x, y), (x, 1 - y), (1 - x, 1 - y)]

        def rows(px, py, pc):
            return buf_ref.at[pl.ds((4 * px + 2 * py + pc) * R, R), :]

        def copy(k, block, to, src=None):
            return pltpu.make_async_remote_copy(
                src_ref=rows(*block) if src is None else src, dst_ref=rows(*block),
                send_sem=send_sems.at[k], recv_sem=recv_sems.at[k], device_id=to, device_id_type=MESH)

        mine = pltpu.make_async_copy(x_ref, rows(*me), local_sem)
        mine.start()
        first = [copy(0, me, sibling, src=x_ref)]
        first += [copy(1 + j, me, (*chip, c), src=x_ref) for j, chip in enumerate(chips)]
        for cp in first:
            cp.start()
        passed = [copy(4 + j, (*chip, c), sibling) for j, chip in enumerate(chips)]
        for j, chip in enumerate(chips):
            copy(1 + j, (*chip, c), me).wait_recv()
            passed[j].start()
        copy(0, sibling, me).wait_recv()
        for j, chip in enumerate(chips):
            copy(4 + j, (*chip, 1 - c), me).wait_recv()
        for cp in first + passed:
            cp.wait_send()
        mine.wait()
        if reduce:
            acc = buf_ref[pl.ds(0, R), :]
            for d in range(1, N_DEV):
                acc = acc + buf_ref[pl.ds(d * R, R), :]
            out_ref[...] = acc

    scratch = [pltpu.SemaphoreType.DMA((7,)), pltpu.SemaphoreType.DMA((7,)), pltpu.SemaphoreType.DMA]
    if reduce:
        scratch = [pltpu.VMEM((N_DEV * R, C), F32)] + scratch
    return _pcall(
        body, name=name, out_shape=jax.ShapeDtypeStruct((R if reduce else N_DEV * R, C), F32),
        in_specs=[pl.BlockSpec(memory_space=pltpu.VMEM)], out_specs=pl.BlockSpec(memory_space=pltpu.VMEM),
        scratch_shapes=scratch, compiler_params=pltpu.CompilerParams(vmem_limit_bytes=VMEM_LIMIT),
    )(x2d)


def _pack(arrs):
    flat = jnp.concatenate([a.reshape(-1).astype(F32) for a in arrs])
    n = flat.shape[0]
    unit = 8 * SMALL_COLS
    flat = jnp.pad(flat, (0, (-n) % unit))
    return flat.reshape(-1, SMALL_COLS)


def _unpack(flat, shapes):
    out, o = [], 0
    for s in shapes:
        n = int(np.prod(s))
        out.append(flat[o:o + n].reshape(s))
        o += n
    return out


def _all_gather_small(arrs, name):
    p = _pack(arrs)
    g = _gather_small(p, reduce=False, name=name).reshape(N_DEV, -1)
    out, o = [], 0
    for a in arrs:
        n = int(np.prod(a.shape))
        out.append(g[:, o:o + n].reshape((N_DEV,) + a.shape))
        o += n
    return out


def _all_reduce_small(arrs, name):
    p = _pack(arrs)
    return _unpack(_gather_small(p, reduce=True, name=name).reshape(-1), [a.shape for a in arrs])


def _half_rows(shard_shape):
    return shard_shape[0] // 2


def _half_slot(ref, kind, rh, cols, half, slot):
    if kind == "col":
        return ref.at[pl.ds(half * rh, rh), pl.ds(slot * cols, cols)]
    return ref.at[pl.ds((slot * 2 + half) * rh, rh), :]


def _full_shape(kind, shard_shape):
    r, c = shard_shape
    return (r, c * N_CHIPS) if kind == "col" else (r * N_CHIPS, c)


def _mesh_pos():
    x, y, c = lax.axis_index("x"), lax.axis_index("y"), lax.axis_index("c")
    return x, y, c


def _gather_big(shards, kinds):
    n = len(shards)
    HBM = pl.BlockSpec(memory_space=pl.ANY)

    def body(*refs):
        srcs, outs = refs[:n], refs[n:2 * n]
        ici_send, ici_recv, d2d_send, d2d_recv, own_send, own_recv = refs[2 * n:]
        x, y, c = _mesh_pos()
        my = 2 * x + y
        geo = [(_half_rows(s.shape), s.shape[1]) for s in shards]

        def region(a, half, slot):
            return _half_slot(outs[a], kinds[a], geo[a][0], geo[a][1], half, slot)

        def slot_of(a, slot):
            rh, cols = geo[a]
            if kinds[a] == "col":
                return outs[a].at[:, pl.ds(slot * cols, cols)]
            return outs[a].at[pl.ds(slot * 2 * rh, 2 * rh), :]

        sends = []
        for a in range(n):
            cp = pltpu.make_async_remote_copy(
                src_ref=srcs[a], dst_ref=slot_of(a, my), send_sem=own_send.at[a], recv_sem=own_recv.at[a],
                device_id=(x, y, 1 - c), device_id_type=MESH)
            cp.start()
            sends.append(cp)
        for a in range(n):
            rh = geo[a][0]
            for r, (fx, fy) in enumerate(CHIP_RELS):
                cp = pltpu.make_async_remote_copy(
                    src_ref=srcs[a].at[pl.ds(c * rh, rh), :], dst_ref=region(a, c, my),
                    send_sem=ici_send.at[3 * a + r], recv_sem=ici_recv.at[3 * a + r],
                    device_id=(x ^ fx, y ^ fy, c), device_id_type=MESH)
                cp.start()
                sends.append(cp)
        passed = []
        for a in range(n):
            for r, (fx, fy) in enumerate(CHIP_RELS):
                frm = 2 * (x ^ fx) + (y ^ fy)
                landed = region(a, c, frm)
                pltpu.make_async_remote_copy(
                    src_ref=landed, dst_ref=landed, send_sem=ici_send.at[3 * a + r], recv_sem=ici_recv.at[3 * a + r],
                    device_id=(x, y, c), device_id_type=MESH).wait_recv()
                cp = pltpu.make_async_remote_copy(
                    src_ref=landed, dst_ref=landed, send_sem=d2d_send.at[3 * a + r], recv_sem=d2d_recv.at[3 * a + r],
                    device_id=(x, y, 1 - c), device_id_type=MESH)
                cp.start()
                passed.append(cp)
        for a in range(n):
            for r, (fx, fy) in enumerate(CHIP_RELS):
                frm = 2 * (x ^ fx) + (y ^ fy)
                other = region(a, 1 - c, frm)
                pltpu.make_async_remote_copy(
                    src_ref=other, dst_ref=other, send_sem=d2d_send.at[3 * a + r], recv_sem=d2d_recv.at[3 * a + r],
                    device_id=(x, y, c), device_id_type=MESH).wait_recv()
        for a in range(n):
            pltpu.make_async_remote_copy(
                src_ref=srcs[a], dst_ref=slot_of(a, my), send_sem=own_send.at[a], recv_sem=own_recv.at[a],
                device_id=(x, y, c), device_id_type=MESH).wait_recv()
        for cp in sends + passed:
            cp.wait_send()

    return _pcall(
        body, name="gather_weights",
        out_shape=[jax.ShapeDtypeStruct(_full_shape(k, s.shape), s.dtype) for s, k in zip(shards, kinds)],
        in_specs=[HBM] * n, out_specs=[HBM] * n,
        scratch_shapes=[pltpu.SemaphoreType.DMA((3 * n,))] * 4 + [pltpu.SemaphoreType.DMA((n,))] * 2,
        compiler_params=pltpu.CompilerParams(vmem_limit_bytes=VMEM_LIMIT),
    )(*shards)


def _swap_halves(fulls, kinds, shard_shapes):
    n = len(fulls)
    HBM = pl.BlockSpec(memory_space=pl.ANY)
    geo = [(_half_rows(s), s[1]) for s in shard_shapes]

    def body(*refs):
        srcs, outs = refs[:n], refs[n:2 * n]
        send_sems, recv_sems = refs[2 * n:]
        x, y, c = _mesh_pos()
        cps = []
        for a in range(n):
            for s in range(N_CHIPS):
                cp = pltpu.make_async_remote_copy(
                    src_ref=_half_slot(srcs[a], kinds[a], geo[a][0], geo[a][1], 1 - c, s), dst_ref=outs[a].at[s],
                    send_sem=send_sems.at[N_CHIPS * a + s], recv_sem=recv_sems.at[N_CHIPS * a + s],
                    device_id=(x, y, 1 - c), device_id_type=MESH)
                cp.start()
                cps.append(cp)
        for cp in cps:
            cp.wait()

    return _pcall(
        body, name="grad_swap_halves",
        out_shape=[jax.ShapeDtypeStruct((N_CHIPS,) + g, f.dtype) for g, f in zip(geo, fulls)],
        in_specs=[HBM] * n, out_specs=[HBM] * n,
        scratch_shapes=[pltpu.SemaphoreType.DMA((N_CHIPS * n,))] * 2,
        compiler_params=pltpu.CompilerParams(vmem_limit_bytes=VMEM_LIMIT),
    )(*fulls)


def _add_own_half(full, recv, kind, shard_shape, cidx, name):
    rh, cols = _half_rows(shard_shape), shard_shape[1]
    tr = _pick(rh, (256, 128, 64, 32, 16))
    nb = rh // tr

    def body(c_ref, f_ref, r_ref, o_ref):
        o_ref[...] = (f_ref[...].astype(F32) + r_ref[...].astype(F32)).astype(o_ref.dtype)

    if kind == "col":
        f_spec = pl.BlockSpec((tr, cols), lambda s, i, c_ref: (c_ref[0] * nb + i, s))
    else:
        f_spec = pl.BlockSpec((tr, cols), lambda s, i, c_ref: ((s * 2 + c_ref[0]) * nb + i, 0))
    blk = pl.BlockSpec((None, tr, cols), lambda s, i, c_ref: (s, i, 0))
    return _pcall(
        body, name=name,
        grid_spec=pltpu.PrefetchScalarGridSpec(num_scalar_prefetch=1, grid=(N_CHIPS, nb), in_specs=[f_spec, blk], out_specs=blk),
        out_shape=jax.ShapeDtypeStruct((N_CHIPS, rh, cols), BF16),
        compiler_params=_cp(("arbitrary", "arbitrary")),
    )(cidx, full, recv)


def _scatter_partials(partials):
    n = len(partials)
    HBM = pl.BlockSpec(memory_space=pl.ANY)

    def body(*refs):
        srcs, outs = refs[:n], refs[n:2 * n]
        send_sems, recv_sems = refs[2 * n:]
        x, y, c = _mesh_pos()
        my = 2 * x + y
        cps = []
        for a in range(n):
            for r, (fx, fy) in enumerate(CHIP_RELS):
                to = 2 * (x ^ fx) + (y ^ fy)
                cp = pltpu.make_async_remote_copy(
                    src_ref=srcs[a].at[to], dst_ref=outs[a].at[my],
                    send_sem=send_sems.at[3 * a + r], recv_sem=recv_sems.at[3 * a + r],
                    device_id=(x ^ fx, y ^ fy, c), device_id_type=MESH)
                cp.start()
                cps.append(cp)
        for a in range(n):
            for r, (fx, fy) in enumerate(CHIP_RELS):
                frm = 2 * (x ^ fx) + (y ^ fy)
                pltpu.make_async_remote_copy(
                    src_ref=outs[a].at[frm], dst_ref=outs[a].at[frm],
                    send_sem=send_sems.at[3 * a + r], recv_sem=recv_sems.at[3 * a + r],
                    device_id=(x, y, c), device_id_type=MESH).wait_recv()
        for cp in cps:
            cp.wait_send()

    return _pcall(
        body, name="grad_scatter_partials",
        out_shape=[jax.ShapeDtypeStruct(p.shape, p.dtype) for p in partials],
        in_specs=[HBM] * n, out_specs=[HBM] * n,
        scratch_shapes=[pltpu.SemaphoreType.DMA((3 * n,))] * 2,
        compiler_params=pltpu.CompilerParams(vmem_limit_bytes=VMEM_LIMIT),
    )(*partials)


def _sum_chips_into(landed, partial, buf, layer, n_layers, pos, name):
    _, rh, cols = landed.shape
    tr = _pick(rh, (256, 128, 64, 32, 16))
    nb = rh // tr

    def body(pos_ref, l_ref, own_ref, *rest):
        o_ref = rest[-1]
        my = pos_ref[1]
        own = own_ref[...].astype(F32)
        acc = jnp.where(my == 0, own, l_ref[0].astype(F32))
        for j in range(1, N_CHIPS):
            acc = acc + jnp.where(my == j, own, l_ref[j].astype(F32))
        o_ref[...] = acc

    in_specs = [pl.BlockSpec((N_CHIPS, tr, cols), lambda i, pos_ref: (0, i, 0)),
                pl.BlockSpec((None, tr, cols), lambda i, pos_ref: (pos_ref[1], i, 0))]
    args = [pos, landed, partial]
    aliases = {}
    if buf is not None:
        in_specs.append(pl.BlockSpec(memory_space=pl.ANY))
        args.append(buf)
        aliases = {3: 0}
    return _pcall(
        body, name=name,
        grid_spec=pltpu.PrefetchScalarGridSpec(
            num_scalar_prefetch=1, grid=(nb,), in_specs=in_specs,
            out_specs=pl.BlockSpec((None, tr, cols), lambda i, pos_ref: (layer, pos_ref[0] * nb + i, 0))),
        out_shape=jax.ShapeDtypeStruct((n_layers, 2 * rh, cols), F32), input_output_aliases=aliases,
        compiler_params=_cp(("arbitrary",)),
    )(*args)


def _exchange_final(bufs):
    n = len(bufs)
    HBM = pl.BlockSpec(memory_space=pl.ANY)
    n_layers = [b.shape[0] for b in bufs]
    base = np.concatenate([[0], np.cumsum(n_layers)])

    def body(*refs):
        outs = refs[n:2 * n]
        send_sems, recv_sems = refs[2 * n:]
        x, y, c = _mesh_pos()
        cps = []
        for w in range(n):
            rh = bufs[w].shape[1] // 2
            for l in range(n_layers[w]):
                k = int(base[w]) + l
                mine = outs[w].at[l, pl.ds(c * rh, rh), :]
                cp = pltpu.make_async_remote_copy(
                    src_ref=mine, dst_ref=mine, send_sem=send_sems.at[k], recv_sem=recv_sems.at[k],
                    device_id=(x, y, 1 - c), device_id_type=MESH)
                cp.start()
                cps.append(cp)
        for w in range(n):
            rh = bufs[w].shape[1] // 2
            for l in range(n_layers[w]):
                k = int(base[w]) + l
                other = outs[w].at[l, pl.ds((1 - c) * rh, rh), :]
                pltpu.make_async_remote_copy(
                    src_ref=other, dst_ref=other, send_sem=send_sems.at[k], recv_sem=recv_sems.at[k],
                    device_id=(x, y, c), device_id_type=MESH).wait_recv()
        for cp in cps:
            cp.wait_send()

    return _pcall(
        body, name="grad_exchange_final", out_shape=[jax.ShapeDtypeStruct(b.shape, b.dtype) for b in bufs],
        in_specs=[HBM] * n, out_specs=[HBM] * n, input_output_aliases={i: i for i in range(n)},
        scratch_shapes=[pltpu.SemaphoreType.DMA((int(base[-1]),))] * 2,
        compiler_params=pltpu.CompilerParams(vmem_limit_bytes=VMEM_LIMIT),
    )(*bufs)


def _reduce_scatter_big(fulls, kinds, shard_shapes, groups, pos):
    recv = _swap_halves(fulls, kinds, shard_shapes)
    partials = [_add_own_half(f, r, k, s, pos, f"grad_add_sibling_{a}")
                for a, (f, r, k, s) in enumerate(zip(fulls, recv, kinds, shard_shapes))]
    landed = _scatter_partials(partials)
    bufs = []
    for idxs in groups:
        buf = None
        for l, a in enumerate(idxs):
            buf = _sum_chips_into(landed[a], partials[a], buf, l, len(idxs), pos, f"grad_sum_chips_{a}")
        bufs.append(buf)
    return _exchange_final(bufs)


def _swap_halves_last(z):
    h = z.shape[-1] // 2
    return jnp.concatenate([z[..., h:], z[..., :h]], axis=-1)


def _ext_uq(w):
    lead = w.shape[:-1]
    wh = w.reshape(lead + (-1, QK_DIM))
    rope = wh[..., NOPE:]
    return jnp.concatenate([wh, _swap_halves_last(rope)], axis=-1).reshape(lead + (-1,))


def _fold_uq(d):
    lead = d.shape[:-1]
    dh = d.reshape(lead + (-1, HEAD_W))
    rope = dh[..., NOPE:QK_DIM] + _swap_halves_last(dh[..., QK_DIM:])
    return jnp.concatenate([dh[..., :NOPE], rope], axis=-1).reshape(lead + (-1,))


def _ext_win(w):
    base, kr = w[..., :-ROPE], w[..., -ROPE:]
    ks = _swap_halves_last(kr)
    return jnp.concatenate([base, kr, ks, ks, kr], axis=-1)


def _fold_win(d):
    n = d.shape[-1] - 4 * ROPE
    a, b, c2, e = [d[..., n + i * ROPE:n + (i + 1) * ROPE] for i in range(4)]
    return jnp.concatenate([d[..., :n], a + e + _swap_halves_last(b + c2)], axis=-1)


def _forward(x, wts, tabs):
    S, D = x.shape
    depth = len(wts["mlp_w1"])
    tile = 256
    tab_q, tab_ka, tab_kb = tabs
    first = _rowop("norm_mod_0", _f_norm_mod, n_x=1, n_nd=0, n_p=3, x_w=[D], nd_w=[], nd_shared=[], p_per_group=[False] * 3,
                   out_w=[D], out_dtypes=[BF16], tile=tile)
    xcur = x
    mix = gate = None
    for l in range(depth):
        sh1, sc1, g1, sh2, sc2, g2 = [wts["mod"][l][i].reshape(1, 1, D) for i in range(6)]
        n1 = wts["norm1_g"][l].reshape(1, 1, D)
        n2 = wts["norm2_g"][l].reshape(1, 1, D)
        if l == 0:
            (h,) = first((xcur,), (), (n1, sh1, sc1))
        else:
            op = _rowop(f"resid_norm_mod_a{l}", _f_resid_norm_mod, n_x=2, n_nd=0, n_p=4, x_w=[D, D], nd_w=[], nd_shared=[],
                        p_per_group=[False] * 4, out_w=[D, D], out_dtypes=[F32, BF16], tile=tile)
            xcur, h = op((xcur, mix), (), (gate, n1, sh1, sc1))
        if l % 2 == 0:
            mix = _even_mixer(h, wts, l // 2, tile)
        else:
            mix = _mla_mixer(h, wts, l // 2, tile, tab_q, tab_ka, tab_kb)
        op = _rowop(f"resid_norm_mod_b{l}", _f_resid_norm_mod, n_x=2, n_nd=0, n_p=4, x_w=[D, D], nd_w=[], nd_shared=[],
                    p_per_group=[False] * 4, out_w=[D, D], out_dtypes=[F32, BF16], tile=tile)
        xcur, h = op((xcur, mix), (), (g1, n2, sh2, sc2))
        mix = _mlp(f"mlp_{l}")(h, wts["mlp_w1"][l], wts["mlp_w2"][l])
        gate = g2
    (y,) = _rowop("resid_last", _f_resid, n_x=2, n_nd=0, n_p=1, x_w=[D, D], nd_w=[], nd_shared=[], p_per_group=[False],
                  out_w=[D], out_dtypes=[F32], tile=tile)((xcur, mix), (), (gate,))
    return y


def _even_mixer(h, wts, e, tile):
    proj = _linear(f"ab_in_{e}")(h, wts["ab_w_in"][e])
    da = proj.shape[1] // 4
    u, v, a, g = [proj[:, i * da:(i + 1) * da] for i in range(4)]
    ng = da // GROUP
    sgu = _rowop(f"sgu_{e}", _f_sgu, n_x=2, n_nd=0, n_p=3, x_w=[GROUP, GROUP], nd_w=[], nd_shared=[], p_per_group=[True] * 3,
                 out_w=[GROUP], out_dtypes=[BF16], tile=CHUNK, groups=ng)
    bexp = jnp.broadcast_to(wts["sgu_b"][e][:, :, None], (ng, CHUNK, GROUP))
    (out_a,) = sgu((u, v), (), (wts["sgu_norm_g"][e].reshape(ng, 1, GROUP), wts["sgu_w"][e], bexp))
    (yglu,) = _rowop(f"glu_{e}", _f_glu, n_x=2, n_nd=0, n_p=0, x_w=[da, da], nd_w=[], nd_shared=[], p_per_group=[],
                     out_w=[da], out_dtypes=[F32], tile=tile)((a, g), (), ())
    z = _conv_op(f"conv_{e}")(yglu, wts["conv_w"][e], wts["conv_b"][e].reshape(1, da))
    (out_b,) = _rowop(f"ln_silu_{e}", _f_ln_silu, n_x=1, n_nd=0, n_p=2, x_w=[da], nd_w=[], nd_shared=[], p_per_group=[False] * 2,
                      out_w=[da], out_dtypes=[BF16], tile=tile)(
        (z,), (), (wts["conv_ln_g"][e].reshape(1, 1, da), wts["conv_ln_b"][e].reshape(1, 1, da)))
    return _linear(f"ab_out_{e}")(jnp.concatenate([out_a, out_b], axis=-1), wts["ab_w_out"][e])


def _mla_mixer(h, wts, o, tile, tab_q, tab_ka, tab_kb):
    proj = _linear(f"mla_in_{o}")(h, wts["mla_w_in"][o])
    rank = (proj.shape[1] - 4 * ROPE) // 2
    c_q, c_kv = proj[:, :rank], proj[:, rank:2 * rank]
    kr_a, kr_b = proj[:, 2 * rank:2 * rank + 2 * ROPE], proj[:, 2 * rank + 2 * ROPE:]

    def rms(name, xx, gg):
        return _rowop(name, _f_rms, n_x=1, n_nd=0, n_p=1, x_w=[rank], nd_w=[], nd_shared=[], p_per_group=[False],
                      out_w=[rank], out_dtypes=[BF16], tile=tile)((xx,), (), (gg.reshape(1, 1, rank),))[0]

    q_raw = _linear(f"mla_uq_{o}")(rms(f"rms_q_{o}", c_q, wts["mla_q_norm_g"][o]), wts["mla_w_uq"][o])
    kv_raw = _linear(f"mla_ukv_{o}")(rms(f"rms_kv_{o}", c_kv, wts["mla_kv_norm_g"][o]), wts["mla_w_ukv"][o])
    gq, gk = wts["mla_q_head_g"][o], wts["mla_k_head_g"][o]
    gk_rope = gk[NOPE:]
    gq_ext = jnp.concatenate([gq, _swap_halves_last(gq[NOPE:])]).reshape(1, 1, HEAD_W)
    gk_ext = jnp.concatenate([gk[:NOPE], jnp.ones((HEAD_W - NOPE,), F32)]).reshape(1, 1, HEAD_W)
    gk_a = jnp.concatenate([gk_rope, _swap_halves_last(gk_rope)]).reshape(1, 1, 2 * ROPE)
    gk_b = jnp.concatenate([_swap_halves_last(gk_rope), gk_rope]).reshape(1, 1, 2 * ROPE)
    head_tile = 4 * tile
    (q,) = _rowop(f"q_head_{o}", _f_qhead, n_x=1, n_nd=1, n_p=1, x_w=[HEAD_W], nd_w=[HEAD_W], nd_shared=[True], p_per_group=[False],
                  out_w=[HEAD_W], out_dtypes=[BF16], tile=head_tile, groups=N_HEADS)((q_raw,), (tab_q,), (gq_ext,))
    (kv,) = _rowop(f"kv_head_{o}", _f_kvhead, n_x=1, n_nd=0, n_p=1, x_w=[HEAD_W], nd_w=[], nd_shared=[], p_per_group=[False],
                   out_w=[HEAD_W], out_dtypes=[BF16], tile=head_tile, groups=N_HEADS)((kv_raw,), (), (gk_ext,))
    (kr,) = _rowop(f"k_rope_{o}", _f_krope, n_x=2, n_nd=2, n_p=2, x_w=[2 * ROPE] * 2, nd_w=[2 * ROPE] * 2, nd_shared=[True] * 2,
                   p_per_group=[False] * 2, out_w=[2 * ROPE], out_dtypes=[BF16], tile=tile)((kr_a, kr_b), (tab_ka, tab_kb), (gk_a, gk_b))
    att = _attn_op(f"attn_{o}")(q, kv, kr)
    return _linear(f"mla_out_{o}")(att, wts["mla_w_out"][o])


def _rope_tabs(S):
    pos = jnp.arange(S, dtype=F32)
    inv = ROPE_THETA ** (-jnp.arange(0, ROPE, 2, dtype=F32) / ROPE)
    ang = pos[:, None] * inv[None, :]
    cos, sin = jnp.cos(ang), jnp.sin(ang)
    cc = jnp.concatenate([cos, cos], axis=-1)
    sg = jnp.concatenate([-sin, sin], axis=-1)
    tab_q = jnp.concatenate([jnp.ones((S, NOPE), F32), cc, sg], axis=-1)
    return tab_q, jnp.concatenate([cc, sg], axis=-1), jnp.concatenate([sg, cc], axis=-1)


def kernel(x, c, norm1_g, norm2_g, ada_w, ada_b, mlp_w1, mlp_w2, ab_w_in, sgu_norm_g, sgu_w, sgu_b, conv_w, conv_b, conv_ln_g, conv_ln_b, ab_w_out, mla_w_in, mla_q_norm_g, mla_kv_norm_g, mla_w_uq, mla_w_ukv, mla_q_head_g, mla_k_head_g, mla_w_out, loss_target, m_norm1_g, m_norm2_g, m_ada_w, m_ada_b, m_mlp_w1, m_mlp_w2, m_ab_w_in, m_sgu_norm_g, m_sgu_w, m_sgu_b, m_conv_w, m_conv_b, m_conv_ln_g, m_conv_ln_b, m_ab_w_out, m_mla_w_in, m_mla_q_norm_g, m_mla_kv_norm_g, m_mla_w_uq, m_mla_w_ukv, m_mla_q_head_g, m_mla_k_head_g, m_mla_w_out, v_norm1_g, v_norm2_g, v_ada_w, v_ada_b, v_mlp_w1, v_mlp_w2, v_ab_w_in, v_sgu_norm_g, v_sgu_w, v_sgu_b, v_conv_w, v_conv_b, v_conv_ln_g, v_conv_ln_b, v_ab_w_out, v_mla_w_in, v_mla_q_norm_g, v_mla_kv_norm_g, v_mla_w_uq, v_mla_w_ukv, v_mla_q_head_g, v_mla_k_head_g, v_mla_w_out):
    names = ["norm1_g", "norm2_g", "ada_w", "ada_b", "mlp_w1", "mlp_w2", "ab_w_in", "sgu_norm_g", "sgu_w", "sgu_b", "conv_w",
             "conv_b", "conv_ln_g", "conv_ln_b", "ab_w_out", "mla_w_in", "mla_q_norm_g", "mla_kv_norm_g", "mla_w_uq", "mla_w_ukv",
             "mla_q_head_g", "mla_k_head_g", "mla_w_out"]
    W = dict(zip(names, [norm1_g, norm2_g, ada_w, ada_b, mlp_w1, mlp_w2, ab_w_in, sgu_norm_g, sgu_w, sgu_b, conv_w, conv_b, conv_ln_g,
                         conv_ln_b, ab_w_out, mla_w_in, mla_q_norm_g, mla_kv_norm_g, mla_w_uq, mla_w_ukv, mla_q_head_g, mla_k_head_g,
                         mla_w_out]))
    M = dict(zip(names, [m_norm1_g, m_norm2_g, m_ada_w, m_ada_b, m_mlp_w1, m_mlp_w2, m_ab_w_in, m_sgu_norm_g, m_sgu_w, m_sgu_b, m_conv_w,
                         m_conv_b, m_conv_ln_g, m_conv_ln_b, m_ab_w_out, m_mla_w_in, m_mla_q_norm_g, m_mla_kv_norm_g, m_mla_w_uq,
                         m_mla_w_ukv, m_mla_q_head_g, m_mla_k_head_g, m_mla_w_out]))
    V = dict(zip(names, [v_norm1_g, v_norm2_g, v_ada_w, v_ada_b, v_mlp_w1, v_mlp_w2, v_ab_w_in, v_sgu_norm_g, v_sgu_w, v_sgu_b, v_conv_w,
                         v_conv_b, v_conv_ln_g, v_conv_ln_b, v_ab_w_out, v_mla_w_in, v_mla_q_norm_g, v_mla_kv_norm_g, v_mla_w_uq,
                         v_mla_w_ukv, v_mla_q_head_g, v_mla_k_head_g, v_mla_w_out]))
    xi, yi, ci = lax.axis_index("x"), lax.axis_index("y"), lax.axis_index("c")
    chip = 2 * xi + yi
    dev = 2 * chip + ci
    pos = jnp.stack([ci, chip]).astype(jnp.int32)
    S, D = x.shape[1], x.shape[2]
    depth = norm1_g.shape[0]

    c_all, conv_w_all, qn_all, kvn_all = _all_gather_small([c, conv_w, mla_q_norm_g, mla_kv_norm_g], "gather_small_inputs")
    c_all = c_all.reshape(N_DEV, D)
    by_chip = lambda a: jnp.concatenate([a[2 * j] for j in range(N_CHIPS)], axis=-1)
    conv_w_full, qn_full, kvn_full = by_chip(conv_w_all), by_chip(qn_all), by_chip(kvn_all)

    (c_act,) = _rowop("silu_c", _f_silu, n_x=1, n_nd=0, n_p=0, x_w=[D], nd_w=[], nd_shared=[], p_per_group=[], out_w=[D],
                      out_dtypes=[F32], tile=N_DEV)((c_all,), (), ())
    c_act_pad = jnp.pad(c_act, ((0, 128 - N_DEV), (0, 0)))
    mod_cols = jnp.stack([_matmul(c_act_pad, ada_w[l], name=f"ada_fwd_{l}")[:N_DEV] for l in range(depth)])
    (mod_all,) = _all_gather_small([mod_cols], "gather_mod")
    mod_mine = jnp.concatenate([lax.dynamic_index_in_dim(mod_all[2 * j], dev, axis=1, keepdims=False) for j in range(N_CHIPS)], axis=-1)
    mod_mine = (mod_mine + ada_b).reshape(depth, 6, D)

    big = {"mlp_w1": "col", "mlp_w2": "row", "ab_w_in": "col", "ab_w_out": "row", "mla_w_in": "row", "mla_w_uq": "col",
           "mla_w_ukv": "col", "mla_w_out": "row"}
    src = dict(W)
    src["mla_w_in"] = _ext_win(mla_w_in)
    src["mla_w_uq"] = _ext_uq(mla_w_uq)
    shards, kinds, owner = [], [], []
    for nme, kind in big.items():
        for l in range(src[nme].shape[0]):
            shards.append(src[nme][l].astype(BF16))
            kinds.append(kind)
            owner.append((nme, l))
    fulls = _gather_big(shards, kinds)
    wts = {nme: [None] * src[nme].shape[0] for nme in big}
    for (nme, l), f in zip(owner, fulls):
        wts[nme][l] = f
    wts.update(norm1_g=norm1_g, norm2_g=norm2_g, mod=mod_mine, sgu_norm_g=sgu_norm_g, sgu_w=sgu_w, sgu_b=sgu_b, conv_w=conv_w_full,
               conv_b=conv_b, conv_ln_g=conv_ln_g, conv_ln_b=conv_ln_b, mla_q_norm_g=qn_full, mla_kv_norm_g=kvn_full,
               mla_q_head_g=mla_q_head_g, mla_k_head_g=mla_k_head_g)

    tabs = _rope_tabs(S)
    y, vjp = jax.vjp(lambda xx, ww: _forward(xx, ww, tabs), x[0], wts)
    dy, loss_mine = _loss_call(y, loss_target[0])
    dx, dw = vjp(dy)
    loss = lax.psum(loss_mine[0, 0], ("x", "y", "c"))

    dmod = dw["mod"].reshape(depth, 6 * D)
    small_names = ["norm1_g", "norm2_g", "sgu_norm_g", "sgu_w", "sgu_b", "conv_w", "conv_b", "conv_ln_g", "conv_ln_b", "mla_q_norm_g",
                   "mla_kv_norm_g", "mla_q_head_g", "mla_k_head_g"]
    red = _all_reduce_small([dmod] + [dw[nme] for nme in small_names], "reduce_small_grads")
    G = dict(zip(["ada_b"] + small_names, red))
    own_cols = lambda a: lax.dynamic_slice_in_dim(a, chip * (a.shape[-1] // N_CHIPS), a.shape[-1] // N_CHIPS, axis=-1)
    for nme in ("conv_w", "mla_q_norm_g", "mla_kv_norm_g"):
        G[nme] = own_cols(G[nme])

    (dmod_all,) = _all_gather_small([dmod], "gather_dmod")
    dmod_cols = own_cols(dmod_all)
    dmod_pad = jnp.pad(dmod_cols, ((0, 128 - N_DEV), (0, 0), (0, 0)))
    G["ada_w"] = jnp.stack([_matmul(c_act_pad, dmod_pad[:, l], ta=True, name=f"ada_wgrad_{l}") for l in range(depth)])

    gfull = [dw[nme][l] for nme, l in owner]
    groups, a0 = [], 0
    for nme in big:
        groups.append(list(range(a0, a0 + src[nme].shape[0])))
        a0 += src[nme].shape[0]
    gsh = _reduce_scatter_big(gfull, kinds, [s.shape for s in shards], groups, pos)
    for nme, g in zip(big, gsh):
        G[nme] = g
    G["mla_w_in"] = _fold_win(G["mla_w_in"])
    G["mla_w_uq"] = _fold_uq(G["mla_w_uq"])

    deltas, new_m, new_v = [], [], []
    for nme in names:
        d, mn, vn = _adamw_call(W[nme], G[nme], M[nme], V[nme], f"adamw_{nme}")
        deltas.append(d)
        new_m.append(mn)
        new_v.append(vn)
    return (loss, dx[None], *[G[nme] for nme in names], *deltas, *new_m, *new_v)
```

```python
import functools
import numpy as np
import jax
import jax.numpy as jnp
from jax import lax
from jax.experimental import pallas as pl
from jax.experimental.pallas import tpu as pltpu
from jax.experimental.pallas import tpu_sc as plsc

F32 = jnp.float32
BF16 = jnp.bfloat16
MESH = pl.DeviceIdType.MESH

EPS = 1e-6
N_HEADS = 16
NOPE = 128
ROPE = 64
VDIM = 128
QK_DIM = NOPE + ROPE
HEAD_W = 256
CHUNK = 128
GROUP = 128
CONV_W = 31
CONV_PAD = 32
ROPE_THETA = 10000.0
ATTN_SCALE = QK_DIM ** -0.5
ADAM_LR, ADAM_B1, ADAM_B2, ADAM_EPS, ADAM_WD, ADAM_STEP = 0.001, 0.9, 0.999, 1e-08, 0.01, 10
N_CHIPS = 4
N_DEV = 8
VMEM_LIMIT = 56 * 1024 * 1024
SMALL_COLS = 1024
CHIP_RELS = ((1, 0), (0, 1), (1, 1))
GATHER_COLLECTIVE_ID = 0


def _pcall(body, **kw):
    return pl.pallas_call(body, **kw)


def _cp(sem=None, **kw):
    return pltpu.CompilerParams(dimension_semantics=sem, vmem_limit_bytes=VMEM_LIMIT, **kw)


def _pick(n, cands):
    for c in cands:
        if n % c == 0:
            return c
    return n


def _matmul(a, b, *, ta=False, tb=False, out_dtype=F32, name, extra=(), epilogue=None, out_dtypes=None):
    if ta:
        K, M = a.shape
    else:
        M, K = a.shape
    if tb:
        N, K2 = b.shape
    else:
        K2, N = b.shape
    assert K == K2, (a.shape, b.shape, ta, tb)
    tm = _pick(M, (1024, 512, 256, 128))
    tn = _pick(N, (1024, 512, 256, 128))
    tk = _pick(K, (2048, 1024, 512, 256, 128))
    nk = K // tk
    dn = (((0 if ta else 1,), (1 if tb else 0,)), ((), ()))

    n_extra = len(extra)
    single = epilogue is None
    if single:
        out_dtypes = [out_dtype]

    def body(a_ref, b_ref, *rest):
        extra_refs, o_refs, acc_ref = rest[:n_extra], rest[n_extra:-1], rest[-1]
        k = pl.program_id(2)

        @pl.when(k == 0)
        def _():
            acc_ref[...] = jnp.zeros_like(acc_ref)

        acc_ref[...] += lax.dot_general(a_ref[...].astype(BF16), b_ref[...].astype(BF16), dn,
                                        preferred_element_type=F32)

        @pl.when(k == nk - 1)
        def _():
            acc = acc_ref[...]
            res = (acc,) if single else epilogue(acc, *[r[...] for r in extra_refs])
            for o_ref, val in zip(o_refs, res):
                o_ref[...] = val.astype(o_ref.dtype)

    a_spec = pl.BlockSpec((tk, tm), lambda i, j, k: (k, i)) if ta else pl.BlockSpec((tm, tk), lambda i, j, k: (i, k))
    b_spec = pl.BlockSpec((tn, tk), lambda i, j, k: (j, k)) if tb else pl.BlockSpec((tk, tn), lambda i, j, k: (k, j))
    mn_spec = pl.BlockSpec((tm, tn), lambda i, j, k: (i, j))
    outs = _pcall(
        body, name=name, grid=(M // tm, N // tn, nk), in_specs=[a_spec, b_spec] + [mn_spec] * n_extra,
        out_specs=[mn_spec] * len(out_dtypes),
        out_shape=[jax.ShapeDtypeStruct((M, N), d) for d in out_dtypes],
        scratch_shapes=[pltpu.VMEM((tm, tn), F32)],
        compiler_params=_cp(("parallel", "parallel", "arbitrary")),
    )(a, b, *extra)
    return outs[0] if single else outs


def _linear(name):
    @jax.custom_vjp
    def mm(a, w):
        return _matmul(a, w, name=name + "_fwd")

    def fwd(a, w):
        return mm(a, w), (a, w)

    def bwd(res, dy):
        a, w = res
        da = _matmul(dy, w, tb=True, out_dtype=a.dtype, name=name + "_dgrad")
        dw = _matmul(a, dy, ta=True, out_dtype=w.dtype, name=name + "_wgrad")
        return da, dw

    mm.defvjp(fwd, bwd)
    return mm


def _relu2_epilogue(acc):
    return acc, jnp.square(jnp.maximum(acc, 0.0))


def _relu2_grad_epilogue(acc, h1):
    return (acc * (2.0 * jnp.maximum(h1.astype(F32), 0.0)),)


def _mlp(name):
    def run(h, w1, w2):
        h1, act = _matmul(h, w1, name=name + "_up_fwd", epilogue=_relu2_epilogue, out_dtypes=[BF16, BF16])
        return _matmul(act, w2, name=name + "_down_fwd"), h1, act

    @jax.custom_vjp
    def mlp(h, w1, w2):
        return run(h, w1, w2)[0]

    def fwd(h, w1, w2):
        y, h1, act = run(h, w1, w2)
        return y, (h, w1, w2, h1, act)

    def bwd(res, dy):
        h, w1, w2, h1, act = res
        dw2 = _matmul(act, dy, ta=True, out_dtype=w2.dtype, name=name + "_down_wgrad")
        (dh1,) = _matmul(dy, w2, tb=True, name=name + "_down_dgrad", extra=(h1,), epilogue=_relu2_grad_epilogue, out_dtypes=[BF16])
        dw1 = _matmul(h, dh1, ta=True, out_dtype=w1.dtype, name=name + "_up_wgrad")
        dh = _matmul(dh1, w1, tb=True, out_dtype=h.dtype, name=name + "_up_dgrad")
        return dh, dw1, dw2

    mlp.defvjp(fwd, bwd)
    return mlp


def _rowop(name, f, *, n_x, n_nd, n_p, x_w, nd_w, nd_shared, p_per_group, out_w, out_dtypes, tile, groups=1):
    G = groups

    def specs(S):
        t = min(tile, S)
        xs = [pl.BlockSpec((t, w), lambda g, r: (r, g)) for w in x_w]
        nds = [pl.BlockSpec((t, w), (lambda g, r: (r, 0)) if sh else (lambda g, r: (r, g))) for w, sh in zip(nd_w, nd_shared)]
        outs = [pl.BlockSpec((t, w), lambda g, r: (r, g)) for w in out_w]
        return t, xs, nds, outs

    def pspecs(ps):
        return [pl.BlockSpec((None,) + p.shape[1:], (lambda g, r: (g, 0, 0)) if pg else (lambda g, r: (0, 0, 0)))
                for p, pg in zip(ps, p_per_group)]

    def fwd_call(xs, nds, ps):
        S = xs[0].shape[0]
        t, xsp, ndsp, osp = specs(S)

        def body(*refs):
            ins, outs = refs[:n_x + n_nd + n_p], refs[n_x + n_nd + n_p:]
            vals = [r[...].astype(F32) for r in ins]
            res = f(*vals)
            for o, r in zip(res, outs):
                r[...] = o.astype(r.dtype)

        return _pcall(
            body, name=name + "_fwd", grid=(G, S // t), in_specs=xsp + ndsp + pspecs(ps), out_specs=osp,
            out_shape=[jax.ShapeDtypeStruct((S, G * w), d) for w, d in zip(out_w, out_dtypes)],
            compiler_params=_cp(("parallel", "parallel")),
        )(*xs, *nds, *ps)

    def bwd_call(xs, nds, ps, douts):
        S = xs[0].shape[0]
        t, xsp, ndsp, osp = specs(S)
        n_in = n_x + n_nd + n_p + len(out_w)

        def body(*refs):
            ins, outs = refs[:n_in], refs[n_in:]
            xv = [r[...].astype(F32) for r in ins[:n_x]]
            ndv = [r[...].astype(F32) for r in ins[n_x:n_x + n_nd]]
            pv = [r[...].astype(F32) for r in ins[n_x + n_nd:n_x + n_nd + n_p]]
            dov = tuple(r[...].astype(F32) for r in ins[n_x + n_nd + n_p:])
            _, vjp = jax.vjp(lambda *a: tuple(f(*a[:n_x], *ndv, *a[n_x:])), *xv, *pv)
            cts = vjp(dov)
            for i in range(n_x):
                outs[i][...] = cts[i].astype(outs[i].dtype)
            g, r = pl.program_id(0), pl.program_id(1)
            for i in range(n_p):
                first = (r == 0) if p_per_group[i] else jnp.logical_and(g == 0, r == 0)
                ref, ct = outs[n_x + i], cts[n_x + i]

                @pl.when(first)
                def _(ref=ref, ct=ct):
                    ref[...] = ct

                @pl.when(jnp.logical_not(first))
                def _(ref=ref, ct=ct):
                    ref[...] += ct

        return _pcall(
            body, name=name + "_bwd", grid=(G, S // t), in_specs=xsp + ndsp + pspecs(ps) + osp,
            out_specs=xsp + pspecs(ps),
            out_shape=[jax.ShapeDtypeStruct(x.shape, x.dtype) for x in xs] + [jax.ShapeDtypeStruct(p.shape, F32) for p in ps],
            compiler_params=_cp(("arbitrary", "arbitrary")),
        )(*xs, *nds, *ps, *douts)

    @jax.custom_vjp
    def op(xs, nds, ps):
        return tuple(fwd_call(xs, nds, ps))

    def op_fwd(xs, nds, ps):
        return op(xs, nds, ps), (xs, nds, ps)

    def op_bwd(res, douts):
        xs, nds, ps = res
        out = bwd_call(xs, nds, ps, douts)
        return tuple(out[:n_x]), tuple(jnp.zeros_like(n) for n in nds), tuple(out[n_x:])

    op.defvjp(op_fwd, op_bwd)
    return op


def _rms_rows(x):
    return x * lax.rsqrt(jnp.mean(x * x, axis=-1, keepdims=True) + EPS)


def _f_norm_mod(x, g, shift, scale):
    return ((_rms_rows(x) * g) * (1.0 + scale) + shift,)


def _f_resid_norm_mod(x, mix, gate, g, shift, scale):
    xn = x + gate * mix
    return xn, (_rms_rows(xn) * g) * (1.0 + scale) + shift


def _f_resid(x, mix, gate):
    return (x + gate * mix,)


def _f_rms(x, g):
    return (_rms_rows(x) * g,)


def _f_glu(a, g):
    return (a * jax.nn.sigmoid(g),)


def _f_ln_silu(z, g, b):
    mu = jnp.mean(z, axis=-1, keepdims=True)
    zc = z - mu
    var = jnp.mean(zc * zc, axis=-1, keepdims=True)
    y = zc * lax.rsqrt(var + EPS) * g + b
    return (y * jax.nn.sigmoid(y),)


def _f_silu(x):
    return (x * jax.nn.sigmoid(x),)


@jax.custom_vjp
def _bdot(a, b):
    return jnp.dot(a.astype(BF16), b.astype(BF16), preferred_element_type=F32)


def _bdot_fwd(a, b):
    return _bdot(a, b), (a, b)


def _bdot_bwd(res, ct):
    a, b = res
    c16 = ct.astype(BF16)
    da = lax.dot_general(c16, b.astype(BF16), (((1,), (1,)), ((), ())), preferred_element_type=F32)
    db = lax.dot_general(a.astype(BF16), c16, (((0,), (0,)), ((), ())), preferred_element_type=F32)
    return da, db


_bdot.defvjp(_bdot_fwd, _bdot_bwd)


def _f_sgu(u, v, ng, w, bexp):
    vn = _rms_rows(jax.nn.gelu(v)) * ng
    row = lax.broadcasted_iota(jnp.int32, w.shape, 0)
    col = lax.broadcasted_iota(jnp.int32, w.shape, 1)
    mixed = _bdot(jnp.where(row >= col, w, 0.0), vn) + bexp
    return (jax.nn.gelu(u) * mixed,)


def _lo_mask():
    return lax.broadcasted_iota(jnp.int32, (1, HEAD_W), 1) < NOPE


def _f_qhead(x, tab, g):
    lo = _lo_mask()
    x2 = x * x
    ms_lo = jnp.sum(jnp.where(lo, x2, 0.0), axis=-1, keepdims=True) * (1.0 / NOPE)
    ms_hi = jnp.sum(jnp.where(lo, 0.0, x2), axis=-1, keepdims=True) * (1.0 / (HEAD_W - NOPE))
    r = jnp.where(lo, lax.rsqrt(ms_lo + EPS), lax.rsqrt(ms_hi + EPS))
    return (((x * r) * g) * tab,)


def _f_kvhead(x, g):
    lo = _lo_mask()
    ms = jnp.sum(jnp.where(lo, x * x, 0.0), axis=-1, keepdims=True) * (1.0 / NOPE)
    return (jnp.where(lo, (x * lax.rsqrt(ms + EPS)) * g, x),)


def _f_krope(a, b, ta, tb, ga, gb):
    r = lax.rsqrt(jnp.mean(a * a, axis=-1, keepdims=True) + EPS)
    return (((a * r) * ga) * ta + ((b * r) * gb) * tb,)


def _conv_fwd_call(y, w, b, name):
    S, C = y.shape
    cw = 128
    rt = _pick(S, (128,))
    w = jnp.pad(w, ((0, CONV_PAD - CONV_W), (0, 0)))

    def body(y_ref, w_ref, b_ref, z_ref, pad_ref):
        pad_ref[pl.ds(0, CONV_PAD), :] = jnp.zeros((CONV_PAD, cw), F32)
        pad_ref[pl.ds(CONV_PAD, S), :] = y_ref[...]
        wv = w_ref[...]
        bv = b_ref[...]

        def chunk(ci, carry):
            r0 = pl.multiple_of(ci * rt, rt)
            win = pad_ref[pl.ds(r0, rt + CONV_PAD), :]
            acc = jnp.broadcast_to(bv, (rt, cw))
            for k in range(CONV_W):
                off = CONV_PAD - (CONV_W - 1) + k
                sh = win if off == 0 else pltpu.roll(win, rt + CONV_PAD - off, axis=0)
                acc = acc + wv[k:k + 1, :] * sh[:rt, :]
            z_ref[pl.ds(r0, rt), :] = acc
            return carry

        lax.fori_loop(0, S // rt, chunk, 0)

    return _pcall(
        body, name=name, grid=(C // cw,),
        in_specs=[pl.BlockSpec((S, cw), lambda j: (0, j)), pl.BlockSpec((CONV_PAD, cw), lambda j: (0, j)),
                  pl.BlockSpec((1, cw), lambda j: (0, j))],
        out_specs=pl.BlockSpec((S, cw), lambda j: (0, j)),
        out_shape=jax.ShapeDtypeStruct((S, C), F32),
        scratch_shapes=[pltpu.VMEM((S + CONV_PAD, cw), F32)],
        compiler_params=_cp(("parallel",)),
    )(y, w, b)


def _conv_bwd_call(y, w, dz, name):
    S, C = y.shape
    cw = 128
    rt = _pick(S, (128,))
    w = jnp.pad(w, ((0, CONV_PAD - CONV_W), (0, 0)))

    def body(y_ref, w_ref, dz_ref, dy_ref, dw_ref, db_ref, ypad_ref, zpad_ref):
        ypad_ref[pl.ds(0, CONV_PAD), :] = jnp.zeros((CONV_PAD, cw), F32)
        ypad_ref[pl.ds(CONV_PAD, S), :] = y_ref[...]
        zpad_ref[pl.ds(0, S), :] = dz_ref[...]
        zpad_ref[pl.ds(S, CONV_PAD), :] = jnp.zeros((CONV_PAD, cw), F32)
        dw_ref[...] = jnp.zeros_like(dw_ref)
        wv = w_ref[...]

        def chunk(ci, dbacc):
            r0 = pl.multiple_of(ci * rt, rt)
            ywin = ypad_ref[pl.ds(r0, rt + CONV_PAD), :]
            zwin = zpad_ref[pl.ds(r0, rt + CONV_PAD), :]
            dzc = zwin[:rt, :]
            acc = jnp.zeros((rt, cw), F32)
            for k in range(CONV_W):
                off_z = (CONV_W - 1) - k
                zs = zwin if off_z == 0 else pltpu.roll(zwin, rt + CONV_PAD - off_z, axis=0)
                acc = acc + wv[k:k + 1, :] * zs[:rt, :]
                off_y = CONV_PAD - (CONV_W - 1) + k
                ys = pltpu.roll(ywin, rt + CONV_PAD - off_y, axis=0)
                dw_ref[k:k + 1, :] += jnp.sum(dzc * ys[:rt, :], axis=0, keepdims=True)
            dy_ref[pl.ds(r0, rt), :] = acc
            return dbacc + jnp.sum(dzc, axis=0, keepdims=True)

        db_ref[...] = lax.fori_loop(0, S // rt, chunk, jnp.zeros((1, cw), F32))

    dy, dw, db = _pcall(
        body, name=name, grid=(C // cw,),
        in_specs=[pl.BlockSpec((S, cw), lambda j: (0, j)), pl.BlockSpec((CONV_PAD, cw), lambda j: (0, j)),
                  pl.BlockSpec((S, cw), lambda j: (0, j))],
        out_specs=[pl.BlockSpec((S, cw), lambda j: (0, j)), pl.BlockSpec((CONV_PAD, cw), lambda j: (0, j)),
                   pl.BlockSpec((1, cw), lambda j: (0, j))],
        out_shape=[jax.ShapeDtypeStruct((S, C), F32), jax.ShapeDtypeStruct((CONV_PAD, C), F32),
                   jax.ShapeDtypeStruct((1, C), F32)],
        scratch_shapes=[pltpu.VMEM((S + CONV_PAD, cw), F32), pltpu.VMEM((S + CONV_PAD, cw), F32)],
        compiler_params=_cp(("parallel",)),
    )(y, w, dz)
    return dy, dw[:CONV_W], db


def _conv_op(name):
    @jax.custom_vjp
    def conv(y, w, b):
        return _conv_fwd_call(y, w, b, name + "_fwd")

    def fwd(y, w, b):
        return conv(y, w, b), (y, w)

    def bwd(res, dz):
        y, w = res
        return _conv_bwd_call(y, w, dz, name + "_bwd")

    conv.defvjp(fwd, bwd)
    return conv


def _attn_tile(S):
    return _pick(S, (512, 256, 128))


def _attn_fwd_call(q, kv, kr, name):
    S = q.shape[0]
    t = _attn_tile(S)

    def body(q_ref, kv_ref, kr_ref, o_ref, lse_ref):
        i = pl.program_id(1)
        qv = q_ref[...]

        def update(jb, carry, diagonal):
            m, l, acc = carry
            off = pl.multiple_of(jb * t, t)
            kc = jnp.concatenate([kv_ref[pl.ds(off, t), pl.ds(0, NOPE)], kr_ref[pl.ds(off, t), :]], axis=-1)
            vv = kv_ref[pl.ds(off, t), pl.ds(NOPE, VDIM)]
            s = lax.dot_general(qv, kc, (((1,), (1,)), ((), ())), preferred_element_type=F32) * ATTN_SCALE
            if diagonal:
                row = lax.broadcasted_iota(jnp.int32, (t, t), 0)
                col = lax.broadcasted_iota(jnp.int32, (t, t), 1)
                s = jnp.where(col <= row, s, -jnp.inf)
            mn = jnp.maximum(m, jnp.max(s, axis=-1, keepdims=True))
            p = jnp.exp(s - mn)
            al = jnp.exp(m - mn)
            l = al * l + jnp.sum(p, axis=-1, keepdims=True)
            acc = al * acc + jnp.dot(p.astype(BF16), vv, preferred_element_type=F32)
            return mn, l, acc

        init = (jnp.full((t, 1), -jnp.inf, F32), jnp.zeros((t, 1), F32), jnp.zeros((t, VDIM), F32))
        carry = lax.fori_loop(0, i, lambda jb, cr: update(jb, cr, False), init)
        m, l, acc = update(i, carry, True)
        o_ref[...] = (acc / l).astype(o_ref.dtype)
        lse_ref[...] = m + jnp.log(l)

    return _pcall(
        body, name=name, grid=(N_HEADS, S // t),
        in_specs=[pl.BlockSpec((t, HEAD_W), lambda h, i: (i, h)), pl.BlockSpec((S, HEAD_W), lambda h, i: (0, h)),
                  pl.BlockSpec((S, 128), lambda h, i: (0, 0))],
        out_specs=[pl.BlockSpec((t, VDIM), lambda h, i: (i, h)), pl.BlockSpec((None, t, 1), lambda h, i: (h, i, 0))],
        out_shape=[jax.ShapeDtypeStruct((S, N_HEADS * VDIM), BF16), jax.ShapeDtypeStruct((N_HEADS, S, 1), F32)],
        compiler_params=_cp(("parallel", "parallel")),
    )(q, kv, kr)


def _attn_dd_call(o, do, name):
    S = o.shape[0]
    t = _attn_tile(S)

    def body(o_ref, do_ref, dd_ref):
        dd_ref[...] = jnp.sum(do_ref[...].astype(F32) * o_ref[...].astype(F32), axis=-1, keepdims=True)

    return _pcall(
        body, name=name, grid=(N_HEADS, S // t),
        in_specs=[pl.BlockSpec((t, VDIM), lambda h, i: (i, h)), pl.BlockSpec((t, VDIM), lambda h, i: (i, h))],
        out_specs=pl.BlockSpec((None, t, 1), lambda h, i: (h, i, 0)),
        out_shape=jax.ShapeDtypeStruct((N_HEADS, S, 1), F32), compiler_params=_cp(("parallel", "parallel")),
    )(o, do)


def _attn_bwd_call(q, kv, kr, do, lse, dd, name):
    S = q.shape[0]
    t = _attn_tile(S)
    nq = S // t

    def body(q_ref, kv_ref, kr_ref, do_ref, lse_ref, dd_ref, dq_ref, dkv_ref, dkr_ref, dq_acc):
        j = pl.program_id(1)

        @pl.when(j == 0)
        def _():
            dq_acc[...] = jnp.zeros_like(dq_acc)

        kc = jnp.concatenate([kv_ref[:, pl.ds(0, NOPE)], kr_ref[...]], axis=-1)
        vv = kv_ref[:, pl.ds(NOPE, VDIM)]

        def update(ib, carry, diagonal):
            dkc, dv = carry
            off = pl.multiple_of(ib * t, t)
            qv = q_ref[pl.ds(off, t), :]
            dov = do_ref[pl.ds(off, t), :]
            st = lax.dot_general(kc, qv, (((1,), (1,)), ((), ())), preferred_element_type=F32) * ATTN_SCALE
            if diagonal:
                key = lax.broadcasted_iota(jnp.int32, (t, t), 0)
                qry = lax.broadcasted_iota(jnp.int32, (t, t), 1)
                st = jnp.where(key <= qry, st, -jnp.inf)
            pt = jnp.exp(st - lse_ref[pl.ds(ib, 1), :])
            dv = dv + jnp.dot(pt.astype(BF16), dov, preferred_element_type=F32)
            dpt = lax.dot_general(vv, dov, (((1,), (1,)), ((), ())), preferred_element_type=F32)
            dst = (pt * (dpt - dd_ref[pl.ds(ib, 1), :]) * ATTN_SCALE).astype(BF16)
            dkc = dkc + jnp.dot(dst, qv, preferred_element_type=F32)
            dq_acc[pl.ds(off, t), :] += lax.dot_general(dst, kc, (((0,), (0,)), ((), ())), preferred_element_type=F32)
            return dkc, dv

        carry = update(j, (jnp.zeros((t, HEAD_W), F32), jnp.zeros((t, VDIM), F32)), True)
        dkc, dv = lax.fori_loop(j + 1, nq, lambda ib, cr: update(ib, cr, False), carry)
        dkv_ref[:, pl.ds(0, NOPE)] = dkc[:, :NOPE].astype(dkv_ref.dtype)
        dkv_ref[:, pl.ds(NOPE, VDIM)] = dv.astype(dkv_ref.dtype)
        dkr_ref[...] = dkc[:, NOPE:]

        @pl.when(j == nq - 1)
        def _():
            dq_ref[...] = dq_acc[...].astype(dq_ref.dtype)

    dq, dkv, dkr_heads = _pcall(
        body, name=name, grid=(N_HEADS, nq),
        in_specs=[pl.BlockSpec((S, HEAD_W), lambda h, j: (0, h)), pl.BlockSpec((t, HEAD_W), lambda h, j: (j, h)),
                  pl.BlockSpec((t, 128), lambda h, j: (j, 0)), pl.BlockSpec((S, VDIM), lambda h, j: (0, h)),
                  pl.BlockSpec((None, nq, t), lambda h, j: (h, 0, 0)), pl.BlockSpec((None, nq, t), lambda h, j: (h, 0, 0))],
        out_specs=[pl.BlockSpec((S, HEAD_W), lambda h, j: (0, h)), pl.BlockSpec((t, HEAD_W), lambda h, j: (j, h)),
                   pl.BlockSpec((None, t, 128), lambda h, j: (h, j, 0))],
        out_shape=[jax.ShapeDtypeStruct(q.shape, q.dtype), jax.ShapeDtypeStruct(kv.shape, kv.dtype),
                   jax.ShapeDtypeStruct((N_HEADS, S, 128), F32)],
        scratch_shapes=[pltpu.VMEM((S, HEAD_W), F32)],
        compiler_params=_cp(("parallel", "arbitrary")),
    )(q, kv, kr, do, lse.reshape(N_HEADS, nq, t), dd.reshape(N_HEADS, nq, t))

    def sum_body(p_ref, o_ref):
        acc = p_ref[0]
        for h in range(1, N_HEADS):
            acc = acc + p_ref[h]
        o_ref[...] = acc.astype(o_ref.dtype)

    dkr = _pcall(
        sum_body, name=name + "_rope_sum", grid=(nq,),
        in_specs=[pl.BlockSpec((N_HEADS, t, 128), lambda i: (0, i, 0))], out_specs=pl.BlockSpec((t, 128), lambda i: (i, 0)),
        out_shape=jax.ShapeDtypeStruct((S, 128), kr.dtype), compiler_params=_cp(("parallel",)),
    )(dkr_heads)
    return dq, dkv, dkr


def _attn_op(name):
    @jax.custom_vjp
    def attn(q, kv, kr):
        return _attn_fwd_call(q, kv, kr, name + "_fwd")[0]

    def fwd(q, kv, kr):
        o, lse = _attn_fwd_call(q, kv, kr, name + "_fwd")
        return o, (q, kv, kr, o, lse)

    def bwd(res, do):
        q, kv, kr, o, lse = res
        dd = _attn_dd_call(o, do, name + "_dd")
        return _attn_bwd_call(q, kv, kr, do, lse, dd, name + "_bwd")

    attn.defvjp(fwd, bwd)
    return attn


def _loss_call(y, target):
    S, D = y.shape
    t = _pick(S, (256, 128))

    def body(y_ref, t_ref, dy_ref, loss_ref):
        @pl.when(pl.program_id(0) == 0)
        def _():
            loss_ref[...] = jnp.zeros_like(loss_ref)

        e = y_ref[...] - t_ref[...]
        dy_ref[...] = e * (1.0 / D)
        loss_ref[...] += 0.5 * jnp.sum(jnp.mean(e * e, axis=-1, keepdims=True), axis=0, keepdims=True)

    return _pcall(
        body, name="loss_head", grid=(S // t,),
        in_specs=[pl.BlockSpec((t, D), lambda i: (i, 0)), pl.BlockSpec((t, D), lambda i: (i, 0))],
        out_specs=[pl.BlockSpec((t, D), lambda i: (i, 0)), pl.BlockSpec((1, 1), lambda i: (0, 0))],
        out_shape=[jax.ShapeDtypeStruct((S, D), F32), jax.ShapeDtypeStruct((1, 1), F32)],
        compiler_params=_cp(("arbitrary",)),
    )(y, target)


def _adamw_call(w, g, m, v, name):
    shape = w.shape
    C = shape[-1]
    R = int(np.prod(shape[:-1]))
    tr = R
    for cand in (512, 256, 128, 64, 32, 16, 8):
        if R % cand == 0 and cand * C * 4 <= 2 * 1024 * 1024:
            tr = cand
            break
    c1 = 1.0 - ADAM_B1 ** ADAM_STEP
    c2 = 1.0 - ADAM_B2 ** ADAM_STEP

    def body(w_ref, g_ref, m_ref, v_ref, d_ref, mo_ref, vo_ref):
        gv = g_ref[...]
        mn = ADAM_B1 * m_ref[...] + (1.0 - ADAM_B1) * gv
        vn = ADAM_B2 * v_ref[...] + (1.0 - ADAM_B2) * (gv * gv)
        d_ref[...] = -ADAM_LR * ((mn / c1) / (jnp.sqrt(vn / c2) + ADAM_EPS) + ADAM_WD * w_ref[...])
        mo_ref[...] = mn
        vo_ref[...] = vn

    spec = pl.BlockSpec((tr, C), lambda i: (i, 0))
    outs = _pcall(
        body, name=name, grid=(R // tr,), in_specs=[spec] * 4, out_specs=[spec] * 3,
        out_shape=[jax.ShapeDtypeStruct((R, C), F32)] * 3, compiler_params=_cp(("parallel",)),
    )(*[a.reshape(R, C) for a in (w, g, m, v)])
    return [o.reshape(shape) for o in outs]


def _gather_small(x2d, *, reduce, name):
    R, C = x2d.shape

    def body(x_ref, out_ref, *scratch):
        if reduce:
            buf_ref, send_sems, recv_sems, local_sem = scratch
        else:
            buf_ref = out_ref
            send_sems, recv_sems, local_sem = scratch
        x, y, c = lax.axis_index("x"), lax.axis_index("y"), lax.axis_index("c")
        me, sibling = (x, y, c), (x, y, 1 - c)
        chips = [(1 - x, y), (x, 1 - y), (1 - x, 1 - y)]

        def rows(px, py, pc):
            return buf_ref.at[pl.ds((4 * px + 2 * py + pc) * R, R), :]

        def copy(k, block, to, src=None):
            return pltpu.make_async_remote_copy(
                src_ref=rows(*block) if src is None else src, dst_ref=rows(*block),
                send_sem=send_sems.at[k], recv_sem=recv_sems.at[k], device_id=to, device_id_type=MESH)

        mine = pltpu.make_async_copy(x_ref, rows(*me), local_sem)
        mine.start()
        first = [copy(0, me, sibling, src=x_ref)]
        first += [copy(1 + j, me, (*chip, c), src=x_ref) for j, chip in enumerate(chips)]
        for cp in first:
            cp.start()
        passed = [copy(4 + j, (*chip, c), sibling) for j, chip in enumerate(chips)]
        for j, chip in enumerate(chips):
            copy(1 + j, (*chip, c), me).wait_recv()
            passed[j].start()
        copy(0, sibling, me).wait_recv()
        for j, chip in enumerate(chips):
            copy(4 + j, (*chip, 1 - c), me).wait_recv()
        for cp in first + passed:
            cp.wait_send()
        mine.wait()
        if reduce:
            acc = buf_ref[pl.ds(0, R), :]
            for d in range(1, N_DEV):
                acc = acc + buf_ref[pl.ds(d * R, R), :]
            out_ref[...] = acc

    scratch = [pltpu.SemaphoreType.DMA((7,)), pltpu.SemaphoreType.DMA((7,)), pltpu.SemaphoreType.DMA]
    if reduce:
        scratch = [pltpu.VMEM((N_DEV * R, C), F32)] + scratch
    return _pcall(
        body, name=name, out_shape=jax.ShapeDtypeStruct((R if reduce else N_DEV * R, C), F32),
        in_specs=[pl.BlockSpec(memory_space=pltpu.VMEM)], out_specs=pl.BlockSpec(memory_space=pltpu.VMEM),
        scratch_shapes=scratch, compiler_params=pltpu.CompilerParams(vmem_limit_bytes=VMEM_LIMIT),
    )(x2d)


def _pack(arrs):
    flat = jnp.concatenate([a.reshape(-1).astype(F32) for a in arrs])
    n = flat.shape[0]
    unit = 8 * SMALL_COLS
    flat = jnp.pad(flat, (0, (-n) % unit))
    return flat.reshape(-1, SMALL_COLS)


def _unpack(flat, shapes):
    out, o = [], 0
    for s in shapes:
        n = int(np.prod(s))
        out.append(flat[o:o + n].reshape(s))
        o += n
    return out


def _all_gather_small(arrs, name):
    p = _pack(arrs)
    g = _gather_small(p, reduce=False, name=name).reshape(N_DEV, -1)
    out, o = [], 0
    for a in arrs:
        n = int(np.prod(a.shape))
        out.append(g[:, o:o + n].reshape((N_DEV,) + a.shape))
        o += n
    return out


def _all_reduce_small(arrs, name):
    p = _pack(arrs)
    return _unpack(_gather_small(p, reduce=True, name=name).reshape(-1), [a.shape for a in arrs])


def _half_rows(shard_shape):
    return shard_shape[0] // 2


def _half_slot(ref, kind, rh, cols, half, slot):
    if kind == "col":
        return ref.at[pl.ds(half * rh, rh), pl.ds(slot * cols, cols)]
    return ref.at[pl.ds((slot * 2 + half) * rh, rh), :]


def _full_shape(kind, shard_shape):
    r, c = shard_shape
    return (r, c * N_CHIPS) if kind == "col" else (r * N_CHIPS, c)


def _mesh_pos():
    x, y, c = lax.axis_index("x"), lax.axis_index("y"), lax.axis_index("c")
    return x, y, c


def _handshake_chip_peers(x, y, c):
    barrier = pltpu.get_barrier_semaphore()
    for peer in [(x, y, 1 - c)] + [(x ^ fx, y ^ fy, c) for fx, fy in CHIP_RELS]:
        pl.semaphore_signal(barrier, inc=1, device_id=peer, device_id_type=MESH)
    pl.semaphore_wait(barrier, 1 + len(CHIP_RELS))


def _gather_big(shards, kinds, name):
    n = len(shards)

    def body(*refs):
        srcs, outs = refs[:n], refs[n:2 * n]
        ici_send, ici_recv, d2d_send, d2d_recv, own_send, own_recv = refs[2 * n:]
        x, y, c = _mesh_pos()
        _handshake_chip_peers(x, y, c)
        my = 2 * x + y
        geo = [(_half_rows(s.shape), s.shape[1]) for s in shards]

        def region(a, half, slot):
            return _half_slot(outs[a], kinds[a], geo[a][0], geo[a][1], half, slot)

        def slot_of(a, slot):
            rh, cols = geo[a]
            if kinds[a] == "col":
                return outs[a].at[:, pl.ds(slot * cols, cols)]
            return outs[a].at[pl.ds(slot * 2 * rh, 2 * rh), :]

        sends = []
        for a in range(n):
            cp = pltpu.make_async_remote_copy(
                src_ref=srcs[a], dst_ref=slot_of(a, my), send_sem=own_send.at[a], recv_sem=own_recv.at[a],
                device_id=(x, y, 1 - c), device_id_type=MESH)
            cp.start()
            sends.append(cp)
        for a in range(n):
            rh = geo[a][0]
            for r, (fx, fy) in enumerate(CHIP_RELS):
                cp = pltpu.make_async_remote_copy(
                    src_ref=srcs[a].at[pl.ds(c * rh, rh), :], dst_ref=region(a, c, my),
                    send_sem=ici_send.at[3 * a + r], recv_sem=ici_recv.at[3 * a + r],
                    device_id=(x ^ fx, y ^ fy, c), device_id_type=MESH)
                cp.start()
                sends.append(cp)
        passed = []
        for a in range(n):
            for r, (fx, fy) in enumerate(CHIP_RELS):
                frm = 2 * (x ^ fx) + (y ^ fy)
                landed = region(a, c, frm)
                pltpu.make_async_remote_copy(
                    src_ref=landed, dst_ref=landed, send_sem=ici_send.at[3 * a + r], recv_sem=ici_recv.at[3 * a + r],
                    device_id=(x, y, c), device_id_type=MESH).wait_recv()
                cp = pltpu.make_async_remote_copy(
                    src_ref=landed, dst_ref=landed, send_sem=d2d_send.at[3 * a + r], recv_sem=d2d_recv.at[3 * a + r],
                    device_id=(x, y, 1 - c), device_id_type=MESH)
                cp.start()
                passed.append(cp)
        for a in range(n):
            for r, (fx, fy) in enumerate(CHIP_RELS):
                frm = 2 * (x ^ fx) + (y ^ fy)
                other = region(a, 1 - c, frm)
                pltpu.make_async_remote_copy(
                    src_ref=other, dst_ref=other, send_sem=d2d_send.at[3 * a + r], recv_sem=d2d_recv.at[3 * a + r],
                    device_id=(x, y, c), device_id_type=MESH).wait_recv()
        for a in range(n):
            pltpu.make_async_remote_copy(
                src_ref=srcs[a], dst_ref=slot_of(a, my), send_sem=own_send.at[a], recv_sem=own_recv.at[a],
                device_id=(x, y, c), device_id_type=MESH).wait_recv()
        for cp in sends + passed:
            cp.wait_send()

    return pl.kernel(
        body, name=name,
        out_type=[jax.ShapeDtypeStruct(_full_shape(k, s.shape), s.dtype) for s, k in zip(shards, kinds)],
        mesh=plsc.ScalarSubcoreMesh(axis_name="sequencer", num_cores=1),
        scratch_types=[pltpu.SemaphoreType.DMA((3 * n,))] * 4 + [pltpu.SemaphoreType.DMA((n,))] * 2,
        compiler_params=pltpu.CompilerParams(collective_id=GATHER_COLLECTIVE_ID),
    )(*shards)


def _swap_halves(fulls, kinds, shard_shapes):
    n = len(fulls)
    HBM = pl.BlockSpec(memory_space=pl.ANY)
    geo = [(_half_rows(s), s[1]) for s in shard_shapes]

    def body(*refs):
        srcs, outs = refs[:n], refs[n:2 * n]
        send_sems, recv_sems = refs[2 * n:]
        x, y, c = _mesh_pos()
        cps = []
        for a in range(n):
            for s in range(N_CHIPS):
                cp = pltpu.make_async_remote_copy(
                    src_ref=_half_slot(srcs[a], kinds[a], geo[a][0], geo[a][1], 1 - c, s), dst_ref=outs[a].at[s],
                    send_sem=send_sems.at[N_CHIPS * a + s], recv_sem=recv_sems.at[N_CHIPS * a + s],
                    device_id=(x, y, 1 - c), device_id_type=MESH)
                cp.start()
                cps.append(cp)
        for cp in cps:
            cp.wait()

    return _pcall(
        body, name="grad_swap_halves",
        out_shape=[jax.ShapeDtypeStruct((N_CHIPS,) + g, f.dtype) for g, f in zip(geo, fulls)],
        in_specs=[HBM] * n, out_specs=[HBM] * n,
        scratch_shapes=[pltpu.SemaphoreType.DMA((N_CHIPS * n,))] * 2,
        compiler_params=pltpu.CompilerParams(vmem_limit_bytes=VMEM_LIMIT),
    )(*fulls)


def _add_own_half(full, recv, kind, shard_shape, cidx, name):
    rh, cols = _half_rows(shard_shape), shard_shape[1]
    tr = _pick(rh, (256, 128, 64, 32, 16))
    nb = rh // tr

    def body(c_ref, f_ref, r_ref, o_ref):
        o_ref[...] = (f_ref[...].astype(F32) + r_ref[...].astype(F32)).astype(o_ref.dtype)

    if kind == "col":
        f_spec = pl.BlockSpec((tr, cols), lambda s, i, c_ref: (c_ref[0] * nb + i, s))
    else:
        f_spec = pl.BlockSpec((tr, cols), lambda s, i, c_ref: ((s * 2 + c_ref[0]) * nb + i, 0))
    blk = pl.BlockSpec((None, tr, cols), lambda s, i, c_ref: (s, i, 0))
    return _pcall(
        body, name=name,
        grid_spec=pltpu.PrefetchScalarGridSpec(num_scalar_prefetch=1, grid=(N_CHIPS, nb), in_specs=[f_spec, blk], out_specs=blk),
        out_shape=jax.ShapeDtypeStruct((N_CHIPS, rh, cols), BF16),
        compiler_params=_cp(("arbitrary", "arbitrary")),
    )(cidx, full, recv)


def _scatter_partials(partials):
    n = len(partials)
    HBM = pl.BlockSpec(memory_space=pl.ANY)

    def body(*refs):
        srcs, outs = refs[:n], refs[n:2 * n]
        send_sems, recv_sems = refs[2 * n:]
        x, y, c = _mesh_pos()
        my = 2 * x + y
        cps = []
        for a in range(n):
            for r, (fx, fy) in enumerate(CHIP_RELS):
                to = 2 * (x ^ fx) + (y ^ fy)
                cp = pltpu.make_async_remote_copy(
                    src_ref=srcs[a].at[to], dst_ref=outs[a].at[my],
                    send_sem=send_sems.at[3 * a + r], recv_sem=recv_sems.at[3 * a + r],
                    device_id=(x ^ fx, y ^ fy, c), device_id_type=MESH)
                cp.start()
                cps.append(cp)
        for a in range(n):
            for r, (fx, fy) in enumerate(CHIP_RELS):
                frm = 2 * (x ^ fx) + (y ^ fy)
                pltpu.make_async_remote_copy(
                    src_ref=outs[a].at[frm], dst_ref=outs[a].at[frm],
                    send_sem=send_sems.at[3 * a + r], recv_sem=recv_sems.at[3 * a + r],
                    device_id=(x, y, c), device_id_type=MESH).wait_recv()
        for cp in cps:
            cp.wait_send()

    return _pcall(
        body, name="grad_scatter_partials",
        out_shape=[jax.ShapeDtypeStruct(p.shape, p.dtype) for p in partials],
        in_specs=[HBM] * n, out_specs=[HBM] * n,
        scratch_shapes=[pltpu.SemaphoreType.DMA((3 * n,))] * 2,
        compiler_params=pltpu.CompilerParams(vmem_limit_bytes=VMEM_LIMIT),
    )(*partials)


def _sum_chips_into(landed, partial, buf, layer, n_layers, pos, name):
    _, rh, cols = landed.shape
    tr = _pick(rh, (256, 128, 64, 32, 16))
    nb = rh // tr

    def body(pos_ref, l_ref, own_ref, *rest):
        o_ref = rest[-1]
        my = pos_ref[1]
        own = own_ref[...].astype(F32)
        acc = jnp.where(my == 0, own, l_ref[0].astype(F32))
        for j in range(1, N_CHIPS):
            acc = acc + jnp.where(my == j, own, l_ref[j].astype(F32))
        o_ref[...] = acc

    in_specs = [pl.BlockSpec((N_CHIPS, tr, cols), lambda i, pos_ref: (0, i, 0)),
                pl.BlockSpec((None, tr, cols), lambda i, pos_ref: (pos_ref[1], i, 0))]
    args = [pos, landed, partial]
    aliases = {}
    if buf is not None:
        in_specs.append(pl.BlockSpec(memory_space=pl.ANY))
        args.append(buf)
        aliases = {3: 0}
    return _pcall(
        body, name=name,
        grid_spec=pltpu.PrefetchScalarGridSpec(
            num_scalar_prefetch=1, grid=(nb,), in_specs=in_specs,
            out_specs=pl.BlockSpec((None, tr, cols), lambda i, pos_ref: (layer, pos_ref[0] * nb + i, 0))),
        out_shape=jax.ShapeDtypeStruct((n_layers, 2 * rh, cols), F32), input_output_aliases=aliases,
        compiler_params=_cp(("arbitrary",)),
    )(*args)


def _exchange_final(bufs):
    n = len(bufs)
    HBM = pl.BlockSpec(memory_space=pl.ANY)
    n_layers = [b.shape[0] for b in bufs]
    base = np.concatenate([[0], np.cumsum(n_layers)])

    def body(*refs):
        outs = refs[n:2 * n]
        send_sems, recv_sems = refs[2 * n:]
        x, y, c = _mesh_pos()
        cps = []
        for w in range(n):
            rh = bufs[w].shape[1] // 2
            for l in range(n_layers[w]):
                k = int(base[w]) + l
                mine = outs[w].at[l, pl.ds(c * rh, rh), :]
                cp = pltpu.make_async_remote_copy(
                    src_ref=mine, dst_ref=mine, send_sem=send_sems.at[k], recv_sem=recv_sems.at[k],
                    device_id=(x, y, 1 - c), device_id_type=MESH)
                cp.start()
                cps.append(cp)
        for w in range(n):
            rh = bufs[w].shape[1] // 2
            for l in range(n_layers[w]):
                k = int(base[w]) + l
                other = outs[w].at[l, pl.ds((1 - c) * rh, rh), :]
                pltpu.make_async_remote_copy(
                    src_ref=other, dst_ref=other, send_sem=send_sems.at[k], recv_sem=recv_sems.at[k],
                    device_id=(x, y, c), device_id_type=MESH).wait_recv()
        for cp in cps:
            cp.wait_send()

    return _pcall(
        body, name="grad_exchange_final", out_shape=[jax.ShapeDtypeStruct(b.shape, b.dtype) for b in bufs],
        in_specs=[HBM] * n, out_specs=[HBM] * n, input_output_aliases={i: i for i in range(n)},
        scratch_shapes=[pltpu.SemaphoreType.DMA((int(base[-1]),))] * 2,
        compiler_params=pltpu.CompilerParams(vmem_limit_bytes=VMEM_LIMIT),
    )(*bufs)


def _reduce_scatter_big(fulls, kinds, shard_shapes, groups, pos):
    recv = _swap_halves(fulls, kinds, shard_shapes)
    partials = [_add_own_half(f, r, k, s, pos, f"grad_add_sibling_{a}")
                for a, (f, r, k, s) in enumerate(zip(fulls, recv, kinds, shard_shapes))]
    landed = _scatter_partials(partials)
    bufs = []
    for idxs in groups:
        buf = None
        for l, a in enumerate(idxs):
            buf = _sum_chips_into(landed[a], partials[a], buf, l, len(idxs), pos, f"grad_sum_chips_{a}")
        bufs.append(buf)
    return _exchange_final(bufs)


def _swap_halves_last(z):
    h = z.shape[-1] // 2
    return jnp.concatenate([z[..., h:], z[..., :h]], axis=-1)


def _ext_uq(w):
    lead = w.shape[:-1]
    wh = w.reshape(lead + (-1, QK_DIM))
    rope = wh[..., NOPE:]
    return jnp.concatenate([wh, _swap_halves_last(rope)], axis=-1).reshape(lead + (-1,))


def _fold_uq(d):
    lead = d.shape[:-1]
    dh = d.reshape(lead + (-1, HEAD_W))
    rope = dh[..., NOPE:QK_DIM] + _swap_halves_last(dh[..., QK_DIM:])
    return jnp.concatenate([dh[..., :NOPE], rope], axis=-1).reshape(lead + (-1,))


def _ext_win(w):
    base, kr = w[..., :-ROPE], w[..., -ROPE:]
    ks = _swap_halves_last(kr)
    return jnp.concatenate([base, kr, ks, ks, kr], axis=-1)


def _fold_win(d):
    n = d.shape[-1] - 4 * ROPE
    a, b, c2, e = [d[..., n + i * ROPE:n + (i + 1) * ROPE] for i in range(4)]
    return jnp.concatenate([d[..., :n], a + e + _swap_halves_last(b + c2)], axis=-1)


def _forward(x, wts, tabs):
    S, D = x.shape
    depth = len(wts["mlp_w1"])
    tile = 256
    tab_q, tab_ka, tab_kb = tabs
    first = _rowop("norm_mod_0", _f_norm_mod, n_x=1, n_nd=0, n_p=3, x_w=[D], nd_w=[], nd_shared=[], p_per_group=[False] * 3,
                   out_w=[D], out_dtypes=[BF16], tile=tile)
    xcur = x
    mix = gate = None
    for l in range(depth):
        sh1, sc1, g1, sh2, sc2, g2 = [wts["mod"][l][i].reshape(1, 1, D) for i in range(6)]
        n1 = wts["norm1_g"][l].reshape(1, 1, D)
        n2 = wts["norm2_g"][l].reshape(1, 1, D)
        if l == 0:
            (h,) = first((xcur,), (), (n1, sh1, sc1))
        else:
            op = _rowop(f"resid_norm_mod_a{l}", _f_resid_norm_mod, n_x=2, n_nd=0, n_p=4, x_w=[D, D], nd_w=[], nd_shared=[],
                        p_per_group=[False] * 4, out_w=[D, D], out_dtypes=[F32, BF16], tile=tile)
            xcur, h = op((xcur, mix), (), (gate, n1, sh1, sc1))
        if l % 2 == 0:
            mix = _even_mixer(h, wts, l // 2, tile)
        else:
            mix = _mla_mixer(h, wts, l // 2, tile, tab_q, tab_ka, tab_kb)
        op = _rowop(f"resid_norm_mod_b{l}", _f_resid_norm_mod, n_x=2, n_nd=0, n_p=4, x_w=[D, D], nd_w=[], nd_shared=[],
                    p_per_group=[False] * 4, out_w=[D, D], out_dtypes=[F32, BF16], tile=tile)
        xcur, h = op((xcur, mix), (), (g1, n2, sh2, sc2))
        mix = _mlp(f"mlp_{l}")(h, wts["mlp_w1"][l], wts["mlp_w2"][l])
        gate = g2
    (y,) = _rowop("resid_last", _f_resid, n_x=2, n_nd=0, n_p=1, x_w=[D, D], nd_w=[], nd_shared=[], p_per_group=[False],
                  out_w=[D], out_dtypes=[F32], tile=tile)((xcur, mix), (), (gate,))
    return y


def _even_mixer(h, wts, e, tile):
    proj = _linear(f"ab_in_{e}")(h, wts["ab_w_in"][e])
    da = proj.shape[1] // 4
    u, v, a, g = [proj[:, i * da:(i + 1) * da] for i in range(4)]
    ng = da // GROUP
    sgu = _rowop(f"sgu_{e}", _f_sgu, n_x=2, n_nd=0, n_p=3, x_w=[GROUP, GROUP], nd_w=[], nd_shared=[], p_per_group=[True] * 3,
                 out_w=[GROUP], out_dtypes=[BF16], tile=CHUNK, groups=ng)
    bexp = jnp.broadcast_to(wts["sgu_b"][e][:, :, None], (ng, CHUNK, GROUP))
    (out_a,) = sgu((u, v), (), (wts["sgu_norm_g"][e].reshape(ng, 1, GROUP), wts["sgu_w"][e], bexp))
    (yglu,) = _rowop(f"glu_{e}", _f_glu, n_x=2, n_nd=0, n_p=0, x_w=[da, da], nd_w=[], nd_shared=[], p_per_group=[],
                     out_w=[da], out_dtypes=[F32], tile=tile)((a, g), (), ())
    z = _conv_op(f"conv_{e}")(yglu, wts["conv_w"][e], wts["conv_b"][e].reshape(1, da))
    (out_b,) = _rowop(f"ln_silu_{e}", _f_ln_silu, n_x=1, n_nd=0, n_p=2, x_w=[da], nd_w=[], nd_shared=[], p_per_group=[False] * 2,
                      out_w=[da], out_dtypes=[BF16], tile=tile)(
        (z,), (), (wts["conv_ln_g"][e].reshape(1, 1, da), wts["conv_ln_b"][e].reshape(1, 1, da)))
    return _linear(f"ab_out_{e}")(jnp.concatenate([out_a, out_b], axis=-1), wts["ab_w_out"][e])


def _mla_mixer(h, wts, o, tile, tab_q, tab_ka, tab_kb):
    proj = _linear(f"mla_in_{o}")(h, wts["mla_w_in"][o])
    rank = (proj.shape[1] - 4 * ROPE) // 2
    c_q, c_kv = proj[:, :rank], proj[:, rank:2 * rank]
    kr_a, kr_b = proj[:, 2 * rank:2 * rank + 2 * ROPE], proj[:, 2 * rank + 2 * ROPE:]

    def rms(name, xx, gg):
        return _rowop(name, _f_rms, n_x=1, n_nd=0, n_p=1, x_w=[rank], nd_w=[], nd_shared=[], p_per_group=[False],
                      out_w=[rank], out_dtypes=[BF16], tile=tile)((xx,), (), (gg.reshape(1, 1, rank),))[0]

    q_raw = _linear(f"mla_uq_{o}")(rms(f"rms_q_{o}", c_q, wts["mla_q_norm_g"][o]), wts["mla_w_uq"][o])
    kv_raw = _linear(f"mla_ukv_{o}")(rms(f"rms_kv_{o}", c_kv, wts["mla_kv_norm_g"][o]), wts["mla_w_ukv"][o])
    gq, gk = wts["mla_q_head_g"][o], wts["mla_k_head_g"][o]
    gk_rope = gk[NOPE:]
    gq_ext = jnp.concatenate([gq, _swap_halves_last(gq[NOPE:])]).reshape(1, 1, HEAD_W)
    gk_ext = jnp.concatenate([gk[:NOPE], jnp.ones((HEAD_W - NOPE,), F32)]).reshape(1, 1, HEAD_W)
    gk_a = jnp.concatenate([gk_rope, _swap_halves_last(gk_rope)]).reshape(1, 1, 2 * ROPE)
    gk_b = jnp.concatenate([_swap_halves_last(gk_rope), gk_rope]).reshape(1, 1, 2 * ROPE)
    head_tile = 4 * tile
    (q,) = _rowop(f"q_head_{o}", _f_qhead, n_x=1, n_nd=1, n_p=1, x_w=[HEAD_W], nd_w=[HEAD_W], nd_shared=[True], p_per_group=[False],
                  out_w=[HEAD_W], out_dtypes=[BF16], tile=head_tile, groups=N_HEADS)((q_raw,), (tab_q,), (gq_ext,))
    (kv,) = _rowop(f"kv_head_{o}", _f_kvhead, n_x=1, n_nd=0, n_p=1, x_w=[HEAD_W], nd_w=[], nd_shared=[], p_per_group=[False],
                   out_w=[HEAD_W], out_dtypes=[BF16], tile=head_tile, groups=N_HEADS)((kv_raw,), (), (gk_ext,))
    (kr,) = _rowop(f"k_rope_{o}", _f_krope, n_x=2, n_nd=2, n_p=2, x_w=[2 * ROPE] * 2, nd_w=[2 * ROPE] * 2, nd_shared=[True] * 2,
                   p_per_group=[False] * 2, out_w=[2 * ROPE], out_dtypes=[BF16], tile=tile)((kr_a, kr_b), (tab_ka, tab_kb), (gk_a, gk_b))
    att = _attn_op(f"attn_{o}")(q, kv, kr)
    return _linear(f"mla_out_{o}")(att, wts["mla_w_out"][o])


def _rope_tabs(S):
    pos = jnp.arange(S, dtype=F32)
    inv = ROPE_THETA ** (-jnp.arange(0, ROPE, 2, dtype=F32) / ROPE)
    ang = pos[:, None] * inv[None, :]
    cos, sin = jnp.cos(ang), jnp.sin(ang)
    cc = jnp.concatenate([cos, cos], axis=-1)
    sg = jnp.concatenate([-sin, sin], axis=-1)
    tab_q = jnp.concatenate([jnp.ones((S, NOPE), F32), cc, sg], axis=-1)
    return tab_q, jnp.concatenate([cc, sg], axis=-1), jnp.concatenate([sg, cc], axis=-1)


def kernel(x, c, norm1_g, norm2_g, ada_w, ada_b, mlp_w1, mlp_w2, ab_w_in, sgu_norm_g, sgu_w, sgu_b, conv_w, conv_b, conv_ln_g, conv_ln_b, ab_w_out, mla_w_in, mla_q_norm_g, mla_kv_norm_g, mla_w_uq, mla_w_ukv, mla_q_head_g, mla_k_head_g, mla_w_out, loss_target, m_norm1_g, m_norm2_g, m_ada_w, m_ada_b, m_mlp_w1, m_mlp_w2, m_ab_w_in, m_sgu_norm_g, m_sgu_w, m_sgu_b, m_conv_w, m_conv_b, m_conv_ln_g, m_conv_ln_b, m_ab_w_out, m_mla_w_in, m_mla_q_norm_g, m_mla_kv_norm_g, m_mla_w_uq, m_mla_w_ukv, m_mla_q_head_g, m_mla_k_head_g, m_mla_w_out, v_norm1_g, v_norm2_g, v_ada_w, v_ada_b, v_mlp_w1, v_mlp_w2, v_ab_w_in, v_sgu_norm_g, v_sgu_w, v_sgu_b, v_conv_w, v_conv_b, v_conv_ln_g, v_conv_ln_b, v_ab_w_out, v_mla_w_in, v_mla_q_norm_g, v_mla_kv_norm_g, v_mla_w_uq, v_mla_w_ukv, v_mla_q_head_g, v_mla_k_head_g, v_mla_w_out):
    names = ["norm1_g", "norm2_g", "ada_w", "ada_b", "mlp_w1", "mlp_w2", "ab_w_in", "sgu_norm_g", "sgu_w", "sgu_b", "conv_w",
             "conv_b", "conv_ln_g", "conv_ln_b", "ab_w_out", "mla_w_in", "mla_q_norm_g", "mla_kv_norm_g", "mla_w_uq", "mla_w_ukv",
             "mla_q_head_g", "mla_k_head_g", "mla_w_out"]
    W = dict(zip(names, [norm1_g, norm2_g, ada_w, ada_b, mlp_w1, mlp_w2, ab_w_in, sgu_norm_g, sgu_w, sgu_b, conv_w, conv_b, conv_ln_g,
                         conv_ln_b, ab_w_out, mla_w_in, mla_q_norm_g, mla_kv_norm_g, mla_w_uq, mla_w_ukv, mla_q_head_g, mla_k_head_g,
                         mla_w_out]))
    M = dict(zip(names, [m_norm1_g, m_norm2_g, m_ada_w, m_ada_b, m_mlp_w1, m_mlp_w2, m_ab_w_in, m_sgu_norm_g, m_sgu_w, m_sgu_b, m_conv_w,
                         m_conv_b, m_conv_ln_g, m_conv_ln_b, m_ab_w_out, m_mla_w_in, m_mla_q_norm_g, m_mla_kv_norm_g, m_mla_w_uq,
                         m_mla_w_ukv, m_mla_q_head_g, m_mla_k_head_g, m_mla_w_out]))
    V = dict(zip(names, [v_norm1_g, v_norm2_g, v_ada_w, v_ada_b, v_mlp_w1, v_mlp_w2, v_ab_w_in, v_sgu_norm_g, v_sgu_w, v_sgu_b, v_conv_w,
                         v_conv_b, v_conv_ln_g, v_conv_ln_b, v_ab_w_out, v_mla_w_in, v_mla_q_norm_g, v_mla_kv_norm_g, v_mla_w_uq,
                         v_mla_w_ukv, v_mla_q_head_g, v_mla_k_head_g, v_mla_w_out]))
    xi, yi, ci = lax.axis_index("x"), lax.axis_index("y"), lax.axis_index("c")
    chip = 2 * xi + yi
    dev = 2 * chip + ci
    pos = jnp.stack([ci, chip]).astype(jnp.int32)
    S, D = x.shape[1], x.shape[2]
    depth = norm1_g.shape[0]

    c_all, conv_w_all, qn_all, kvn_all = _all_gather_small([c, conv_w, mla_q_norm_g, mla_kv_norm_g], "gather_small_inputs")
    c_all = c_all.reshape(N_DEV, D)
    by_chip = lambda a: jnp.concatenate([a[2 * j] for j in range(N_CHIPS)], axis=-1)
    conv_w_full, qn_full, kvn_full = by_chip(conv_w_all), by_chip(qn_all), by_chip(kvn_all)

    (c_act,) = _rowop("silu_c", _f_silu, n_x=1, n_nd=0, n_p=0, x_w=[D], nd_w=[], nd_shared=[], p_per_group=[], out_w=[D],
                      out_dtypes=[F32], tile=N_DEV)((c_all,), (), ())
    c_act_pad = jnp.pad(c_act, ((0, 128 - N_DEV), (0, 0)))
    mod_cols = jnp.stack([_matmul(c_act_pad, ada_w[l], name=f"ada_fwd_{l}")[:N_DEV] for l in range(depth)])
    (mod_all,) = _all_gather_small([mod_cols], "gather_mod")
    mod_mine = jnp.concatenate([lax.dynamic_index_in_dim(mod_all[2 * j], dev, axis=1, keepdims=False) for j in range(N_CHIPS)], axis=-1)
    mod_mine = (mod_mine + ada_b).reshape(depth, 6, D)

    big = {"mlp_w1": "col", "mlp_w2": "row", "ab_w_in": "col", "ab_w_out": "row", "mla_w_in": "row", "mla_w_uq": "col",
           "mla_w_ukv": "col", "mla_w_out": "row"}
    src = dict(W)
    src["mla_w_in"] = _ext_win(mla_w_in)
    src["mla_w_uq"] = _ext_uq(mla_w_uq)
    shards, kinds, owner = [], [], []
    for nme, kind in big.items():
        for l in range(src[nme].shape[0]):
            shards.append(src[nme][l].astype(BF16))
            kinds.append(kind)
            owner.append((nme, l))
    def model_layer(nme, l):
        return l if nme.startswith("mlp") else (2 * l if nme.startswith("ab") else 2 * l + 1)

    fulls = [None] * len(shards)
    for ml in range(depth):
        idx = [a for a, (nme, l) in enumerate(owner) if model_layer(nme, l) == ml]
        got = _gather_big([shards[a] for a in idx], [kinds[a] for a in idx], f"gather_weights_{ml}")
        for a, f in zip(idx, got):
            fulls[a] = f
    wts = {nme: [None] * src[nme].shape[0] for nme in big}
    for (nme, l), f in zip(owner, fulls):
        wts[nme][l] = f
    wts.update(norm1_g=norm1_g, norm2_g=norm2_g, mod=mod_mine, sgu_norm_g=sgu_norm_g, sgu_w=sgu_w, sgu_b=sgu_b, conv_w=conv_w_full,
               conv_b=conv_b, conv_ln_g=conv_ln_g, conv_ln_b=conv_ln_b, mla_q_norm_g=qn_full, mla_kv_norm_g=kvn_full,
               mla_q_head_g=mla_q_head_g, mla_k_head_g=mla_k_head_g)

    tabs = _rope_tabs(S)
    y, vjp = jax.vjp(lambda xx, ww: _forward(xx, ww, tabs), x[0], wts)
    dy, loss_mine = _loss_call(y, loss_target[0])
    dx, dw = vjp(dy)
    loss = lax.psum(loss_mine[0, 0], ("x", "y", "c"))

    dmod = dw["mod"].reshape(depth, 6 * D)
    small_names = ["norm1_g", "norm2_g", "sgu_norm_g", "sgu_w", "sgu_b", "conv_w", "conv_b", "conv_ln_g", "conv_ln_b", "mla_q_norm_g",
                   "mla_kv_norm_g", "mla_q_head_g", "mla_k_head_g"]
    red = _all_reduce_small([dmod] + [dw[nme] for nme in small_names], "reduce_small_grads")
    G = dict(zip(["ada_b"] + small_names, red))
    own_cols = lambda a: lax.dynamic_slice_in_dim(a, chip * (a.shape[-1] // N_CHIPS), a.shape[-1] // N_CHIPS, axis=-1)
    for nme in ("conv_w", "mla_q_norm_g", "mla_kv_norm_g"):
        G[nme] = own_cols(G[nme])

    (dmod_all,) = _all_gather_small([dmod], "gather_dmod")
    dmod_cols = own_cols(dmod_all)
    dmod_pad = jnp.pad(dmod_cols, ((0, 128 - N_DEV), (0, 0), (0, 0)))
    G["ada_w"] = jnp.stack([_matmul(c_act_pad, dmod_pad[:, l], ta=True, name=f"ada_wgrad_{l}") for l in range(depth)])

    gfull = [dw[nme][l] for nme, l in owner]
    groups, a0 = [], 0
    for nme in big:
        groups.append(list(range(a0, a0 + src[nme].shape[0])))
        a0 += src[nme].shape[0]
    gsh = _reduce_scatter_big(gfull, kinds, [s.shape for s in shards], groups, pos)
    for nme, g in zip(big, gsh):
        G[nme] = g
    G["mla_w_in"] = _fold_win(G["mla_w_in"])
    G["mla_w_uq"] = _fold_uq(G["mla_w_uq"])

    deltas, new_m, new_v = [], [], []
    for nme in names:
        d, mn, vn = _adamw_call(W[nme], G[nme], M[nme], V[nme], f"adamw_{nme}")
        deltas.append(d)
        new_m.append(mn)
        new_v.append(vn)
    return (loss, dx[None], *[G[nme] for nme in names], *deltas, *new_m, *new_v)
```

```python
import functools
import numpy as np
import jax
import jax.numpy as jnp
from jax import lax
from jax.experimental import pallas as pl
from jax.experimental.pallas import tpu as pltpu
from jax.experimental.pallas import tpu_sc as plsc

F32 = jnp.float32
BF16 = jnp.bfloat16
MESH = pl.DeviceIdType.MESH

EPS = 1e-6
N_HEADS = 16
NOPE = 128
ROPE = 64
VDIM = 128
QK_DIM = NOPE + ROPE
HEAD_W = 256
CHUNK = 128
GROUP = 128
CONV_W = 31
CONV_PAD = 32
ROPE_THETA = 10000.0
ATTN_SCALE = QK_DIM ** -0.5
ADAM_LR, ADAM_B1, ADAM_B2, ADAM_EPS, ADAM_WD, ADAM_STEP = 0.001, 0.9, 0.999, 1e-08, 0.01, 10
N_CHIPS = 4
N_DEV = 8
VMEM_LIMIT = 56 * 1024 * 1024
SMALL_COLS = 1024
CHIP_RELS = ((1, 0), (0, 1), (1, 1))
GATHER_COLLECTIVE_ID = 0
SWAP_COLLECTIVE_ID = 1
SCATTER_COLLECTIVE_ID = 2


def _pcall(body, **kw):
    return pl.pallas_call(body, **kw)


def _cp(sem=None, **kw):
    return pltpu.CompilerParams(dimension_semantics=sem, vmem_limit_bytes=VMEM_LIMIT, **kw)


def _pick(n, cands):
    for c in cands:
        if n % c == 0:
            return c
    return n


def _matmul(a, b, *, ta=False, tb=False, out_dtype=F32, name, extra=(), epilogue=None, out_dtypes=None):
    if ta:
        K, M = a.shape
    else:
        M, K = a.shape
    if tb:
        N, K2 = b.shape
    else:
        K2, N = b.shape
    assert K == K2, (a.shape, b.shape, ta, tb)
    tm = _pick(M, (1024, 512, 256, 128))
    tn = _pick(N, (1024, 512, 256, 128))
    tk = _pick(K, (2048, 1024, 512, 256, 128))
    nk = K // tk
    dn = (((0 if ta else 1,), (1 if tb else 0,)), ((), ()))

    n_extra = len(extra)
    single = epilogue is None
    if single:
        out_dtypes = [out_dtype]

    def body(a_ref, b_ref, *rest):
        extra_refs, o_refs, acc_ref = rest[:n_extra], rest[n_extra:-1], rest[-1]
        k = pl.program_id(2)

        @pl.when(k == 0)
        def _():
            acc_ref[...] = jnp.zeros_like(acc_ref)

        acc_ref[...] += lax.dot_general(a_ref[...].astype(BF16), b_ref[...].astype(BF16), dn,
                                        preferred_element_type=F32)

        @pl.when(k == nk - 1)
        def _():
            acc = acc_ref[...]
            res = (acc,) if single else epilogue(acc, *[r[...] for r in extra_refs])
            for o_ref, val in zip(o_refs, res):
                o_ref[...] = val.astype(o_ref.dtype)

    a_spec = pl.BlockSpec((tk, tm), lambda i, j, k: (k, i)) if ta else pl.BlockSpec((tm, tk), lambda i, j, k: (i, k))
    b_spec = pl.BlockSpec((tn, tk), lambda i, j, k: (j, k)) if tb else pl.BlockSpec((tk, tn), lambda i, j, k: (k, j))
    mn_spec = pl.BlockSpec((tm, tn), lambda i, j, k: (i, j))
    outs = _pcall(
        body, name=name, grid=(M // tm, N // tn, nk), in_specs=[a_spec, b_spec] + [mn_spec] * n_extra,
        out_specs=[mn_spec] * len(out_dtypes),
        out_shape=[jax.ShapeDtypeStruct((M, N), d) for d in out_dtypes],
        scratch_shapes=[pltpu.VMEM((tm, tn), F32)],
        compiler_params=_cp(("parallel", "parallel", "arbitrary")),
    )(a, b, *extra)
    return outs[0] if single else outs


def _linear(name):
    @jax.custom_vjp
    def mm(a, w):
        return _matmul(a, w, name=name + "_fwd")

    def fwd(a, w):
        return mm(a, w), (a, w)

    def bwd(res, dy):
        a, w = res
        da = _matmul(dy, w, tb=True, out_dtype=a.dtype, name=name + "_dgrad")
        dw = _matmul(a, dy, ta=True, out_dtype=w.dtype, name=name + "_wgrad")
        return da, dw

    mm.defvjp(fwd, bwd)
    return mm


def _relu2_epilogue(acc):
    return acc, jnp.square(jnp.maximum(acc, 0.0))


def _relu2_grad_epilogue(acc, h1):
    return (acc * (2.0 * jnp.maximum(h1.astype(F32), 0.0)),)


def _mlp(name):
    def run(h, w1, w2):
        h1, act = _matmul(h, w1, name=name + "_up_fwd", epilogue=_relu2_epilogue, out_dtypes=[BF16, BF16])
        return _matmul(act, w2, name=name + "_down_fwd"), h1, act

    @jax.custom_vjp
    def mlp(h, w1, w2):
        return run(h, w1, w2)[0]

    def fwd(h, w1, w2):
        y, h1, act = run(h, w1, w2)
        return y, (h, w1, w2, h1, act)

    def bwd(res, dy):
        h, w1, w2, h1, act = res
        dw2 = _matmul(act, dy, ta=True, out_dtype=w2.dtype, name=name + "_down_wgrad")
        (dh1,) = _matmul(dy, w2, tb=True, name=name + "_down_dgrad", extra=(h1,), epilogue=_relu2_grad_epilogue, out_dtypes=[BF16])
        dw1 = _matmul(h, dh1, ta=True, out_dtype=w1.dtype, name=name + "_up_wgrad")
        dh = _matmul(dh1, w1, tb=True, out_dtype=h.dtype, name=name + "_up_dgrad")
        return dh, dw1, dw2

    mlp.defvjp(fwd, bwd)
    return mlp


def _rowop(name, f, *, n_x, n_nd, n_p, x_w, nd_w, nd_shared, p_per_group, out_w, out_dtypes, tile, groups=1):
    G = groups

    def specs(S):
        t = min(tile, S)
        xs = [pl.BlockSpec((t, w), lambda g, r: (r, g)) for w in x_w]
        nds = [pl.BlockSpec((t, w), (lambda g, r: (r, 0)) if sh else (lambda g, r: (r, g))) for w, sh in zip(nd_w, nd_shared)]
        outs = [pl.BlockSpec((t, w), lambda g, r: (r, g)) for w in out_w]
        return t, xs, nds, outs

    def pspecs(ps):
        return [pl.BlockSpec((None,) + p.shape[1:], (lambda g, r: (g, 0, 0)) if pg else (lambda g, r: (0, 0, 0)))
                for p, pg in zip(ps, p_per_group)]

    def fwd_call(xs, nds, ps):
        S = xs[0].shape[0]
        t, xsp, ndsp, osp = specs(S)

        def body(*refs):
            ins, outs = refs[:n_x + n_nd + n_p], refs[n_x + n_nd + n_p:]
            vals = [r[...].astype(F32) for r in ins]
            res = f(*vals)
            for o, r in zip(res, outs):
                r[...] = o.astype(r.dtype)

        return _pcall(
            body, name=name + "_fwd", grid=(G, S // t), in_specs=xsp + ndsp + pspecs(ps), out_specs=osp,
            out_shape=[jax.ShapeDtypeStruct((S, G * w), d) for w, d in zip(out_w, out_dtypes)],
            compiler_params=_cp(("parallel", "parallel")),
        )(*xs, *nds, *ps)

    def bwd_call(xs, nds, ps, douts):
        S = xs[0].shape[0]
        t, xsp, ndsp, osp = specs(S)
        n_in = n_x + n_nd + n_p + len(out_w)

        def body(*refs):
            ins, outs = refs[:n_in], refs[n_in:]
            xv = [r[...].astype(F32) for r in ins[:n_x]]
            ndv = [r[...].astype(F32) for r in ins[n_x:n_x + n_nd]]
            pv = [r[...].astype(F32) for r in ins[n_x + n_nd:n_x + n_nd + n_p]]
            dov = tuple(r[...].astype(F32) for r in ins[n_x + n_nd + n_p:])
            _, vjp = jax.vjp(lambda *a: tuple(f(*a[:n_x], *ndv, *a[n_x:])), *xv, *pv)
            cts = vjp(dov)
            for i in range(n_x):
                outs[i][...] = cts[i].astype(outs[i].dtype)
            g, r = pl.program_id(0), pl.program_id(1)
            for i in range(n_p):
                first = (r == 0) if p_per_group[i] else jnp.logical_and(g == 0, r == 0)
                ref, ct = outs[n_x + i], cts[n_x + i]

                @pl.when(first)
                def _(ref=ref, ct=ct):
                    ref[...] = ct

                @pl.when(jnp.logical_not(first))
                def _(ref=ref, ct=ct):
                    ref[...] += ct

        return _pcall(
            body, name=name + "_bwd", grid=(G, S // t), in_specs=xsp + ndsp + pspecs(ps) + osp,
            out_specs=xsp + pspecs(ps),
            out_shape=[jax.ShapeDtypeStruct(x.shape, x.dtype) for x in xs] + [jax.ShapeDtypeStruct(p.shape, F32) for p in ps],
            compiler_params=_cp(("arbitrary", "arbitrary")),
        )(*xs, *nds, *ps, *douts)

    @jax.custom_vjp
    def op(xs, nds, ps):
        return tuple(fwd_call(xs, nds, ps))

    def op_fwd(xs, nds, ps):
        return op(xs, nds, ps), (xs, nds, ps)

    def op_bwd(res, douts):
        xs, nds, ps = res
        out = bwd_call(xs, nds, ps, douts)
        return tuple(out[:n_x]), tuple(jnp.zeros_like(n) for n in nds), tuple(out[n_x:])

    op.defvjp(op_fwd, op_bwd)
    return op


def _rms_rows(x):
    return x * lax.rsqrt(jnp.mean(x * x, axis=-1, keepdims=True) + EPS)


def _f_norm_mod(x, g, shift, scale):
    return ((_rms_rows(x) * g) * (1.0 + scale) + shift,)


def _f_resid_norm_mod(x, mix, gate, g, shift, scale):
    xn = x + gate * mix
    return xn, (_rms_rows(xn) * g) * (1.0 + scale) + shift


def _f_resid(x, mix, gate):
    return (x + gate * mix,)


def _f_rms(x, g):
    return (_rms_rows(x) * g,)


def _f_glu(a, g):
    return (a * jax.nn.sigmoid(g),)


def _f_ln_silu(z, g, b):
    mu = jnp.mean(z, axis=-1, keepdims=True)
    zc = z - mu
    var = jnp.mean(zc * zc, axis=-1, keepdims=True)
    y = zc * lax.rsqrt(var + EPS) * g + b
    return (y * jax.nn.sigmoid(y),)


def _f_silu(x):
    return (x * jax.nn.sigmoid(x),)


@jax.custom_vjp
def _bdot(a, b):
    return jnp.dot(a.astype(BF16), b.astype(BF16), preferred_element_type=F32)


def _bdot_fwd(a, b):
    return _bdot(a, b), (a, b)


def _bdot_bwd(res, ct):
    a, b = res
    c16 = ct.astype(BF16)
    da = lax.dot_general(c16, b.astype(BF16), (((1,), (1,)), ((), ())), preferred_element_type=F32)
    db = lax.dot_general(a.astype(BF16), c16, (((0,), (0,)), ((), ())), preferred_element_type=F32)
    return da, db


_bdot.defvjp(_bdot_fwd, _bdot_bwd)


def _f_sgu(u, v, ng, w, bexp):
    vn = _rms_rows(jax.nn.gelu(v)) * ng
    row = lax.broadcasted_iota(jnp.int32, w.shape, 0)
    col = lax.broadcasted_iota(jnp.int32, w.shape, 1)
    mixed = _bdot(jnp.where(row >= col, w, 0.0), vn) + bexp
    return (jax.nn.gelu(u) * mixed,)


def _lo_mask():
    return lax.broadcasted_iota(jnp.int32, (1, HEAD_W), 1) < NOPE


def _f_qhead(x, tab, g):
    lo = _lo_mask()
    x2 = x * x
    ms_lo = jnp.sum(jnp.where(lo, x2, 0.0), axis=-1, keepdims=True) * (1.0 / NOPE)
    ms_hi = jnp.sum(jnp.where(lo, 0.0, x2), axis=-1, keepdims=True) * (1.0 / (HEAD_W - NOPE))
    r = jnp.where(lo, lax.rsqrt(ms_lo + EPS), lax.rsqrt(ms_hi + EPS))
    return (((x * r) * g) * tab,)


def _f_kvhead(x, g):
    lo = _lo_mask()
    ms = jnp.sum(jnp.where(lo, x * x, 0.0), axis=-1, keepdims=True) * (1.0 / NOPE)
    return (jnp.where(lo, (x * lax.rsqrt(ms + EPS)) * g, x),)


def _f_krope(a, b, ta, tb, ga, gb):
    r = lax.rsqrt(jnp.mean(a * a, axis=-1, keepdims=True) + EPS)
    return (((a * r) * ga) * ta + ((b * r) * gb) * tb,)


def _conv_fwd_call(y, w, b, name):
    S, C = y.shape
    cw = 128
    rt = _pick(S, (128,))
    w = jnp.pad(w, ((0, CONV_PAD - CONV_W), (0, 0)))

    def body(y_ref, w_ref, b_ref, z_ref, pad_ref):
        pad_ref[pl.ds(0, CONV_PAD), :] = jnp.zeros((CONV_PAD, cw), F32)
        pad_ref[pl.ds(CONV_PAD, S), :] = y_ref[...]
        wv = w_ref[...]
        bv = b_ref[...]

        def chunk(ci, carry):
            r0 = pl.multiple_of(ci * rt, rt)
            win = pad_ref[pl.ds(r0, rt + CONV_PAD), :]
            acc = jnp.broadcast_to(bv, (rt, cw))
            for k in range(CONV_W):
                off = CONV_PAD - (CONV_W - 1) + k
                sh = win if off == 0 else pltpu.roll(win, rt + CONV_PAD - off, axis=0)
                acc = acc + wv[k:k + 1, :] * sh[:rt, :]
            z_ref[pl.ds(r0, rt), :] = acc
            return carry

        lax.fori_loop(0, S // rt, chunk, 0)

    return _pcall(
        body, name=name, grid=(C // cw,),
        in_specs=[pl.BlockSpec((S, cw), lambda j: (0, j)), pl.BlockSpec((CONV_PAD, cw), lambda j: (0, j)),
                  pl.BlockSpec((1, cw), lambda j: (0, j))],
        out_specs=pl.BlockSpec((S, cw), lambda j: (0, j)),
        out_shape=jax.ShapeDtypeStruct((S, C), F32),
        scratch_shapes=[pltpu.VMEM((S + CONV_PAD, cw), F32)],
        compiler_params=_cp(("parallel",)),
    )(y, w, b)


def _conv_bwd_call(y, w, dz, name):
    S, C = y.shape
    cw = 128
    rt = _pick(S, (128,))
    w = jnp.pad(w, ((0, CONV_PAD - CONV_W), (0, 0)))

    def body(y_ref, w_ref, dz_ref, dy_ref, dw_ref, db_ref, ypad_ref, zpad_ref):
        ypad_ref[pl.ds(0, CONV_PAD), :] = jnp.zeros((CONV_PAD, cw), F32)
        ypad_ref[pl.ds(CONV_PAD, S), :] = y_ref[...]
        zpad_ref[pl.ds(0, S), :] = dz_ref[...]
        zpad_ref[pl.ds(S, CONV_PAD), :] = jnp.zeros((CONV_PAD, cw), F32)
        dw_ref[...] = jnp.zeros_like(dw_ref)
        wv = w_ref[...]

        def chunk(ci, dbacc):
            r0 = pl.multiple_of(ci * rt, rt)
            ywin = ypad_ref[pl.ds(r0, rt + CONV_PAD), :]
            zwin = zpad_ref[pl.ds(r0, rt + CONV_PAD), :]
            dzc = zwin[:rt, :]
            acc = jnp.zeros((rt, cw), F32)
            for k in range(CONV_W):
                off_z = (CONV_W - 1) - k
                zs = zwin if off_z == 0 else pltpu.roll(zwin, rt + CONV_PAD - off_z, axis=0)
                acc = acc + wv[k:k + 1, :] * zs[:rt, :]
                off_y = CONV_PAD - (CONV_W - 1) + k
                ys = pltpu.roll(ywin, rt + CONV_PAD - off_y, axis=0)
                dw_ref[k:k + 1, :] += jnp.sum(dzc * ys[:rt, :], axis=0, keepdims=True)
            dy_ref[pl.ds(r0, rt), :] = acc
            return dbacc + jnp.sum(dzc, axis=0, keepdims=True)

        db_ref[...] = lax.fori_loop(0, S // rt, chunk, jnp.zeros((1, cw), F32))

    dy, dw, db = _pcall(
        body, name=name, grid=(C // cw,),
        in_specs=[pl.BlockSpec((S, cw), lambda j: (0, j)), pl.BlockSpec((CONV_PAD, cw), lambda j: (0, j)),
                  pl.BlockSpec((S, cw), lambda j: (0, j))],
        out_specs=[pl.BlockSpec((S, cw), lambda j: (0, j)), pl.BlockSpec((CONV_PAD, cw), lambda j: (0, j)),
                   pl.BlockSpec((1, cw), lambda j: (0, j))],
        out_shape=[jax.ShapeDtypeStruct((S, C), F32), jax.ShapeDtypeStruct((CONV_PAD, C), F32),
                   jax.ShapeDtypeStruct((1, C), F32)],
        scratch_shapes=[pltpu.VMEM((S + CONV_PAD, cw), F32), pltpu.VMEM((S + CONV_PAD, cw), F32)],
        compiler_params=_cp(("parallel",)),
    )(y, w, dz)
    return dy, dw[:CONV_W], db


def _conv_op(name):
    @jax.custom_vjp
    def conv(y, w, b):
        return _conv_fwd_call(y, w, b, name + "_fwd")

    def fwd(y, w, b):
        return conv(y, w, b), (y, w)

    def bwd(res, dz):
        y, w = res
        return _conv_bwd_call(y, w, dz, name + "_bwd")

    conv.defvjp(fwd, bwd)
    return conv


def _attn_tile(S):
    return _pick(S, (512, 256, 128))


def _attn_fwd_call(q, kv, kr, name):
    S = q.shape[0]
    t = _attn_tile(S)

    def body(q_ref, kv_ref, kr_ref, o_ref, lse_ref):
        i = pl.program_id(1)
        qv = q_ref[...]

        def update(jb, carry, diagonal):
            m, l, acc = carry
            off = pl.multiple_of(jb * t, t)
            kc = jnp.concatenate([kv_ref[pl.ds(off, t), pl.ds(0, NOPE)], kr_ref[pl.ds(off, t), :]], axis=-1)
            vv = kv_ref[pl.ds(off, t), pl.ds(NOPE, VDIM)]
            s = lax.dot_general(qv, kc, (((1,), (1,)), ((), ())), preferred_element_type=F32) * ATTN_SCALE
            if diagonal:
                row = lax.broadcasted_iota(jnp.int32, (t, t), 0)
                col = lax.broadcasted_iota(jnp.int32, (t, t), 1)
                s = jnp.where(col <= row, s, -jnp.inf)
            mn = jnp.maximum(m, jnp.max(s, axis=-1, keepdims=True))
            p = jnp.exp(s - mn)
            al = jnp.exp(m - mn)
            l = al * l + jnp.sum(p, axis=-1, keepdims=True)
            acc = al * acc + jnp.dot(p.astype(BF16), vv, preferred_element_type=F32)
            return mn, l, acc

        init = (jnp.full((t, 1), -jnp.inf, F32), jnp.zeros((t, 1), F32), jnp.zeros((t, VDIM), F32))
        carry = lax.fori_loop(0, i, lambda jb, cr: update(jb, cr, False), init)
        m, l, acc = update(i, carry, True)
        o_ref[...] = (acc / l).astype(o_ref.dtype)
        lse_ref[...] = m + jnp.log(l)

    return _pcall(
        body, name=name, grid=(N_HEADS, S // t),
        in_specs=[pl.BlockSpec((t, HEAD_W), lambda h, i: (i, h)), pl.BlockSpec((S, HEAD_W), lambda h, i: (0, h)),
                  pl.BlockSpec((S, 128), lambda h, i: (0, 0))],
        out_specs=[pl.BlockSpec((t, VDIM), lambda h, i: (i, h)), pl.BlockSpec((None, t, 1), lambda h, i: (h, i, 0))],
        out_shape=[jax.ShapeDtypeStruct((S, N_HEADS * VDIM), BF16), jax.ShapeDtypeStruct((N_HEADS, S, 1), F32)],
        compiler_params=_cp(("parallel", "parallel")),
    )(q, kv, kr)


def _attn_dd_call(o, do, name):
    S = o.shape[0]
    t = _attn_tile(S)

    def body(o_ref, do_ref, dd_ref):
        dd_ref[...] = jnp.sum(do_ref[...].astype(F32) * o_ref[...].astype(F32), axis=-1, keepdims=True)

    return _pcall(
        body, name=name, grid=(N_HEADS, S // t),
        in_specs=[pl.BlockSpec((t, VDIM), lambda h, i: (i, h)), pl.BlockSpec((t, VDIM), lambda h, i: (i, h))],
        out_specs=pl.BlockSpec((None, t, 1), lambda h, i: (h, i, 0)),
        out_shape=jax.ShapeDtypeStruct((N_HEADS, S, 1), F32), compiler_params=_cp(("parallel", "parallel")),
    )(o, do)


def _attn_bwd_call(q, kv, kr, do, lse, dd, name):
    S = q.shape[0]
    t = _attn_tile(S)
    nq = S // t

    def body(q_ref, kv_ref, kr_ref, do_ref, lse_ref, dd_ref, dq_ref, dkv_ref, dkr_ref, dq_acc):
        j = pl.program_id(1)

        @pl.when(j == 0)
        def _():
            dq_acc[...] = jnp.zeros_like(dq_acc)

        kc = jnp.concatenate([kv_ref[:, pl.ds(0, NOPE)], kr_ref[...]], axis=-1)
        vv = kv_ref[:, pl.ds(NOPE, VDIM)]

        def update(ib, carry, diagonal):
            dkc, dv = carry
            off = pl.multiple_of(ib * t, t)
            qv = q_ref[pl.ds(off, t), :]
            dov = do_ref[pl.ds(off, t), :]
            st = lax.dot_general(kc, qv, (((1,), (1,)), ((), ())), preferred_element_type=F32) * ATTN_SCALE
            if diagonal:
                key = lax.broadcasted_iota(jnp.int32, (t, t), 0)
                qry = lax.broadcasted_iota(jnp.int32, (t, t), 1)
                st = jnp.where(key <= qry, st, -jnp.inf)
            pt = jnp.exp(st - lse_ref[pl.ds(ib, 1), :])
            dv = dv + jnp.dot(pt.astype(BF16), dov, preferred_element_type=F32)
            dpt = lax.dot_general(vv, dov, (((1,), (1,)), ((), ())), preferred_element_type=F32)
            dst = (pt * (dpt - dd_ref[pl.ds(ib, 1), :]) * ATTN_SCALE).astype(BF16)
            dkc = dkc + jnp.dot(dst, qv, preferred_element_type=F32)
            dq_acc[pl.ds(off, t), :] += lax.dot_general(dst, kc, (((0,), (0,)), ((), ())), preferred_element_type=F32)
            return dkc, dv

        carry = update(j, (jnp.zeros((t, HEAD_W), F32), jnp.zeros((t, VDIM), F32)), True)
        dkc, dv = lax.fori_loop(j + 1, nq, lambda ib, cr: update(ib, cr, False), carry)
        dkv_ref[:, pl.ds(0, NOPE)] = dkc[:, :NOPE].astype(dkv_ref.dtype)
        dkv_ref[:, pl.ds(NOPE, VDIM)] = dv.astype(dkv_ref.dtype)
        dkr_ref[...] = dkc[:, NOPE:]

        @pl.when(j == nq - 1)
        def _():
            dq_ref[...] = dq_acc[...].astype(dq_ref.dtype)

    dq, dkv, dkr_heads = _pcall(
        body, name=name, grid=(N_HEADS, nq),
        in_specs=[pl.BlockSpec((S, HEAD_W), lambda h, j: (0, h)), pl.BlockSpec((t, HEAD_W), lambda h, j: (j, h)),
                  pl.BlockSpec((t, 128), lambda h, j: (j, 0)), pl.BlockSpec((S, VDIM), lambda h, j: (0, h)),
                  pl.BlockSpec((None, nq, t), lambda h, j: (h, 0, 0)), pl.BlockSpec((None, nq, t), lambda h, j: (h, 0, 0))],
        out_specs=[pl.BlockSpec((S, HEAD_W), lambda h, j: (0, h)), pl.BlockSpec((t, HEAD_W), lambda h, j: (j, h)),
                   pl.BlockSpec((None, t, 128), lambda h, j: (h, j, 0))],
        out_shape=[jax.ShapeDtypeStruct(q.shape, q.dtype), jax.ShapeDtypeStruct(kv.shape, kv.dtype),
                   jax.ShapeDtypeStruct((N_HEADS, S, 128), F32)],
        scratch_shapes=[pltpu.VMEM((S, HEAD_W), F32)],
        compiler_params=_cp(("parallel", "arbitrary")),
    )(q, kv, kr, do, lse.reshape(N_HEADS, nq, t), dd.reshape(N_HEADS, nq, t))

    def sum_body(p_ref, o_ref):
        acc = p_ref[0]
        for h in range(1, N_HEADS):
            acc = acc + p_ref[h]
        o_ref[...] = acc.astype(o_ref.dtype)

    dkr = _pcall(
        sum_body, name=name + "_rope_sum", grid=(nq,),
        in_specs=[pl.BlockSpec((N_HEADS, t, 128), lambda i: (0, i, 0))], out_specs=pl.BlockSpec((t, 128), lambda i: (i, 0)),
        out_shape=jax.ShapeDtypeStruct((S, 128), kr.dtype), compiler_params=_cp(("parallel",)),
    )(dkr_heads)
    return dq, dkv, dkr


def _attn_op(name):
    @jax.custom_vjp
    def attn(q, kv, kr):
        return _attn_fwd_call(q, kv, kr, name + "_fwd")[0]

    def fwd(q, kv, kr):
        o, lse = _attn_fwd_call(q, kv, kr, name + "_fwd")
        return o, (q, kv, kr, o, lse)

    def bwd(res, do):
        q, kv, kr, o, lse = res
        dd = _attn_dd_call(o, do, name + "_dd")
        return _attn_bwd_call(q, kv, kr, do, lse, dd, name + "_bwd")

    attn.defvjp(fwd, bwd)
    return attn


def _loss_call(y, target):
    S, D = y.shape
    t = _pick(S, (256, 128))

    def body(y_ref, t_ref, dy_ref, loss_ref):
        @pl.when(pl.program_id(0) == 0)
        def _():
            loss_ref[...] = jnp.zeros_like(loss_ref)

        e = y_ref[...] - t_ref[...]
        dy_ref[...] = e * (1.0 / D)
        loss_ref[...] += 0.5 * jnp.sum(jnp.mean(e * e, axis=-1, keepdims=True), axis=0, keepdims=True)

    return _pcall(
        body, name="loss_head", grid=(S // t,),
        in_specs=[pl.BlockSpec((t, D), lambda i: (i, 0)), pl.BlockSpec((t, D), lambda i: (i, 0))],
        out_specs=[pl.BlockSpec((t, D), lambda i: (i, 0)), pl.BlockSpec((1, 1), lambda i: (0, 0))],
        out_shape=[jax.ShapeDtypeStruct((S, D), F32), jax.ShapeDtypeStruct((1, 1), F32)],
        compiler_params=_cp(("arbitrary",)),
    )(y, target)


def _adamw_call(w, g, m, v, name):
    shape = w.shape
    C = shape[-1]
    R = int(np.prod(shape[:-1]))
    tr = R
    for cand in (512, 256, 128, 64, 32, 16, 8):
        if R % cand == 0 and cand * C * 4 <= 2 * 1024 * 1024:
            tr = cand
            break
    c1 = 1.0 - ADAM_B1 ** ADAM_STEP
    c2 = 1.0 - ADAM_B2 ** ADAM_STEP

    def body(w_ref, g_ref, m_ref, v_ref, d_ref, mo_ref, vo_ref):
        gv = g_ref[...]
        mn = ADAM_B1 * m_ref[...] + (1.0 - ADAM_B1) * gv
        vn = ADAM_B2 * v_ref[...] + (1.0 - ADAM_B2) * (gv * gv)
        d_ref[...] = -ADAM_LR * ((mn / c1) / (jnp.sqrt(vn / c2) + ADAM_EPS) + ADAM_WD * w_ref[...])
        mo_ref[...] = mn
        vo_ref[...] = vn

    spec = pl.BlockSpec((tr, C), lambda i: (i, 0))
    outs = _pcall(
        body, name=name, grid=(R // tr,), in_specs=[spec] * 4, out_specs=[spec] * 3,
        out_shape=[jax.ShapeDtypeStruct((R, C), F32)] * 3, compiler_params=_cp(("parallel",)),
    )(*[a.reshape(R, C) for a in (w, g, m, v)])
    return [o.reshape(shape) for o in outs]


def _gather_small(x2d, *, reduce, name):
    R, C = x2d.shape

    def body(x_ref, out_ref, *scratch):
        if reduce:
            buf_ref, send_sems, recv_sems, local_sem = scratch
        else:
            buf_ref = out_ref
            send_sems, recv_sems, local_sem = scratch
        x, y, c = lax.axis_index("x"), lax.axis_index("y"), lax.axis_index("c")
        me, sibling = (x, y, c), (x, y, 1 - c)
        chips = [(1 - x, y), (x, 1 - y), (1 - x, 1 - y)]

        def rows(px, py, pc):
            return buf_ref.at[pl.ds((4 * px + 2 * py + pc) * R, R), :]

        def copy(k, block, to, src=None):
            return pltpu.make_async_remote_copy(
                src_ref=rows(*block) if src is None else src, dst_ref=rows(*block),
                send_sem=send_sems.at[k], recv_sem=recv_sems.at[k], device_id=to, device_id_type=MESH)

        mine = pltpu.make_async_copy(x_ref, rows(*me), local_sem)
        mine.start()
        first = [copy(0, me, sibling, src=x_ref)]
        first += [copy(1 + j, me, (*chip, c), src=x_ref) for j, chip in enumerate(chips)]
        for cp in first:
            cp.start()
        passed = [copy(4 + j, (*chip, c), sibling) for j, chip in enumerate(chips)]
        for j, chip in enumerate(chips):
            copy(1 + j, (*chip, c), me).wait_recv()
            passed[j].start()
        copy(0, sibling, me).wait_recv()
        for j, chip in enumerate(chips):
            copy(4 + j, (*chip, 1 - c), me).wait_recv()
        for cp in first + passed:
            cp.wait_send()
        mine.wait()
        if reduce:
            acc = buf_ref[pl.ds(0, R), :]
            for d in range(1, N_DEV):
                acc = acc + buf_ref[pl.ds(d * R, R), :]
            out_ref[...] = acc

    scratch = [pltpu.SemaphoreType.DMA((7,)), pltpu.SemaphoreType.DMA((7,)), pltpu.SemaphoreType.DMA]
    if reduce:
        scratch = [pltpu.VMEM((N_DEV * R, C), F32)] + scratch
    return _pcall(
        body, name=name, out_shape=jax.ShapeDtypeStruct((R if reduce else N_DEV * R, C), F32),
        in_specs=[pl.BlockSpec(memory_space=pltpu.VMEM)], out_specs=pl.BlockSpec(memory_space=pltpu.VMEM),
        scratch_shapes=scratch, compiler_params=pltpu.CompilerParams(vmem_limit_bytes=VMEM_LIMIT),
    )(x2d)


def _pack(arrs):
    flat = jnp.concatenate([a.reshape(-1).astype(F32) for a in arrs])
    n = flat.shape[0]
    unit = 8 * SMALL_COLS
    flat = jnp.pad(flat, (0, (-n) % unit))
    return flat.reshape(-1, SMALL_COLS)


def _unpack(flat, shapes):
    out, o = [], 0
    for s in shapes:
        n = int(np.prod(s))
        out.append(flat[o:o + n].reshape(s))
        o += n
    return out


def _all_gather_small(arrs, name):
    p = _pack(arrs)
    g = _gather_small(p, reduce=False, name=name).reshape(N_DEV, -1)
    out, o = [], 0
    for a in arrs:
        n = int(np.prod(a.shape))
        out.append(g[:, o:o + n].reshape((N_DEV,) + a.shape))
        o += n
    return out


def _all_reduce_small(arrs, name):
    p = _pack(arrs)
    return _unpack(_gather_small(p, reduce=True, name=name).reshape(-1), [a.shape for a in arrs])


def _half_rows(shard_shape):
    return shard_shape[0] // 2


def _half_slot(ref, kind, rh, cols, half, slot):
    if kind == "col":
        return ref.at[pl.ds(half * rh, rh), pl.ds(slot * cols, cols)]
    return ref.at[pl.ds((slot * 2 + half) * rh, rh), :]


def _full_shape(kind, shard_shape):
    r, c = shard_shape
    return (r, c * N_CHIPS) if kind == "col" else (r * N_CHIPS, c)


def _mesh_pos():
    x, y, c = lax.axis_index("x"), lax.axis_index("y"), lax.axis_index("c")
    return x, y, c


def _handshake_chip_peers(x, y, c):
    barrier = pltpu.get_barrier_semaphore()
    for peer in [(x, y, 1 - c)] + [(x ^ fx, y ^ fy, c) for fx, fy in CHIP_RELS]:
        pl.semaphore_signal(barrier, inc=1, device_id=peer, device_id_type=MESH)
    pl.semaphore_wait(barrier, 1 + len(CHIP_RELS))


def _gather_big(shards, kinds, name):
    n = len(shards)

    def body(*refs):
        srcs, outs = refs[:n], refs[n:2 * n]
        ici_send, ici_recv, d2d_send, d2d_recv, own_send, own_recv = refs[2 * n:]
        x, y, c = _mesh_pos()
        _handshake_chip_peers(x, y, c)
        my = 2 * x + y
        geo = [(_half_rows(s.shape), s.shape[1]) for s in shards]

        def region(a, half, slot):
            return _half_slot(outs[a], kinds[a], geo[a][0], geo[a][1], half, slot)

        def slot_of(a, slot):
            rh, cols = geo[a]
            if kinds[a] == "col":
                return outs[a].at[:, pl.ds(slot * cols, cols)]
            return outs[a].at[pl.ds(slot * 2 * rh, 2 * rh), :]

        sends = []
        for a in range(n):
            cp = pltpu.make_async_remote_copy(
                src_ref=srcs[a], dst_ref=slot_of(a, my), send_sem=own_send.at[a], recv_sem=own_recv.at[a],
                device_id=(x, y, 1 - c), device_id_type=MESH)
            cp.start()
            sends.append(cp)
        for a in range(n):
            rh = geo[a][0]
            for r, (fx, fy) in enumerate(CHIP_RELS):
                cp = pltpu.make_async_remote_copy(
                    src_ref=srcs[a].at[pl.ds(c * rh, rh), :], dst_ref=region(a, c, my),
                    send_sem=ici_send.at[3 * a + r], recv_sem=ici_recv.at[3 * a + r],
                    device_id=(x ^ fx, y ^ fy, c), device_id_type=MESH)
                cp.start()
                sends.append(cp)
        passed = []
        for a in range(n):
            for r, (fx, fy) in enumerate(CHIP_RELS):
                frm = 2 * (x ^ fx) + (y ^ fy)
                landed = region(a, c, frm)
                pltpu.make_async_remote_copy(
                    src_ref=landed, dst_ref=landed, send_sem=ici_send.at[3 * a + r], recv_sem=ici_recv.at[3 * a + r],
                    device_id=(x, y, c), device_id_type=MESH).wait_recv()
                cp = pltpu.make_async_remote_copy(
                    src_ref=landed, dst_ref=landed, send_sem=d2d_send.at[3 * a + r], recv_sem=d2d_recv.at[3 * a + r],
                    device_id=(x, y, 1 - c), device_id_type=MESH)
                cp.start()
                passed.append(cp)
        for a in range(n):
            for r, (fx, fy) in enumerate(CHIP_RELS):
                frm = 2 * (x ^ fx) + (y ^ fy)
                other = region(a, 1 - c, frm)
                pltpu.make_async_remote_copy(
                    src_ref=other, dst_ref=other, send_sem=d2d_send.at[3 * a + r], recv_sem=d2d_recv.at[3 * a + r],
                    device_id=(x, y, c), device_id_type=MESH).wait_recv()
        for a in range(n):
            pltpu.make_async_remote_copy(
                src_ref=srcs[a], dst_ref=slot_of(a, my), send_sem=own_send.at[a], recv_sem=own_recv.at[a],
                device_id=(x, y, c), device_id_type=MESH).wait_recv()
        for cp in sends + passed:
            cp.wait_send()

    return pl.kernel(
        body, name=name,
        out_type=[jax.ShapeDtypeStruct(_full_shape(k, s.shape), s.dtype) for s, k in zip(shards, kinds)],
        mesh=plsc.ScalarSubcoreMesh(axis_name="sequencer", num_cores=1),
        scratch_types=[pltpu.SemaphoreType.DMA((3 * n,))] * 4 + [pltpu.SemaphoreType.DMA((n,))] * 2,
        compiler_params=pltpu.CompilerParams(collective_id=GATHER_COLLECTIVE_ID),
    )(*shards)


def _swap_halves(fulls, kinds, shard_shapes, name):
    n = len(fulls)
    geo = [(_half_rows(s), s[1]) for s in shard_shapes]

    def body(*refs):
        srcs, outs = refs[:n], refs[n:2 * n]
        send_sems, recv_sems = refs[2 * n:]
        x, y, c = _mesh_pos()
        barrier = pltpu.get_barrier_semaphore()
        pl.semaphore_signal(barrier, inc=1, device_id=(x, y, 1 - c), device_id_type=MESH)
        pl.semaphore_wait(barrier, 1)
        cps = []
        for a in range(n):
            for s in range(N_CHIPS):
                cp = pltpu.make_async_remote_copy(
                    src_ref=_half_slot(srcs[a], kinds[a], geo[a][0], geo[a][1], 1 - c, s), dst_ref=outs[a].at[s],
                    send_sem=send_sems.at[N_CHIPS * a + s], recv_sem=recv_sems.at[N_CHIPS * a + s],
                    device_id=(x, y, 1 - c), device_id_type=MESH)
                cp.start()
                cps.append(cp)
        for cp in cps:
            cp.wait()

    return pl.kernel(
        body, name=name,
        out_type=[jax.ShapeDtypeStruct((N_CHIPS,) + g, f.dtype) for g, f in zip(geo, fulls)],
        mesh=plsc.ScalarSubcoreMesh(axis_name="sequencer", num_cores=1),
        scratch_types=[pltpu.SemaphoreType.DMA((N_CHIPS * n,))] * 2,
        compiler_params=pltpu.CompilerParams(collective_id=SWAP_COLLECTIVE_ID),
    )(*fulls)


def _add_own_half(full, recv, kind, shard_shape, cidx, name):
    rh, cols = _half_rows(shard_shape), shard_shape[1]
    tr = _pick(rh, (256, 128, 64, 32, 16))
    nb = rh // tr

    def body(c_ref, f_ref, r_ref, o_ref):
        o_ref[...] = (f_ref[...].astype(F32) + r_ref[...].astype(F32)).astype(o_ref.dtype)

    if kind == "col":
        f_spec = pl.BlockSpec((tr, cols), lambda s, i, c_ref: (c_ref[0] * nb + i, s))
    else:
        f_spec = pl.BlockSpec((tr, cols), lambda s, i, c_ref: ((s * 2 + c_ref[0]) * nb + i, 0))
    blk = pl.BlockSpec((None, tr, cols), lambda s, i, c_ref: (s, i, 0))
    return _pcall(
        body, name=name,
        grid_spec=pltpu.PrefetchScalarGridSpec(num_scalar_prefetch=1, grid=(N_CHIPS, nb), in_specs=[f_spec, blk], out_specs=blk),
        out_shape=jax.ShapeDtypeStruct((N_CHIPS, rh, cols), BF16),
        compiler_params=_cp(("arbitrary", "arbitrary")),
    )(cidx, full, recv)


def _scatter_partials(partials, name):
    n = len(partials)

    def body(*refs):
        srcs, outs = refs[:n], refs[n:2 * n]
        send_sems, recv_sems = refs[2 * n:]
        x, y, c = _mesh_pos()
        barrier = pltpu.get_barrier_semaphore()
        for fx, fy in CHIP_RELS:
            pl.semaphore_signal(barrier, inc=1, device_id=(x ^ fx, y ^ fy, c), device_id_type=MESH)
        pl.semaphore_wait(barrier, len(CHIP_RELS))
        my = 2 * x + y
        cps = []
        for a in range(n):
            for r, (fx, fy) in enumerate(CHIP_RELS):
                to = 2 * (x ^ fx) + (y ^ fy)
                cp = pltpu.make_async_remote_copy(
                    src_ref=srcs[a].at[to], dst_ref=outs[a].at[my],
                    send_sem=send_sems.at[3 * a + r], recv_sem=recv_sems.at[3 * a + r],
                    device_id=(x ^ fx, y ^ fy, c), device_id_type=MESH)
                cp.start()
                cps.append(cp)
        for a in range(n):
            for r, (fx, fy) in enumerate(CHIP_RELS):
                frm = 2 * (x ^ fx) + (y ^ fy)
                pltpu.make_async_remote_copy(
                    src_ref=outs[a].at[frm], dst_ref=outs[a].at[frm],
                    send_sem=send_sems.at[3 * a + r], recv_sem=recv_sems.at[3 * a + r],
                    device_id=(x, y, c), device_id_type=MESH).wait_recv()
        for cp in cps:
            cp.wait_send()

    return pl.kernel(
        body, name=name,
        out_type=[jax.ShapeDtypeStruct(p.shape, p.dtype) for p in partials],
        mesh=plsc.ScalarSubcoreMesh(axis_name="sequencer", num_cores=1),
        scratch_types=[pltpu.SemaphoreType.DMA((3 * n,))] * 2,
        compiler_params=pltpu.CompilerParams(collective_id=SCATTER_COLLECTIVE_ID),
    )(*partials)


def _sum_chips_into(landed, partial, buf, layer, n_layers, pos, name):
    _, rh, cols = landed.shape
    tr = _pick(rh, (256, 128, 64, 32, 16))
    nb = rh // tr

    def body(pos_ref, l_ref, own_ref, *rest):
        o_ref = rest[-1]
        my = pos_ref[1]
        own = own_ref[...].astype(F32)
        acc = jnp.where(my == 0, own, l_ref[0].astype(F32))
        for j in range(1, N_CHIPS):
            acc = acc + jnp.where(my == j, own, l_ref[j].astype(F32))
        o_ref[...] = acc

    in_specs = [pl.BlockSpec((N_CHIPS, tr, cols), lambda i, pos_ref: (0, i, 0)),
                pl.BlockSpec((None, tr, cols), lambda i, pos_ref: (pos_ref[1], i, 0))]
    args = [pos, landed, partial]
    aliases = {}
    if buf is not None:
        in_specs.append(pl.BlockSpec(memory_space=pl.ANY))
        args.append(buf)
        aliases = {3: 0}
    return _pcall(
        body, name=name,
        grid_spec=pltpu.PrefetchScalarGridSpec(
            num_scalar_prefetch=1, grid=(nb,), in_specs=in_specs,
            out_specs=pl.BlockSpec((None, tr, cols), lambda i, pos_ref: (layer, pos_ref[0] * nb + i, 0))),
        out_shape=jax.ShapeDtypeStruct((n_layers, 2 * rh, cols), F32), input_output_aliases=aliases,
        compiler_params=_cp(("arbitrary",)),
    )(*args)


def _exchange_final(bufs):
    n = len(bufs)
    HBM = pl.BlockSpec(memory_space=pl.ANY)
    n_layers = [b.shape[0] for b in bufs]
    base = np.concatenate([[0], np.cumsum(n_layers)])

    def body(*refs):
        outs = refs[n:2 * n]
        send_sems, recv_sems = refs[2 * n:]
        x, y, c = _mesh_pos()
        cps = []
        for w in range(n):
            rh = bufs[w].shape[1] // 2
            for l in range(n_layers[w]):
                k = int(base[w]) + l
                mine = outs[w].at[l, pl.ds(c * rh, rh), :]
                cp = pltpu.make_async_remote_copy(
                    src_ref=mine, dst_ref=mine, send_sem=send_sems.at[k], recv_sem=recv_sems.at[k],
                    device_id=(x, y, 1 - c), device_id_type=MESH)
                cp.start()
                cps.append(cp)
        for w in range(n):
            rh = bufs[w].shape[1] // 2
            for l in range(n_layers[w]):
                k = int(base[w]) + l
                other = outs[w].at[l, pl.ds((1 - c) * rh, rh), :]
                pltpu.make_async_remote_copy(
                    src_ref=other, dst_ref=other, send_sem=send_sems.at[k], recv_sem=recv_sems.at[k],
                    device_id=(x, y, c), device_id_type=MESH).wait_recv()
        for cp in cps:
            cp.wait_send()

    return _pcall(
        body, name="grad_exchange_final", out_shape=[jax.ShapeDtypeStruct(b.shape, b.dtype) for b in bufs],
        in_specs=[HBM] * n, out_specs=[HBM] * n, input_output_aliases={i: i for i in range(n)},
        scratch_shapes=[pltpu.SemaphoreType.DMA((int(base[-1]),))] * 2,
        compiler_params=pltpu.CompilerParams(vmem_limit_bytes=VMEM_LIMIT),
    )(*bufs)


def _reduce_scatter_big(fulls, kinds, shard_shapes, groups, stages, pos):
    where = {a: (wi, l) for wi, idxs in enumerate(groups) for l, a in enumerate(idxs)}
    bufs = [None] * len(groups)
    for si, idx in enumerate(stages):
        recv = _swap_halves([fulls[a] for a in idx], [kinds[a] for a in idx], [shard_shapes[a] for a in idx], f"grad_swap_halves_{si}")
        partials = [_add_own_half(fulls[a], r, kinds[a], shard_shapes[a], pos, f"grad_add_sibling_{a}") for a, r in zip(idx, recv)]
        landed = _scatter_partials(partials, f"grad_scatter_partials_{si}")
        for a, part, land in zip(idx, partials, landed):
            wi, l = where[a]
            bufs[wi] = _sum_chips_into(land, part, bufs[wi], l, len(groups[wi]), pos, f"grad_sum_chips_{a}")
    return _exchange_final(bufs)


def _swap_halves_last(z):
    h = z.shape[-1] // 2
    return jnp.concatenate([z[..., h:], z[..., :h]], axis=-1)


def _ext_uq(w):
    lead = w.shape[:-1]
    wh = w.reshape(lead + (-1, QK_DIM))
    rope = wh[..., NOPE:]
    return jnp.concatenate([wh, _swap_halves_last(rope)], axis=-1).reshape(lead + (-1,))


def _fold_uq(d):
    lead = d.shape[:-1]
    dh = d.reshape(lead + (-1, HEAD_W))
    rope = dh[..., NOPE:QK_DIM] + _swap_halves_last(dh[..., QK_DIM:])
    return jnp.concatenate([dh[..., :NOPE], rope], axis=-1).reshape(lead + (-1,))


def _ext_win(w):
    base, kr = w[..., :-ROPE], w[..., -ROPE:]
    ks = _swap_halves_last(kr)
    return jnp.concatenate([base, kr, ks, ks, kr], axis=-1)


def _fold_win(d):
    n = d.shape[-1] - 4 * ROPE
    a, b, c2, e = [d[..., n + i * ROPE:n + (i + 1) * ROPE] for i in range(4)]
    return jnp.concatenate([d[..., :n], a + e + _swap_halves_last(b + c2)], axis=-1)


def _forward(x, wts, tabs):
    S, D = x.shape
    depth = len(wts["mlp_w1"])
    tile = 256
    tab_q, tab_ka, tab_kb = tabs
    first = _rowop("norm_mod_0", _f_norm_mod, n_x=1, n_nd=0, n_p=3, x_w=[D], nd_w=[], nd_shared=[], p_per_group=[False] * 3,
                   out_w=[D], out_dtypes=[BF16], tile=tile)
    xcur = x
    mix = gate = None
    for l in range(depth):
        sh1, sc1, g1, sh2, sc2, g2 = [wts["mod"][l][i].reshape(1, 1, D) for i in range(6)]
        n1 = wts["norm1_g"][l].reshape(1, 1, D)
        n2 = wts["norm2_g"][l].reshape(1, 1, D)
        if l == 0:
            (h,) = first((xcur,), (), (n1, sh1, sc1))
        else:
            op = _rowop(f"resid_norm_mod_a{l}", _f_resid_norm_mod, n_x=2, n_nd=0, n_p=4, x_w=[D, D], nd_w=[], nd_shared=[],
                        p_per_group=[False] * 4, out_w=[D, D], out_dtypes=[F32, BF16], tile=tile)
            xcur, h = op((xcur, mix), (), (gate, n1, sh1, sc1))
        if l % 2 == 0:
            mix = _even_mixer(h, wts, l // 2, tile)
        else:
            mix = _mla_mixer(h, wts, l // 2, tile, tab_q, tab_ka, tab_kb)
        op = _rowop(f"resid_norm_mod_b{l}", _f_resid_norm_mod, n_x=2, n_nd=0, n_p=4, x_w=[D, D], nd_w=[], nd_shared=[],
                    p_per_group=[False] * 4, out_w=[D, D], out_dtypes=[F32, BF16], tile=tile)
        xcur, h = op((xcur, mix), (), (g1, n2, sh2, sc2))
        mix = _mlp(f"mlp_{l}")(h, wts["mlp_w1"][l], wts["mlp_w2"][l])
        gate = g2
    (y,) = _rowop("resid_last", _f_resid, n_x=2, n_nd=0, n_p=1, x_w=[D, D], nd_w=[], nd_shared=[], p_per_group=[False],
                  out_w=[D], out_dtypes=[F32], tile=tile)((xcur, mix), (), (gate,))
    return y


def _even_mixer(h, wts, e, tile):
    proj = _linear(f"ab_in_{e}")(h, wts["ab_w_in"][e])
    da = proj.shape[1] // 4
    u, v, a, g = [proj[:, i * da:(i + 1) * da] for i in range(4)]
    ng = da // GROUP
    sgu = _rowop(f"sgu_{e}", _f_sgu, n_x=2, n_nd=0, n_p=3, x_w=[GROUP, GROUP], nd_w=[], nd_shared=[], p_per_group=[True] * 3,
                 out_w=[GROUP], out_dtypes=[BF16], tile=CHUNK, groups=ng)
    bexp = jnp.broadcast_to(wts["sgu_b"][e][:, :, None], (ng, CHUNK, GROUP))
    (out_a,) = sgu((u, v), (), (wts["sgu_norm_g"][e].reshape(ng, 1, GROUP), wts["sgu_w"][e], bexp))
    (yglu,) = _rowop(f"glu_{e}", _f_glu, n_x=2, n_nd=0, n_p=0, x_w=[da, da], nd_w=[], nd_shared=[], p_per_group=[],
                     out_w=[da], out_dtypes=[F32], tile=tile)((a, g), (), ())
    z = _conv_op(f"conv_{e}")(yglu, wts["conv_w"][e], wts["conv_b"][e].reshape(1, da))
    (out_b,) = _rowop(f"ln_silu_{e}", _f_ln_silu, n_x=1, n_nd=0, n_p=2, x_w=[da], nd_w=[], nd_shared=[], p_per_group=[False] * 2,
                      out_w=[da], out_dtypes=[BF16], tile=tile)(
        (z,), (), (wts["conv_ln_g"][e].reshape(1, 1, da), wts["conv_ln_b"][e].reshape(1, 1, da)))
    return _linear(f"ab_out_{e}")(jnp.concatenate([out_a, out_b], axis=-1), wts["ab_w_out"][e])


def _mla_mixer(h, wts, o, tile, tab_q, tab_ka, tab_kb):
    proj = _linear(f"mla_in_{o}")(h, wts["mla_w_in"][o])
    rank = (proj.shape[1] - 4 * ROPE) // 2
    c_q, c_kv = proj[:, :rank], proj[:, rank:2 * rank]
    kr_a, kr_b = proj[:, 2 * rank:2 * rank + 2 * ROPE], proj[:, 2 * rank + 2 * ROPE:]

    def rms(name, xx, gg):
        return _rowop(name, _f_rms, n_x=1, n_nd=0, n_p=1, x_w=[rank], nd_w=[], nd_shared=[], p_per_group=[False],
                      out_w=[rank], out_dtypes=[BF16], tile=tile)((xx,), (), (gg.reshape(1, 1, rank),))[0]

    q_raw = _linear(f"mla_uq_{o}")(rms(f"rms_q_{o}", c_q, wts["mla_q_norm_g"][o]), wts["mla_w_uq"][o])
    kv_raw = _linear(f"mla_ukv_{o}")(rms(f"rms_kv_{o}", c_kv, wts["mla_kv_norm_g"][o]), wts["mla_w_ukv"][o])
    gq, gk = wts["mla_q_head_g"][o], wts["mla_k_head_g"][o]
    gk_rope = gk[NOPE:]
    gq_ext = jnp.concatenate([gq, _swap_halves_last(gq[NOPE:])]).reshape(1, 1, HEAD_W)
    gk_ext = jnp.concatenate([gk[:NOPE], jnp.ones((HEAD_W - NOPE,), F32)]).reshape(1, 1, HEAD_W)
    gk_a = jnp.concatenate([gk_rope, _swap_halves_last(gk_rope)]).reshape(1, 1, 2 * ROPE)
    gk_b = jnp.concatenate([_swap_halves_last(gk_rope), gk_rope]).reshape(1, 1, 2 * ROPE)
    head_tile = 4 * tile
    (q,) = _rowop(f"q_head_{o}", _f_qhead, n_x=1, n_nd=1, n_p=1, x_w=[HEAD_W], nd_w=[HEAD_W], nd_shared=[True], p_per_group=[False],
                  out_w=[HEAD_W], out_dtypes=[BF16], tile=head_tile, groups=N_HEADS)((q_raw,), (tab_q,), (gq_ext,))
    (kv,) = _rowop(f"kv_head_{o}", _f_kvhead, n_x=1, n_nd=0, n_p=1, x_w=[HEAD_W], nd_w=[], nd_shared=[], p_per_group=[False],
                   out_w=[HEAD_W], out_dtypes=[BF16], tile=head_tile, groups=N_HEADS)((kv_raw,), (), (gk_ext,))
    (kr,) = _rowop(f"k_rope_{o}", _f_krope, n_x=2, n_nd=2, n_p=2, x_w=[2 * ROPE] * 2, nd_w=[2 * ROPE] * 2, nd_shared=[True] * 2,
                   p_per_group=[False] * 2, out_w=[2 * ROPE], out_dtypes=[BF16], tile=tile)((kr_a, kr_b), (tab_ka, tab_kb), (gk_a, gk_b))
    att = _attn_op(f"attn_{o}")(q, kv, kr)
    return _linear(f"mla_out_{o}")(att, wts["mla_w_out"][o])


def _rope_tabs(S):
    pos = jnp.arange(S, dtype=F32)
    inv = ROPE_THETA ** (-jnp.arange(0, ROPE, 2, dtype=F32) / ROPE)
    ang = pos[:, None] * inv[None, :]
    cos, sin = jnp.cos(ang), jnp.sin(ang)
    cc = jnp.concatenate([cos, cos], axis=-1)
    sg = jnp.concatenate([-sin, sin], axis=-1)
    tab_q = jnp.concatenate([jnp.ones((S, NOPE), F32), cc, sg], axis=-1)
    return tab_q, jnp.concatenate([cc, sg], axis=-1), jnp.concatenate([sg, cc], axis=-1)


def kernel(x, c, norm1_g, norm2_g, ada_w, ada_b, mlp_w1, mlp_w2, ab_w_in, sgu_norm_g, sgu_w, sgu_b, conv_w, conv_b, conv_ln_g, conv_ln_b, ab_w_out, mla_w_in, mla_q_norm_g, mla_kv_norm_g, mla_w_uq, mla_w_ukv, mla_q_head_g, mla_k_head_g, mla_w_out, loss_target, m_norm1_g, m_norm2_g, m_ada_w, m_ada_b, m_mlp_w1, m_mlp_w2, m_ab_w_in, m_sgu_norm_g, m_sgu_w, m_sgu_b, m_conv_w, m_conv_b, m_conv_ln_g, m_conv_ln_b, m_ab_w_out, m_mla_w_in, m_mla_q_norm_g, m_mla_kv_norm_g, m_mla_w_uq, m_mla_w_ukv, m_mla_q_head_g, m_mla_k_head_g, m_mla_w_out, v_norm1_g, v_norm2_g, v_ada_w, v_ada_b, v_mlp_w1, v_mlp_w2, v_ab_w_in, v_sgu_norm_g, v_sgu_w, v_sgu_b, v_conv_w, v_conv_b, v_conv_ln_g, v_conv_ln_b, v_ab_w_out, v_mla_w_in, v_mla_q_norm_g, v_mla_kv_norm_g, v_mla_w_uq, v_mla_w_ukv, v_mla_q_head_g, v_mla_k_head_g, v_mla_w_out):
    names = ["norm1_g", "norm2_g", "ada_w", "ada_b", "mlp_w1", "mlp_w2", "ab_w_in", "sgu_norm_g", "sgu_w", "sgu_b", "conv_w",
             "conv_b", "conv_ln_g", "conv_ln_b", "ab_w_out", "mla_w_in", "mla_q_norm_g", "mla_kv_norm_g", "mla_w_uq", "mla_w_ukv",
             "mla_q_head_g", "mla_k_head_g", "mla_w_out"]
    W = dict(zip(names, [norm1_g, norm2_g, ada_w, ada_b, mlp_w1, mlp_w2, ab_w_in, sgu_norm_g, sgu_w, sgu_b, conv_w, conv_b, conv_ln_g,
                         conv_ln_b, ab_w_out, mla_w_in, mla_q_norm_g, mla_kv_norm_g, mla_w_uq, mla_w_ukv, mla_q_head_g, mla_k_head_g,
                         mla_w_out]))
    M = dict(zip(names, [m_norm1_g, m_norm2_g, m_ada_w, m_ada_b, m_mlp_w1, m_mlp_w2, m_ab_w_in, m_sgu_norm_g, m_sgu_w, m_sgu_b, m_conv_w,
                         m_conv_b, m_conv_ln_g, m_conv_ln_b, m_ab_w_out, m_mla_w_in, m_mla_q_norm_g, m_mla_kv_norm_g, m_mla_w_uq,
                         m_mla_w_ukv, m_mla_q_head_g, m_mla_k_head_g, m_mla_w_out]))
    V = dict(zip(names, [v_norm1_g, v_norm2_g, v_ada_w, v_ada_b, v_mlp_w1, v_mlp_w2, v_ab_w_in, v_sgu_norm_g, v_sgu_w, v_sgu_b, v_conv_w,
                         v_conv_b, v_conv_ln_g, v_conv_ln_b, v_ab_w_out, v_mla_w_in, v_mla_q_norm_g, v_mla_kv_norm_g, v_mla_w_uq,
                         v_mla_w_ukv, v_mla_q_head_g, v_mla_k_head_g, v_mla_w_out]))
    xi, yi, ci = lax.axis_index("x"), lax.axis_index("y"), lax.axis_index("c")
    chip = 2 * xi + yi
    dev = 2 * chip + ci
    pos = jnp.stack([ci, chip]).astype(jnp.int32)
    S, D = x.shape[1], x.shape[2]
    depth = norm1_g.shape[0]

    c_all, conv_w_all, qn_all, kvn_all = _all_gather_small([c, conv_w, mla_q_norm_g, mla_kv_norm_g], "gather_small_inputs")
    c_all = c_all.reshape(N_DEV, D)
    by_chip = lambda a: jnp.concatenate([a[2 * j] for j in range(N_CHIPS)], axis=-1)
    conv_w_full, qn_full, kvn_full = by_chip(conv_w_all), by_chip(qn_all), by_chip(kvn_all)

    (c_act,) = _rowop("silu_c", _f_silu, n_x=1, n_nd=0, n_p=0, x_w=[D], nd_w=[], nd_shared=[], p_per_group=[], out_w=[D],
                      out_dtypes=[F32], tile=N_DEV)((c_all,), (), ())
    c_act_pad = jnp.pad(c_act, ((0, 128 - N_DEV), (0, 0)))
    mod_cols = jnp.stack([_matmul(c_act_pad, ada_w[l], name=f"ada_fwd_{l}")[:N_DEV] for l in range(depth)])
    (mod_all,) = _all_gather_small([mod_cols], "gather_mod")
    mod_mine = jnp.concatenate([lax.dynamic_index_in_dim(mod_all[2 * j], dev, axis=1, keepdims=False) for j in range(N_CHIPS)], axis=-1)
    mod_mine = (mod_mine + ada_b).reshape(depth, 6, D)

    big = {"mlp_w1": "col", "mlp_w2": "row", "ab_w_in": "col", "ab_w_out": "row", "mla_w_in": "row", "mla_w_uq": "col",
           "mla_w_ukv": "col", "mla_w_out": "row"}
    src = dict(W)
    src["mla_w_in"] = _ext_win(mla_w_in)
    src["mla_w_uq"] = _ext_uq(mla_w_uq)
    shards, kinds, owner = [], [], []
    for nme, kind in big.items():
        for l in range(src[nme].shape[0]):
            shards.append(src[nme][l].astype(BF16))
            kinds.append(kind)
            owner.append((nme, l))
    def stage_of(nme, l):
        return 2 * l + 1 if nme.startswith("mlp") else (4 * l if nme.startswith("ab") else 4 * l + 2)

    fwd_stages = [[a for a, (nme, l) in enumerate(owner) if stage_of(nme, l) == st] for st in range(2 * depth)]
    fulls = [None] * len(shards)
    for st, idx in enumerate(fwd_stages):
        got = _gather_big([shards[a] for a in idx], [kinds[a] for a in idx], f"gather_weights_{st}")
        for a, f in zip(idx, got):
            fulls[a] = f
    wts = {nme: [None] * src[nme].shape[0] for nme in big}
    for (nme, l), f in zip(owner, fulls):
        wts[nme][l] = f
    wts.update(norm1_g=norm1_g, norm2_g=norm2_g, mod=mod_mine, sgu_norm_g=sgu_norm_g, sgu_w=sgu_w, sgu_b=sgu_b, conv_w=conv_w_full,
               conv_b=conv_b, conv_ln_g=conv_ln_g, conv_ln_b=conv_ln_b, mla_q_norm_g=qn_full, mla_kv_norm_g=kvn_full,
               mla_q_head_g=mla_q_head_g, mla_k_head_g=mla_k_head_g)

    tabs = _rope_tabs(S)
    y, vjp = jax.vjp(lambda xx, ww: _forward(xx, ww, tabs), x[0], wts)
    dy, loss_mine = _loss_call(y, loss_target[0])
    dx, dw = vjp(dy)
    loss = lax.psum(loss_mine[0, 0], ("x", "y", "c"))

    dmod = dw["mod"].reshape(depth, 6 * D)
    small_names = ["norm1_g", "norm2_g", "sgu_norm_g", "sgu_w", "sgu_b", "conv_w", "conv_b", "conv_ln_g", "conv_ln_b", "mla_q_norm_g",
                   "mla_kv_norm_g", "mla_q_head_g", "mla_k_head_g"]
    red = _all_reduce_small([dmod] + [dw[nme] for nme in small_names], "reduce_small_grads")
    G = dict(zip(["ada_b"] + small_names, red))
    own_cols = lambda a: lax.dynamic_slice_in_dim(a, chip * (a.shape[-1] // N_CHIPS), a.shape[-1] // N_CHIPS, axis=-1)
    for nme in ("conv_w", "mla_q_norm_g", "mla_kv_norm_g"):
        G[nme] = own_cols(G[nme])

    (dmod_all,) = _all_gather_small([dmod], "gather_dmod")
    dmod_cols = own_cols(dmod_all)
    dmod_pad = jnp.pad(dmod_cols, ((0, 128 - N_DEV), (0, 0), (0, 0)))
    G["ada_w"] = jnp.stack([_matmul(c_act_pad, dmod_pad[:, l], ta=True, name=f"ada_wgrad_{l}") for l in range(depth)])

    gfull = [dw[nme][l] for nme, l in owner]
    groups, a0 = [], 0
    for nme in big:
        groups.append(list(range(a0, a0 + src[nme].shape[0])))
        a0 += src[nme].shape[0]
    gsh = _reduce_scatter_big(gfull, kinds, [s.shape for s in shards], groups, fwd_stages[::-1], pos)
    for nme, g in zip(big, gsh):
        G[nme] = g
    G["mla_w_in"] = _fold_win(G["mla_w_in"])
    G["mla_w_uq"] = _fold_uq(G["mla_w_uq"])

    deltas, new_m, new_v = [], [], []
    for nme in names:
        d, mn, vn = _adamw_call(W[nme], G[nme], M[nme], V[nme], f"adamw_{nme}")
        deltas.append(d)
        new_m.append(mn)
        new_v.append(vn)
    return (loss, dx[None], *[G[nme] for nme in names], *deltas, *new_m, *new_v)
```

```python
import functools
import numpy as np
import jax
import jax.numpy as jnp
from jax import lax
from jax.experimental import pallas as pl
from jax.experimental.pallas import tpu as pltpu
from jax.experimental.pallas import tpu_sc as plsc

F32 = jnp.float32
BF16 = jnp.bfloat16
MESH = pl.DeviceIdType.MESH

EPS = 1e-6
N_HEADS = 16
NOPE = 128
ROPE = 64
VDIM = 128
QK_DIM = NOPE + ROPE
HEAD_W = 256
CHUNK = 128
GROUP = 128
CONV_W = 31
CONV_PAD = 32
ROPE_THETA = 10000.0
ATTN_SCALE = QK_DIM ** -0.5
ADAM_LR, ADAM_B1, ADAM_B2, ADAM_EPS, ADAM_WD, ADAM_STEP = 0.001, 0.9, 0.999, 1e-08, 0.01, 10
N_CHIPS = 4
N_DEV = 8
VMEM_LIMIT = 56 * 1024 * 1024
SMALL_COLS = 1024
CHIP_RELS = ((1, 0), (0, 1), (1, 1))
GATHER_COLLECTIVE_ID = 0
SWAP_COLLECTIVE_ID = 1
SCATTER_COLLECTIVE_ID = 2


def _pcall(body, **kw):
    return pl.pallas_call(body, **kw)


def _cp(sem=None, **kw):
    return pltpu.CompilerParams(dimension_semantics=sem, vmem_limit_bytes=VMEM_LIMIT, **kw)


def _pick(n, cands):
    for c in cands:
        if n % c == 0:
            return c
    return n


def _matmul(a, b, *, ta=False, tb=False, out_dtype=F32, name, extra=(), epilogue=None, out_dtypes=None):
    if ta:
        K, M = a.shape
    else:
        M, K = a.shape
    if tb:
        N, K2 = b.shape
    else:
        K2, N = b.shape
    assert K == K2, (a.shape, b.shape, ta, tb)
    tm = _pick(M, (1024, 512, 256, 128))
    tn = _pick(N, (1024, 512, 256, 128))
    tk = _pick(K, (2048, 1024, 512, 256, 128))
    nk = K // tk
    dn = (((0 if ta else 1,), (1 if tb else 0,)), ((), ()))

    n_extra = len(extra)
    single = epilogue is None
    if single:
        out_dtypes = [out_dtype]

    def body(a_ref, b_ref, *rest):
        extra_refs, o_refs, acc_ref = rest[:n_extra], rest[n_extra:-1], rest[-1]
        k = pl.program_id(2)

        @pl.when(k == 0)
        def _():
            acc_ref[...] = jnp.zeros_like(acc_ref)

        acc_ref[...] += lax.dot_general(a_ref[...].astype(BF16), b_ref[...].astype(BF16), dn,
                                        preferred_element_type=F32)

        @pl.when(k == nk - 1)
        def _():
            acc = acc_ref[...]
            res = (acc,) if single else epilogue(acc, *[r[...] for r in extra_refs])
            for o_ref, val in zip(o_refs, res):
                o_ref[...] = val.astype(o_ref.dtype)

    a_spec = pl.BlockSpec((tk, tm), lambda i, j, k: (k, i)) if ta else pl.BlockSpec((tm, tk), lambda i, j, k: (i, k))
    b_spec = pl.BlockSpec((tn, tk), lambda i, j, k: (j, k)) if tb else pl.BlockSpec((tk, tn), lambda i, j, k: (k, j))
    mn_spec = pl.BlockSpec((tm, tn), lambda i, j, k: (i, j))
    outs = _pcall(
        body, name=name, grid=(M // tm, N // tn, nk), in_specs=[a_spec, b_spec] + [mn_spec] * n_extra,
        out_specs=[mn_spec] * len(out_dtypes),
        out_shape=[jax.ShapeDtypeStruct((M, N), d) for d in out_dtypes],
        scratch_shapes=[pltpu.VMEM((tm, tn), F32)],
        compiler_params=_cp(("parallel", "parallel", "arbitrary")),
    )(a, b, *extra)
    return outs[0] if single else outs


def _linear(name):
    @jax.custom_vjp
    def mm(a, w):
        return _matmul(a, w, name=name + "_fwd")

    def fwd(a, w):
        return mm(a, w), (a, w)

    def bwd(res, dy):
        a, w = res
        da = _matmul(dy, w, tb=True, out_dtype=a.dtype, name=name + "_dgrad")
        dw = _matmul(a, dy, ta=True, out_dtype=w.dtype, name=name + "_wgrad")
        return da, dw

    mm.defvjp(fwd, bwd)
    return mm


def _relu2_epilogue(acc):
    return acc, jnp.square(jnp.maximum(acc, 0.0))


def _relu2_grad_epilogue(acc, h1):
    return (acc * (2.0 * jnp.maximum(h1.astype(F32), 0.0)),)


def _mlp(name):
    def run(h, w1, w2):
        h1, act = _matmul(h, w1, name=name + "_up_fwd", epilogue=_relu2_epilogue, out_dtypes=[BF16, BF16])
        return _matmul(act, w2, name=name + "_down_fwd"), h1, act

    @jax.custom_vjp
    def mlp(h, w1, w2):
        return run(h, w1, w2)[0]

    def fwd(h, w1, w2):
        y, h1, act = run(h, w1, w2)
        return y, (h, w1, w2, h1, act)

    def bwd(res, dy):
        h, w1, w2, h1, act = res
        dw2 = _matmul(act, dy, ta=True, out_dtype=w2.dtype, name=name + "_down_wgrad")
        (dh1,) = _matmul(dy, w2, tb=True, name=name + "_down_dgrad", extra=(h1,), epilogue=_relu2_grad_epilogue, out_dtypes=[BF16])
        dw1 = _matmul(h, dh1, ta=True, out_dtype=w1.dtype, name=name + "_up_wgrad")
        dh = _matmul(dh1, w1, tb=True, out_dtype=h.dtype, name=name + "_up_dgrad")
        return dh, dw1, dw2

    mlp.defvjp(fwd, bwd)
    return mlp


def _rowop(name, f, *, n_x, n_nd, n_p, x_w, nd_w, nd_shared, p_per_group, out_w, out_dtypes, tile, groups=1):
    G = groups

    def specs(S):
        t = min(tile, S)
        xs = [pl.BlockSpec((t, w), lambda g, r: (r, g)) for w in x_w]
        nds = [pl.BlockSpec((t, w), (lambda g, r: (r, 0)) if sh else (lambda g, r: (r, g))) for w, sh in zip(nd_w, nd_shared)]
        outs = [pl.BlockSpec((t, w), lambda g, r: (r, g)) for w in out_w]
        return t, xs, nds, outs

    def pspecs(ps):
        return [pl.BlockSpec((None,) + p.shape[1:], (lambda g, r: (g, 0, 0)) if pg else (lambda g, r: (0, 0, 0)))
                for p, pg in zip(ps, p_per_group)]

    def fwd_call(xs, nds, ps):
        S = xs[0].shape[0]
        t, xsp, ndsp, osp = specs(S)

        def body(*refs):
            ins, outs = refs[:n_x + n_nd + n_p], refs[n_x + n_nd + n_p:]
            vals = [r[...].astype(F32) for r in ins]
            res = f(*vals)
            for o, r in zip(res, outs):
                r[...] = o.astype(r.dtype)

        return _pcall(
            body, name=name + "_fwd", grid=(G, S // t), in_specs=xsp + ndsp + pspecs(ps), out_specs=osp,
            out_shape=[jax.ShapeDtypeStruct((S, G * w), d) for w, d in zip(out_w, out_dtypes)],
            compiler_params=_cp(("parallel", "parallel")),
        )(*xs, *nds, *ps)

    def bwd_call(xs, nds, ps, douts):
        S = xs[0].shape[0]
        t, xsp, ndsp, osp = specs(S)
        n_in = n_x + n_nd + n_p + len(out_w)

        def body(*refs):
            ins, outs = refs[:n_in], refs[n_in:]
            xv = [r[...].astype(F32) for r in ins[:n_x]]
            ndv = [r[...].astype(F32) for r in ins[n_x:n_x + n_nd]]
            pv = [r[...].astype(F32) for r in ins[n_x + n_nd:n_x + n_nd + n_p]]
            dov = tuple(r[...].astype(F32) for r in ins[n_x + n_nd + n_p:])
            _, vjp = jax.vjp(lambda *a: tuple(f(*a[:n_x], *ndv, *a[n_x:])), *xv, *pv)
            cts = vjp(dov)
            for i in range(n_x):
                outs[i][...] = cts[i].astype(outs[i].dtype)
            g, r = pl.program_id(0), pl.program_id(1)
            for i in range(n_p):
                first = (r == 0) if p_per_group[i] else jnp.logical_and(g == 0, r == 0)
                ref, ct = outs[n_x + i], cts[n_x + i]

                @pl.when(first)
                def _(ref=ref, ct=ct):
                    ref[...] = ct

                @pl.when(jnp.logical_not(first))
                def _(ref=ref, ct=ct):
                    ref[...] += ct

        return _pcall(
            body, name=name + "_bwd", grid=(G, S // t), in_specs=xsp + ndsp + pspecs(ps) + osp,
            out_specs=xsp + pspecs(ps),
            out_shape=[jax.ShapeDtypeStruct(x.shape, x.dtype) for x in xs] + [jax.ShapeDtypeStruct(p.shape, F32) for p in ps],
            compiler_params=_cp(("arbitrary", "arbitrary")),
        )(*xs, *nds, *ps, *douts)

    @jax.custom_vjp
    def op(xs, nds, ps):
        return tuple(fwd_call(xs, nds, ps))

    def op_fwd(xs, nds, ps):
        return op(xs, nds, ps), (xs, nds, ps)

    def op_bwd(res, douts):
        xs, nds, ps = res
        out = bwd_call(xs, nds, ps, douts)
        return tuple(out[:n_x]), tuple(jnp.zeros_like(n) for n in nds), tuple(out[n_x:])

    op.defvjp(op_fwd, op_bwd)
    return op


def _rms_rows(x):
    return x * lax.rsqrt(jnp.mean(x * x, axis=-1, keepdims=True) + EPS)


def _f_norm_mod(x, g, shift, scale):
    return ((_rms_rows(x) * g) * (1.0 + scale) + shift,)


def _f_resid_norm_mod(x, mix, gate, g, shift, scale):
    xn = x + gate * mix
    return xn, (_rms_rows(xn) * g) * (1.0 + scale) + shift


def _f_resid(x, mix, gate):
    return (x + gate * mix,)


def _f_rms(x, g):
    return (_rms_rows(x) * g,)


def _f_glu(a, g):
    return (a * jax.nn.sigmoid(g),)


def _f_ln_silu(z, g, b):
    mu = jnp.mean(z, axis=-1, keepdims=True)
    zc = z - mu
    var = jnp.mean(zc * zc, axis=-1, keepdims=True)
    y = zc * lax.rsqrt(var + EPS) * g + b
    return (y * jax.nn.sigmoid(y),)


def _f_silu(x):
    return (x * jax.nn.sigmoid(x),)


@jax.custom_vjp
def _bdot(a, b):
    return jnp.dot(a.astype(BF16), b.astype(BF16), preferred_element_type=F32)


def _bdot_fwd(a, b):
    return _bdot(a, b), (a, b)


def _bdot_bwd(res, ct):
    a, b = res
    c16 = ct.astype(BF16)
    da = lax.dot_general(c16, b.astype(BF16), (((1,), (1,)), ((), ())), preferred_element_type=F32)
    db = lax.dot_general(a.astype(BF16), c16, (((0,), (0,)), ((), ())), preferred_element_type=F32)
    return da, db


_bdot.defvjp(_bdot_fwd, _bdot_bwd)


def _f_sgu(u, v, ng, w, bexp):
    vn = _rms_rows(jax.nn.gelu(v)) * ng
    row = lax.broadcasted_iota(jnp.int32, w.shape, 0)
    col = lax.broadcasted_iota(jnp.int32, w.shape, 1)
    mixed = _bdot(jnp.where(row >= col, w, 0.0), vn) + bexp
    return (jax.nn.gelu(u) * mixed,)


def _lo_mask():
    return lax.broadcasted_iota(jnp.int32, (1, HEAD_W), 1) < NOPE


def _f_qhead(x, tab, g):
    lo = _lo_mask()
    x2 = x * x
    ms_lo = jnp.sum(jnp.where(lo, x2, 0.0), axis=-1, keepdims=True) * (1.0 / NOPE)
    ms_hi = jnp.sum(jnp.where(lo, 0.0, x2), axis=-1, keepdims=True) * (1.0 / (HEAD_W - NOPE))
    r = jnp.where(lo, lax.rsqrt(ms_lo + EPS), lax.rsqrt(ms_hi + EPS))
    return (((x * r) * g) * tab,)


def _f_kvhead(x, g):
    lo = _lo_mask()
    ms = jnp.sum(jnp.where(lo, x * x, 0.0), axis=-1, keepdims=True) * (1.0 / NOPE)
    return (jnp.where(lo, (x * lax.rsqrt(ms + EPS)) * g, x),)


def _f_krope(a, b, ta, tb, ga, gb):
    r = lax.rsqrt(jnp.mean(a * a, axis=-1, keepdims=True) + EPS)
    return (((a * r) * ga) * ta + ((b * r) * gb) * tb,)


def _conv_fwd_call(y, w, b, name):
    S, C = y.shape
    cw = 128
    rt = _pick(S, (128,))
    w = jnp.pad(w, ((0, CONV_PAD - CONV_W), (0, 0)))

    def body(y_ref, w_ref, b_ref, z_ref, pad_ref):
        pad_ref[pl.ds(0, CONV_PAD), :] = jnp.zeros((CONV_PAD, cw), F32)
        pad_ref[pl.ds(CONV_PAD, S), :] = y_ref[...]
        wv = w_ref[...]
        bv = b_ref[...]

        def chunk(ci, carry):
            r0 = pl.multiple_of(ci * rt, rt)
            win = pad_ref[pl.ds(r0, rt + CONV_PAD), :]
            acc = jnp.broadcast_to(bv, (rt, cw))
            for k in range(CONV_W):
                off = CONV_PAD - (CONV_W - 1) + k
                sh = win if off == 0 else pltpu.roll(win, rt + CONV_PAD - off, axis=0)
                acc = acc + wv[k:k + 1, :] * sh[:rt, :]
            z_ref[pl.ds(r0, rt), :] = acc
            return carry

        lax.fori_loop(0, S // rt, chunk, 0)

    return _pcall(
        body, name=name, grid=(C // cw,),
        in_specs=[pl.BlockSpec((S, cw), lambda j: (0, j)), pl.BlockSpec((CONV_PAD, cw), lambda j: (0, j)),
                  pl.BlockSpec((1, cw), lambda j: (0, j))],
        out_specs=pl.BlockSpec((S, cw), lambda j: (0, j)),
        out_shape=jax.ShapeDtypeStruct((S, C), F32),
        scratch_shapes=[pltpu.VMEM((S + CONV_PAD, cw), F32)],
        compiler_params=_cp(("parallel",)),
    )(y, w, b)


def _conv_bwd_call(y, w, dz, name):
    S, C = y.shape
    cw = 128
    rt = _pick(S, (128,))
    w = jnp.pad(w, ((0, CONV_PAD - CONV_W), (0, 0)))

    def body(y_ref, w_ref, dz_ref, dy_ref, dw_ref, db_ref, ypad_ref, zpad_ref):
        ypad_ref[pl.ds(0, CONV_PAD), :] = jnp.zeros((CONV_PAD, cw), F32)
        ypad_ref[pl.ds(CONV_PAD, S), :] = y_ref[...]
        zpad_ref[pl.ds(0, S), :] = dz_ref[...]
        zpad_ref[pl.ds(S, CONV_PAD), :] = jnp.zeros((CONV_PAD, cw), F32)
        dw_ref[...] = jnp.zeros_like(dw_ref)
        wv = w_ref[...]

        def chunk(ci, dbacc):
            r0 = pl.multiple_of(ci * rt, rt)
            ywin = ypad_ref[pl.ds(r0, rt + CONV_PAD), :]
            zwin = zpad_ref[pl.ds(r0, rt + CONV_PAD), :]
            dzc = zwin[:rt, :]
            acc = jnp.zeros((rt, cw), F32)
            for k in range(CONV_W):
                off_z = (CONV_W - 1) - k
                zs = zwin if off_z == 0 else pltpu.roll(zwin, rt + CONV_PAD - off_z, axis=0)
                acc = acc + wv[k:k + 1, :] * zs[:rt, :]
                off_y = CONV_PAD - (CONV_W - 1) + k
                ys = pltpu.roll(ywin, rt + CONV_PAD - off_y, axis=0)
                dw_ref[k:k + 1, :] += jnp.sum(dzc * ys[:rt, :], axis=0, keepdims=True)
            dy_ref[pl.ds(r0, rt), :] = acc
            return dbacc + jnp.sum(dzc, axis=0, keepdims=True)

        db_ref[...] = lax.fori_loop(0, S // rt, chunk, jnp.zeros((1, cw), F32))

    dy, dw, db = _pcall(
        body, name=name, grid=(C // cw,),
        in_specs=[pl.BlockSpec((S, cw), lambda j: (0, j)), pl.BlockSpec((CONV_PAD, cw), lambda j: (0, j)),
                  pl.BlockSpec((S, cw), lambda j: (0, j))],
        out_specs=[pl.BlockSpec((S, cw), lambda j: (0, j)), pl.BlockSpec((CONV_PAD, cw), lambda j: (0, j)),
                   pl.BlockSpec((1, cw), lambda j: (0, j))],
        out_shape=[jax.ShapeDtypeStruct((S, C), F32), jax.ShapeDtypeStruct((CONV_PAD, C), F32),
                   jax.ShapeDtypeStruct((1, C), F32)],
        scratch_shapes=[pltpu.VMEM((S + CONV_PAD, cw), F32), pltpu.VMEM((S + CONV_PAD, cw), F32)],
        compiler_params=_cp(("parallel",)),
    )(y, w, dz)
    return dy, dw[:CONV_W], db


def _conv_op(name):
    @jax.custom_vjp
    def conv(y, w, b):
        return _conv_fwd_call(y, w, b, name + "_fwd")

    def fwd(y, w, b):
        return conv(y, w, b), (y, w)

    def bwd(res, dz):
        y, w = res
        return _conv_bwd_call(y, w, dz, name + "_bwd")

    conv.defvjp(fwd, bwd)
    return conv


def _attn_tile(S):
    return _pick(S, (512, 256, 128))


def _attn_fwd_call(q, kv, kr, name):
    S = q.shape[0]
    t = _attn_tile(S)

    def body(q_ref, kv_ref, kr_ref, o_ref, lse_ref):
        i = pl.program_id(1)
        qv = q_ref[...]

        def update(jb, carry, diagonal):
            m, l, acc = carry
            off = pl.multiple_of(jb * t, t)
            kc = jnp.concatenate([kv_ref[pl.ds(off, t), pl.ds(0, NOPE)], kr_ref[pl.ds(off, t), :]], axis=-1)
            vv = kv_ref[pl.ds(off, t), pl.ds(NOPE, VDIM)]
            s = lax.dot_general(qv, kc, (((1,), (1,)), ((), ())), preferred_element_type=F32) * ATTN_SCALE
            if diagonal:
                row = lax.broadcasted_iota(jnp.int32, (t, t), 0)
                col = lax.broadcasted_iota(jnp.int32, (t, t), 1)
                s = jnp.where(col <= row, s, -jnp.inf)
            mn = jnp.maximum(m, jnp.max(s, axis=-1, keepdims=True))
            p = jnp.exp(s - mn)
            al = jnp.exp(m - mn)
            l = al * l + jnp.sum(p, axis=-1, keepdims=True)
            acc = al * acc + jnp.dot(p.astype(BF16), vv, preferred_element_type=F32)
            return mn, l, acc

        init = (jnp.full((t, 1), -jnp.inf, F32), jnp.zeros((t, 1), F32), jnp.zeros((t, VDIM), F32))
        carry = lax.fori_loop(0, i, lambda jb, cr: update(jb, cr, False), init)
        m, l, acc = update(i, carry, True)
        o_ref[...] = (acc / l).astype(o_ref.dtype)
        lse_ref[...] = m + jnp.log(l)

    return _pcall(
        body, name=name, grid=(N_HEADS, S // t),
        in_specs=[pl.BlockSpec((t, HEAD_W), lambda h, i: (i, h)), pl.BlockSpec((S, HEAD_W), lambda h, i: (0, h)),
                  pl.BlockSpec((S, 128), lambda h, i: (0, 0))],
        out_specs=[pl.BlockSpec((t, VDIM), lambda h, i: (i, h)), pl.BlockSpec((None, t, 1), lambda h, i: (h, i, 0))],
        out_shape=[jax.ShapeDtypeStruct((S, N_HEADS * VDIM), BF16), jax.ShapeDtypeStruct((N_HEADS, S, 1), F32)],
        compiler_params=_cp(("parallel", "parallel")),
    )(q, kv, kr)


def _attn_dd_call(o, do, name):
    S = o.shape[0]
    t = _attn_tile(S)

    def body(o_ref, do_ref, dd_ref):
        dd_ref[...] = jnp.sum(do_ref[...].astype(F32) * o_ref[...].astype(F32), axis=-1, keepdims=True)

    return _pcall(
        body, name=name, grid=(N_HEADS, S // t),
        in_specs=[pl.BlockSpec((t, VDIM), lambda h, i: (i, h)), pl.BlockSpec((t, VDIM), lambda h, i: (i, h))],
        out_specs=pl.BlockSpec((None, t, 1), lambda h, i: (h, i, 0)),
        out_shape=jax.ShapeDtypeStruct((N_HEADS, S, 1), F32), compiler_params=_cp(("parallel", "parallel")),
    )(o, do)


def _attn_bwd_call(q, kv, kr, do, lse, dd, name):
    S = q.shape[0]
    t = _attn_tile(S)
    nq = S // t

    def body(q_ref, kv_ref, kr_ref, do_ref, lse_ref, dd_ref, dq_ref, dkv_ref, dkr_ref, dq_acc):
        j = pl.program_id(1)

        @pl.when(j == 0)
        def _():
            dq_acc[...] = jnp.zeros_like(dq_acc)

        kc = jnp.concatenate([kv_ref[:, pl.ds(0, NOPE)], kr_ref[...]], axis=-1)
        vv = kv_ref[:, pl.ds(NOPE, VDIM)]

        def update(ib, carry, diagonal):
            dkc, dv = carry
            off = pl.multiple_of(ib * t, t)
            qv = q_ref[pl.ds(off, t), :]
            dov = do_ref[pl.ds(off, t), :]
            st = lax.dot_general(kc, qv, (((1,), (1,)), ((), ())), preferred_element_type=F32) * ATTN_SCALE
            if diagonal:
                key = lax.broadcasted_iota(jnp.int32, (t, t), 0)
                qry = lax.broadcasted_iota(jnp.int32, (t, t), 1)
                st = jnp.where(key <= qry, st, -jnp.inf)
            pt = jnp.exp(st - lse_ref[pl.ds(ib, 1), :])
            dv = dv + jnp.dot(pt.astype(BF16), dov, preferred_element_type=F32)
            dpt = lax.dot_general(vv, dov, (((1,), (1,)), ((), ())), preferred_element_type=F32)
            dst = (pt * (dpt - dd_ref[pl.ds(ib, 1), :]) * ATTN_SCALE).astype(BF16)
            dkc = dkc + jnp.dot(dst, qv, preferred_element_type=F32)
            dq_acc[pl.ds(off, t), :] += lax.dot_general(dst, kc, (((0,), (0,)), ((), ())), preferred_element_type=F32)
            return dkc, dv

        carry = update(j, (jnp.zeros((t, HEAD_W), F32), jnp.zeros((t, VDIM), F32)), True)
        dkc, dv = lax.fori_loop(j + 1, nq, lambda ib, cr: update(ib, cr, False), carry)
        dkv_ref[:, pl.ds(0, NOPE)] = dkc[:, :NOPE].astype(dkv_ref.dtype)
        dkv_ref[:, pl.ds(NOPE, VDIM)] = dv.astype(dkv_ref.dtype)
        dkr_ref[...] = dkc[:, NOPE:]

        @pl.when(j == nq - 1)
        def _():
            dq_ref[...] = dq_acc[...].astype(dq_ref.dtype)

    dq, dkv, dkr_heads = _pcall(
        body, name=name, grid=(N_HEADS, nq),
        in_specs=[pl.BlockSpec((S, HEAD_W), lambda h, j: (0, h)), pl.BlockSpec((t, HEAD_W), lambda h, j: (j, h)),
                  pl.BlockSpec((t, 128), lambda h, j: (j, 0)), pl.BlockSpec((S, VDIM), lambda h, j: (0, h)),
                  pl.BlockSpec((None, nq, t), lambda h, j: (h, 0, 0)), pl.BlockSpec((None, nq, t), lambda h, j: (h, 0, 0))],
        out_specs=[pl.BlockSpec((S, HEAD_W), lambda h, j: (0, h)), pl.BlockSpec((t, HEAD_W), lambda h, j: (j, h)),
                   pl.BlockSpec((None, t, 128), lambda h, j: (h, j, 0))],
        out_shape=[jax.ShapeDtypeStruct(q.shape, q.dtype), jax.ShapeDtypeStruct(kv.shape, kv.dtype),
                   jax.ShapeDtypeStruct((N_HEADS, S, 128), F32)],
        scratch_shapes=[pltpu.VMEM((S, HEAD_W), F32)],
        compiler_params=_cp(("parallel", "arbitrary")),
    )(q, kv, kr, do, lse.reshape(N_HEADS, nq, t), dd.reshape(N_HEADS, nq, t))

    def sum_body(p_ref, o_ref):
        acc = p_ref[0]
        for h in range(1, N_HEADS):
            acc = acc + p_ref[h]
        o_ref[...] = acc.astype(o_ref.dtype)

    dkr = _pcall(
        sum_body, name=name + "_rope_sum", grid=(nq,),
        in_specs=[pl.BlockSpec((N_HEADS, t, 128), lambda i: (0, i, 0))], out_specs=pl.BlockSpec((t, 128), lambda i: (i, 0)),
        out_shape=jax.ShapeDtypeStruct((S, 128), kr.dtype), compiler_params=_cp(("parallel",)),
    )(dkr_heads)
    return dq, dkv, dkr


def _attn_op(name):
    @jax.custom_vjp
    def attn(q, kv, kr):
        return _attn_fwd_call(q, kv, kr, name + "_fwd")[0]

    def fwd(q, kv, kr):
        o, lse = _attn_fwd_call(q, kv, kr, name + "_fwd")
        return o, (q, kv, kr, o, lse)

    def bwd(res, do):
        q, kv, kr, o, lse = res
        dd = _attn_dd_call(o, do, name + "_dd")
        return _attn_bwd_call(q, kv, kr, do, lse, dd, name + "_bwd")

    attn.defvjp(fwd, bwd)
    return attn


def _loss_call(y, target):
    S, D = y.shape
    t = _pick(S, (256, 128))

    def body(y_ref, t_ref, dy_ref, loss_ref):
        @pl.when(pl.program_id(0) == 0)
        def _():
            loss_ref[...] = jnp.zeros_like(loss_ref)

        e = y_ref[...] - t_ref[...]
        dy_ref[...] = e * (1.0 / D)
        loss_ref[...] += 0.5 * jnp.sum(jnp.mean(e * e, axis=-1, keepdims=True), axis=0, keepdims=True)

    return _pcall(
        body, name="loss_head", grid=(S // t,),
        in_specs=[pl.BlockSpec((t, D), lambda i: (i, 0)), pl.BlockSpec((t, D), lambda i: (i, 0))],
        out_specs=[pl.BlockSpec((t, D), lambda i: (i, 0)), pl.BlockSpec((1, 1), lambda i: (0, 0))],
        out_shape=[jax.ShapeDtypeStruct((S, D), F32), jax.ShapeDtypeStruct((1, 1), F32)],
        compiler_params=_cp(("arbitrary",)),
    )(y, target)


def _adamw_call(w, g, m, v, name):
    shape = w.shape
    C = shape[-1]
    R = int(np.prod(shape[:-1]))
    tr = R
    for cand in (512, 256, 128, 64, 32, 16, 8):
        if R % cand == 0 and cand * C * 4 <= 2 * 1024 * 1024:
            tr = cand
            break
    c1 = 1.0 - ADAM_B1 ** ADAM_STEP
    c2 = 1.0 - ADAM_B2 ** ADAM_STEP

    def body(w_ref, g_ref, m_ref, v_ref, d_ref, mo_ref, vo_ref):
        gv = g_ref[...]
        mn = ADAM_B1 * m_ref[...] + (1.0 - ADAM_B1) * gv
        vn = ADAM_B2 * v_ref[...] + (1.0 - ADAM_B2) * (gv * gv)
        d_ref[...] = -ADAM_LR * ((mn / c1) / (jnp.sqrt(vn / c2) + ADAM_EPS) + ADAM_WD * w_ref[...])
        mo_ref[...] = mn
        vo_ref[...] = vn

    spec = pl.BlockSpec((tr, C), lambda i: (i, 0))
    outs = _pcall(
        body, name=name, grid=(R // tr,), in_specs=[spec] * 4, out_specs=[spec] * 3,
        out_shape=[jax.ShapeDtypeStruct((R, C), F32)] * 3, compiler_params=_cp(("parallel",)),
    )(*[a.reshape(R, C) for a in (w, g, m, v)])
    return [o.reshape(shape) for o in outs]


def _gather_small(x2d, *, reduce, name):
    R, C = x2d.shape

    def body(x_ref, out_ref, *scratch):
        if reduce:
            buf_ref, send_sems, recv_sems, local_sem = scratch
        else:
            buf_ref = out_ref
            send_sems, recv_sems, local_sem = scratch
        x, y, c = lax.axis_index("x"), lax.axis_index("y"), lax.axis_index("c")
        me, sibling = (x, y, c), (x, y, 1 - c)
        chips = [(1 - x, y), (x, 1 - y), (1 - x, 1 - y)]

        def rows(px, py, pc):
            return buf_ref.at[pl.ds((4 * px + 2 * py + pc) * R, R), :]

        def copy(k, block, to, src=None):
            return pltpu.make_async_remote_copy(
                src_ref=rows(*block) if src is None else src, dst_ref=rows(*block),
                send_sem=send_sems.at[k], recv_sem=recv_sems.at[k], device_id=to, device_id_type=MESH)

        mine = pltpu.make_async_copy(x_ref, rows(*me), local_sem)
        mine.start()
        first = [copy(0, me, sibling, src=x_ref)]
        first += [copy(1 + j, me, (*chip, c), src=x_ref) for j, chip in enumerate(chips)]
        for cp in first:
            cp.start()
        passed = [copy(4 + j, (*chip, c), sibling) for j, chip in enumerate(chips)]
        for j, chip in enumerate(chips):
            copy(1 + j, (*chip, c), me).wait_recv()
            passed[j].start()
        copy(0, sibling, me).wait_recv()
        for j, chip in enumerate(chips):
            copy(4 + j, (*chip, 1 - c), me).wait_recv()
        for cp in first + passed:
            cp.wait_send()
        mine.wait()
        if reduce:
            acc = buf_ref[pl.ds(0, R), :]
            for d in range(1, N_DEV):
                acc = acc + buf_ref[pl.ds(d * R, R), :]
            out_ref[...] = acc

    scratch = [pltpu.SemaphoreType.DMA((7,)), pltpu.SemaphoreType.DMA((7,)), pltpu.SemaphoreType.DMA]
    if reduce:
        scratch = [pltpu.VMEM((N_DEV * R, C), F32)] + scratch
    return _pcall(
        body, name=name, out_shape=jax.ShapeDtypeStruct((R if reduce else N_DEV * R, C), F32),
        in_specs=[pl.BlockSpec(memory_space=pltpu.VMEM)], out_specs=pl.BlockSpec(memory_space=pltpu.VMEM),
        scratch_shapes=scratch, compiler_params=pltpu.CompilerParams(vmem_limit_bytes=VMEM_LIMIT),
    )(x2d)


def _pack(arrs):
    flat = jnp.concatenate([a.reshape(-1).astype(F32) for a in arrs])
    n = flat.shape[0]
    unit = 8 * SMALL_COLS
    flat = jnp.pad(flat, (0, (-n) % unit))
    return flat.reshape(-1, SMALL_COLS)


def _unpack(flat, shapes):
    out, o = [], 0
    for s in shapes:
        n = int(np.prod(s))
        out.append(flat[o:o + n].reshape(s))
        o += n
    return out


def _all_gather_small(arrs, name):
    p = _pack(arrs)
    g = _gather_small(p, reduce=False, name=name).reshape(N_DEV, -1)
    out, o = [], 0
    for a in arrs:
        n = int(np.prod(a.shape))
        out.append(g[:, o:o + n].reshape((N_DEV,) + a.shape))
        o += n
    return out


def _all_reduce_small(arrs, name):
    p = _pack(arrs)
    return _unpack(_gather_small(p, reduce=True, name=name).reshape(-1), [a.shape for a in arrs])


def _half_rows(shard_shape):
    return shard_shape[0] // 2


def _half_slot(ref, kind, rh, cols, half, slot):
    if kind == "col":
        return ref.at[pl.ds(half * rh, rh), pl.ds(slot * cols, cols)]
    return ref.at[pl.ds((slot * 2 + half) * rh, rh), :]


def _full_shape(kind, shard_shape):
    r, c = shard_shape
    return (r, c * N_CHIPS) if kind == "col" else (r * N_CHIPS, c)


def _mesh_pos():
    x, y, c = lax.axis_index("x"), lax.axis_index("y"), lax.axis_index("c")
    return x, y, c


def _handshake_chip_peers(x, y, c):
    barrier = pltpu.get_barrier_semaphore()
    for peer in [(x, y, 1 - c)] + [(x ^ fx, y ^ fy, c) for fx, fy in CHIP_RELS]:
        pl.semaphore_signal(barrier, inc=1, device_id=peer, device_id_type=MESH)
    pl.semaphore_wait(barrier, 1 + len(CHIP_RELS))


def _gather_big(shards, kinds, name):
    n = len(shards)

    def body(*refs):
        srcs, outs = refs[:n], refs[n:2 * n]
        ici_send, ici_recv, d2d_send, d2d_recv, own_send, own_recv = refs[2 * n:]
        x, y, c = _mesh_pos()
        _handshake_chip_peers(x, y, c)
        my = 2 * x + y
        geo = [(_half_rows(s.shape), s.shape[1]) for s in shards]

        def region(a, half, slot):
            return _half_slot(outs[a], kinds[a], geo[a][0], geo[a][1], half, slot)

        def slot_of(a, slot):
            rh, cols = geo[a]
            if kinds[a] == "col":
                return outs[a].at[:, pl.ds(slot * cols, cols)]
            return outs[a].at[pl.ds(slot * 2 * rh, 2 * rh), :]

        sends = []
        for a in range(n):
            cp = pltpu.make_async_remote_copy(
                src_ref=srcs[a], dst_ref=slot_of(a, my), send_sem=own_send.at[a], recv_sem=own_recv.at[a],
                device_id=(x, y, 1 - c), device_id_type=MESH)
            cp.start()
            sends.append(cp)
        for a in range(n):
            rh = geo[a][0]
            for r, (fx, fy) in enumerate(CHIP_RELS):
                cp = pltpu.make_async_remote_copy(
                    src_ref=srcs[a].at[pl.ds(c * rh, rh), :], dst_ref=region(a, c, my),
                    send_sem=ici_send.at[3 * a + r], recv_sem=ici_recv.at[3 * a + r],
                    device_id=(x ^ fx, y ^ fy, c), device_id_type=MESH)
                cp.start()
                sends.append(cp)
        passed = []
        for a in range(n):
            for r, (fx, fy) in enumerate(CHIP_RELS):
                frm = 2 * (x ^ fx) + (y ^ fy)
                landed = region(a, c, frm)
                pltpu.make_async_remote_copy(
                    src_ref=landed, dst_ref=landed, send_sem=ici_send.at[3 * a + r], recv_sem=ici_recv.at[3 * a + r],
                    device_id=(x, y, c), device_id_type=MESH).wait_recv()
                cp = pltpu.make_async_remote_copy(
                    src_ref=landed, dst_ref=landed, send_sem=d2d_send.at[3 * a + r], recv_sem=d2d_recv.at[3 * a + r],
                    device_id=(x, y, 1 - c), device_id_type=MESH)
                cp.start()
                passed.append(cp)
        for a in range(n):
            for r, (fx, fy) in enumerate(CHIP_RELS):
                frm = 2 * (x ^ fx) + (y ^ fy)
                other = region(a, 1 - c, frm)
                pltpu.make_async_remote_copy(
                    src_ref=other, dst_ref=other, send_sem=d2d_send.at[3 * a + r], recv_sem=d2d_recv.at[3 * a + r],
                    device_id=(x, y, c), device_id_type=MESH).wait_recv()
        for a in range(n):
            pltpu.make_async_remote_copy(
                src_ref=srcs[a], dst_ref=slot_of(a, my), send_sem=own_send.at[a], recv_sem=own_recv.at[a],
                device_id=(x, y, c), device_id_type=MESH).wait_recv()
        for cp in sends + passed:
            cp.wait_send()

    return pl.kernel(
        body, name=name,
        out_type=[jax.ShapeDtypeStruct(_full_shape(k, s.shape), s.dtype) for s, k in zip(shards, kinds)],
        mesh=plsc.ScalarSubcoreMesh(axis_name="sequencer", num_cores=1),
        scratch_types=[pltpu.SemaphoreType.DMA((3 * n,))] * 4 + [pltpu.SemaphoreType.DMA((n,))] * 2,
        compiler_params=pltpu.CompilerParams(collective_id=GATHER_COLLECTIVE_ID),
    )(*shards)


def _swap_halves(fulls, kinds, shard_shapes, name):
    n = len(fulls)
    geo = [(_half_rows(s), s[1]) for s in shard_shapes]

    def body(*refs):
        srcs, outs = refs[:n], refs[n:2 * n]
        send_sems, recv_sems = refs[2 * n:]
        x, y, c = _mesh_pos()
        barrier = pltpu.get_barrier_semaphore()
        pl.semaphore_signal(barrier, inc=1, device_id=(x, y, 1 - c), device_id_type=MESH)
        pl.semaphore_wait(barrier, 1)
        cps = []
        for a in range(n):
            for s in range(N_CHIPS):
                cp = pltpu.make_async_remote_copy(
                    src_ref=_half_slot(srcs[a], kinds[a], geo[a][0], geo[a][1], 1 - c, s), dst_ref=outs[a].at[s],
                    send_sem=send_sems.at[N_CHIPS * a + s], recv_sem=recv_sems.at[N_CHIPS * a + s],
                    device_id=(x, y, 1 - c), device_id_type=MESH)
                cp.start()
                cps.append(cp)
        for cp in cps:
            cp.wait()

    return pl.kernel(
        body, name=name,
        out_type=[jax.ShapeDtypeStruct((N_CHIPS,) + g, f.dtype) for g, f in zip(geo, fulls)],
        mesh=plsc.ScalarSubcoreMesh(axis_name="sequencer", num_cores=1),
        scratch_types=[pltpu.SemaphoreType.DMA((N_CHIPS * n,))] * 2,
        compiler_params=pltpu.CompilerParams(collective_id=SWAP_COLLECTIVE_ID),
    )(*fulls)


def _add_own_half(full, recv, kind, shard_shape, cidx, name):
    rh, cols = _half_rows(shard_shape), shard_shape[1]
    tr = _pick(rh, (256, 128, 64, 32, 16))
    nb = rh // tr

    def body(c_ref, f_ref, r_ref, o_ref):
        o_ref[...] = (f_ref[...].astype(F32) + r_ref[...].astype(F32)).astype(o_ref.dtype)

    if kind == "col":
        f_spec = pl.BlockSpec((tr, cols), lambda s, i, c_ref: (c_ref[0] * nb + i, s))
    else:
        f_spec = pl.BlockSpec((tr, cols), lambda s, i, c_ref: ((s * 2 + c_ref[0]) * nb + i, 0))
    blk = pl.BlockSpec((None, tr, cols), lambda s, i, c_ref: (s, i, 0))
    return _pcall(
        body, name=name,
        grid_spec=pltpu.PrefetchScalarGridSpec(num_scalar_prefetch=1, grid=(N_CHIPS, nb), in_specs=[f_spec, blk], out_specs=blk),
        out_shape=jax.ShapeDtypeStruct((N_CHIPS, rh, cols), BF16),
        compiler_params=_cp(("arbitrary", "arbitrary")),
    )(cidx, full, recv)


def _scatter_partials(partials, name):
    n = len(partials)

    def body(*refs):
        srcs, outs = refs[:n], refs[n:2 * n]
        send_sems, recv_sems = refs[2 * n:]
        x, y, c = _mesh_pos()
        barrier = pltpu.get_barrier_semaphore()
        for fx, fy in CHIP_RELS:
            pl.semaphore_signal(barrier, inc=1, device_id=(x ^ fx, y ^ fy, c), device_id_type=MESH)
        pl.semaphore_wait(barrier, len(CHIP_RELS))
        my = 2 * x + y
        cps = []
        for a in range(n):
            for r, (fx, fy) in enumerate(CHIP_RELS):
                to = 2 * (x ^ fx) + (y ^ fy)
                cp = pltpu.make_async_remote_copy(
                    src_ref=srcs[a].at[to], dst_ref=outs[a].at[my],
                    send_sem=send_sems.at[3 * a + r], recv_sem=recv_sems.at[3 * a + r],
                    device_id=(x ^ fx, y ^ fy, c), device_id_type=MESH)
                cp.start()
                cps.append(cp)
        for a in range(n):
            for r, (fx, fy) in enumerate(CHIP_RELS):
                frm = 2 * (x ^ fx) + (y ^ fy)
                pltpu.make_async_remote_copy(
                    src_ref=outs[a].at[frm], dst_ref=outs[a].at[frm],
                    send_sem=send_sems.at[3 * a + r], recv_sem=recv_sems.at[3 * a + r],
                    device_id=(x, y, c), device_id_type=MESH).wait_recv()
        for cp in cps:
            cp.wait_send()

    return pl.kernel(
        body, name=name,
        out_type=[jax.ShapeDtypeStruct(p.shape, p.dtype) for p in partials],
        mesh=plsc.ScalarSubcoreMesh(axis_name="sequencer", num_cores=1),
        scratch_types=[pltpu.SemaphoreType.DMA((3 * n,))] * 2,
        compiler_params=pltpu.CompilerParams(collective_id=SCATTER_COLLECTIVE_ID),
    )(*partials)


def _sum_chips_into(landed, partial, buf, layer, n_layers, pos, name):
    _, rh, cols = landed.shape
    tr = _pick(rh, (256, 128, 64, 32, 16))
    nb = rh // tr

    def body(pos_ref, l_ref, own_ref, *rest):
        o_ref = rest[-1]
        my = pos_ref[1]
        own = own_ref[...].astype(F32)
        acc = jnp.where(my == 0, own, l_ref[0].astype(F32))
        for j in range(1, N_CHIPS):
            acc = acc + jnp.where(my == j, own, l_ref[j].astype(F32))
        o_ref[...] = acc

    in_specs = [pl.BlockSpec((N_CHIPS, tr, cols), lambda i, pos_ref: (0, i, 0)),
                pl.BlockSpec((None, tr, cols), lambda i, pos_ref: (pos_ref[1], i, 0))]
    args = [pos, landed, partial]
    aliases = {}
    if buf is not None:
        in_specs.append(pl.BlockSpec(memory_space=pl.ANY))
        args.append(buf)
        aliases = {3: 0}
    return _pcall(
        body, name=name,
        grid_spec=pltpu.PrefetchScalarGridSpec(
            num_scalar_prefetch=1, grid=(nb,), in_specs=in_specs,
            out_specs=pl.BlockSpec((None, tr, cols), lambda i, pos_ref: (layer, pos_ref[0] * nb + i, 0))),
        out_shape=jax.ShapeDtypeStruct((n_layers, 2 * rh, cols), F32), input_output_aliases=aliases,
        compiler_params=_cp(("arbitrary",)),
    )(*args)


def _exchange_final(bufs):
    n = len(bufs)
    HBM = pl.BlockSpec(memory_space=pl.ANY)
    n_layers = [b.shape[0] for b in bufs]
    base = np.concatenate([[0], np.cumsum(n_layers)])

    def body(*refs):
        outs = refs[n:2 * n]
        send_sems, recv_sems = refs[2 * n:]
        x, y, c = _mesh_pos()
        cps = []
        for w in range(n):
            rh = bufs[w].shape[1] // 2
            for l in range(n_layers[w]):
                k = int(base[w]) + l
                mine = outs[w].at[l, pl.ds(c * rh, rh), :]
                cp = pltpu.make_async_remote_copy(
                    src_ref=mine, dst_ref=mine, send_sem=send_sems.at[k], recv_sem=recv_sems.at[k],
                    device_id=(x, y, 1 - c), device_id_type=MESH)
                cp.start()
                cps.append(cp)
        for w in range(n):
            rh = bufs[w].shape[1] // 2
            for l in range(n_layers[w]):
                k = int(base[w]) + l
                other = outs[w].at[l, pl.ds((1 - c) * rh, rh), :]
                pltpu.make_async_remote_copy(
                    src_ref=other, dst_ref=other, send_sem=send_sems.at[k], recv_sem=recv_sems.at[k],
                    device_id=(x, y, c), device_id_type=MESH).wait_recv()
        for cp in cps:
            cp.wait_send()

    return _pcall(
        body, name="grad_exchange_final", out_shape=[jax.ShapeDtypeStruct(b.shape, b.dtype) for b in bufs],
        in_specs=[HBM] * n, out_specs=[HBM] * n, input_output_aliases={i: i for i in range(n)},
        scratch_shapes=[pltpu.SemaphoreType.DMA((int(base[-1]),))] * 2,
        compiler_params=pltpu.CompilerParams(vmem_limit_bytes=VMEM_LIMIT),
    )(*bufs)


def _swap_halves_last(z):
    h = z.shape[-1] // 2
    return jnp.concatenate([z[..., h:], z[..., :h]], axis=-1)


def _ext_uq(w):
    lead = w.shape[:-1]
    wh = w.reshape(lead + (-1, QK_DIM))
    rope = wh[..., NOPE:]
    return jnp.concatenate([wh, _swap_halves_last(rope)], axis=-1).reshape(lead + (-1,))


def _fold_uq(d):
    lead = d.shape[:-1]
    dh = d.reshape(lead + (-1, HEAD_W))
    rope = dh[..., NOPE:QK_DIM] + _swap_halves_last(dh[..., QK_DIM:])
    return jnp.concatenate([dh[..., :NOPE], rope], axis=-1).reshape(lead + (-1,))


def _ext_win(w):
    base, kr = w[..., :-ROPE], w[..., -ROPE:]
    ks = _swap_halves_last(kr)
    return jnp.concatenate([base, kr, ks, ks, kr], axis=-1)


def _fold_win(d):
    n = d.shape[-1] - 4 * ROPE
    a, b, c2, e = [d[..., n + i * ROPE:n + (i + 1) * ROPE] for i in range(4)]
    return jnp.concatenate([d[..., :n], a + e + _swap_halves_last(b + c2)], axis=-1)


ROW_TILE = 256


def _stage(st, xcur, mix, big, small, tabs):
    D = xcur.shape[1]
    l = st // 2
    mod = [[small["mod"][k][i].reshape(1, 1, D) for i in range(6)] for k in range(l + 1)]
    sh1, sc1, g1, sh2, sc2, _ = mod[l]
    if st % 2 == 0:
        n1 = small["norm1_g"][l].reshape(1, 1, D)
        if l == 0:
            (h,) = _rowop("norm_mod_0", _f_norm_mod, n_x=1, n_nd=0, n_p=3, x_w=[D], nd_w=[], nd_shared=[], p_per_group=[False] * 3,
                          out_w=[D], out_dtypes=[BF16], tile=ROW_TILE)((xcur,), (), (n1, sh1, sc1))
        else:
            op = _rowop(f"resid_norm_mod_a{l}", _f_resid_norm_mod, n_x=2, n_nd=0, n_p=4, x_w=[D, D], nd_w=[], nd_shared=[],
                        p_per_group=[False] * 4, out_w=[D, D], out_dtypes=[F32, BF16], tile=ROW_TILE)
            xcur, h = op((xcur, mix), (), (mod[l - 1][5], n1, sh1, sc1))
        if l % 2 == 0:
            return xcur, _even_mixer(h, big, small, l // 2, ROW_TILE)
        return xcur, _mla_mixer(h, big, small, l // 2, ROW_TILE, *tabs)
    n2 = small["norm2_g"][l].reshape(1, 1, D)
    op = _rowop(f"resid_norm_mod_b{l}", _f_resid_norm_mod, n_x=2, n_nd=0, n_p=4, x_w=[D, D], nd_w=[], nd_shared=[],
                p_per_group=[False] * 4, out_w=[D, D], out_dtypes=[F32, BF16], tile=ROW_TILE)
    xcur, h = op((xcur, mix), (), (g1, n2, sh2, sc2))
    return xcur, _mlp(f"mlp_{l}")(h, big["mlp_w1"], big["mlp_w2"])


def _last_residual(xcur, mix, small):
    D = xcur.shape[1]
    gate = small["mod"][-1][5].reshape(1, 1, D)
    return _rowop("resid_last", _f_resid, n_x=2, n_nd=0, n_p=1, x_w=[D, D], nd_w=[], nd_shared=[], p_per_group=[False],
                  out_w=[D], out_dtypes=[F32], tile=ROW_TILE)((xcur, mix), (), (gate,))[0]


def _even_mixer(h, big, wts, e, tile):
    proj = _linear(f"ab_in_{e}")(h, big["ab_w_in"])
    da = proj.shape[1] // 4
    u, v, a, g = [proj[:, i * da:(i + 1) * da] for i in range(4)]
    ng = da // GROUP
    sgu = _rowop(f"sgu_{e}", _f_sgu, n_x=2, n_nd=0, n_p=3, x_w=[GROUP, GROUP], nd_w=[], nd_shared=[], p_per_group=[True] * 3,
                 out_w=[GROUP], out_dtypes=[BF16], tile=CHUNK, groups=ng)
    bexp = jnp.broadcast_to(wts["sgu_b"][e][:, :, None], (ng, CHUNK, GROUP))
    (out_a,) = sgu((u, v), (), (wts["sgu_norm_g"][e].reshape(ng, 1, GROUP), wts["sgu_w"][e], bexp))
    (yglu,) = _rowop(f"glu_{e}", _f_glu, n_x=2, n_nd=0, n_p=0, x_w=[da, da], nd_w=[], nd_shared=[], p_per_group=[],
                     out_w=[da], out_dtypes=[F32], tile=tile)((a, g), (), ())
    z = _conv_op(f"conv_{e}")(yglu, wts["conv_w"][e], wts["conv_b"][e].reshape(1, da))
    (out_b,) = _rowop(f"ln_silu_{e}", _f_ln_silu, n_x=1, n_nd=0, n_p=2, x_w=[da], nd_w=[], nd_shared=[], p_per_group=[False] * 2,
                      out_w=[da], out_dtypes=[BF16], tile=tile)(
        (z,), (), (wts["conv_ln_g"][e].reshape(1, 1, da), wts["conv_ln_b"][e].reshape(1, 1, da)))
    return _linear(f"ab_out_{e}")(jnp.concatenate([out_a, out_b], axis=-1), big["ab_w_out"])


def _mla_mixer(h, big, wts, o, tile, tab_q, tab_ka, tab_kb):
    proj = _linear(f"mla_in_{o}")(h, big["mla_w_in"])
    rank = (proj.shape[1] - 4 * ROPE) // 2
    c_q, c_kv = proj[:, :rank], proj[:, rank:2 * rank]
    kr_a, kr_b = proj[:, 2 * rank:2 * rank + 2 * ROPE], proj[:, 2 * rank + 2 * ROPE:]

    def rms(name, xx, gg):
        return _rowop(name, _f_rms, n_x=1, n_nd=0, n_p=1, x_w=[rank], nd_w=[], nd_shared=[], p_per_group=[False],
                      out_w=[rank], out_dtypes=[BF16], tile=tile)((xx,), (), (gg.reshape(1, 1, rank),))[0]

    q_raw = _linear(f"mla_uq_{o}")(rms(f"rms_q_{o}", c_q, wts["mla_q_norm_g"][o]), big["mla_w_uq"])
    kv_raw = _linear(f"mla_ukv_{o}")(rms(f"rms_kv_{o}", c_kv, wts["mla_kv_norm_g"][o]), big["mla_w_ukv"])
    gq, gk = wts["mla_q_head_g"][o], wts["mla_k_head_g"][o]
    gk_rope = gk[NOPE:]
    gq_ext = jnp.concatenate([gq, _swap_halves_last(gq[NOPE:])]).reshape(1, 1, HEAD_W)
    gk_ext = jnp.concatenate([gk[:NOPE], jnp.ones((HEAD_W - NOPE,), F32)]).reshape(1, 1, HEAD_W)
    gk_a = jnp.concatenate([gk_rope, _swap_halves_last(gk_rope)]).reshape(1, 1, 2 * ROPE)
    gk_b = jnp.concatenate([_swap_halves_last(gk_rope), gk_rope]).reshape(1, 1, 2 * ROPE)
    head_tile = 4 * tile
    (q,) = _rowop(f"q_head_{o}", _f_qhead, n_x=1, n_nd=1, n_p=1, x_w=[HEAD_W], nd_w=[HEAD_W], nd_shared=[True], p_per_group=[False],
                  out_w=[HEAD_W], out_dtypes=[BF16], tile=head_tile, groups=N_HEADS)((q_raw,), (tab_q,), (gq_ext,))
    (kv,) = _rowop(f"kv_head_{o}", _f_kvhead, n_x=1, n_nd=0, n_p=1, x_w=[HEAD_W], nd_w=[], nd_shared=[], p_per_group=[False],
                   out_w=[HEAD_W], out_dtypes=[BF16], tile=head_tile, groups=N_HEADS)((kv_raw,), (), (gk_ext,))
    (kr,) = _rowop(f"k_rope_{o}", _f_krope, n_x=2, n_nd=2, n_p=2, x_w=[2 * ROPE] * 2, nd_w=[2 * ROPE] * 2, nd_shared=[True] * 2,
                   p_per_group=[False] * 2, out_w=[2 * ROPE], out_dtypes=[BF16], tile=tile)((kr_a, kr_b), (tab_ka, tab_kb), (gk_a, gk_b))
    att = _attn_op(f"attn_{o}")(q, kv, kr)
    return _linear(f"mla_out_{o}")(att, big["mla_w_out"])


def _rope_tabs(S):
    pos = jnp.arange(S, dtype=F32)
    inv = ROPE_THETA ** (-jnp.arange(0, ROPE, 2, dtype=F32) / ROPE)
    ang = pos[:, None] * inv[None, :]
    cos, sin = jnp.cos(ang), jnp.sin(ang)
    cc = jnp.concatenate([cos, cos], axis=-1)
    sg = jnp.concatenate([-sin, sin], axis=-1)
    tab_q = jnp.concatenate([jnp.ones((S, NOPE), F32), cc, sg], axis=-1)
    return tab_q, jnp.concatenate([cc, sg], axis=-1), jnp.concatenate([sg, cc], axis=-1)


def kernel(x, c, norm1_g, norm2_g, ada_w, ada_b, mlp_w1, mlp_w2, ab_w_in, sgu_norm_g, sgu_w, sgu_b, conv_w, conv_b, conv_ln_g, conv_ln_b, ab_w_out, mla_w_in, mla_q_norm_g, mla_kv_norm_g, mla_w_uq, mla_w_ukv, mla_q_head_g, mla_k_head_g, mla_w_out, loss_target, m_norm1_g, m_norm2_g, m_ada_w, m_ada_b, m_mlp_w1, m_mlp_w2, m_ab_w_in, m_sgu_norm_g, m_sgu_w, m_sgu_b, m_conv_w, m_conv_b, m_conv_ln_g, m_conv_ln_b, m_ab_w_out, m_mla_w_in, m_mla_q_norm_g, m_mla_kv_norm_g, m_mla_w_uq, m_mla_w_ukv, m_mla_q_head_g, m_mla_k_head_g, m_mla_w_out, v_norm1_g, v_norm2_g, v_ada_w, v_ada_b, v_mlp_w1, v_mlp_w2, v_ab_w_in, v_sgu_norm_g, v_sgu_w, v_sgu_b, v_conv_w, v_conv_b, v_conv_ln_g, v_conv_ln_b, v_ab_w_out, v_mla_w_in, v_mla_q_norm_g, v_mla_kv_norm_g, v_mla_w_uq, v_mla_w_ukv, v_mla_q_head_g, v_mla_k_head_g, v_mla_w_out):
    names = ["norm1_g", "norm2_g", "ada_w", "ada_b", "mlp_w1", "mlp_w2", "ab_w_in", "sgu_norm_g", "sgu_w", "sgu_b", "conv_w",
             "conv_b", "conv_ln_g", "conv_ln_b", "ab_w_out", "mla_w_in", "mla_q_norm_g", "mla_kv_norm_g", "mla_w_uq", "mla_w_ukv",
             "mla_q_head_g", "mla_k_head_g", "mla_w_out"]
    W = dict(zip(names, [norm1_g, norm2_g, ada_w, ada_b, mlp_w1, mlp_w2, ab_w_in, sgu_norm_g, sgu_w, sgu_b, conv_w, conv_b, conv_ln_g,
                         conv_ln_b, ab_w_out, mla_w_in, mla_q_norm_g, mla_kv_norm_g, mla_w_uq, mla_w_ukv, mla_q_head_g, mla_k_head_g,
                         mla_w_out]))
    M = dict(zip(names, [m_norm1_g, m_norm2_g, m_ada_w, m_ada_b, m_mlp_w1, m_mlp_w2, m_ab_w_in, m_sgu_norm_g, m_sgu_w, m_sgu_b, m_conv_w,
                         m_conv_b, m_conv_ln_g, m_conv_ln_b, m_ab_w_out, m_mla_w_in, m_mla_q_norm_g, m_mla_kv_norm_g, m_mla_w_uq,
                         m_mla_w_ukv, m_mla_q_head_g, m_mla_k_head_g, m_mla_w_out]))
    V = dict(zip(names, [v_norm1_g, v_norm2_g, v_ada_w, v_ada_b, v_mlp_w1, v_mlp_w2, v_ab_w_in, v_sgu_norm_g, v_sgu_w, v_sgu_b, v_conv_w,
                         v_conv_b, v_conv_ln_g, v_conv_ln_b, v_ab_w_out, v_mla_w_in, v_mla_q_norm_g, v_mla_kv_norm_g, v_mla_w_uq,
                         v_mla_w_ukv, v_mla_q_head_g, v_mla_k_head_g, v_mla_w_out]))
    xi, yi, ci = lax.axis_index("x"), lax.axis_index("y"), lax.axis_index("c")
    chip = 2 * xi + yi
    dev = 2 * chip + ci
    pos = jnp.stack([ci, chip]).astype(jnp.int32)
    S, D = x.shape[1], x.shape[2]
    depth = norm1_g.shape[0]

    c_all, conv_w_all, qn_all, kvn_all = _all_gather_small([c, conv_w, mla_q_norm_g, mla_kv_norm_g], "gather_small_inputs")
    c_all = c_all.reshape(N_DEV, D)
    by_chip = lambda a: jnp.concatenate([a[2 * j] for j in range(N_CHIPS)], axis=-1)
    conv_w_full, qn_full, kvn_full = by_chip(conv_w_all), by_chip(qn_all), by_chip(kvn_all)

    (c_act,) = _rowop("silu_c", _f_silu, n_x=1, n_nd=0, n_p=0, x_w=[D], nd_w=[], nd_shared=[], p_per_group=[], out_w=[D],
                      out_dtypes=[F32], tile=N_DEV)((c_all,), (), ())
    c_act_pad = jnp.pad(c_act, ((0, 128 - N_DEV), (0, 0)))
    mod_cols = jnp.stack([_matmul(c_act_pad, ada_w[l], name=f"ada_fwd_{l}")[:N_DEV] for l in range(depth)])
    (mod_all,) = _all_gather_small([mod_cols], "gather_mod")
    mod_mine = jnp.concatenate([lax.dynamic_index_in_dim(mod_all[2 * j], dev, axis=1, keepdims=False) for j in range(N_CHIPS)], axis=-1)
    mod_mine = (mod_mine + ada_b).reshape(depth, 6, D)

    big = {"mlp_w1": "col", "mlp_w2": "row", "ab_w_in": "col", "ab_w_out": "row", "mla_w_in": "row", "mla_w_uq": "col",
           "mla_w_ukv": "col", "mla_w_out": "row"}
    src = dict(W)
    src["mla_w_in"] = _ext_win(mla_w_in)
    src["mla_w_uq"] = _ext_uq(mla_w_uq)
    shards, kinds, owner = [], [], []
    for nme, kind in big.items():
        for l in range(src[nme].shape[0]):
            shards.append(src[nme][l].astype(BF16))
            kinds.append(kind)
            owner.append((nme, l))
    def stage_of(nme, l):
        return 2 * l + 1 if nme.startswith("mlp") else (4 * l if nme.startswith("ab") else 4 * l + 2)

    fwd_stages = [[a for a, (nme, l) in enumerate(owner) if stage_of(nme, l) == st] for st in range(2 * depth)]
    fulls = [None] * len(shards)
    for st, idx in enumerate(fwd_stages):
        got = _gather_big([shards[a] for a in idx], [kinds[a] for a in idx], f"gather_weights_{st}")
        for a, f in zip(idx, got):
            fulls[a] = f
    small = dict(norm1_g=norm1_g, norm2_g=norm2_g, mod=mod_mine, sgu_norm_g=sgu_norm_g, sgu_w=sgu_w, sgu_b=sgu_b, conv_w=conv_w_full,
                 conv_b=conv_b, conv_ln_g=conv_ln_g, conv_ln_b=conv_ln_b, mla_q_norm_g=qn_full, mla_kv_norm_g=kvn_full,
                 mla_q_head_g=mla_q_head_g, mla_k_head_g=mla_k_head_g)
    groups, a0 = [], 0
    for nme in big:
        groups.append(list(range(a0, a0 + src[nme].shape[0])))
        a0 += src[nme].shape[0]
    shard_shapes = [s.shape for s in shards]
    where = {a: (wi, l) for wi, idxs in enumerate(groups) for l, a in enumerate(idxs)}

    tabs = _rope_tabs(S)
    n_stages = 2 * depth
    xcur, mix, vjps = x[0], None, []
    for st, idx in enumerate(fwd_stages):
        stage_big = {owner[a][0]: fulls[a] for a in idx}
        if st == 0:
            (xcur, mix), vjp = jax.vjp(lambda xx, bb, ss: _stage(0, xx, None, bb, ss, tabs), xcur, stage_big, small)
        else:
            (xcur, mix), vjp = jax.vjp(lambda xx, mm, bb, ss, st=st: _stage(st, xx, mm, bb, ss, tabs), xcur, mix, stage_big, small)
        vjps.append(vjp)
    y, vjp_last = jax.vjp(_last_residual, xcur, mix, small)
    dy, loss_mine = _loss_call(y, loss_target[0])
    loss = lax.psum(loss_mine[0, 0], ("x", "y", "c"))

    dxc, dmix, dsmall = vjp_last(dy)
    bufs = [None] * len(groups)
    swapped, scattered = {}, {}

    def add_and_scatter(st):
        idx, gr, recv = swapped.pop(st)
        partials = [_add_own_half(gr[a], r, kinds[a], shard_shapes[a], pos, f"grad_add_sibling_{a}") for a, r in zip(idx, recv)]
        scattered[st] = (idx, partials, _scatter_partials(partials, f"grad_scatter_partials_{st}"))

    def sum_chips(st):
        idx, partials, landed = scattered.pop(st)
        for a, part, land in zip(idx, partials, landed):
            wi, l = where[a]
            bufs[wi] = _sum_chips_into(land, part, bufs[wi], l, len(groups[wi]), pos, f"grad_sum_chips_{a}")

    for st in reversed(range(n_stages)):
        idx = fwd_stages[st]
        if st == 0:
            dxc, dbig, ds = vjps[st]((dxc, dmix))
        else:
            dxc, dmix, dbig, ds = vjps[st]((dxc, dmix))
        dsmall = jax.tree.map(jnp.add, dsmall, ds)
        gr = {a: dbig[owner[a][0]] for a in idx}
        swapped[st] = (idx, gr, _swap_halves([gr[a] for a in idx], [kinds[a] for a in idx], [shard_shapes[a] for a in idx],
                                             f"grad_swap_halves_{st}"))
        started = [s for s in ((st + 1, st) if st == 1 else (st + 1,)) if s in swapped]
        for s in started:
            add_and_scatter(s)
        if st + 2 in scattered:
            sum_chips(st + 2)
        if st > 0:
            pinned = {s: scattered[s][1] for s in started}
            (dxc, dmix), pinned, bufs = lax.optimization_barrier(((dxc, dmix), pinned, bufs))
            for s in started:
                scattered[s] = (scattered[s][0], pinned[s], scattered[s][2])
    dx = dxc
    for st in sorted(swapped, reverse=True):
        add_and_scatter(st)
    for st in sorted(scattered, reverse=True):
        sum_chips(st)
    gsh = _exchange_final(bufs)

    dw = dsmall
    dmod = dw["mod"].reshape(depth, 6 * D)
    small_names = ["norm1_g", "norm2_g", "sgu_norm_g", "sgu_w", "sgu_b", "conv_w", "conv_b", "conv_ln_g", "conv_ln_b", "mla_q_norm_g",
                   "mla_kv_norm_g", "mla_q_head_g", "mla_k_head_g"]
    red = _all_reduce_small([dmod] + [dw[nme] for nme in small_names], "reduce_small_grads")
    G = dict(zip(["ada_b"] + small_names, red))
    own_cols = lambda a: lax.dynamic_slice_in_dim(a, chip * (a.shape[-1] // N_CHIPS), a.shape[-1] // N_CHIPS, axis=-1)
    for nme in ("conv_w", "mla_q_norm_g", "mla_kv_norm_g"):
        G[nme] = own_cols(G[nme])

    (dmod_all,) = _all_gather_small([dmod], "gather_dmod")
    dmod_cols = own_cols(dmod_all)
    dmod_pad = jnp.pad(dmod_cols, ((0, 128 - N_DEV), (0, 0), (0, 0)))
    G["ada_w"] = jnp.stack([_matmul(c_act_pad, dmod_pad[:, l], ta=True, name=f"ada_wgrad_{l}") for l in range(depth)])

    for nme, g in zip(big, gsh):
        G[nme] = g
    G["mla_w_in"] = _fold_win(G["mla_w_in"])
    G["mla_w_uq"] = _fold_uq(G["mla_w_uq"])

    deltas, new_m, new_v = [], [], []
    for nme in names:
        d, mn, vn = _adamw_call(W[nme], G[nme], M[nme], V[nme], f"adamw_{nme}")
        deltas.append(d)
        new_m.append(mn)
        new_v.append(vn)
    return (loss, dx[None], *[G[nme] for nme in names], *deltas, *new_m, *new_v)
```

```python
import functools
import numpy as np
import jax
import jax.numpy as jnp
from jax import lax
from jax.experimental import pallas as pl
from jax.experimental.pallas import tpu as pltpu
from jax.experimental.pallas import tpu_sc as plsc

F32 = jnp.float32
BF16 = jnp.bfloat16
MESH = pl.DeviceIdType.MESH

EPS = 1e-6
N_HEADS = 16
NOPE = 128
ROPE = 64
VDIM = 128
QK_DIM = NOPE + ROPE
HEAD_W = 256
CHUNK = 128
GROUP = 128
SGU_CHUNKS_PER_STEP = 4
CONV_W = 31
CONV_PAD = 32
ROPE_THETA = 10000.0
ATTN_SCALE = QK_DIM ** -0.5
LOG2E = 1.4426950408889634
ATTN_SCALE_LOG2E = ATTN_SCALE * LOG2E
ADAM_LR, ADAM_B1, ADAM_B2, ADAM_EPS, ADAM_WD, ADAM_STEP = 0.001, 0.9, 0.999, 1e-08, 0.01, 10
N_CHIPS = 4
N_DEV = 8
VMEM_LIMIT = 56 * 1024 * 1024
SMALL_COLS = 1024
CHIP_RELS = ((1, 0), (0, 1), (1, 1))
GATHER_COLLECTIVE_ID = 0
SWAP_COLLECTIVE_ID = 1
SCATTER_COLLECTIVE_ID = 2


def _pcall(body, **kw):
    return pl.pallas_call(body, **kw)


def _cp(sem=None, **kw):
    return pltpu.CompilerParams(dimension_semantics=sem, vmem_limit_bytes=VMEM_LIMIT, **kw)


def _pick(n, cands):
    for c in cands:
        if n % c == 0:
            return c
    return n


def _matmul(a, b, *, ta=False, tb=False, out_dtype=F32, name, extra=(), epilogue=None, out_dtypes=None):
    if ta:
        K, M = a.shape
    else:
        M, K = a.shape
    if tb:
        N, K2 = b.shape
    else:
        K2, N = b.shape
    assert K == K2, (a.shape, b.shape, ta, tb)
    tm = _pick(M, (1024, 512, 256, 128))
    tn = _pick(N, (1024, 512, 256, 128))
    tk = _pick(K, (2048, 1024, 512, 256, 128))
    nk = K // tk
    dn = (((0 if ta else 1,), (1 if tb else 0,)), ((), ()))

    n_extra = len(extra)
    single = epilogue is None
    if single:
        out_dtypes = [out_dtype]

    def body(a_ref, b_ref, *rest):
        extra_refs, o_refs, acc_ref = rest[:n_extra], rest[n_extra:-1], rest[-1]
        k = pl.program_id(2)

        @pl.when(k == 0)
        def _():
            acc_ref[...] = jnp.zeros_like(acc_ref)

        acc_ref[...] += lax.dot_general(a_ref[...].astype(BF16), b_ref[...].astype(BF16), dn,
                                        preferred_element_type=F32)

        @pl.when(k == nk - 1)
        def _():
            acc = acc_ref[...]
            res = (acc,) if single else epilogue(acc, *[r[...] for r in extra_refs])
            for o_ref, val in zip(o_refs, res):
                o_ref[...] = val.astype(o_ref.dtype)

    a_spec = pl.BlockSpec((tk, tm), lambda i, j, k: (k, i)) if ta else pl.BlockSpec((tm, tk), lambda i, j, k: (i, k))
    b_spec = pl.BlockSpec((tn, tk), lambda i, j, k: (j, k)) if tb else pl.BlockSpec((tk, tn), lambda i, j, k: (k, j))
    mn_spec = pl.BlockSpec((tm, tn), lambda i, j, k: (i, j))
    outs = _pcall(
        body, name=name, grid=(M // tm, N // tn, nk), in_specs=[a_spec, b_spec] + [mn_spec] * n_extra,
        out_specs=[mn_spec] * len(out_dtypes),
        out_shape=[jax.ShapeDtypeStruct((M, N), d) for d in out_dtypes],
        scratch_shapes=[pltpu.VMEM((tm, tn), F32)],
        compiler_params=_cp(("parallel", "parallel", "arbitrary")),
    )(a, b, *extra)
    return outs[0] if single else outs


def _linear(name):
    @jax.custom_vjp
    def mm(a, w):
        return _matmul(a, w, name=name + "_fwd")

    def fwd(a, w):
        return mm(a, w), (a, w)

    def bwd(res, dy):
        a, w = res
        da = _matmul(dy, w, tb=True, out_dtype=a.dtype, name=name + "_dgrad")
        dw = _matmul(a, dy, ta=True, out_dtype=w.dtype, name=name + "_wgrad")
        return da, dw

    mm.defvjp(fwd, bwd)
    return mm


def _relu2_epilogue(acc):
    return acc, jnp.square(jnp.maximum(acc, 0.0))


def _relu2_grad_epilogue(acc, h1):
    return (acc * (2.0 * jnp.maximum(h1.astype(F32), 0.0)),)


def _mlp(name):
    def run(h, w1, w2):
        h1, act = _matmul(h, w1, name=name + "_up_fwd", epilogue=_relu2_epilogue, out_dtypes=[BF16, BF16])
        return _matmul(act, w2, name=name + "_down_fwd"), h1, act

    @jax.custom_vjp
    def mlp(h, w1, w2):
        return run(h, w1, w2)[0]

    def fwd(h, w1, w2):
        y, h1, act = run(h, w1, w2)
        return y, (h, w1, w2, h1, act)

    def bwd(res, dy):
        h, w1, w2, h1, act = res
        dw2 = _matmul(act, dy, ta=True, out_dtype=w2.dtype, name=name + "_down_wgrad")
        (dh1,) = _matmul(dy, w2, tb=True, name=name + "_down_dgrad", extra=(h1,), epilogue=_relu2_grad_epilogue, out_dtypes=[BF16])
        dw1 = _matmul(h, dh1, ta=True, out_dtype=w1.dtype, name=name + "_up_wgrad")
        dh = _matmul(dh1, w1, tb=True, out_dtype=h.dtype, name=name + "_up_dgrad")
        return dh, dw1, dw2

    mlp.defvjp(fwd, bwd)
    return mlp


def _rowop(name, f, *, n_x, n_nd, n_p, x_w, nd_w, nd_shared, p_per_group, out_w, out_dtypes, tile, groups=1):
    G = groups

    def specs(S):
        t = min(tile, S)
        xs = [pl.BlockSpec((t, w), lambda g, r: (r, g)) for w in x_w]
        nds = [pl.BlockSpec((t, w), (lambda g, r: (r, 0)) if sh else (lambda g, r: (r, g))) for w, sh in zip(nd_w, nd_shared)]
        outs = [pl.BlockSpec((t, w), lambda g, r: (r, g)) for w in out_w]
        return t, xs, nds, outs

    def pspecs(ps):
        return [pl.BlockSpec((None,) + p.shape[1:], (lambda g, r: (g, 0, 0)) if pg else (lambda g, r: (0, 0, 0)))
                for p, pg in zip(ps, p_per_group)]

    def fwd_call(xs, nds, ps):
        S = xs[0].shape[0]
        t, xsp, ndsp, osp = specs(S)

        def body(*refs):
            ins, outs = refs[:n_x + n_nd + n_p], refs[n_x + n_nd + n_p:]
            vals = [r[...].astype(F32) for r in ins]
            res = f(*vals)
            for o, r in zip(res, outs):
                r[...] = o.astype(r.dtype)

        return _pcall(
            body, name=name + "_fwd", grid=(G, S // t), in_specs=xsp + ndsp + pspecs(ps), out_specs=osp,
            out_shape=[jax.ShapeDtypeStruct((S, G * w), d) for w, d in zip(out_w, out_dtypes)],
            compiler_params=_cp(("parallel", "parallel")),
        )(*xs, *nds, *ps)

    def bwd_call(xs, nds, ps, douts):
        S = xs[0].shape[0]
        t, xsp, ndsp, osp = specs(S)
        n_in = n_x + n_nd + n_p + len(out_w)

        def body(*refs):
            ins, outs = refs[:n_in], refs[n_in:]
            xv = [r[...].astype(F32) for r in ins[:n_x]]
            ndv = [r[...].astype(F32) for r in ins[n_x:n_x + n_nd]]
            pv = [r[...].astype(F32) for r in ins[n_x + n_nd:n_x + n_nd + n_p]]
            dov = tuple(r[...].astype(F32) for r in ins[n_x + n_nd + n_p:])
            _, vjp = jax.vjp(lambda *a: tuple(f(*a[:n_x], *ndv, *a[n_x:])), *xv, *pv)
            cts = vjp(dov)
            for i in range(n_x):
                outs[i][...] = cts[i].astype(outs[i].dtype)
            g, r = pl.program_id(0), pl.program_id(1)
            for i in range(n_p):
                first = (r == 0) if p_per_group[i] else jnp.logical_and(g == 0, r == 0)
                ref, ct = outs[n_x + i], cts[n_x + i]

                @pl.when(first)
                def _(ref=ref, ct=ct):
                    ref[...] = ct

                @pl.when(jnp.logical_not(first))
                def _(ref=ref, ct=ct):
                    ref[...] += ct

        return _pcall(
            body, name=name + "_bwd", grid=(G, S // t), in_specs=xsp + ndsp + pspecs(ps) + osp,
            out_specs=xsp + pspecs(ps),
            out_shape=[jax.ShapeDtypeStruct(x.shape, x.dtype) for x in xs] + [jax.ShapeDtypeStruct(p.shape, F32) for p in ps],
            compiler_params=_cp(("arbitrary", "arbitrary")),
        )(*xs, *nds, *ps, *douts)

    @jax.custom_vjp
    def op(xs, nds, ps):
        return tuple(fwd_call(xs, nds, ps))

    def op_fwd(xs, nds, ps):
        return op(xs, nds, ps), (xs, nds, ps)

    def op_bwd(res, douts):
        xs, nds, ps = res
        out = bwd_call(xs, nds, ps, douts)
        return tuple(out[:n_x]), tuple(jnp.zeros_like(n) for n in nds), tuple(out[n_x:])

    op.defvjp(op_fwd, op_bwd)
    return op


def _rms_rows(x):
    return x * lax.rsqrt(jnp.mean(x * x, axis=-1, keepdims=True) + EPS)


def _f_norm_mod(x, g, shift, scale):
    return ((_rms_rows(x) * g) * (1.0 + scale) + shift,)


def _f_resid_norm_mod(x, mix, gate, g, shift, scale):
    xn = x + gate * mix
    return xn, (_rms_rows(xn) * g) * (1.0 + scale) + shift


def _f_resid(x, mix, gate):
    return (x + gate * mix,)


def _f_rms(x, g):
    return (_rms_rows(x) * g,)


def _f_glu(a, g):
    return (a * jax.nn.sigmoid(g),)


def _f_ln_silu(z, g, b):
    mu = jnp.mean(z, axis=-1, keepdims=True)
    zc = z - mu
    var = jnp.mean(zc * zc, axis=-1, keepdims=True)
    y = zc * lax.rsqrt(var + EPS) * g + b
    return (y * jax.nn.sigmoid(y),)


def _f_silu(x):
    return (x * jax.nn.sigmoid(x),)


@jax.custom_vjp
def _bdot_chunks(a, b):
    return lax.dot_general(a.astype(BF16), b.astype(BF16), (((2,), (1,)), ((0,), (0,))), preferred_element_type=F32)


def _bdot_chunks_fwd(a, b):
    return _bdot_chunks(a, b), (a, b)


def _bdot_chunks_bwd(res, ct):
    a, b = res
    c16 = ct.astype(BF16)
    da = lax.dot_general(c16, b.astype(BF16), (((2,), (2,)), ((0,), (0,))), preferred_element_type=F32)
    db = lax.dot_general(a.astype(BF16), c16, (((1,), (1,)), ((0,), (0,))), preferred_element_type=F32)
    return da, db


_bdot_chunks.defvjp(_bdot_chunks_fwd, _bdot_chunks_bwd)


def _f_sgu(u, v, ng, w, bexp):
    n = u.shape[0] // CHUNK
    vn = (_rms_rows(jax.nn.gelu(v)) * ng).reshape(n, CHUNK, GROUP)
    row = lax.broadcasted_iota(jnp.int32, w.shape, 0)
    col = lax.broadcasted_iota(jnp.int32, w.shape, 1)
    wm = jnp.broadcast_to(jnp.where(row >= col, w, 0.0)[None], (n, CHUNK, CHUNK))
    mixed = _bdot_chunks(wm, vn) + bexp[None]
    return (jax.nn.gelu(u) * mixed.reshape(n * CHUNK, GROUP),)


def _lo_mask():
    return lax.broadcasted_iota(jnp.int32, (1, HEAD_W), 1) < NOPE


def _f_qhead(x, tab, g):
    lo = _lo_mask()
    x2 = x * x
    ms_lo = jnp.sum(jnp.where(lo, x2, 0.0), axis=-1, keepdims=True) * (1.0 / NOPE)
    ms_hi = jnp.sum(jnp.where(lo, 0.0, x2), axis=-1, keepdims=True) * (1.0 / (HEAD_W - NOPE))
    r = jnp.where(lo, lax.rsqrt(ms_lo + EPS), lax.rsqrt(ms_hi + EPS))
    return (((x * r) * g) * tab,)


def _f_kvhead(x, g):
    lo = _lo_mask()
    ms = jnp.sum(jnp.where(lo, x * x, 0.0), axis=-1, keepdims=True) * (1.0 / NOPE)
    return (jnp.where(lo, (x * lax.rsqrt(ms + EPS)) * g, x),)


def _f_krope(a, b, ta, tb, ga, gb):
    r = lax.rsqrt(jnp.mean(a * a, axis=-1, keepdims=True) + EPS)
    return (((a * r) * ga) * ta + ((b * r) * gb) * tb,)


def _conv_fwd_call(y, w, b, name):
    S, C = y.shape
    cw = 128
    rt = _pick(S, (128,))
    w = jnp.pad(w, ((0, CONV_PAD - CONV_W), (0, 0)))

    def body(y_ref, w_ref, b_ref, z_ref, pad_ref):
        pad_ref[pl.ds(0, CONV_PAD), :] = jnp.zeros((CONV_PAD, cw), F32)
        pad_ref[pl.ds(CONV_PAD, S), :] = y_ref[...]
        wv = w_ref[...]
        bv = b_ref[...]

        def chunk(ci, carry):
            r0 = pl.multiple_of(ci * rt, rt)
            win = pad_ref[pl.ds(r0, rt + CONV_PAD), :]
            acc = jnp.broadcast_to(bv, (rt, cw))
            for k in range(CONV_W):
                off = CONV_PAD - (CONV_W - 1) + k
                sh = win if off == 0 else pltpu.roll(win, rt + CONV_PAD - off, axis=0)
                acc = acc + wv[k:k + 1, :] * sh[:rt, :]
            z_ref[pl.ds(r0, rt), :] = acc
            return carry

        lax.fori_loop(0, S // rt, chunk, 0)

    return _pcall(
        body, name=name, grid=(C // cw,),
        in_specs=[pl.BlockSpec((S, cw), lambda j: (0, j)), pl.BlockSpec((CONV_PAD, cw), lambda j: (0, j)),
                  pl.BlockSpec((1, cw), lambda j: (0, j))],
        out_specs=pl.BlockSpec((S, cw), lambda j: (0, j)),
        out_shape=jax.ShapeDtypeStruct((S, C), F32),
        scratch_shapes=[pltpu.VMEM((S + CONV_PAD, cw), F32)],
        compiler_params=_cp(("parallel",)),
    )(y, w, b)


def _conv_bwd_call(y, w, dz, name):
    S, C = y.shape
    cw = 128
    rt = _pick(S, (128,))
    w = jnp.pad(w, ((0, CONV_PAD - CONV_W), (0, 0)))

    def body(y_ref, w_ref, dz_ref, dy_ref, dw_ref, db_ref, ypad_ref, zpad_ref):
        ypad_ref[pl.ds(0, CONV_PAD), :] = jnp.zeros((CONV_PAD, cw), F32)
        ypad_ref[pl.ds(CONV_PAD, S), :] = y_ref[...]
        zpad_ref[pl.ds(0, S), :] = dz_ref[...]
        zpad_ref[pl.ds(S, CONV_PAD), :] = jnp.zeros((CONV_PAD, cw), F32)
        dw_ref[...] = jnp.zeros_like(dw_ref)
        wv = w_ref[...]

        def chunk(ci, dbacc):
            r0 = pl.multiple_of(ci * rt, rt)
            ywin = ypad_ref[pl.ds(r0, rt + CONV_PAD), :]
            zwin = zpad_ref[pl.ds(r0, rt + CONV_PAD), :]
            dzc = zwin[:rt, :]
            acc = jnp.zeros((rt, cw), F32)
            for k in range(CONV_W):
                off_z = (CONV_W - 1) - k
                zs = zwin if off_z == 0 else pltpu.roll(zwin, rt + CONV_PAD - off_z, axis=0)
                acc = acc + wv[k:k + 1, :] * zs[:rt, :]
                off_y = CONV_PAD - (CONV_W - 1) + k
                ys = pltpu.roll(ywin, rt + CONV_PAD - off_y, axis=0)
                dw_ref[k:k + 1, :] += jnp.sum(dzc * ys[:rt, :], axis=0, keepdims=True)
            dy_ref[pl.ds(r0, rt), :] = acc
            return dbacc + jnp.sum(dzc, axis=0, keepdims=True)

        db_ref[...] = lax.fori_loop(0, S // rt, chunk, jnp.zeros((1, cw), F32))

    dy, dw, db = _pcall(
        body, name=name, grid=(C // cw,),
        in_specs=[pl.BlockSpec((S, cw), lambda j: (0, j)), pl.BlockSpec((CONV_PAD, cw), lambda j: (0, j)),
                  pl.BlockSpec((S, cw), lambda j: (0, j))],
        out_specs=[pl.BlockSpec((S, cw), lambda j: (0, j)), pl.BlockSpec((CONV_PAD, cw), lambda j: (0, j)),
                   pl.BlockSpec((1, cw), lambda j: (0, j))],
        out_shape=[jax.ShapeDtypeStruct((S, C), F32), jax.ShapeDtypeStruct((CONV_PAD, C), F32),
                   jax.ShapeDtypeStruct((1, C), F32)],
        scratch_shapes=[pltpu.VMEM((S + CONV_PAD, cw), F32), pltpu.VMEM((S + CONV_PAD, cw), F32)],
        compiler_params=_cp(("parallel",)),
    )(y, w, dz)
    return dy, dw[:CONV_W], db


def _conv_op(name):
    @jax.custom_vjp
    def conv(y, w, b):
        return _conv_fwd_call(y, w, b, name + "_fwd")

    def fwd(y, w, b):
        return conv(y, w, b), (y, w)

    def bwd(res, dz):
        y, w = res
        return _conv_bwd_call(y, w, dz, name + "_bwd")

    conv.defvjp(fwd, bwd)
    return conv


def _attn_tile(S):
    return _pick(S, (512, 256, 128))


def _attn_fwd_call(q, kv, kr, name):
    S = q.shape[0]
    t = _attn_tile(S)

    def body(q_ref, kv_ref, kr_ref, o_ref, lse_ref):
        i = pl.program_id(1)
        qv = q_ref[...]

        def update(jb, carry, diagonal):
            m, l, acc = carry
            off = pl.multiple_of(jb * t, t)
            kc = jnp.concatenate([kv_ref[pl.ds(off, t), pl.ds(0, NOPE)], kr_ref[pl.ds(off, t), :]], axis=-1)
            vv = kv_ref[pl.ds(off, t), pl.ds(NOPE, VDIM)]
            s = lax.dot_general(qv, kc, (((1,), (1,)), ((), ())), preferred_element_type=F32)
            if diagonal:
                row = lax.broadcasted_iota(jnp.int32, (t, t), 0)
                col = lax.broadcasted_iota(jnp.int32, (t, t), 1)
                s = jnp.where(col <= row, s, -jnp.inf)
            mn = jnp.maximum(m, jnp.max(s, axis=-1, keepdims=True))
            p = jnp.exp2((s - mn) * ATTN_SCALE_LOG2E)
            al = jnp.exp2((m - mn) * ATTN_SCALE_LOG2E)
            l = al * l + jnp.sum(p, axis=-1, keepdims=True)
            acc = al * acc + jnp.dot(p.astype(BF16), vv, preferred_element_type=F32)
            return mn, l, acc

        init = (jnp.full((t, 1), -jnp.inf, F32), jnp.zeros((t, 1), F32), jnp.zeros((t, VDIM), F32))
        carry = lax.fori_loop(0, i, lambda jb, cr: update(jb, cr, False), init)
        m, l, acc = update(i, carry, True)
        o_ref[...] = (acc / l).astype(o_ref.dtype)
        lse_ref[...] = m * ATTN_SCALE + jnp.log(l)

    return _pcall(
        body, name=name, grid=(N_HEADS, S // t),
        in_specs=[pl.BlockSpec((t, HEAD_W), lambda h, i: (i, h)), pl.BlockSpec((S, HEAD_W), lambda h, i: (0, h)),
                  pl.BlockSpec((S, 128), lambda h, i: (0, 0))],
        out_specs=[pl.BlockSpec((t, VDIM), lambda h, i: (i, h)), pl.BlockSpec((None, t, 1), lambda h, i: (h, i, 0))],
        out_shape=[jax.ShapeDtypeStruct((S, N_HEADS * VDIM), BF16), jax.ShapeDtypeStruct((N_HEADS, S, 1), F32)],
        compiler_params=_cp(("parallel", "parallel")),
    )(q, kv, kr)


def _attn_dd_call(o, do, name):
    S = o.shape[0]
    t = _attn_tile(S)

    def body(o_ref, do_ref, dd_ref):
        dd_ref[...] = jnp.sum(do_ref[...].astype(F32) * o_ref[...].astype(F32), axis=-1, keepdims=True)

    return _pcall(
        body, name=name, grid=(N_HEADS, S // t),
        in_specs=[pl.BlockSpec((t, VDIM), lambda h, i: (i, h)), pl.BlockSpec((t, VDIM), lambda h, i: (i, h))],
        out_specs=pl.BlockSpec((None, t, 1), lambda h, i: (h, i, 0)),
        out_shape=jax.ShapeDtypeStruct((N_HEADS, S, 1), F32), compiler_params=_cp(("parallel", "parallel")),
    )(o, do)


def _attn_bwd_call(q, kv, kr, do, lse, dd, name):
    S = q.shape[0]
    t = _attn_tile(S)
    nq = S // t

    def body(q_ref, kv_ref, kr_ref, do_ref, lse_ref, dd_ref, dq_ref, dkv_ref, dkr_ref, dq_acc):
        j = pl.program_id(1)

        @pl.when(j == 0)
        def _():
            dq_acc[...] = jnp.zeros_like(dq_acc)

        kc = jnp.concatenate([kv_ref[:, pl.ds(0, NOPE)], kr_ref[...]], axis=-1)
        vv = kv_ref[:, pl.ds(NOPE, VDIM)]

        def update(ib, carry, diagonal):
            dkc, dv = carry
            off = pl.multiple_of(ib * t, t)
            qv = q_ref[pl.ds(off, t), :]
            dov = do_ref[pl.ds(off, t), :]
            st = lax.dot_general(kc, qv, (((1,), (1,)), ((), ())), preferred_element_type=F32)
            if diagonal:
                key = lax.broadcasted_iota(jnp.int32, (t, t), 0)
                qry = lax.broadcasted_iota(jnp.int32, (t, t), 1)
                st = jnp.where(key <= qry, st, -jnp.inf)
            pt = jnp.exp2(st * ATTN_SCALE_LOG2E - lse_ref[pl.ds(ib, 1), :])
            dv = dv + jnp.dot(pt.astype(BF16), dov, preferred_element_type=F32)
            dpt = lax.dot_general(vv, dov, (((1,), (1,)), ((), ())), preferred_element_type=F32)
            dst = (pt * (dpt - dd_ref[pl.ds(ib, 1), :]) * ATTN_SCALE).astype(BF16)
            dkc = dkc + jnp.dot(dst, qv, preferred_element_type=F32)
            dq_acc[pl.ds(off, t), :] += lax.dot_general(dst, kc, (((0,), (0,)), ((), ())), preferred_element_type=F32)
            return dkc, dv

        carry = update(j, (jnp.zeros((t, HEAD_W), F32), jnp.zeros((t, VDIM), F32)), True)
        dkc, dv = lax.fori_loop(j + 1, nq, lambda ib, cr: update(ib, cr, False), carry)
        dkv_ref[:, pl.ds(0, NOPE)] = dkc[:, :NOPE].astype(dkv_ref.dtype)
        dkv_ref[:, pl.ds(NOPE, VDIM)] = dv.astype(dkv_ref.dtype)
        dkr_ref[...] = dkc[:, NOPE:]

        @pl.when(j == nq - 1)
        def _():
            dq_ref[...] = dq_acc[...].astype(dq_ref.dtype)

    dq, dkv, dkr_heads = _pcall(
        body, name=name, grid=(N_HEADS, nq),
        in_specs=[pl.BlockSpec((S, HEAD_W), lambda h, j: (0, h)), pl.BlockSpec((t, HEAD_W), lambda h, j: (j, h)),
                  pl.BlockSpec((t, 128), lambda h, j: (j, 0)), pl.BlockSpec((S, VDIM), lambda h, j: (0, h)),
                  pl.BlockSpec((None, nq, t), lambda h, j: (h, 0, 0)), pl.BlockSpec((None, nq, t), lambda h, j: (h, 0, 0))],
        out_specs=[pl.BlockSpec((S, HEAD_W), lambda h, j: (0, h)), pl.BlockSpec((t, HEAD_W), lambda h, j: (j, h)),
                   pl.BlockSpec((None, t, 128), lambda h, j: (h, j, 0))],
        out_shape=[jax.ShapeDtypeStruct(q.shape, q.dtype), jax.ShapeDtypeStruct(kv.shape, kv.dtype),
                   jax.ShapeDtypeStruct((N_HEADS, S, 128), F32)],
        scratch_shapes=[pltpu.VMEM((S, HEAD_W), F32)],
        compiler_params=_cp(("parallel", "arbitrary")),
    )(q, kv, kr, do, (lse * LOG2E).reshape(N_HEADS, nq, t), dd.reshape(N_HEADS, nq, t))

    def sum_body(p_ref, o_ref):
        acc = p_ref[0]
        for h in range(1, N_HEADS):
            acc = acc + p_ref[h]
        o_ref[...] = acc.astype(o_ref.dtype)

    dkr = _pcall(
        sum_body, name=name + "_rope_sum", grid=(nq,),
        in_specs=[pl.BlockSpec((N_HEADS, t, 128), lambda i: (0, i, 0))], out_specs=pl.BlockSpec((t, 128), lambda i: (i, 0)),
        out_shape=jax.ShapeDtypeStruct((S, 128), kr.dtype), compiler_params=_cp(("parallel",)),
    )(dkr_heads)
    return dq, dkv, dkr


def _attn_op(name):
    @jax.custom_vjp
    def attn(q, kv, kr):
        return _attn_fwd_call(q, kv, kr, name + "_fwd")[0]

    def fwd(q, kv, kr):
        o, lse = _attn_fwd_call(q, kv, kr, name + "_fwd")
        return o, (q, kv, kr, o, lse)

    def bwd(res, do):
        q, kv, kr, o, lse = res
        dd = _attn_dd_call(o, do, name + "_dd")
        return _attn_bwd_call(q, kv, kr, do, lse, dd, name + "_bwd")

    attn.defvjp(fwd, bwd)
    return attn


def _loss_call(y, target):
    S, D = y.shape
    t = _pick(S, (256, 128))

    def body(y_ref, t_ref, dy_ref, loss_ref):
        @pl.when(pl.program_id(0) == 0)
        def _():
            loss_ref[...] = jnp.zeros_like(loss_ref)

        e = y_ref[...] - t_ref[...]
        dy_ref[...] = e * (1.0 / D)
        loss_ref[...] += 0.5 * jnp.sum(jnp.mean(e * e, axis=-1, keepdims=True), axis=0, keepdims=True)

    return _pcall(
        body, name="loss_head", grid=(S // t,),
        in_specs=[pl.BlockSpec((t, D), lambda i: (i, 0)), pl.BlockSpec((t, D), lambda i: (i, 0))],
        out_specs=[pl.BlockSpec((t, D), lambda i: (i, 0)), pl.BlockSpec((1, 1), lambda i: (0, 0))],
        out_shape=[jax.ShapeDtypeStruct((S, D), F32), jax.ShapeDtypeStruct((1, 1), F32)],
        compiler_params=_cp(("arbitrary",)),
    )(y, target)


def _adamw_call(w, g, m, v, name):
    shape = w.shape
    C = shape[-1]
    R = int(np.prod(shape[:-1]))
    tr = R
    for cand in (512, 256, 128, 64, 32, 16, 8):
        if R % cand == 0 and cand * C * 4 <= 2 * 1024 * 1024:
            tr = cand
            break
    c1 = 1.0 - ADAM_B1 ** ADAM_STEP
    c2 = 1.0 - ADAM_B2 ** ADAM_STEP

    def body(w_ref, g_ref, m_ref, v_ref, d_ref, mo_ref, vo_ref):
        gv = g_ref[...]
        mn = ADAM_B1 * m_ref[...] + (1.0 - ADAM_B1) * gv
        vn = ADAM_B2 * v_ref[...] + (1.0 - ADAM_B2) * (gv * gv)
        d_ref[...] = -ADAM_LR * ((mn / c1) / (jnp.sqrt(vn / c2) + ADAM_EPS) + ADAM_WD * w_ref[...])
        mo_ref[...] = mn
        vo_ref[...] = vn

    spec = pl.BlockSpec((tr, C), lambda i: (i, 0))
    outs = _pcall(
        body, name=name, grid=(R // tr,), in_specs=[spec] * 4, out_specs=[spec] * 3,
        out_shape=[jax.ShapeDtypeStruct((R, C), F32)] * 3, compiler_params=_cp(("parallel",)),
    )(*[a.reshape(R, C) for a in (w, g, m, v)])
    return [o.reshape(shape) for o in outs]


def _gather_small(x2d, *, reduce, name):
    R, C = x2d.shape

    def body(x_ref, out_ref, *scratch):
        if reduce:
            buf_ref, send_sems, recv_sems, local_sem = scratch
        else:
            buf_ref = out_ref
            send_sems, recv_sems, local_sem = scratch
        x, y, c = lax.axis_index("x"), lax.axis_index("y"), lax.axis_index("c")
        me, sibling = (x, y, c), (x, y, 1 - c)
        chips = [(1 - x, y), (x, 1 - y), (1 - x, 1 - y)]

        def rows(px, py, pc):
            return buf_ref.at[pl.ds((4 * px + 2 * py + pc) * R, R), :]

        def copy(k, block, to, src=None):
            return pltpu.make_async_remote_copy(
                src_ref=rows(*block) if src is None else src, dst_ref=rows(*block),
                send_sem=send_sems.at[k], recv_sem=recv_sems.at[k], device_id=to, device_id_type=MESH)

        mine = pltpu.make_async_copy(x_ref, rows(*me), local_sem)
        mine.start()
        first = [copy(0, me, sibling, src=x_ref)]
        first += [copy(1 + j, me, (*chip, c), src=x_ref) for j, chip in enumerate(chips)]
        for cp in first:
            cp.start()
        passed = [copy(4 + j, (*chip, c), sibling) for j, chip in enumerate(chips)]
        for j, chip in enumerate(chips):
            copy(1 + j, (*chip, c), me).wait_recv()
            passed[j].start()
        copy(0, sibling, me).wait_recv()
        for j, chip in enumerate(chips):
            copy(4 + j, (*chip, 1 - c), me).wait_recv()
        for cp in first + passed:
            cp.wait_send()
        mine.wait()
        if reduce:
            acc = buf_ref[pl.ds(0, R), :]
            for d in range(1, N_DEV):
                acc = acc + buf_ref[pl.ds(d * R, R), :]
            out_ref[...] = acc

    scratch = [pltpu.SemaphoreType.DMA((7,)), pltpu.SemaphoreType.DMA((7,)), pltpu.SemaphoreType.DMA]
    if reduce:
        scratch = [pltpu.VMEM((N_DEV * R, C), F32)] + scratch
    return _pcall(
        body, name=name, out_shape=jax.ShapeDtypeStruct((R if reduce else N_DEV * R, C), F32),
        in_specs=[pl.BlockSpec(memory_space=pltpu.VMEM)], out_specs=pl.BlockSpec(memory_space=pltpu.VMEM),
        scratch_shapes=scratch, compiler_params=pltpu.CompilerParams(vmem_limit_bytes=VMEM_LIMIT),
    )(x2d)


def _pack(arrs):
    flat = jnp.concatenate([a.reshape(-1).astype(F32) for a in arrs])
    n = flat.shape[0]
    unit = 8 * SMALL_COLS
    flat = jnp.pad(flat, (0, (-n) % unit))
    return flat.reshape(-1, SMALL_COLS)


def _unpack(flat, shapes):
    out, o = [], 0
    for s in shapes:
        n = int(np.prod(s))
        out.append(flat[o:o + n].reshape(s))
        o += n
    return out


def _all_gather_small(arrs, name):
    p = _pack(arrs)
    g = _gather_small(p, reduce=False, name=name).reshape(N_DEV, -1)
    out, o = [], 0
    for a in arrs:
        n = int(np.prod(a.shape))
        out.append(g[:, o:o + n].reshape((N_DEV,) + a.shape))
        o += n
    return out


def _all_reduce_small(arrs, name):
    p = _pack(arrs)
    return _unpack(_gather_small(p, reduce=True, name=name).reshape(-1), [a.shape for a in arrs])


def _half_rows(shard_shape):
    return shard_shape[0] // 2


def _half_slot(ref, kind, rh, cols, half, slot):
    if kind == "col":
        return ref.at[pl.ds(half * rh, rh), pl.ds(slot * cols, cols)]
    return ref.at[pl.ds((slot * 2 + half) * rh, rh), :]


def _full_shape(kind, shard_shape):
    r, c = shard_shape
    return (r, c * N_CHIPS) if kind == "col" else (r * N_CHIPS, c)


def _mesh_pos():
    x, y, c = lax.axis_index("x"), lax.axis_index("y"), lax.axis_index("c")
    return x, y, c


def _handshake_chip_peers(x, y, c):
    barrier = pltpu.get_barrier_semaphore()
    for peer in [(x, y, 1 - c)] + [(x ^ fx, y ^ fy, c) for fx, fy in CHIP_RELS]:
        pl.semaphore_signal(barrier, inc=1, device_id=peer, device_id_type=MESH)
    pl.semaphore_wait(barrier, 1 + len(CHIP_RELS))


def _gather_big(shards, kinds, name):
    n = len(shards)

    def body(*refs):
        srcs, outs = refs[:n], refs[n:2 * n]
        ici_send, ici_recv, d2d_send, d2d_recv, own_send, own_recv = refs[2 * n:]
        x, y, c = _mesh_pos()
        _handshake_chip_peers(x, y, c)
        my = 2 * x + y
        geo = [(_half_rows(s.shape), s.shape[1]) for s in shards]

        def region(a, half, slot):
            return _half_slot(outs[a], kinds[a], geo[a][0], geo[a][1], half, slot)

        def slot_of(a, slot):
            rh, cols = geo[a]
            if kinds[a] == "col":
                return outs[a].at[:, pl.ds(slot * cols, cols)]
            return outs[a].at[pl.ds(slot * 2 * rh, 2 * rh), :]

        sends = []
        for a in range(n):
            cp = pltpu.make_async_remote_copy(
                src_ref=srcs[a], dst_ref=slot_of(a, my), send_sem=own_send.at[a], recv_sem=own_recv.at[a],
                device_id=(x, y, 1 - c), device_id_type=MESH)
            cp.start()
            sends.append(cp)
        for a in range(n):
            rh = geo[a][0]
            for r, (fx, fy) in enumerate(CHIP_RELS):
                cp = pltpu.make_async_remote_copy(
                    src_ref=srcs[a].at[pl.ds(c * rh, rh), :], dst_ref=region(a, c, my),
                    send_sem=ici_send.at[3 * a + r], recv_sem=ici_recv.at[3 * a + r],
                    device_id=(x ^ fx, y ^ fy, c), device_id_type=MESH)
                cp.start()
                sends.append(cp)
        passed = []
        for a in range(n):
            for r, (fx, fy) in enumerate(CHIP_RELS):
                frm = 2 * (x ^ fx) + (y ^ fy)
                landed = region(a, c, frm)
                pltpu.make_async_remote_copy(
                    src_ref=landed, dst_ref=landed, send_sem=ici_send.at[3 * a + r], recv_sem=ici_recv.at[3 * a + r],
                    device_id=(x, y, c), device_id_type=MESH).wait_recv()
                cp = pltpu.make_async_remote_copy(
                    src_ref=landed, dst_ref=landed, send_sem=d2d_send.at[3 * a + r], recv_sem=d2d_recv.at[3 * a + r],
                    device_id=(x, y, 1 - c), device_id_type=MESH)
                cp.start()
                passed.append(cp)
        for a in range(n):
            for r, (fx, fy) in enumerate(CHIP_RELS):
                frm = 2 * (x ^ fx) + (y ^ fy)
                other = region(a, 1 - c, frm)
                pltpu.make_async_remote_copy(
                    src_ref=other, dst_ref=other, send_sem=d2d_send.at[3 * a + r], recv_sem=d2d_recv.at[3 * a + r],
                    device_id=(x, y, c), device_id_type=MESH).wait_recv()
        for a in range(n):
            pltpu.make_async_remote_copy(
                src_ref=srcs[a], dst_ref=slot_of(a, my), send_sem=own_send.at[a], recv_sem=own_recv.at[a],
                device_id=(x, y, c), device_id_type=MESH).wait_recv()
        for cp in sends + passed:
            cp.wait_send()

    return pl.kernel(
        body, name=name,
        out_type=[jax.ShapeDtypeStruct(_full_shape(k, s.shape), s.dtype) for s, k in zip(shards, kinds)],
        mesh=plsc.ScalarSubcoreMesh(axis_name="sequencer", num_cores=1),
        scratch_types=[pltpu.SemaphoreType.DMA((3 * n,))] * 4 + [pltpu.SemaphoreType.DMA((n,))] * 2,
        compiler_params=pltpu.CompilerParams(collective_id=GATHER_COLLECTIVE_ID),
    )(*shards)


def _swap_halves(fulls, kinds, shard_shapes, name):
    n = len(fulls)
    geo = [(_half_rows(s), s[1]) for s in shard_shapes]

    def body(*refs):
        srcs, outs = refs[:n], refs[n:2 * n]
        send_sems, recv_sems = refs[2 * n:]
        x, y, c = _mesh_pos()
        barrier = pltpu.get_barrier_semaphore()
        pl.semaphore_signal(barrier, inc=1, device_id=(x, y, 1 - c), device_id_type=MESH)
        pl.semaphore_wait(barrier, 1)
        cps = []
        for a in range(n):
            for s in range(N_CHIPS):
                cp = pltpu.make_async_remote_copy(
                    src_ref=_half_slot(srcs[a], kinds[a], geo[a][0], geo[a][1], 1 - c, s), dst_ref=outs[a].at[s],
                    send_sem=send_sems.at[N_CHIPS * a + s], recv_sem=recv_sems.at[N_CHIPS * a + s],
                    device_id=(x, y, 1 - c), device_id_type=MESH)
                cp.start()
                cps.append(cp)
        for cp in cps:
            cp.wait()

    return pl.kernel(
        body, name=name,
        out_type=[jax.ShapeDtypeStruct((N_CHIPS,) + g, f.dtype) for g, f in zip(geo, fulls)],
        mesh=plsc.ScalarSubcoreMesh(axis_name="sequencer", num_cores=1),
        scratch_types=[pltpu.SemaphoreType.DMA((N_CHIPS * n,))] * 2,
        compiler_params=pltpu.CompilerParams(collective_id=SWAP_COLLECTIVE_ID),
    )(*fulls)


def _add_own_half(full, recv, kind, shard_shape, cidx, name):
    rh, cols = _half_rows(shard_shape), shard_shape[1]
    tr = _pick(rh, (256, 128, 64, 32, 16))
    nb = rh // tr

    def body(c_ref, f_ref, r_ref, o_ref):
        o_ref[...] = (f_ref[...].astype(F32) + r_ref[...].astype(F32)).astype(o_ref.dtype)

    if kind == "col":
        f_spec = pl.BlockSpec((tr, cols), lambda s, i, c_ref: (c_ref[0] * nb + i, s))
    else:
        f_spec = pl.BlockSpec((tr, cols), lambda s, i, c_ref: ((s * 2 + c_ref[0]) * nb + i, 0))
    blk = pl.BlockSpec((None, tr, cols), lambda s, i, c_ref: (s, i, 0))
    return _pcall(
        body, name=name,
        grid_spec=pltpu.PrefetchScalarGridSpec(num_scalar_prefetch=1, grid=(N_CHIPS, nb), in_specs=[f_spec, blk], out_specs=blk),
        out_shape=jax.ShapeDtypeStruct((N_CHIPS, rh, cols), BF16),
        compiler_params=_cp(("arbitrary", "arbitrary")),
    )(cidx, full, recv)


def _scatter_partials(partials, name):
    n = len(partials)

    def body(*refs):
        srcs, outs = refs[:n], refs[n:2 * n]
        send_sems, recv_sems = refs[2 * n:]
        x, y, c = _mesh_pos()
        barrier = pltpu.get_barrier_semaphore()
        for fx, fy in CHIP_RELS:
            pl.semaphore_signal(barrier, inc=1, device_id=(x ^ fx, y ^ fy, c), device_id_type=MESH)
        pl.semaphore_wait(barrier, len(CHIP_RELS))
        my = 2 * x + y
        cps = []
        for a in range(n):
            for r, (fx, fy) in enumerate(CHIP_RELS):
                to = 2 * (x ^ fx) + (y ^ fy)
                cp = pltpu.make_async_remote_copy(
                    src_ref=srcs[a].at[to], dst_ref=outs[a].at[my],
                    send_sem=send_sems.at[3 * a + r], recv_sem=recv_sems.at[3 * a + r],
                    device_id=(x ^ fx, y ^ fy, c), device_id_type=MESH)
                cp.start()
                cps.append(cp)
        for a in range(n):
            for r, (fx, fy) in enumerate(CHIP_RELS):
                frm = 2 * (x ^ fx) + (y ^ fy)
                pltpu.make_async_remote_copy(
                    src_ref=outs[a].at[frm], dst_ref=outs[a].at[frm],
                    send_sem=send_sems.at[3 * a + r], recv_sem=recv_sems.at[3 * a + r],
                    device_id=(x, y, c), device_id_type=MESH).wait_recv()
        for cp in cps:
            cp.wait_send()

    return pl.kernel(
        body, name=name,
        out_type=[jax.ShapeDtypeStruct(p.shape, p.dtype) for p in partials],
        mesh=plsc.ScalarSubcoreMesh(axis_name="sequencer", num_cores=1),
        scratch_types=[pltpu.SemaphoreType.DMA((3 * n,))] * 2,
        compiler_params=pltpu.CompilerParams(collective_id=SCATTER_COLLECTIVE_ID),
    )(*partials)


def _sum_chips_into(landed, partial, buf, layer, n_layers, pos, name):
    _, rh, cols = landed.shape
    tr = _pick(rh, (256, 128, 64, 32, 16))
    nb = rh // tr

    def body(pos_ref, l_ref, own_ref, *rest):
        o_ref = rest[-1]
        my = pos_ref[1]
        own = own_ref[...].astype(F32)
        acc = jnp.where(my == 0, own, l_ref[0].astype(F32))
        for j in range(1, N_CHIPS):
            acc = acc + jnp.where(my == j, own, l_ref[j].astype(F32))
        o_ref[...] = acc

    in_specs = [pl.BlockSpec((N_CHIPS, tr, cols), lambda i, pos_ref: (0, i, 0)),
                pl.BlockSpec((None, tr, cols), lambda i, pos_ref: (pos_ref[1], i, 0))]
    args = [pos, landed, partial]
    aliases = {}
    if buf is not None:
        in_specs.append(pl.BlockSpec(memory_space=pl.ANY))
        args.append(buf)
        aliases = {3: 0}
    return _pcall(
        body, name=name,
        grid_spec=pltpu.PrefetchScalarGridSpec(
            num_scalar_prefetch=1, grid=(nb,), in_specs=in_specs,
            out_specs=pl.BlockSpec((None, tr, cols), lambda i, pos_ref: (layer, pos_ref[0] * nb + i, 0))),
        out_shape=jax.ShapeDtypeStruct((n_layers, 2 * rh, cols), F32), input_output_aliases=aliases,
        compiler_params=_cp(("arbitrary",)),
    )(*args)


def _exchange_final(bufs):
    n = len(bufs)
    HBM = pl.BlockSpec(memory_space=pl.ANY)
    n_layers = [b.shape[0] for b in bufs]
    base = np.concatenate([[0], np.cumsum(n_layers)])

    def body(*refs):
        outs = refs[n:2 * n]
        send_sems, recv_sems = refs[2 * n:]
        x, y, c = _mesh_pos()
        cps = []
        for w in range(n):
            rh = bufs[w].shape[1] // 2
            for l in range(n_layers[w]):
                k = int(base[w]) + l
                mine = outs[w].at[l, pl.ds(c * rh, rh), :]
                cp = pltpu.make_async_remote_copy(
                    src_ref=mine, dst_ref=mine, send_sem=send_sems.at[k], recv_sem=recv_sems.at[k],
                    device_id=(x, y, 1 - c), device_id_type=MESH)
                cp.start()
                cps.append(cp)
        for w in range(n):
            rh = bufs[w].shape[1] // 2
            for l in range(n_layers[w]):
                k = int(base[w]) + l
                other = outs[w].at[l, pl.ds((1 - c) * rh, rh), :]
                pltpu.make_async_remote_copy(
                    src_ref=other, dst_ref=other, send_sem=send_sems.at[k], recv_sem=recv_sems.at[k],
                    device_id=(x, y, c), device_id_type=MESH).wait_recv()
        for cp in cps:
            cp.wait_send()

    return _pcall(
        body, name="grad_exchange_final", out_shape=[jax.ShapeDtypeStruct(b.shape, b.dtype) for b in bufs],
        in_specs=[HBM] * n, out_specs=[HBM] * n, input_output_aliases={i: i for i in range(n)},
        scratch_shapes=[pltpu.SemaphoreType.DMA((int(base[-1]),))] * 2,
        compiler_params=pltpu.CompilerParams(vmem_limit_bytes=VMEM_LIMIT),
    )(*bufs)


def _swap_halves_last(z):
    h = z.shape[-1] // 2
    return jnp.concatenate([z[..., h:], z[..., :h]], axis=-1)


def _ext_uq(w):
    lead = w.shape[:-1]
    wh = w.reshape(lead + (-1, QK_DIM))
    rope = wh[..., NOPE:]
    return jnp.concatenate([wh, _swap_halves_last(rope)], axis=-1).reshape(lead + (-1,))


def _fold_uq(d):
    lead = d.shape[:-1]
    dh = d.reshape(lead + (-1, HEAD_W))
    rope = dh[..., NOPE:QK_DIM] + _swap_halves_last(dh[..., QK_DIM:])
    return jnp.concatenate([dh[..., :NOPE], rope], axis=-1).reshape(lead + (-1,))


def _ext_win(w):
    base, kr = w[..., :-ROPE], w[..., -ROPE:]
    ks = _swap_halves_last(kr)
    return jnp.concatenate([base, kr, ks, ks, kr], axis=-1)


def _fold_win(d):
    n = d.shape[-1] - 4 * ROPE
    a, b, c2, e = [d[..., n + i * ROPE:n + (i + 1) * ROPE] for i in range(4)]
    return jnp.concatenate([d[..., :n], a + e + _swap_halves_last(b + c2)], axis=-1)


ROW_TILE = 256


def _stage(st, xcur, mix, big, small, tabs):
    D = xcur.shape[1]
    l = st // 2
    mod = [[small["mod"][k][i].reshape(1, 1, D) for i in range(6)] for k in range(l + 1)]
    sh1, sc1, g1, sh2, sc2, _ = mod[l]
    if st % 2 == 0:
        n1 = small["norm1_g"][l].reshape(1, 1, D)
        if l == 0:
            (h,) = _rowop("norm_mod_0", _f_norm_mod, n_x=1, n_nd=0, n_p=3, x_w=[D], nd_w=[], nd_shared=[], p_per_group=[False] * 3,
                          out_w=[D], out_dtypes=[BF16], tile=ROW_TILE)((xcur,), (), (n1, sh1, sc1))
        else:
            op = _rowop(f"resid_norm_mod_a{l}", _f_resid_norm_mod, n_x=2, n_nd=0, n_p=4, x_w=[D, D], nd_w=[], nd_shared=[],
                        p_per_group=[False] * 4, out_w=[D, D], out_dtypes=[F32, BF16], tile=ROW_TILE)
            xcur, h = op((xcur, mix), (), (mod[l - 1][5], n1, sh1, sc1))
        if l % 2 == 0:
            return xcur, _even_mixer(h, big, small, l // 2, ROW_TILE)
        return xcur, _mla_mixer(h, big, small, l // 2, ROW_TILE, *tabs)
    n2 = small["norm2_g"][l].reshape(1, 1, D)
    op = _rowop(f"resid_norm_mod_b{l}", _f_resid_norm_mod, n_x=2, n_nd=0, n_p=4, x_w=[D, D], nd_w=[], nd_shared=[],
                p_per_group=[False] * 4, out_w=[D, D], out_dtypes=[F32, BF16], tile=ROW_TILE)
    xcur, h = op((xcur, mix), (), (g1, n2, sh2, sc2))
    return xcur, _mlp(f"mlp_{l}")(h, big["mlp_w1"], big["mlp_w2"])


def _last_residual(xcur, mix, small):
    D = xcur.shape[1]
    gate = small["mod"][-1][5].reshape(1, 1, D)
    return _rowop("resid_last", _f_resid, n_x=2, n_nd=0, n_p=1, x_w=[D, D], nd_w=[], nd_shared=[], p_per_group=[False],
                  out_w=[D], out_dtypes=[F32], tile=ROW_TILE)((xcur, mix), (), (gate,))[0]


def _even_mixer(h, big, wts, e, tile):
    proj = _linear(f"ab_in_{e}")(h, big["ab_w_in"])
    da = proj.shape[1] // 4
    u, v, a, g = [proj[:, i * da:(i + 1) * da] for i in range(4)]
    ng = da // GROUP
    sgu = _rowop(f"sgu_{e}", _f_sgu, n_x=2, n_nd=0, n_p=3, x_w=[GROUP, GROUP], nd_w=[], nd_shared=[], p_per_group=[True] * 3,
                 out_w=[GROUP], out_dtypes=[BF16], tile=SGU_CHUNKS_PER_STEP * CHUNK, groups=ng)
    bexp = jnp.broadcast_to(wts["sgu_b"][e][:, :, None], (ng, CHUNK, GROUP))
    (out_a,) = sgu((u, v), (), (wts["sgu_norm_g"][e].reshape(ng, 1, GROUP), wts["sgu_w"][e], bexp))
    (yglu,) = _rowop(f"glu_{e}", _f_glu, n_x=2, n_nd=0, n_p=0, x_w=[da, da], nd_w=[], nd_shared=[], p_per_group=[],
                     out_w=[da], out_dtypes=[F32], tile=tile)((a, g), (), ())
    z = _conv_op(f"conv_{e}")(yglu, wts["conv_w"][e], wts["conv_b"][e].reshape(1, da))
    (out_b,) = _rowop(f"ln_silu_{e}", _f_ln_silu, n_x=1, n_nd=0, n_p=2, x_w=[da], nd_w=[], nd_shared=[], p_per_group=[False] * 2,
                      out_w=[da], out_dtypes=[BF16], tile=tile)(
        (z,), (), (wts["conv_ln_g"][e].reshape(1, 1, da), wts["conv_ln_b"][e].reshape(1, 1, da)))
    return _linear(f"ab_out_{e}")(jnp.concatenate([out_a, out_b], axis=-1), big["ab_w_out"])


def _mla_mixer(h, big, wts, o, tile, tab_q, tab_ka, tab_kb):
    proj = _linear(f"mla_in_{o}")(h, big["mla_w_in"])
    rank = (proj.shape[1] - 4 * ROPE) // 2
    c_q, c_kv = proj[:, :rank], proj[:, rank:2 * rank]
    kr_a, kr_b = proj[:, 2 * rank:2 * rank + 2 * ROPE], proj[:, 2 * rank + 2 * ROPE:]

    def rms(name, xx, gg):
        return _rowop(name, _f_rms, n_x=1, n_nd=0, n_p=1, x_w=[rank], nd_w=[], nd_shared=[], p_per_group=[False],
                      out_w=[rank], out_dtypes=[BF16], tile=tile)((xx,), (), (gg.reshape(1, 1, rank),))[0]

    q_raw = _linear(f"mla_uq_{o}")(rms(f"rms_q_{o}", c_q, wts["mla_q_norm_g"][o]), big["mla_w_uq"])
    kv_raw = _linear(f"mla_ukv_{o}")(rms(f"rms_kv_{o}", c_kv, wts["mla_kv_norm_g"][o]), big["mla_w_ukv"])
    gq, gk = wts["mla_q_head_g"][o], wts["mla_k_head_g"][o]
    gk_rope = gk[NOPE:]
    gq_ext = jnp.concatenate([gq, _swap_halves_last(gq[NOPE:])]).reshape(1, 1, HEAD_W)
    gk_ext = jnp.concatenate([gk[:NOPE], jnp.ones((HEAD_W - NOPE,), F32)]).reshape(1, 1, HEAD_W)
    gk_a = jnp.concatenate([gk_rope, _swap_halves_last(gk_rope)]).reshape(1, 1, 2 * ROPE)
    gk_b = jnp.concatenate([_swap_halves_last(gk_rope), gk_rope]).reshape(1, 1, 2 * ROPE)
    head_tile = 4 * tile
    (q,) = _rowop(f"q_head_{o}", _f_qhead, n_x=1, n_nd=1, n_p=1, x_w=[HEAD_W], nd_w=[HEAD_W], nd_shared=[True], p_per_group=[False],
                  out_w=[HEAD_W], out_dtypes=[BF16], tile=head_tile, groups=N_HEADS)((q_raw,), (tab_q,), (gq_ext,))
    (kv,) = _rowop(f"kv_head_{o}", _f_kvhead, n_x=1, n_nd=0, n_p=1, x_w=[HEAD_W], nd_w=[], nd_shared=[], p_per_group=[False],
                   out_w=[HEAD_W], out_dtypes=[BF16], tile=head_tile, groups=N_HEADS)((kv_raw,), (), (gk_ext,))
    (kr,) = _rowop(f"k_rope_{o}", _f_krope, n_x=2, n_nd=2, n_p=2, x_w=[2 * ROPE] * 2, nd_w=[2 * ROPE] * 2, nd_shared=[True] * 2,
                   p_per_group=[False] * 2, out_w=[2 * ROPE], out_dtypes=[BF16], tile=tile)((kr_a, kr_b), (tab_ka, tab_kb), (gk_a, gk_b))
    att = _attn_op(f"attn_{o}")(q, kv, kr)
    return _linear(f"mla_out_{o}")(att, big["mla_w_out"])


def _rope_tabs(S):
    pos = jnp.arange(S, dtype=F32)
    inv = ROPE_THETA ** (-jnp.arange(0, ROPE, 2, dtype=F32) / ROPE)
    ang = pos[:, None] * inv[None, :]
    cos, sin = jnp.cos(ang), jnp.sin(ang)
    cc = jnp.concatenate([cos, cos], axis=-1)
    sg = jnp.concatenate([-sin, sin], axis=-1)
    tab_q = jnp.concatenate([jnp.ones((S, NOPE), F32), cc, sg], axis=-1)
    return tab_q, jnp.concatenate([cc, sg], axis=-1), jnp.concatenate([sg, cc], axis=-1)


def kernel(x, c, norm1_g, norm2_g, ada_w, ada_b, mlp_w1, mlp_w2, ab_w_in, sgu_norm_g, sgu_w, sgu_b, conv_w, conv_b, conv_ln_g, conv_ln_b, ab_w_out, mla_w_in, mla_q_norm_g, mla_kv_norm_g, mla_w_uq, mla_w_ukv, mla_q_head_g, mla_k_head_g, mla_w_out, loss_target, m_norm1_g, m_norm2_g, m_ada_w, m_ada_b, m_mlp_w1, m_mlp_w2, m_ab_w_in, m_sgu_norm_g, m_sgu_w, m_sgu_b, m_conv_w, m_conv_b, m_conv_ln_g, m_conv_ln_b, m_ab_w_out, m_mla_w_in, m_mla_q_norm_g, m_mla_kv_norm_g, m_mla_w_uq, m_mla_w_ukv, m_mla_q_head_g, m_mla_k_head_g, m_mla_w_out, v_norm1_g, v_norm2_g, v_ada_w, v_ada_b, v_mlp_w1, v_mlp_w2, v_ab_w_in, v_sgu_norm_g, v_sgu_w, v_sgu_b, v_conv_w, v_conv_b, v_conv_ln_g, v_conv_ln_b, v_ab_w_out, v_mla_w_in, v_mla_q_norm_g, v_mla_kv_norm_g, v_mla_w_uq, v_mla_w_ukv, v_mla_q_head_g, v_mla_k_head_g, v_mla_w_out):
    names = ["norm1_g", "norm2_g", "ada_w", "ada_b", "mlp_w1", "mlp_w2", "ab_w_in", "sgu_norm_g", "sgu_w", "sgu_b", "conv_w",
             "conv_b", "conv_ln_g", "conv_ln_b", "ab_w_out", "mla_w_in", "mla_q_norm_g", "mla_kv_norm_g", "mla_w_uq", "mla_w_ukv",
             "mla_q_head_g", "mla_k_head_g", "mla_w_out"]
    W = dict(zip(names, [norm1_g, norm2_g, ada_w, ada_b, mlp_w1, mlp_w2, ab_w_in, sgu_norm_g, sgu_w, sgu_b, conv_w, conv_b, conv_ln_g,
                         conv_ln_b, ab_w_out, mla_w_in, mla_q_norm_g, mla_kv_norm_g, mla_w_uq, mla_w_ukv, mla_q_head_g, mla_k_head_g,
                         mla_w_out]))
    M = dict(zip(names, [m_norm1_g, m_norm2_g, m_ada_w, m_ada_b, m_mlp_w1, m_mlp_w2, m_ab_w_in, m_sgu_norm_g, m_sgu_w, m_sgu_b, m_conv_w,
                         m_conv_b, m_conv_ln_g, m_conv_ln_b, m_ab_w_out, m_mla_w_in, m_mla_q_norm_g, m_mla_kv_norm_g, m_mla_w_uq,
                         m_mla_w_ukv, m_mla_q_head_g, m_mla_k_head_g, m_mla_w_out]))
    V = dict(zip(names, [v_norm1_g, v_norm2_g, v_ada_w, v_ada_b, v_mlp_w1, v_mlp_w2, v_ab_w_in, v_sgu_norm_g, v_sgu_w, v_sgu_b, v_conv_w,
                         v_conv_b, v_conv_ln_g, v_conv_ln_b, v_ab_w_out, v_mla_w_in, v_mla_q_norm_g, v_mla_kv_norm_g, v_mla_w_uq,
                         v_mla_w_ukv, v_mla_q_head_g, v_mla_k_head_g, v_mla_w_out]))
    xi, yi, ci = lax.axis_index("x"), lax.axis_index("y"), lax.axis_index("c")
    chip = 2 * xi + yi
    dev = 2 * chip + ci
    pos = jnp.stack([ci, chip]).astype(jnp.int32)
    S, D = x.shape[1], x.shape[2]
    depth = norm1_g.shape[0]

    c_all, conv_w_all, qn_all, kvn_all = _all_gather_small([c, conv_w, mla_q_norm_g, mla_kv_norm_g], "gather_small_inputs")
    c_all = c_all.reshape(N_DEV, D)
    by_chip = lambda a: jnp.concatenate([a[2 * j] for j in range(N_CHIPS)], axis=-1)
    conv_w_full, qn_full, kvn_full = by_chip(conv_w_all), by_chip(qn_all), by_chip(kvn_all)

    (c_act,) = _rowop("silu_c", _f_silu, n_x=1, n_nd=0, n_p=0, x_w=[D], nd_w=[], nd_shared=[], p_per_group=[], out_w=[D],
                      out_dtypes=[F32], tile=N_DEV)((c_all,), (), ())
    c_act_pad = jnp.pad(c_act, ((0, 128 - N_DEV), (0, 0)))
    mod_cols = jnp.stack([_matmul(c_act_pad, ada_w[l], name=f"ada_fwd_{l}")[:N_DEV] for l in range(depth)])
    (mod_all,) = _all_gather_small([mod_cols], "gather_mod")
    mod_mine = jnp.concatenate([lax.dynamic_index_in_dim(mod_all[2 * j], dev, axis=1, keepdims=False) for j in range(N_CHIPS)], axis=-1)
    mod_mine = (mod_mine + ada_b).reshape(depth, 6, D)

    big = {"mlp_w1": "col", "mlp_w2": "row", "ab_w_in": "col", "ab_w_out": "row", "mla_w_in": "row", "mla_w_uq": "col",
           "mla_w_ukv": "col", "mla_w_out": "row"}
    src = dict(W)
    src["mla_w_in"] = _ext_win(mla_w_in)
    src["mla_w_uq"] = _ext_uq(mla_w_uq)
    shards, kinds, owner = [], [], []
    for nme, kind in big.items():
        for l in range(src[nme].shape[0]):
            shards.append(src[nme][l].astype(BF16))
            kinds.append(kind)
            owner.append((nme, l))
    def stage_of(nme, l):
        return 2 * l + 1 if nme.startswith("mlp") else (4 * l if nme.startswith("ab") else 4 * l + 2)

    fwd_stages = [[a for a, (nme, l) in enumerate(owner) if stage_of(nme, l) == st] for st in range(2 * depth)]
    fulls = [None] * len(shards)
    for st, idx in enumerate(fwd_stages):
        got = _gather_big([shards[a] for a in idx], [kinds[a] for a in idx], f"gather_weights_{st}")
        for a, f in zip(idx, got):
            fulls[a] = f
    small = dict(norm1_g=norm1_g, norm2_g=norm2_g, mod=mod_mine, sgu_norm_g=sgu_norm_g, sgu_w=sgu_w, sgu_b=sgu_b, conv_w=conv_w_full,
                 conv_b=conv_b, conv_ln_g=conv_ln_g, conv_ln_b=conv_ln_b, mla_q_norm_g=qn_full, mla_kv_norm_g=kvn_full,
                 mla_q_head_g=mla_q_head_g, mla_k_head_g=mla_k_head_g)
    groups, a0 = [], 0
    for nme in big:
        groups.append(list(range(a0, a0 + src[nme].shape[0])))
        a0 += src[nme].shape[0]
    shard_shapes = [s.shape for s in shards]
    where = {a: (wi, l) for wi, idxs in enumerate(groups) for l, a in enumerate(idxs)}

    tabs = _rope_tabs(S)
    n_stages = 2 * depth
    xcur, mix, vjps = x[0], None, []
    for st, idx in enumerate(fwd_stages):
        stage_big = {owner[a][0]: fulls[a] for a in idx}
        if st == 0:
            (xcur, mix), vjp = jax.vjp(lambda xx, bb, ss: _stage(0, xx, None, bb, ss, tabs), xcur, stage_big, small)
        else:
            (xcur, mix), vjp = jax.vjp(lambda xx, mm, bb, ss, st=st: _stage(st, xx, mm, bb, ss, tabs), xcur, mix, stage_big, small)
        vjps.append(vjp)
    y, vjp_last = jax.vjp(_last_residual, xcur, mix, small)
    dy, loss_mine = _loss_call(y, loss_target[0])
    loss = lax.psum(loss_mine[0, 0], ("x", "y", "c"))

    dxc, dmix, dsmall = vjp_last(dy)
    bufs = [None] * len(groups)
    swapped, scattered = {}, {}

    def add_and_scatter(st):
        idx, gr, recv = swapped.pop(st)
        partials = [_add_own_half(gr[a], r, kinds[a], shard_shapes[a], pos, f"grad_add_sibling_{a}") for a, r in zip(idx, recv)]
        scattered[st] = (idx, partials, _scatter_partials(partials, f"grad_scatter_partials_{st}"))

    def sum_chips(st):
        idx, partials, landed = scattered.pop(st)
        for a, part, land in zip(idx, partials, landed):
            wi, l = where[a]
            bufs[wi] = _sum_chips_into(land, part, bufs[wi], l, len(groups[wi]), pos, f"grad_sum_chips_{a}")

    for st in reversed(range(n_stages)):
        idx = fwd_stages[st]
        if st == 0:
            dxc, dbig, ds = vjps[st]((dxc, dmix))
        else:
            dxc, dmix, dbig, ds = vjps[st]((dxc, dmix))
        dsmall = jax.tree.map(jnp.add, dsmall, ds)
        gr = {a: dbig[owner[a][0]] for a in idx}
        swapped[st] = (idx, gr, _swap_halves([gr[a] for a in idx], [kinds[a] for a in idx], [shard_shapes[a] for a in idx],
                                             f"grad_swap_halves_{st}"))
        started = [s for s in ((st + 1, st) if st == 1 else (st + 1,)) if s in swapped]
        for s in started:
            add_and_scatter(s)
        if st + 2 in scattered:
            sum_chips(st + 2)
        if st > 0:
            pinned = {s: scattered[s][1] for s in started}
            (dxc, dmix), pinned, bufs = lax.optimization_barrier(((dxc, dmix), pinned, bufs))
            for s in started:
                scattered[s] = (scattered[s][0], pinned[s], scattered[s][2])
    dx = dxc
    for st in sorted(swapped, reverse=True):
        add_and_scatter(st)
    for st in sorted(scattered, reverse=True):
        sum_chips(st)
    gsh = _exchange_final(bufs)

    dw = dsmall
    dmod = dw["mod"].reshape(depth, 6 * D)
    small_names = ["norm1_g", "norm2_g", "sgu_norm_g", "sgu_w", "sgu_b", "conv_w", "conv_b", "conv_ln_g", "conv_ln_b", "mla_q_norm_g",
                   "mla_kv_norm_g", "mla_q_head_g", "mla_k_head_g"]
    red = _all_reduce_small([dmod] + [dw[nme] for nme in small_names], "reduce_small_grads")
    G = dict(zip(["ada_b"] + small_names, red))
    own_cols = lambda a: lax.dynamic_slice_in_dim(a, chip * (a.shape[-1] // N_CHIPS), a.shape[-1] // N_CHIPS, axis=-1)
    for nme in ("conv_w", "mla_q_norm_g", "mla_kv_norm_g"):
        G[nme] = own_cols(G[nme])

    (dmod_all,) = _all_gather_small([dmod], "gather_dmod")
    dmod_cols = own_cols(dmod_all)
    dmod_pad = jnp.pad(dmod_cols, ((0, 128 - N_DEV), (0, 0), (0, 0)))
    G["ada_w"] = jnp.stack([_matmul(c_act_pad, dmod_pad[:, l], ta=True, name=f"ada_wgrad_{l}") for l in range(depth)])

    for nme, g in zip(big, gsh):
        G[nme] = g
    G["mla_w_in"] = _fold_win(G["mla_w_in"])
    G["mla_w_uq"] = _fold_uq(G["mla_w_uq"])

    deltas, new_m, new_v = [], [], []
    for nme in names:
        d, mn, vn = _adamw_call(W[nme], G[nme], M[nme], V[nme], f"adamw_{nme}")
        deltas.append(d)
        new_m.append(mn)
        new_v.append(vn)
    return (loss, dx[None], *[G[nme] for nme in names], *deltas, *new_m, *new_v)
```

```python
import functools
import numpy as np
import jax
import jax.numpy as jnp
from jax import lax
from jax.experimental import pallas as pl
from jax.experimental.pallas import tpu as pltpu
from jax.experimental.pallas import tpu_sc as plsc

F32 = jnp.float32
BF16 = jnp.bfloat16
MESH = pl.DeviceIdType.MESH

EPS = 1e-6
N_HEADS = 16
NOPE = 128
ROPE = 64
VDIM = 128
QK_DIM = NOPE + ROPE
HEAD_W = 256
CHUNK = 128
GROUP = 128
SGU_CHUNKS_PER_STEP = 4
CONV_W = 31
CONV_PAD = 32
ROPE_THETA = 10000.0
ATTN_SCALE = QK_DIM ** -0.5
LOG2E = 1.4426950408889634
ATTN_SCALE_LOG2E = ATTN_SCALE * LOG2E
ADAM_LR, ADAM_B1, ADAM_B2, ADAM_EPS, ADAM_WD, ADAM_STEP = 0.001, 0.9, 0.999, 1e-08, 0.01, 10
N_CHIPS = 4
N_DEV = 8
VMEM_LIMIT = 56 * 1024 * 1024
SMALL_COLS = 1024
CHIP_RELS = ((1, 0), (0, 1), (1, 1))
GATHER_COLLECTIVE_ID = 0
SWAP_COLLECTIVE_ID = 1
SCATTER_COLLECTIVE_ID = 2


def _pcall(body, **kw):
    return pl.pallas_call(body, **kw)


def _cp(sem=None, **kw):
    return pltpu.CompilerParams(dimension_semantics=sem, vmem_limit_bytes=VMEM_LIMIT, **kw)


def _pick(n, cands):
    for c in cands:
        if n % c == 0:
            return c
    return n


def _matmul(a, b, *, ta=False, tb=False, out_dtype=F32, name, extra=(), epilogue=None, out_dtypes=None):
    if ta:
        K, M = a.shape
    else:
        M, K = a.shape
    if tb:
        N, K2 = b.shape
    else:
        K2, N = b.shape
    assert K == K2, (a.shape, b.shape, ta, tb)
    tm = _pick(M, (1024, 512, 256, 128))
    tn = _pick(N, (1024, 512, 256, 128))
    tk = _pick(K, (2048, 1024, 512, 256, 128))
    nk = K // tk
    dn = (((0 if ta else 1,), (1 if tb else 0,)), ((), ()))

    n_extra = len(extra)
    single = epilogue is None
    if single:
        out_dtypes = [out_dtype]

    n_out = len(out_dtypes)

    def body(a_ref, b_ref, *rest):
        extra_refs, o_refs = rest[:n_extra], rest[n_extra:n_extra + n_out]
        part = lax.dot_general(a_ref[...].astype(BF16), b_ref[...].astype(BF16), dn, preferred_element_type=F32)

        def finish(acc):
            res = (acc,) if single else epilogue(acc, *[r[...] for r in extra_refs])
            for o_ref, val in zip(o_refs, res):
                o_ref[...] = val.astype(o_ref.dtype)

        if nk == 1:
            finish(part)
            return
        acc_ref = rest[-1]
        k = pl.program_id(2)

        @pl.when(k == 0)
        def _():
            acc_ref[...] = part

        @pl.when(jnp.logical_and(k > 0, k < nk - 1))
        def _():
            acc_ref[...] += part

        @pl.when(k == nk - 1)
        def _():
            finish(acc_ref[...] + part)

    a_spec = pl.BlockSpec((tk, tm), lambda i, j, k: (k, i)) if ta else pl.BlockSpec((tm, tk), lambda i, j, k: (i, k))
    b_spec = pl.BlockSpec((tn, tk), lambda i, j, k: (j, k)) if tb else pl.BlockSpec((tk, tn), lambda i, j, k: (k, j))
    mn_spec = pl.BlockSpec((tm, tn), lambda i, j, k: (i, j))
    outs = _pcall(
        body, name=name, grid=(M // tm, N // tn, nk), in_specs=[a_spec, b_spec] + [mn_spec] * n_extra,
        out_specs=[mn_spec] * len(out_dtypes),
        out_shape=[jax.ShapeDtypeStruct((M, N), d) for d in out_dtypes],
        scratch_shapes=[pltpu.VMEM((tm, tn), F32)] if nk > 1 else [],
        compiler_params=_cp(("parallel", "parallel", "arbitrary")),
    )(a, b, *extra)
    return outs[0] if single else outs


def _linear(name, out_dtype=F32):
    @jax.custom_vjp
    def mm(a, w):
        return _matmul(a, w, out_dtype=out_dtype, name=name + "_fwd")

    def fwd(a, w):
        return mm(a, w), (a, w)

    def bwd(res, dy):
        a, w = res
        da = _matmul(dy, w, tb=True, out_dtype=a.dtype, name=name + "_dgrad")
        dw = _matmul(a, dy, ta=True, out_dtype=w.dtype, name=name + "_wgrad")
        return da, dw

    mm.defvjp(fwd, bwd)
    return mm


def _relu2_epilogue(acc):
    return acc, jnp.square(jnp.maximum(acc, 0.0))


def _relu2_grad_epilogue(acc, h1):
    return (acc * (2.0 * jnp.maximum(h1.astype(F32), 0.0)),)


def _mlp(name):
    def run(h, w1, w2):
        h1, act = _matmul(h, w1, name=name + "_up_fwd", epilogue=_relu2_epilogue, out_dtypes=[BF16, BF16])
        return _matmul(act, w2, name=name + "_down_fwd"), h1, act

    @jax.custom_vjp
    def mlp(h, w1, w2):
        return run(h, w1, w2)[0]

    def fwd(h, w1, w2):
        y, h1, act = run(h, w1, w2)
        return y, (h, w1, w2, h1, act)

    def bwd(res, dy):
        h, w1, w2, h1, act = res
        dw2 = _matmul(act, dy, ta=True, out_dtype=w2.dtype, name=name + "_down_wgrad")
        (dh1,) = _matmul(dy, w2, tb=True, name=name + "_down_dgrad", extra=(h1,), epilogue=_relu2_grad_epilogue, out_dtypes=[BF16])
        dw1 = _matmul(h, dh1, ta=True, out_dtype=w1.dtype, name=name + "_up_wgrad")
        dh = _matmul(dh1, w1, tb=True, out_dtype=h.dtype, name=name + "_up_dgrad")
        return dh, dw1, dw2

    mlp.defvjp(fwd, bwd)
    return mlp


def _rowop(name, f, *, n_x, n_nd, n_p, x_w, nd_w, nd_shared, p_per_group, out_w, out_dtypes, tile, groups=1):
    G = groups

    def specs(S):
        t = min(tile, S)
        xs = [pl.BlockSpec((t, w), lambda g, r: (r, g)) for w in x_w]
        nds = [pl.BlockSpec((t, w), (lambda g, r: (r, 0)) if sh else (lambda g, r: (r, g))) for w, sh in zip(nd_w, nd_shared)]
        outs = [pl.BlockSpec((t, w), lambda g, r: (r, g)) for w in out_w]
        return t, xs, nds, outs

    def pspecs(ps):
        return [pl.BlockSpec((None,) + p.shape[1:], (lambda g, r: (g, 0, 0)) if pg else (lambda g, r: (0, 0, 0)))
                for p, pg in zip(ps, p_per_group)]

    def fwd_call(xs, nds, ps):
        S = xs[0].shape[0]
        t, xsp, ndsp, osp = specs(S)

        def body(*refs):
            ins, outs = refs[:n_x + n_nd + n_p], refs[n_x + n_nd + n_p:]
            vals = [r[...].astype(F32) for r in ins]
            res = f(*vals)
            for o, r in zip(res, outs):
                r[...] = o.astype(r.dtype)

        return _pcall(
            body, name=name + "_fwd", grid=(G, S // t), in_specs=xsp + ndsp + pspecs(ps), out_specs=osp,
            out_shape=[jax.ShapeDtypeStruct((S, G * w), d) for w, d in zip(out_w, out_dtypes)],
            compiler_params=_cp(("parallel", "parallel")),
        )(*xs, *nds, *ps)

    def bwd_call(xs, nds, ps, douts):
        S = xs[0].shape[0]
        t, xsp, ndsp, osp = specs(S)
        n_in = n_x + n_nd + n_p + len(out_w)

        def body(*refs):
            ins, outs = refs[:n_in], refs[n_in:]
            xv = [r[...].astype(F32) for r in ins[:n_x]]
            ndv = [r[...].astype(F32) for r in ins[n_x:n_x + n_nd]]
            pv = [r[...].astype(F32) for r in ins[n_x + n_nd:n_x + n_nd + n_p]]
            dov = tuple(r[...].astype(F32) for r in ins[n_x + n_nd + n_p:])
            _, vjp = jax.vjp(lambda *a: tuple(f(*a[:n_x], *ndv, *a[n_x:])), *xv, *pv)
            cts = vjp(dov)
            for i in range(n_x):
                outs[i][...] = cts[i].astype(outs[i].dtype)
            g, r = pl.program_id(0), pl.program_id(1)
            for i in range(n_p):
                first = (r == 0) if p_per_group[i] else jnp.logical_and(g == 0, r == 0)
                ref, ct = outs[n_x + i], cts[n_x + i]

                @pl.when(first)
                def _(ref=ref, ct=ct):
                    ref[...] = ct

                @pl.when(jnp.logical_not(first))
                def _(ref=ref, ct=ct):
                    ref[...] += ct

        return _pcall(
            body, name=name + "_bwd", grid=(G, S // t), in_specs=xsp + ndsp + pspecs(ps) + osp,
            out_specs=xsp + pspecs(ps),
            out_shape=[jax.ShapeDtypeStruct(x.shape, x.dtype) for x in xs] + [jax.ShapeDtypeStruct(p.shape, F32) for p in ps],
            compiler_params=_cp(("arbitrary", "arbitrary")),
        )(*xs, *nds, *ps, *douts)

    @jax.custom_vjp
    def op(xs, nds, ps):
        return tuple(fwd_call(xs, nds, ps))

    def op_fwd(xs, nds, ps):
        return op(xs, nds, ps), (xs, nds, ps)

    def op_bwd(res, douts):
        xs, nds, ps = res
        out = bwd_call(xs, nds, ps, douts)
        return tuple(out[:n_x]), tuple(jnp.zeros_like(n) for n in nds), tuple(out[n_x:])

    op.defvjp(op_fwd, op_bwd)
    return op


def _rms_rows(x):
    return x * lax.rsqrt(jnp.mean(x * x, axis=-1, keepdims=True) + EPS)


def _f_norm_mod(x, g, shift, scale):
    return ((_rms_rows(x) * g) * (1.0 + scale) + shift,)


def _f_resid_norm_mod(x, mix, gate, g, shift, scale):
    xn = x + gate * mix
    return xn, (_rms_rows(xn) * g) * (1.0 + scale) + shift


def _f_resid(x, mix, gate):
    return (x + gate * mix,)


def _f_rms(x, g):
    return (_rms_rows(x) * g,)


def _f_glu(a, g):
    return (a * jax.nn.sigmoid(g),)


def _f_ln_silu(z, g, b):
    mu = jnp.mean(z, axis=-1, keepdims=True)
    zc = z - mu
    var = jnp.mean(zc * zc, axis=-1, keepdims=True)
    y = zc * lax.rsqrt(var + EPS) * g + b
    return (y * jax.nn.sigmoid(y),)


def _f_silu(x):
    return (x * jax.nn.sigmoid(x),)


@jax.custom_vjp
def _bdot_chunks(a, b):
    return lax.dot_general(a.astype(BF16), b.astype(BF16), (((2,), (1,)), ((0,), (0,))), preferred_element_type=F32)


def _bdot_chunks_fwd(a, b):
    return _bdot_chunks(a, b), (a, b)


def _bdot_chunks_bwd(res, ct):
    a, b = res
    c16 = ct.astype(BF16)
    da = lax.dot_general(c16, b.astype(BF16), (((2,), (2,)), ((0,), (0,))), preferred_element_type=F32)
    db = lax.dot_general(a.astype(BF16), c16, (((1,), (1,)), ((0,), (0,))), preferred_element_type=F32)
    return da, db


_bdot_chunks.defvjp(_bdot_chunks_fwd, _bdot_chunks_bwd)


def _f_sgu(u, v, ng, w, bexp):
    n = u.shape[0] // CHUNK
    vn = (_rms_rows(jax.nn.gelu(v)) * ng).reshape(n, CHUNK, GROUP)
    row = lax.broadcasted_iota(jnp.int32, w.shape, 0)
    col = lax.broadcasted_iota(jnp.int32, w.shape, 1)
    wm = jnp.broadcast_to(jnp.where(row >= col, w, 0.0)[None], (n, CHUNK, CHUNK))
    mixed = _bdot_chunks(wm, vn) + bexp[None]
    return (jax.nn.gelu(u) * mixed.reshape(n * CHUNK, GROUP),)


def _lo_mask():
    return lax.broadcasted_iota(jnp.int32, (1, HEAD_W), 1) < NOPE


def _f_qhead(x, tab, g):
    lo = _lo_mask()
    x2 = x * x
    ms_lo = jnp.sum(jnp.where(lo, x2, 0.0), axis=-1, keepdims=True) * (1.0 / NOPE)
    ms_hi = jnp.sum(jnp.where(lo, 0.0, x2), axis=-1, keepdims=True) * (1.0 / (HEAD_W - NOPE))
    r = jnp.where(lo, lax.rsqrt(ms_lo + EPS), lax.rsqrt(ms_hi + EPS))
    return (((x * r) * g) * tab,)


def _f_kvhead(x, g):
    lo = _lo_mask()
    ms = jnp.sum(jnp.where(lo, x * x, 0.0), axis=-1, keepdims=True) * (1.0 / NOPE)
    return (jnp.where(lo, (x * lax.rsqrt(ms + EPS)) * g, x),)


def _f_krope(a, b, ta, tb, ga, gb):
    r = lax.rsqrt(jnp.mean(a * a, axis=-1, keepdims=True) + EPS)
    return (((a * r) * ga) * ta + ((b * r) * gb) * tb,)


def _conv_fwd_call(y, w, b, name):
    S, C = y.shape
    cw = 128
    rt = _pick(S, (128,))
    w = jnp.pad(w, ((0, CONV_PAD - CONV_W), (0, 0)))

    def body(y_ref, w_ref, b_ref, z_ref, pad_ref):
        pad_ref[pl.ds(0, CONV_PAD), :] = jnp.zeros((CONV_PAD, cw), F32)
        pad_ref[pl.ds(CONV_PAD, S), :] = y_ref[...]
        wv = w_ref[...]
        bv = b_ref[...]

        def chunk(ci, carry):
            r0 = pl.multiple_of(ci * rt, rt)
            win = pad_ref[pl.ds(r0, rt + CONV_PAD), :]
            acc = jnp.broadcast_to(bv, (rt, cw))
            for k in range(CONV_W):
                off = CONV_PAD - (CONV_W - 1) + k
                sh = win if off == 0 else pltpu.roll(win, rt + CONV_PAD - off, axis=0)
                acc = acc + wv[k:k + 1, :] * sh[:rt, :]
            z_ref[pl.ds(r0, rt), :] = acc
            return carry

        lax.fori_loop(0, S // rt, chunk, 0)

    return _pcall(
        body, name=name, grid=(C // cw,),
        in_specs=[pl.BlockSpec((S, cw), lambda j: (0, j)), pl.BlockSpec((CONV_PAD, cw), lambda j: (0, j)),
                  pl.BlockSpec((1, cw), lambda j: (0, j))],
        out_specs=pl.BlockSpec((S, cw), lambda j: (0, j)),
        out_shape=jax.ShapeDtypeStruct((S, C), F32),
        scratch_shapes=[pltpu.VMEM((S + CONV_PAD, cw), F32)],
        compiler_params=_cp(("parallel",)),
    )(y, w, b)


def _conv_bwd_call(y, w, dz, name):
    S, C = y.shape
    cw = 128
    rt = _pick(S, (128,))
    w = jnp.pad(w, ((0, CONV_PAD - CONV_W), (0, 0)))

    def body(y_ref, w_ref, dz_ref, dy_ref, dw_ref, db_ref, ypad_ref, zpad_ref):
        ypad_ref[pl.ds(0, CONV_PAD), :] = jnp.zeros((CONV_PAD, cw), F32)
        ypad_ref[pl.ds(CONV_PAD, S), :] = y_ref[...]
        zpad_ref[pl.ds(0, S), :] = dz_ref[...]
        zpad_ref[pl.ds(S, CONV_PAD), :] = jnp.zeros((CONV_PAD, cw), F32)
        dw_ref[...] = jnp.zeros_like(dw_ref)
        wv = w_ref[...]

        def chunk(ci, dbacc):
            r0 = pl.multiple_of(ci * rt, rt)
            ywin = ypad_ref[pl.ds(r0, rt + CONV_PAD), :]
            zwin = zpad_ref[pl.ds(r0, rt + CONV_PAD), :]
            dzc = zwin[:rt, :]
            acc = jnp.zeros((rt, cw), F32)
            for k in range(CONV_W):
                off_z = (CONV_W - 1) - k
                zs = zwin if off_z == 0 else pltpu.roll(zwin, rt + CONV_PAD - off_z, axis=0)
                acc = acc + wv[k:k + 1, :] * zs[:rt, :]
                off_y = CONV_PAD - (CONV_W - 1) + k
                ys = pltpu.roll(ywin, rt + CONV_PAD - off_y, axis=0)
                dw_ref[k:k + 1, :] += jnp.sum(dzc * ys[:rt, :], axis=0, keepdims=True)
            dy_ref[pl.ds(r0, rt), :] = acc
            return dbacc + jnp.sum(dzc, axis=0, keepdims=True)

        db_ref[...] = lax.fori_loop(0, S // rt, chunk, jnp.zeros((1, cw), F32))

    dy, dw, db = _pcall(
        body, name=name, grid=(C // cw,),
        in_specs=[pl.BlockSpec((S, cw), lambda j: (0, j)), pl.BlockSpec((CONV_PAD, cw), lambda j: (0, j)),
                  pl.BlockSpec((S, cw), lambda j: (0, j))],
        out_specs=[pl.BlockSpec((S, cw), lambda j: (0, j)), pl.BlockSpec((CONV_PAD, cw), lambda j: (0, j)),
                   pl.BlockSpec((1, cw), lambda j: (0, j))],
        out_shape=[jax.ShapeDtypeStruct((S, C), F32), jax.ShapeDtypeStruct((CONV_PAD, C), F32),
                   jax.ShapeDtypeStruct((1, C), F32)],
        scratch_shapes=[pltpu.VMEM((S + CONV_PAD, cw), F32), pltpu.VMEM((S + CONV_PAD, cw), F32)],
        compiler_params=_cp(("parallel",)),
    )(y, w, dz)
    return dy, dw[:CONV_W], db


def _conv_op(name):
    @jax.custom_vjp
    def conv(y, w, b):
        return _conv_fwd_call(y, w, b, name + "_fwd")

    def fwd(y, w, b):
        return conv(y, w, b), (y, w)

    def bwd(res, dz):
        y, w = res
        return _conv_bwd_call(y, w, dz, name + "_bwd")

    conv.defvjp(fwd, bwd)
    return conv


def _attn_tile(S):
    return _pick(S, (512, 256, 128))


def _attn_fwd_call(q, kv, kr, name):
    S = q.shape[0]
    t = _attn_tile(S)

    def body(q_ref, kv_ref, kr_ref, o_ref, lse_ref):
        i = pl.program_id(1)
        qv = q_ref[...]

        def update(jb, carry, diagonal):
            m, l, acc = carry
            off = pl.multiple_of(jb * t, t)
            kc = jnp.concatenate([kv_ref[pl.ds(off, t), pl.ds(0, NOPE)], kr_ref[pl.ds(off, t), :]], axis=-1)
            vv = kv_ref[pl.ds(off, t), pl.ds(NOPE, VDIM)]
            s = lax.dot_general(qv, kc, (((1,), (1,)), ((), ())), preferred_element_type=F32)
            if diagonal:
                row = lax.broadcasted_iota(jnp.int32, (t, t), 0)
                col = lax.broadcasted_iota(jnp.int32, (t, t), 1)
                s = jnp.where(col <= row, s, -jnp.inf)
            mn = jnp.maximum(m, jnp.max(s, axis=-1, keepdims=True))
            p = jnp.exp2((s - mn) * ATTN_SCALE_LOG2E)
            al = jnp.exp2((m - mn) * ATTN_SCALE_LOG2E)
            l = al * l + jnp.sum(p, axis=-1, keepdims=True)
            acc = al * acc + jnp.dot(p.astype(BF16), vv, preferred_element_type=F32)
            return mn, l, acc

        init = (jnp.full((t, 1), -jnp.inf, F32), jnp.zeros((t, 1), F32), jnp.zeros((t, VDIM), F32))
        carry = lax.fori_loop(0, i, lambda jb, cr: update(jb, cr, False), init)
        m, l, acc = update(i, carry, True)
        o_ref[...] = (acc / l).astype(o_ref.dtype)
        lse_ref[...] = m * ATTN_SCALE + jnp.log(l)

    return _pcall(
        body, name=name, grid=(N_HEADS, S // t),
        in_specs=[pl.BlockSpec((t, HEAD_W), lambda h, i: (i, h)), pl.BlockSpec((S, HEAD_W), lambda h, i: (0, h)),
                  pl.BlockSpec((S, 128), lambda h, i: (0, 0))],
        out_specs=[pl.BlockSpec((t, VDIM), lambda h, i: (i, h)), pl.BlockSpec((None, t, 1), lambda h, i: (h, i, 0))],
        out_shape=[jax.ShapeDtypeStruct((S, N_HEADS * VDIM), BF16), jax.ShapeDtypeStruct((N_HEADS, S, 1), F32)],
        compiler_params=_cp(("parallel", "parallel")),
    )(q, kv, kr)


def _attn_dd_call(o, do, name):
    S = o.shape[0]
    t = _attn_tile(S)

    def body(o_ref, do_ref, dd_ref):
        dd_ref[...] = jnp.sum(do_ref[...].astype(F32) * o_ref[...].astype(F32), axis=-1, keepdims=True)

    return _pcall(
        body, name=name, grid=(N_HEADS, S // t),
        in_specs=[pl.BlockSpec((t, VDIM), lambda h, i: (i, h)), pl.BlockSpec((t, VDIM), lambda h, i: (i, h))],
        out_specs=pl.BlockSpec((None, t, 1), lambda h, i: (h, i, 0)),
        out_shape=jax.ShapeDtypeStruct((N_HEADS, S, 1), F32), compiler_params=_cp(("parallel", "parallel")),
    )(o, do)


def _attn_bwd_call(q, kv, kr, do, lse, dd, name):
    S = q.shape[0]
    t = _attn_tile(S)
    nq = S // t

    def body(q_ref, kv_ref, kr_ref, do_ref, lse_ref, dd_ref, dq_ref, dkv_ref, dkr_ref, dq_acc):
        j = pl.program_id(1)

        @pl.when(j == 0)
        def _():
            dq_acc[...] = jnp.zeros_like(dq_acc)

        kc = jnp.concatenate([kv_ref[:, pl.ds(0, NOPE)], kr_ref[...]], axis=-1)
        vv = kv_ref[:, pl.ds(NOPE, VDIM)]

        def update(ib, carry, diagonal):
            dkc, dv = carry
            off = pl.multiple_of(ib * t, t)
            qv = q_ref[pl.ds(off, t), :]
            dov = do_ref[pl.ds(off, t), :]
            st = lax.dot_general(kc, qv, (((1,), (1,)), ((), ())), preferred_element_type=F32)
            if diagonal:
                key = lax.broadcasted_iota(jnp.int32, (t, t), 0)
                qry = lax.broadcasted_iota(jnp.int32, (t, t), 1)
                st = jnp.where(key <= qry, st, -jnp.inf)
            pt = jnp.exp2(st * ATTN_SCALE_LOG2E - lse_ref[pl.ds(ib, 1), :])
            dv = dv + jnp.dot(pt.astype(BF16), dov, preferred_element_type=F32)
            dpt = lax.dot_general(vv, dov, (((1,), (1,)), ((), ())), preferred_element_type=F32)
            dst = (pt * (dpt - dd_ref[pl.ds(ib, 1), :]) * ATTN_SCALE).astype(BF16)
            dkc = dkc + jnp.dot(dst, qv, preferred_element_type=F32)
            dq_acc[pl.ds(off, t), :] += lax.dot_general(dst, kc, (((0,), (0,)), ((), ())), preferred_element_type=F32)
            return dkc, dv

        carry = update(j, (jnp.zeros((t, HEAD_W), F32), jnp.zeros((t, VDIM), F32)), True)
        dkc, dv = lax.fori_loop(j + 1, nq, lambda ib, cr: update(ib, cr, False), carry)
        dkv_ref[:, pl.ds(0, NOPE)] = dkc[:, :NOPE].astype(dkv_ref.dtype)
        dkv_ref[:, pl.ds(NOPE, VDIM)] = dv.astype(dkv_ref.dtype)
        dkr_ref[...] = dkc[:, NOPE:]

        @pl.when(j == nq - 1)
        def _():
            dq_ref[...] = dq_acc[...].astype(dq_ref.dtype)

    dq, dkv, dkr_heads = _pcall(
        body, name=name, grid=(N_HEADS, nq),
        in_specs=[pl.BlockSpec((S, HEAD_W), lambda h, j: (0, h)), pl.BlockSpec((t, HEAD_W), lambda h, j: (j, h)),
                  pl.BlockSpec((t, 128), lambda h, j: (j, 0)), pl.BlockSpec((S, VDIM), lambda h, j: (0, h)),
                  pl.BlockSpec((None, nq, t), lambda h, j: (h, 0, 0)), pl.BlockSpec((None, nq, t), lambda h, j: (h, 0, 0))],
        out_specs=[pl.BlockSpec((S, HEAD_W), lambda h, j: (0, h)), pl.BlockSpec((t, HEAD_W), lambda h, j: (j, h)),
                   pl.BlockSpec((None, t, 128), lambda h, j: (h, j, 0))],
        out_shape=[jax.ShapeDtypeStruct(q.shape, q.dtype), jax.ShapeDtypeStruct(kv.shape, kv.dtype),
                   jax.ShapeDtypeStruct((N_HEADS, S, 128), F32)],
        scratch_shapes=[pltpu.VMEM((S, HEAD_W), F32)],
        compiler_params=_cp(("parallel", "arbitrary")),
    )(q, kv, kr, do, (lse * LOG2E).reshape(N_HEADS, nq, t), dd.reshape(N_HEADS, nq, t))

    def sum_body(p_ref, o_ref):
        acc = p_ref[0]
        for h in range(1, N_HEADS):
            acc = acc + p_ref[h]
        o_ref[...] = acc.astype(o_ref.dtype)

    dkr = _pcall(
        sum_body, name=name + "_rope_sum", grid=(nq,),
        in_specs=[pl.BlockSpec((N_HEADS, t, 128), lambda i: (0, i, 0))], out_specs=pl.BlockSpec((t, 128), lambda i: (i, 0)),
        out_shape=jax.ShapeDtypeStruct((S, 128), kr.dtype), compiler_params=_cp(("parallel",)),
    )(dkr_heads)
    return dq, dkv, dkr


def _attn_op(name):
    @jax.custom_vjp
    def attn(q, kv, kr):
        return _attn_fwd_call(q, kv, kr, name + "_fwd")[0]

    def fwd(q, kv, kr):
        o, lse = _attn_fwd_call(q, kv, kr, name + "_fwd")
        return o, (q, kv, kr, o, lse)

    def bwd(res, do):
        q, kv, kr, o, lse = res
        dd = _attn_dd_call(o, do, name + "_dd")
        return _attn_bwd_call(q, kv, kr, do, lse, dd, name + "_bwd")

    attn.defvjp(fwd, bwd)
    return attn


def _loss_call(y, target):
    S, D = y.shape
    t = _pick(S, (256, 128))

    def body(y_ref, t_ref, dy_ref, loss_ref):
        @pl.when(pl.program_id(0) == 0)
        def _():
            loss_ref[...] = jnp.zeros_like(loss_ref)

        e = y_ref[...] - t_ref[...]
        dy_ref[...] = e * (1.0 / D)
        loss_ref[...] += 0.5 * jnp.sum(jnp.mean(e * e, axis=-1, keepdims=True), axis=0, keepdims=True)

    return _pcall(
        body, name="loss_head", grid=(S // t,),
        in_specs=[pl.BlockSpec((t, D), lambda i: (i, 0)), pl.BlockSpec((t, D), lambda i: (i, 0))],
        out_specs=[pl.BlockSpec((t, D), lambda i: (i, 0)), pl.BlockSpec((1, 1), lambda i: (0, 0))],
        out_shape=[jax.ShapeDtypeStruct((S, D), F32), jax.ShapeDtypeStruct((1, 1), F32)],
        compiler_params=_cp(("arbitrary",)),
    )(y, target)


def _adamw_call(w, g, m, v, name):
    shape = w.shape
    C = shape[-1]
    R = int(np.prod(shape[:-1]))
    tr = R
    for cand in (512, 256, 128, 64, 32, 16, 8):
        if R % cand == 0 and cand * C * 4 <= 2 * 1024 * 1024:
            tr = cand
            break
    c1 = 1.0 - ADAM_B1 ** ADAM_STEP
    c2 = 1.0 - ADAM_B2 ** ADAM_STEP

    def body(w_ref, g_ref, m_ref, v_ref, d_ref, mo_ref, vo_ref):
        gv = g_ref[...]
        mn = ADAM_B1 * m_ref[...] + (1.0 - ADAM_B1) * gv
        vn = ADAM_B2 * v_ref[...] + (1.0 - ADAM_B2) * (gv * gv)
        d_ref[...] = -ADAM_LR * ((mn / c1) / (jnp.sqrt(vn / c2) + ADAM_EPS) + ADAM_WD * w_ref[...])
        mo_ref[...] = mn
        vo_ref[...] = vn

    spec = pl.BlockSpec((tr, C), lambda i: (i, 0))
    outs = _pcall(
        body, name=name, grid=(R // tr,), in_specs=[spec] * 4, out_specs=[spec] * 3,
        out_shape=[jax.ShapeDtypeStruct((R, C), F32)] * 3, compiler_params=_cp(("parallel",)),
    )(*[a.reshape(R, C) for a in (w, g, m, v)])
    return [o.reshape(shape) for o in outs]


def _gather_small(x2d, *, reduce, name):
    R, C = x2d.shape

    def body(x_ref, out_ref, *scratch):
        if reduce:
            buf_ref, send_sems, recv_sems, local_sem = scratch
        else:
            buf_ref = out_ref
            send_sems, recv_sems, local_sem = scratch
        x, y, c = lax.axis_index("x"), lax.axis_index("y"), lax.axis_index("c")
        me, sibling = (x, y, c), (x, y, 1 - c)
        chips = [(1 - x, y), (x, 1 - y), (1 - x, 1 - y)]

        def rows(px, py, pc):
            return buf_ref.at[pl.ds((4 * px + 2 * py + pc) * R, R), :]

        def copy(k, block, to, src=None):
            return pltpu.make_async_remote_copy(
                src_ref=rows(*block) if src is None else src, dst_ref=rows(*block),
                send_sem=send_sems.at[k], recv_sem=recv_sems.at[k], device_id=to, device_id_type=MESH)

        mine = pltpu.make_async_copy(x_ref, rows(*me), local_sem)
        mine.start()
        first = [copy(0, me, sibling, src=x_ref)]
        first += [copy(1 + j, me, (*chip, c), src=x_ref) for j, chip in enumerate(chips)]
        for cp in first:
            cp.start()
        passed = [copy(4 + j, (*chip, c), sibling) for j, chip in enumerate(chips)]
        for j, chip in enumerate(chips):
            copy(1 + j, (*chip, c), me).wait_recv()
            passed[j].start()
        copy(0, sibling, me).wait_recv()
        for j, chip in enumerate(chips):
            copy(4 + j, (*chip, 1 - c), me).wait_recv()
        for cp in first + passed:
            cp.wait_send()
        mine.wait()
        if reduce:
            acc = buf_ref[pl.ds(0, R), :]
            for d in range(1, N_DEV):
                acc = acc + buf_ref[pl.ds(d * R, R), :]
            out_ref[...] = acc

    scratch = [pltpu.SemaphoreType.DMA((7,)), pltpu.SemaphoreType.DMA((7,)), pltpu.SemaphoreType.DMA]
    if reduce:
        scratch = [pltpu.VMEM((N_DEV * R, C), F32)] + scratch
    return _pcall(
        body, name=name, out_shape=jax.ShapeDtypeStruct((R if reduce else N_DEV * R, C), F32),
        in_specs=[pl.BlockSpec(memory_space=pltpu.VMEM)], out_specs=pl.BlockSpec(memory_space=pltpu.VMEM),
        scratch_shapes=scratch, compiler_params=pltpu.CompilerParams(vmem_limit_bytes=VMEM_LIMIT),
    )(x2d)


def _pack(arrs):
    flat = jnp.concatenate([a.reshape(-1).astype(F32) for a in arrs])
    n = flat.shape[0]
    unit = 8 * SMALL_COLS
    flat = jnp.pad(flat, (0, (-n) % unit))
    return flat.reshape(-1, SMALL_COLS)


def _unpack(flat, shapes):
    out, o = [], 0
    for s in shapes:
        n = int(np.prod(s))
        out.append(flat[o:o + n].reshape(s))
        o += n
    return out


def _all_gather_small(arrs, name):
    p = _pack(arrs)
    g = _gather_small(p, reduce=False, name=name).reshape(N_DEV, -1)
    out, o = [], 0
    for a in arrs:
        n = int(np.prod(a.shape))
        out.append(g[:, o:o + n].reshape((N_DEV,) + a.shape))
        o += n
    return out


def _all_reduce_small(arrs, name):
    p = _pack(arrs)
    return _unpack(_gather_small(p, reduce=True, name=name).reshape(-1), [a.shape for a in arrs])


def _half_rows(shard_shape):
    return shard_shape[0] // 2


def _half_slot(ref, kind, rh, cols, half, slot):
    if kind == "col":
        return ref.at[pl.ds(half * rh, rh), pl.ds(slot * cols, cols)]
    return ref.at[pl.ds((slot * 2 + half) * rh, rh), :]


def _full_shape(kind, shard_shape):
    r, c = shard_shape
    return (r, c * N_CHIPS) if kind == "col" else (r * N_CHIPS, c)


def _mesh_pos():
    x, y, c = lax.axis_index("x"), lax.axis_index("y"), lax.axis_index("c")
    return x, y, c


def _handshake_chip_peers(x, y, c):
    barrier = pltpu.get_barrier_semaphore()
    for peer in [(x, y, 1 - c)] + [(x ^ fx, y ^ fy, c) for fx, fy in CHIP_RELS]:
        pl.semaphore_signal(barrier, inc=1, device_id=peer, device_id_type=MESH)
    pl.semaphore_wait(barrier, 1 + len(CHIP_RELS))


def _gather_big(shards, kinds, name):
    n = len(shards)

    def body(*refs):
        srcs, outs = refs[:n], refs[n:2 * n]
        ici_send, ici_recv, d2d_send, d2d_recv, own_send, own_recv = refs[2 * n:]
        x, y, c = _mesh_pos()
        _handshake_chip_peers(x, y, c)
        my = 2 * x + y
        geo = [(_half_rows(s.shape), s.shape[1]) for s in shards]

        def region(a, half, slot):
            return _half_slot(outs[a], kinds[a], geo[a][0], geo[a][1], half, slot)

        def slot_of(a, slot):
            rh, cols = geo[a]
            if kinds[a] == "col":
                return outs[a].at[:, pl.ds(slot * cols, cols)]
            return outs[a].at[pl.ds(slot * 2 * rh, 2 * rh), :]

        sends = []
        for a in range(n):
            cp = pltpu.make_async_remote_copy(
                src_ref=srcs[a], dst_ref=slot_of(a, my), send_sem=own_send.at[a], recv_sem=own_recv.at[a],
                device_id=(x, y, 1 - c), device_id_type=MESH)
            cp.start()
            sends.append(cp)
        for a in range(n):
            rh = geo[a][0]
            for r, (fx, fy) in enumerate(CHIP_RELS):
                cp = pltpu.make_async_remote_copy(
                    src_ref=srcs[a].at[pl.ds(c * rh, rh), :], dst_ref=region(a, c, my),
                    send_sem=ici_send.at[3 * a + r], recv_sem=ici_recv.at[3 * a + r],
                    device_id=(x ^ fx, y ^ fy, c), device_id_type=MESH)
                cp.start()
                sends.append(cp)
        passed = []
        for a in range(n):
            for r, (fx, fy) in enumerate(CHIP_RELS):
                frm = 2 * (x ^ fx) + (y ^ fy)
                landed = region(a, c, frm)
                pltpu.make_async_remote_copy(
                    src_ref=landed, dst_ref=landed, send_sem=ici_send.at[3 * a + r], recv_sem=ici_recv.at[3 * a + r],
                    device_id=(x, y, c), device_id_type=MESH).wait_recv()
                cp = pltpu.make_async_remote_copy(
                    src_ref=landed, dst_ref=landed, send_sem=d2d_send.at[3 * a + r], recv_sem=d2d_recv.at[3 * a + r],
                    device_id=(x, y, 1 - c), device_id_type=MESH)
                cp.start()
                passed.append(cp)
        for a in range(n):
            for r, (fx, fy) in enumerate(CHIP_RELS):
                frm = 2 * (x ^ fx) + (y ^ fy)
                other = region(a, 1 - c, frm)
                pltpu.make_async_remote_copy(
                    src_ref=other, dst_ref=other, send_sem=d2d_send.at[3 * a + r], recv_sem=d2d_recv.at[3 * a + r],
                    device_id=(x, y, c), device_id_type=MESH).wait_recv()
        for a in range(n):
            pltpu.make_async_remote_copy(
                src_ref=srcs[a], dst_ref=slot_of(a, my), send_sem=own_send.at[a], recv_sem=own_recv.at[a],
                device_id=(x, y, c), device_id_type=MESH).wait_recv()
        for cp in sends + passed:
            cp.wait_send()

    return pl.kernel(
        body, name=name,
        out_type=[jax.ShapeDtypeStruct(_full_shape(k, s.shape), s.dtype) for s, k in zip(shards, kinds)],
        mesh=plsc.ScalarSubcoreMesh(axis_name="sequencer", num_cores=1),
        scratch_types=[pltpu.SemaphoreType.DMA((3 * n,))] * 4 + [pltpu.SemaphoreType.DMA((n,))] * 2,
        compiler_params=pltpu.CompilerParams(collective_id=GATHER_COLLECTIVE_ID),
    )(*shards)


def _swap_halves(fulls, kinds, shard_shapes, name):
    n = len(fulls)
    geo = [(_half_rows(s), s[1]) for s in shard_shapes]

    def body(*refs):
        srcs, outs = refs[:n], refs[n:2 * n]
        send_sems, recv_sems = refs[2 * n:]
        x, y, c = _mesh_pos()
        barrier = pltpu.get_barrier_semaphore()
        pl.semaphore_signal(barrier, inc=1, device_id=(x, y, 1 - c), device_id_type=MESH)
        pl.semaphore_wait(barrier, 1)
        cps = []
        for a in range(n):
            for s in range(N_CHIPS):
                cp = pltpu.make_async_remote_copy(
                    src_ref=_half_slot(srcs[a], kinds[a], geo[a][0], geo[a][1], 1 - c, s), dst_ref=outs[a].at[s],
                    send_sem=send_sems.at[N_CHIPS * a + s], recv_sem=recv_sems.at[N_CHIPS * a + s],
                    device_id=(x, y, 1 - c), device_id_type=MESH)
                cp.start()
                cps.append(cp)
        for cp in cps:
            cp.wait()

    return pl.kernel(
        body, name=name,
        out_type=[jax.ShapeDtypeStruct((N_CHIPS,) + g, f.dtype) for g, f in zip(geo, fulls)],
        mesh=plsc.ScalarSubcoreMesh(axis_name="sequencer", num_cores=1),
        scratch_types=[pltpu.SemaphoreType.DMA((N_CHIPS * n,))] * 2,
        compiler_params=pltpu.CompilerParams(collective_id=SWAP_COLLECTIVE_ID),
    )(*fulls)


def _add_own_half(full, recv, kind, shard_shape, cidx, name):
    rh, cols = _half_rows(shard_shape), shard_shape[1]
    tr = _pick(rh, (256, 128, 64, 32, 16))
    nb = rh // tr

    def body(c_ref, f_ref, r_ref, o_ref):
        o_ref[...] = (f_ref[...].astype(F32) + r_ref[...].astype(F32)).astype(o_ref.dtype)

    if kind == "col":
        f_spec = pl.BlockSpec((tr, cols), lambda s, i, c_ref: (c_ref[0] * nb + i, s))
    else:
        f_spec = pl.BlockSpec((tr, cols), lambda s, i, c_ref: ((s * 2 + c_ref[0]) * nb + i, 0))
    blk = pl.BlockSpec((None, tr, cols), lambda s, i, c_ref: (s, i, 0))
    return _pcall(
        body, name=name,
        grid_spec=pltpu.PrefetchScalarGridSpec(num_scalar_prefetch=1, grid=(N_CHIPS, nb), in_specs=[f_spec, blk], out_specs=blk),
        out_shape=jax.ShapeDtypeStruct((N_CHIPS, rh, cols), BF16),
        compiler_params=_cp(("arbitrary", "arbitrary")),
    )(cidx, full, recv)


def _scatter_partials(partials, name):
    n = len(partials)

    def body(*refs):
        srcs, outs = refs[:n], refs[n:2 * n]
        send_sems, recv_sems = refs[2 * n:]
        x, y, c = _mesh_pos()
        barrier = pltpu.get_barrier_semaphore()
        for fx, fy in CHIP_RELS:
            pl.semaphore_signal(barrier, inc=1, device_id=(x ^ fx, y ^ fy, c), device_id_type=MESH)
        pl.semaphore_wait(barrier, len(CHIP_RELS))
        my = 2 * x + y
        cps = []
        for a in range(n):
            for r, (fx, fy) in enumerate(CHIP_RELS):
                to = 2 * (x ^ fx) + (y ^ fy)
                cp = pltpu.make_async_remote_copy(
                    src_ref=srcs[a].at[to], dst_ref=outs[a].at[my],
                    send_sem=send_sems.at[3 * a + r], recv_sem=recv_sems.at[3 * a + r],
                    device_id=(x ^ fx, y ^ fy, c), device_id_type=MESH)
                cp.start()
                cps.append(cp)
        for a in range(n):
            for r, (fx, fy) in enumerate(CHIP_RELS):
                frm = 2 * (x ^ fx) + (y ^ fy)
                pltpu.make_async_remote_copy(
                    src_ref=outs[a].at[frm], dst_ref=outs[a].at[frm],
                    send_sem=send_sems.at[3 * a + r], recv_sem=recv_sems.at[3 * a + r],
                    device_id=(x, y, c), device_id_type=MESH).wait_recv()
        for cp in cps:
            cp.wait_send()

    return pl.kernel(
        body, name=name,
        out_type=[jax.ShapeDtypeStruct(p.shape, p.dtype) for p in partials],
        mesh=plsc.ScalarSubcoreMesh(axis_name="sequencer", num_cores=1),
        scratch_types=[pltpu.SemaphoreType.DMA((3 * n,))] * 2,
        compiler_params=pltpu.CompilerParams(collective_id=SCATTER_COLLECTIVE_ID),
    )(*partials)


def _sum_chips_into(landed, partial, buf, layer, n_layers, pos, name):
    _, rh, cols = landed.shape
    tr = _pick(rh, (256, 128, 64, 32, 16))
    nb = rh // tr

    def body(pos_ref, l_ref, own_ref, *rest):
        o_ref = rest[-1]
        my = pos_ref[1]
        own = own_ref[...].astype(F32)
        acc = jnp.where(my == 0, own, l_ref[0].astype(F32))
        for j in range(1, N_CHIPS):
            acc = acc + jnp.where(my == j, own, l_ref[j].astype(F32))
        o_ref[...] = acc

    in_specs = [pl.BlockSpec((N_CHIPS, tr, cols), lambda i, pos_ref: (0, i, 0)),
                pl.BlockSpec((None, tr, cols), lambda i, pos_ref: (pos_ref[1], i, 0))]
    args = [pos, landed, partial]
    aliases = {}
    if buf is not None:
        in_specs.append(pl.BlockSpec(memory_space=pl.ANY))
        args.append(buf)
        aliases = {3: 0}
    return _pcall(
        body, name=name,
        grid_spec=pltpu.PrefetchScalarGridSpec(
            num_scalar_prefetch=1, grid=(nb,), in_specs=in_specs,
            out_specs=pl.BlockSpec((None, tr, cols), lambda i, pos_ref: (layer, pos_ref[0] * nb + i, 0))),
        out_shape=jax.ShapeDtypeStruct((n_layers, 2 * rh, cols), F32), input_output_aliases=aliases,
        compiler_params=_cp(("arbitrary",)),
    )(*args)


def _exchange_final(bufs):
    n = len(bufs)
    HBM = pl.BlockSpec(memory_space=pl.ANY)
    n_layers = [b.shape[0] for b in bufs]
    base = np.concatenate([[0], np.cumsum(n_layers)])

    def body(*refs):
        outs = refs[n:2 * n]
        send_sems, recv_sems = refs[2 * n:]
        x, y, c = _mesh_pos()
        cps = []
        for w in range(n):
            rh = bufs[w].shape[1] // 2
            for l in range(n_layers[w]):
                k = int(base[w]) + l
                mine = outs[w].at[l, pl.ds(c * rh, rh), :]
                cp = pltpu.make_async_remote_copy(
                    src_ref=mine, dst_ref=mine, send_sem=send_sems.at[k], recv_sem=recv_sems.at[k],
                    device_id=(x, y, 1 - c), device_id_type=MESH)
                cp.start()
                cps.append(cp)
        for w in range(n):
            rh = bufs[w].shape[1] // 2
            for l in range(n_layers[w]):
                k = int(base[w]) + l
                other = outs[w].at[l, pl.ds((1 - c) * rh, rh), :]
                pltpu.make_async_remote_copy(
                    src_ref=other, dst_ref=other, send_sem=send_sems.at[k], recv_sem=recv_sems.at[k],
                    device_id=(x, y, c), device_id_type=MESH).wait_recv()
        for cp in cps:
            cp.wait_send()

    return _pcall(
        body, name="grad_exchange_final", out_shape=[jax.ShapeDtypeStruct(b.shape, b.dtype) for b in bufs],
        in_specs=[HBM] * n, out_specs=[HBM] * n, input_output_aliases={i: i for i in range(n)},
        scratch_shapes=[pltpu.SemaphoreType.DMA((int(base[-1]),))] * 2,
        compiler_params=pltpu.CompilerParams(vmem_limit_bytes=VMEM_LIMIT),
    )(*bufs)


def _swap_halves_last(z):
    h = z.shape[-1] // 2
    return jnp.concatenate([z[..., h:], z[..., :h]], axis=-1)


def _ext_uq(w):
    lead = w.shape[:-1]
    wh = w.reshape(lead + (-1, QK_DIM))
    rope = wh[..., NOPE:]
    return jnp.concatenate([wh, _swap_halves_last(rope)], axis=-1).reshape(lead + (-1,))


def _fold_uq(d):
    lead = d.shape[:-1]
    dh = d.reshape(lead + (-1, HEAD_W))
    rope = dh[..., NOPE:QK_DIM] + _swap_halves_last(dh[..., QK_DIM:])
    return jnp.concatenate([dh[..., :NOPE], rope], axis=-1).reshape(lead + (-1,))


def _ext_win(w):
    base, kr = w[..., :-ROPE], w[..., -ROPE:]
    ks = _swap_halves_last(kr)
    return jnp.concatenate([base, kr, ks, ks, kr], axis=-1)


def _fold_win(d):
    n = d.shape[-1] - 4 * ROPE
    a, b, c2, e = [d[..., n + i * ROPE:n + (i + 1) * ROPE] for i in range(4)]
    return jnp.concatenate([d[..., :n], a + e + _swap_halves_last(b + c2)], axis=-1)


ROW_TILE = 256


def _stage(st, xcur, mix, big, small, tabs):
    D = xcur.shape[1]
    l = st // 2
    mod = [[small["mod"][k][i].reshape(1, 1, D) for i in range(6)] for k in range(l + 1)]
    sh1, sc1, g1, sh2, sc2, _ = mod[l]
    if st % 2 == 0:
        n1 = small["norm1_g"][l].reshape(1, 1, D)
        if l == 0:
            (h,) = _rowop("norm_mod_0", _f_norm_mod, n_x=1, n_nd=0, n_p=3, x_w=[D], nd_w=[], nd_shared=[], p_per_group=[False] * 3,
                          out_w=[D], out_dtypes=[BF16], tile=ROW_TILE)((xcur,), (), (n1, sh1, sc1))
        else:
            op = _rowop(f"resid_norm_mod_a{l}", _f_resid_norm_mod, n_x=2, n_nd=0, n_p=4, x_w=[D, D], nd_w=[], nd_shared=[],
                        p_per_group=[False] * 4, out_w=[D, D], out_dtypes=[F32, BF16], tile=ROW_TILE)
            xcur, h = op((xcur, mix), (), (mod[l - 1][5], n1, sh1, sc1))
        if l % 2 == 0:
            return xcur, _even_mixer(h, big, small, l // 2, ROW_TILE)
        return xcur, _mla_mixer(h, big, small, l // 2, ROW_TILE, *tabs)
    n2 = small["norm2_g"][l].reshape(1, 1, D)
    op = _rowop(f"resid_norm_mod_b{l}", _f_resid_norm_mod, n_x=2, n_nd=0, n_p=4, x_w=[D, D], nd_w=[], nd_shared=[],
                p_per_group=[False] * 4, out_w=[D, D], out_dtypes=[F32, BF16], tile=ROW_TILE)
    xcur, h = op((xcur, mix), (), (g1, n2, sh2, sc2))
    return xcur, _mlp(f"mlp_{l}")(h, big["mlp_w1"], big["mlp_w2"])


def _last_residual(xcur, mix, small):
    D = xcur.shape[1]
    gate = small["mod"][-1][5].reshape(1, 1, D)
    return _rowop("resid_last", _f_resid, n_x=2, n_nd=0, n_p=1, x_w=[D, D], nd_w=[], nd_shared=[], p_per_group=[False],
                  out_w=[D], out_dtypes=[F32], tile=ROW_TILE)((xcur, mix), (), (gate,))[0]


def _even_mixer(h, big, wts, e, tile):
    proj = _linear(f"ab_in_{e}", BF16)(h, big["ab_w_in"])
    da = proj.shape[1] // 4
    u, v, a, g = [proj[:, i * da:(i + 1) * da] for i in range(4)]
    ng = da // GROUP
    sgu = _rowop(f"sgu_{e}", _f_sgu, n_x=2, n_nd=0, n_p=3, x_w=[GROUP, GROUP], nd_w=[], nd_shared=[], p_per_group=[True] * 3,
                 out_w=[GROUP], out_dtypes=[BF16], tile=SGU_CHUNKS_PER_STEP * CHUNK, groups=ng)
    bexp = jnp.broadcast_to(wts["sgu_b"][e][:, :, None], (ng, CHUNK, GROUP))
    (out_a,) = sgu((u, v), (), (wts["sgu_norm_g"][e].reshape(ng, 1, GROUP), wts["sgu_w"][e], bexp))
    (yglu,) = _rowop(f"glu_{e}", _f_glu, n_x=2, n_nd=0, n_p=0, x_w=[da, da], nd_w=[], nd_shared=[], p_per_group=[],
                     out_w=[da], out_dtypes=[F32], tile=tile)((a, g), (), ())
    z = _conv_op(f"conv_{e}")(yglu, wts["conv_w"][e], wts["conv_b"][e].reshape(1, da))
    (out_b,) = _rowop(f"ln_silu_{e}", _f_ln_silu, n_x=1, n_nd=0, n_p=2, x_w=[da], nd_w=[], nd_shared=[], p_per_group=[False] * 2,
                      out_w=[da], out_dtypes=[BF16], tile=tile)(
        (z,), (), (wts["conv_ln_g"][e].reshape(1, 1, da), wts["conv_ln_b"][e].reshape(1, 1, da)))
    return _linear(f"ab_out_{e}")(jnp.concatenate([out_a, out_b], axis=-1), big["ab_w_out"])


def _mla_mixer(h, big, wts, o, tile, tab_q, tab_ka, tab_kb):
    proj = _linear(f"mla_in_{o}", BF16)(h, big["mla_w_in"])
    rank = (proj.shape[1] - 4 * ROPE) // 2
    c_q, c_kv = proj[:, :rank], proj[:, rank:2 * rank]
    kr_a, kr_b = proj[:, 2 * rank:2 * rank + 2 * ROPE], proj[:, 2 * rank + 2 * ROPE:]

    def rms(name, xx, gg):
        return _rowop(name, _f_rms, n_x=1, n_nd=0, n_p=1, x_w=[rank], nd_w=[], nd_shared=[], p_per_group=[False],
                      out_w=[rank], out_dtypes=[BF16], tile=tile)((xx,), (), (gg.reshape(1, 1, rank),))[0]

    q_raw = _linear(f"mla_uq_{o}", BF16)(rms(f"rms_q_{o}", c_q, wts["mla_q_norm_g"][o]), big["mla_w_uq"])
    kv_raw = _linear(f"mla_ukv_{o}", BF16)(rms(f"rms_kv_{o}", c_kv, wts["mla_kv_norm_g"][o]), big["mla_w_ukv"])
    gq, gk = wts["mla_q_head_g"][o], wts["mla_k_head_g"][o]
    gk_rope = gk[NOPE:]
    gq_ext = jnp.concatenate([gq, _swap_halves_last(gq[NOPE:])]).reshape(1, 1, HEAD_W)
    gk_ext = jnp.concatenate([gk[:NOPE], jnp.ones((HEAD_W - NOPE,), F32)]).reshape(1, 1, HEAD_W)
    gk_a = jnp.concatenate([gk_rope, _swap_halves_last(gk_rope)]).reshape(1, 1, 2 * ROPE)
    gk_b = jnp.concatenate([_swap_halves_last(gk_rope), gk_rope]).reshape(1, 1, 2 * ROPE)
    head_tile = 4 * tile
    (q,) = _rowop(f"q_head_{o}", _f_qhead, n_x=1, n_nd=1, n_p=1, x_w=[HEAD_W], nd_w=[HEAD_W], nd_shared=[True], p_per_group=[False],
                  out_w=[HEAD_W], out_dtypes=[BF16], tile=head_tile, groups=N_HEADS)((q_raw,), (tab_q,), (gq_ext,))
    (kv,) = _rowop(f"kv_head_{o}", _f_kvhead, n_x=1, n_nd=0, n_p=1, x_w=[HEAD_W], nd_w=[], nd_shared=[], p_per_group=[False],
                   out_w=[HEAD_W], out_dtypes=[BF16], tile=head_tile, groups=N_HEADS)((kv_raw,), (), (gk_ext,))
    (kr,) = _rowop(f"k_rope_{o}", _f_krope, n_x=2, n_nd=2, n_p=2, x_w=[2 * ROPE] * 2, nd_w=[2 * ROPE] * 2, nd_shared=[True] * 2,
                   p_per_group=[False] * 2, out_w=[2 * ROPE], out_dtypes=[BF16], tile=tile)((kr_a, kr_b), (tab_ka, tab_kb), (gk_a, gk_b))
    att = _attn_op(f"attn_{o}")(q, kv, kr)
    return _linear(f"mla_out_{o}")(att, big["mla_w_out"])


def _rope_tabs(S):
    pos = jnp.arange(S, dtype=F32)
    inv = ROPE_THETA ** (-jnp.arange(0, ROPE, 2, dtype=F32) / ROPE)
    ang = pos[:, None] * inv[None, :]
    cos, sin = jnp.cos(ang), jnp.sin(ang)
    cc = jnp.concatenate([cos, cos], axis=-1)
    sg = jnp.concatenate([-sin, sin], axis=-1)
    tab_q = jnp.concatenate([jnp.ones((S, NOPE), F32), cc, sg], axis=-1)
    return tab_q, jnp.concatenate([cc, sg], axis=-1), jnp.concatenate([sg, cc], axis=-1)


def kernel(x, c, norm1_g, norm2_g, ada_w, ada_b, mlp_w1, mlp_w2, ab_w_in, sgu_norm_g, sgu_w, sgu_b, conv_w, conv_b, conv_ln_g, conv_ln_b, ab_w_out, mla_w_in, mla_q_norm_g, mla_kv_norm_g, mla_w_uq, mla_w_ukv, mla_q_head_g, mla_k_head_g, mla_w_out, loss_target, m_norm1_g, m_norm2_g, m_ada_w, m_ada_b, m_mlp_w1, m_mlp_w2, m_ab_w_in, m_sgu_norm_g, m_sgu_w, m_sgu_b, m_conv_w, m_conv_b, m_conv_ln_g, m_conv_ln_b, m_ab_w_out, m_mla_w_in, m_mla_q_norm_g, m_mla_kv_norm_g, m_mla_w_uq, m_mla_w_ukv, m_mla_q_head_g, m_mla_k_head_g, m_mla_w_out, v_norm1_g, v_norm2_g, v_ada_w, v_ada_b, v_mlp_w1, v_mlp_w2, v_ab_w_in, v_sgu_norm_g, v_sgu_w, v_sgu_b, v_conv_w, v_conv_b, v_conv_ln_g, v_conv_ln_b, v_ab_w_out, v_mla_w_in, v_mla_q_norm_g, v_mla_kv_norm_g, v_mla_w_uq, v_mla_w_ukv, v_mla_q_head_g, v_mla_k_head_g, v_mla_w_out):
    names = ["norm1_g", "norm2_g", "ada_w", "ada_b", "mlp_w1", "mlp_w2", "ab_w_in", "sgu_norm_g", "sgu_w", "sgu_b", "conv_w",
             "conv_b", "conv_ln_g", "conv_ln_b", "ab_w_out", "mla_w_in", "mla_q_norm_g", "mla_kv_norm_g", "mla_w_uq", "mla_w_ukv",
             "mla_q_head_g", "mla_k_head_g", "mla_w_out"]
    W = dict(zip(names, [norm1_g, norm2_g, ada_w, ada_b, mlp_w1, mlp_w2, ab_w_in, sgu_norm_g, sgu_w, sgu_b, conv_w, conv_b, conv_ln_g,
                         conv_ln_b, ab_w_out, mla_w_in, mla_q_norm_g, mla_kv_norm_g, mla_w_uq, mla_w_ukv, mla_q_head_g, mla_k_head_g,
                         mla_w_out]))
    M = dict(zip(names, [m_norm1_g, m_norm2_g, m_ada_w, m_ada_b, m_mlp_w1, m_mlp_w2, m_ab_w_in, m_sgu_norm_g, m_sgu_w, m_sgu_b, m_conv_w,
                         m_conv_b, m_conv_ln_g, m_conv_ln_b, m_ab_w_out, m_mla_w_in, m_mla_q_norm_g, m_mla_kv_norm_g, m_mla_w_uq,
                         m_mla_w_ukv, m_mla_q_head_g, m_mla_k_head_g, m_mla_w_out]))
    V = dict(zip(names, [v_norm1_g, v_norm2_g, v_ada_w, v_ada_b, v_mlp_w1, v_mlp_w2, v_ab_w_in, v_sgu_norm_g, v_sgu_w, v_sgu_b, v_conv_w,
                         v_conv_b, v_conv_ln_g, v_conv_ln_b, v_ab_w_out, v_mla_w_in, v_mla_q_norm_g, v_mla_kv_norm_g, v_mla_w_uq,
                         v_mla_w_ukv, v_mla_q_head_g, v_mla_k_head_g, v_mla_w_out]))
    xi, yi, ci = lax.axis_index("x"), lax.axis_index("y"), lax.axis_index("c")
    chip = 2 * xi + yi
    dev = 2 * chip + ci
    pos = jnp.stack([ci, chip]).astype(jnp.int32)
    S, D = x.shape[1], x.shape[2]
    depth = norm1_g.shape[0]

    c_all, conv_w_all, qn_all, kvn_all = _all_gather_small([c, conv_w, mla_q_norm_g, mla_kv_norm_g], "gather_small_inputs")
    c_all = c_all.reshape(N_DEV, D)
    by_chip = lambda a: jnp.concatenate([a[2 * j] for j in range(N_CHIPS)], axis=-1)
    conv_w_full, qn_full, kvn_full = by_chip(conv_w_all), by_chip(qn_all), by_chip(kvn_all)

    (c_act,) = _rowop("silu_c", _f_silu, n_x=1, n_nd=0, n_p=0, x_w=[D], nd_w=[], nd_shared=[], p_per_group=[], out_w=[D],
                      out_dtypes=[F32], tile=N_DEV)((c_all,), (), ())
    c_act_pad = jnp.pad(c_act, ((0, 128 - N_DEV), (0, 0)))
    mod_cols = jnp.stack([_matmul(c_act_pad, ada_w[l], name=f"ada_fwd_{l}")[:N_DEV] for l in range(depth)])
    (mod_all,) = _all_gather_small([mod_cols], "gather_mod")
    mod_mine = jnp.concatenate([lax.dynamic_index_in_dim(mod_all[2 * j], dev, axis=1, keepdims=False) for j in range(N_CHIPS)], axis=-1)
    mod_mine = (mod_mine + ada_b).reshape(depth, 6, D)

    big = {"mlp_w1": "col", "mlp_w2": "row", "ab_w_in": "col", "ab_w_out": "row", "mla_w_in": "row", "mla_w_uq": "col",
           "mla_w_ukv": "col", "mla_w_out": "row"}
    src = dict(W)
    src["mla_w_in"] = _ext_win(mla_w_in)
    src["mla_w_uq"] = _ext_uq(mla_w_uq)
    shards, kinds, owner = [], [], []
    for nme, kind in big.items():
        for l in range(src[nme].shape[0]):
            shards.append(src[nme][l].astype(BF16))
            kinds.append(kind)
            owner.append((nme, l))
    def stage_of(nme, l):
        return 2 * l + 1 if nme.startswith("mlp") else (4 * l if nme.startswith("ab") else 4 * l + 2)

    fwd_stages = [[a for a, (nme, l) in enumerate(owner) if stage_of(nme, l) == st] for st in range(2 * depth)]
    fulls = [None] * len(shards)
    for st, idx in enumerate(fwd_stages):
        got = _gather_big([shards[a] for a in idx], [kinds[a] for a in idx], f"gather_weights_{st}")
        for a, f in zip(idx, got):
            fulls[a] = f
    small = dict(norm1_g=norm1_g, norm2_g=norm2_g, mod=mod_mine, sgu_norm_g=sgu_norm_g, sgu_w=sgu_w, sgu_b=sgu_b, conv_w=conv_w_full,
                 conv_b=conv_b, conv_ln_g=conv_ln_g, conv_ln_b=conv_ln_b, mla_q_norm_g=qn_full, mla_kv_norm_g=kvn_full,
                 mla_q_head_g=mla_q_head_g, mla_k_head_g=mla_k_head_g)
    groups, a0 = [], 0
    for nme in big:
        groups.append(list(range(a0, a0 + src[nme].shape[0])))
        a0 += src[nme].shape[0]
    shard_shapes = [s.shape for s in shards]
    where = {a: (wi, l) for wi, idxs in enumerate(groups) for l, a in enumerate(idxs)}

    tabs = _rope_tabs(S)
    n_stages = 2 * depth
    xcur, mix, vjps = x[0], None, []
    for st, idx in enumerate(fwd_stages):
        stage_big = {owner[a][0]: fulls[a] for a in idx}
        if st == 0:
            (xcur, mix), vjp = jax.vjp(lambda xx, bb, ss: _stage(0, xx, None, bb, ss, tabs), xcur, stage_big, small)
        else:
            (xcur, mix), vjp = jax.vjp(lambda xx, mm, bb, ss, st=st: _stage(st, xx, mm, bb, ss, tabs), xcur, mix, stage_big, small)
        vjps.append(vjp)
    y, vjp_last = jax.vjp(_last_residual, xcur, mix, small)
    dy, loss_mine = _loss_call(y, loss_target[0])
    loss = lax.psum(loss_mine[0, 0], ("x", "y", "c"))

    dxc, dmix, dsmall = vjp_last(dy)
    bufs = [None] * len(groups)
    swapped, scattered = {}, {}

    def add_and_scatter(st):
        idx, gr, recv = swapped.pop(st)
        partials = [_add_own_half(gr[a], r, kinds[a], shard_shapes[a], pos, f"grad_add_sibling_{a}") for a, r in zip(idx, recv)]
        scattered[st] = (idx, partials, _scatter_partials(partials, f"grad_scatter_partials_{st}"))

    def sum_chips(st):
        idx, partials, landed = scattered.pop(st)
        for a, part, land in zip(idx, partials, landed):
            wi, l = where[a]
            bufs[wi] = _sum_chips_into(land, part, bufs[wi], l, len(groups[wi]), pos, f"grad_sum_chips_{a}")

    for st in reversed(range(n_stages)):
        idx = fwd_stages[st]
        if st == 0:
            dxc, dbig, ds = vjps[st]((dxc, dmix))
        else:
            dxc, dmix, dbig, ds = vjps[st]((dxc, dmix))
        dsmall = jax.tree.map(jnp.add, dsmall, ds)
        gr = {a: dbig[owner[a][0]] for a in idx}
        swapped[st] = (idx, gr, _swap_halves([gr[a] for a in idx], [kinds[a] for a in idx], [shard_shapes[a] for a in idx],
                                             f"grad_swap_halves_{st}"))
        started = [s for s in ((st + 1, st) if st == 1 else (st + 1,)) if s in swapped]
        for s in started:
            add_and_scatter(s)
        if st + 2 in scattered:
            sum_chips(st + 2)
        if st > 0:
            pinned = {s: scattered[s][1] for s in started}
            (dxc, dmix), pinned, bufs = lax.optimization_barrier(((dxc, dmix), pinned, bufs))
            for s in started:
                scattered[s] = (scattered[s][0], pinned[s], scattered[s][2])
    dx = dxc
    for st in sorted(swapped, reverse=True):
        add_and_scatter(st)
    for st in sorted(scattered, reverse=True):
        sum_chips(st)
    gsh = _exchange_final(bufs)

    dw = dsmall
    dmod = dw["mod"].reshape(depth, 6 * D)
    small_names = ["norm1_g", "norm2_g", "sgu_norm_g", "sgu_w", "sgu_b", "conv_w", "conv_b", "conv_ln_g", "conv_ln_b", "mla_q_norm_g",
                   "mla_kv_norm_g", "mla_q_head_g", "mla_k_head_g"]
    red = _all_reduce_small([dmod] + [dw[nme] for nme in small_names], "reduce_small_grads")
    G = dict(zip(["ada_b"] + small_names, red))
    own_cols = lambda a: lax.dynamic_slice_in_dim(a, chip * (a.shape[-1] // N_CHIPS), a.shape[-1] // N_CHIPS, axis=-1)
    for nme in ("conv_w", "mla_q_norm_g", "mla_kv_norm_g"):
        G[nme] = own_cols(G[nme])

    (dmod_all,) = _all_gather_small([dmod], "gather_dmod")
    dmod_cols = own_cols(dmod_all)
    dmod_pad = jnp.pad(dmod_cols, ((0, 128 - N_DEV), (0, 0), (0, 0)))
    G["ada_w"] = jnp.stack([_matmul(c_act_pad, dmod_pad[:, l], ta=True, name=f"ada_wgrad_{l}") for l in range(depth)])

    for nme, g in zip(big, gsh):
        G[nme] = g
    G["mla_w_in"] = _fold_win(G["mla_w_in"])
    G["mla_w_uq"] = _fold_uq(G["mla_w_uq"])

    deltas, new_m, new_v = [], [], []
    for nme in names:
        d, mn, vn = _adamw_call(W[nme], G[nme], M[nme], V[nme], f"adamw_{nme}")
        deltas.append(d)
        new_m.append(mn)
        new_v.append(vn)
    return (loss, dx[None], *[G[nme] for nme in names], *deltas, *new_m, *new_v)
```

```python
import functools
import numpy as np
import jax
import jax.numpy as jnp
from jax import lax
from jax.experimental import pallas as pl
from jax.experimental.pallas import tpu as pltpu
from jax.experimental.pallas import tpu_sc as plsc

F32 = jnp.float32
BF16 = jnp.bfloat16
MESH = pl.DeviceIdType.MESH

EPS = 1e-6
N_HEADS = 16
NOPE = 128
ROPE = 64
VDIM = 128
QK_DIM = NOPE + ROPE
HEAD_W = 256
CHUNK = 128
GROUP = 128
SGU_CHUNKS_PER_STEP = 4
CONV_W = 31
CONV_PAD = 32
ROPE_THETA = 10000.0
ATTN_SCALE = QK_DIM ** -0.5
LOG2E = 1.4426950408889634
ATTN_SCALE_LOG2E = ATTN_SCALE * LOG2E
ADAM_LR, ADAM_B1, ADAM_B2, ADAM_EPS, ADAM_WD, ADAM_STEP = 0.001, 0.9, 0.999, 1e-08, 0.01, 10
N_CHIPS = 4
N_DEV = 8
VMEM_LIMIT = 56 * 1024 * 1024
SMALL_COLS = 1024
CHIP_RELS = ((1, 0), (0, 1), (1, 1))
GATHER_COLLECTIVE_ID = 0
SWAP_COLLECTIVE_ID = 1
SCATTER_COLLECTIVE_ID = 2


def _pcall(body, **kw):
    return pl.pallas_call(body, **kw)


def _cp(sem=None, **kw):
    return pltpu.CompilerParams(dimension_semantics=sem, vmem_limit_bytes=VMEM_LIMIT, **kw)


def _pick(n, cands):
    for c in cands:
        if n % c == 0:
            return c
    return n


def _matmul(a, b, *, ta=False, tb=False, out_dtype=F32, name, extra=(), epilogue=None, out_dtypes=None, tiles=None):
    if ta:
        K, M = a.shape
    else:
        M, K = a.shape
    if tb:
        N, K2 = b.shape
    else:
        K2, N = b.shape
    assert K == K2, (a.shape, b.shape, ta, tb)
    tm = _pick(M, (1024, 512, 256, 128))
    tn = N if N <= 1536 else _pick(N, (1024, 512, 256, 128))
    tk = _pick(K, (2048, 1024, 512, 256, 128))
    if tiles is not None and M % tiles[0] == 0 and N % tiles[1] == 0 and K % tiles[2] == 0:
        tm, tn, tk = tiles
    nk = K // tk
    dn = (((0 if ta else 1,), (1 if tb else 0,)), ((), ()))

    n_extra = len(extra)
    single = epilogue is None
    if single:
        out_dtypes = [out_dtype]

    n_out = len(out_dtypes)

    def body(a_ref, b_ref, *rest):
        extra_refs, o_refs = rest[:n_extra], rest[n_extra:n_extra + n_out]
        part = lax.dot_general(a_ref[...].astype(BF16), b_ref[...].astype(BF16), dn, preferred_element_type=F32)

        def finish(acc):
            res = (acc,) if single else epilogue(acc, *[r[...] for r in extra_refs])
            for o_ref, val in zip(o_refs, res):
                o_ref[...] = val.astype(o_ref.dtype)

        if nk == 1:
            finish(part)
            return
        acc_ref = rest[-1]
        k = pl.program_id(2)

        @pl.when(k == 0)
        def _():
            acc_ref[...] = jnp.zeros_like(acc_ref)

        acc_ref[...] += part

        @pl.when(k == nk - 1)
        def _():
            finish(acc_ref[...])

    a_spec = pl.BlockSpec((tk, tm), lambda i, j, k: (k, i)) if ta else pl.BlockSpec((tm, tk), lambda i, j, k: (i, k))
    b_spec = pl.BlockSpec((tn, tk), lambda i, j, k: (j, k)) if tb else pl.BlockSpec((tk, tn), lambda i, j, k: (k, j))
    mn_spec = pl.BlockSpec((tm, tn), lambda i, j, k: (i, j))
    outs = _pcall(
        body, name=name, grid=(M // tm, N // tn, nk), in_specs=[a_spec, b_spec] + [mn_spec] * n_extra,
        out_specs=[mn_spec] * len(out_dtypes),
        out_shape=[jax.ShapeDtypeStruct((M, N), d) for d in out_dtypes],
        scratch_shapes=[pltpu.VMEM((tm, tn), F32)] if nk > 1 else [],
        compiler_params=_cp(("parallel", "parallel", "arbitrary")),
    )(a, b, *extra)
    return outs[0] if single else outs


def _linear(name, out_dtype=F32):
    @jax.custom_vjp
    def mm(a, w):
        return _matmul(a, w, out_dtype=out_dtype, name=name + "_fwd")

    def fwd(a, w):
        return mm(a, w), (a, w)

    def bwd(res, dy):
        a, w = res
        da = _matmul(dy, w, tb=True, out_dtype=a.dtype, name=name + "_dgrad")
        dw = _matmul(a, dy, ta=True, out_dtype=w.dtype, name=name + "_wgrad")
        return da, dw

    mm.defvjp(fwd, bwd)
    return mm


def _ada_wgrad_call(c_act, dmod):
    P, D = c_act.shape
    _, L, C = dmod.shape
    tm = _pick(D, (1024, 512, 256, 128))
    tn = _pick(C, (1024, 512, 256, 128))
    nj = C // tn

    def body(a_ref, b_ref, o_ref):
        o_ref[...] = lax.dot_general(a_ref[...].astype(BF16), b_ref[...].astype(BF16), (((0,), (0,)), ((), ())),
                                     preferred_element_type=F32)

    return _pcall(
        body, name="ada_wgrad", grid=(L, D // tm, nj),
        in_specs=[pl.BlockSpec((P, tm), lambda l, i, j: (0, i)), pl.BlockSpec((P, tn), lambda l, i, j: (0, l * nj + j))],
        out_specs=pl.BlockSpec((None, tm, tn), lambda l, i, j: (l, i, j)),
        out_shape=jax.ShapeDtypeStruct((L, D, C), F32), compiler_params=_cp(("parallel", "parallel", "parallel")),
    )(c_act, dmod.reshape(P, L * C))


def _relu2_epilogue(acc):
    return acc, jnp.square(jnp.maximum(acc, 0.0))


def _relu2_grad_epilogue(acc, h1):
    return (acc * (2.0 * jnp.maximum(h1.astype(F32), 0.0)),)


def _mlp(name, wide):
    t_one = None
    t_acc = (2048, 1024, 1024) if wide else None
    t_upw = (1024, 2048, 1024) if wide else None

    def run(h, w1, w2):
        h1, act = _matmul(h, w1, name=name + "_up_fwd", epilogue=_relu2_epilogue, out_dtypes=[BF16, BF16], tiles=t_one)
        return _matmul(act, w2, name=name + "_down_fwd", tiles=t_acc), h1, act

    @jax.custom_vjp
    def mlp(h, w1, w2):
        return run(h, w1, w2)[0]

    def fwd(h, w1, w2):
        y, h1, act = run(h, w1, w2)
        return y, (h, w1, w2, h1, act)

    def bwd(res, dy):
        h, w1, w2, h1, act = res
        dw2 = _matmul(act, dy, ta=True, out_dtype=w2.dtype, name=name + "_down_wgrad", tiles=t_acc)
        (dh1,) = _matmul(dy, w2, tb=True, name=name + "_down_dgrad", extra=(h1,), epilogue=_relu2_grad_epilogue, out_dtypes=[BF16],
                         tiles=t_one)
        dw1 = _matmul(h, dh1, ta=True, out_dtype=w1.dtype, name=name + "_up_wgrad", tiles=t_upw)
        dh = _matmul(dh1, w1, tb=True, out_dtype=h.dtype, name=name + "_up_dgrad", tiles=t_acc)
        return dh, dw1, dw2

    mlp.defvjp(fwd, bwd)
    return mlp


def _rowop(name, f, *, n_x, n_nd, n_p, x_w, nd_w, nd_shared, p_per_group, out_w, out_dtypes, tile, groups=1):
    G = groups

    def specs(S):
        t = min(tile, S)
        xs = [pl.BlockSpec((t, w), lambda g, r: (r, g)) for w in x_w]
        nds = [pl.BlockSpec((t, w), (lambda g, r: (r, 0)) if sh else (lambda g, r: (r, g))) for w, sh in zip(nd_w, nd_shared)]
        outs = [pl.BlockSpec((t, w), lambda g, r: (r, g)) for w in out_w]
        return t, xs, nds, outs

    def pspecs(ps):
        return [pl.BlockSpec((None,) + p.shape[1:], (lambda g, r: (g, 0, 0)) if pg else (lambda g, r: (0, 0, 0)))
                for p, pg in zip(ps, p_per_group)]

    def fwd_call(xs, nds, ps):
        S = xs[0].shape[0]
        t, xsp, ndsp, osp = specs(S)

        def body(*refs):
            ins, outs = refs[:n_x + n_nd + n_p], refs[n_x + n_nd + n_p:]
            vals = [r[...].astype(F32) for r in ins]
            res = f(*vals)
            for o, r in zip(res, outs):
                r[...] = o.astype(r.dtype)

        return _pcall(
            body, name=name + "_fwd", grid=(G, S // t), in_specs=xsp + ndsp + pspecs(ps), out_specs=osp,
            out_shape=[jax.ShapeDtypeStruct((S, G * w), d) for w, d in zip(out_w, out_dtypes)],
            compiler_params=_cp(("parallel", "parallel")),
        )(*xs, *nds, *ps)

    def bwd_call(xs, nds, ps, douts):
        S = xs[0].shape[0]
        t, xsp, ndsp, osp = specs(S)
        n_in = n_x + n_nd + n_p + len(out_w)

        def body(*refs):
            ins, outs = refs[:n_in], refs[n_in:]
            xv = [r[...].astype(F32) for r in ins[:n_x]]
            ndv = [r[...].astype(F32) for r in ins[n_x:n_x + n_nd]]
            pv = [r[...].astype(F32) for r in ins[n_x + n_nd:n_x + n_nd + n_p]]
            dov = tuple(r[...].astype(F32) for r in ins[n_x + n_nd + n_p:])
            _, vjp = jax.vjp(lambda *a: tuple(f(*a[:n_x], *ndv, *a[n_x:])), *xv, *pv)
            cts = vjp(dov)
            for i in range(n_x):
                outs[i][...] = cts[i].astype(outs[i].dtype)
            g, r = pl.program_id(0), pl.program_id(1)
            for i in range(n_p):
                first = (r == 0) if p_per_group[i] else jnp.logical_and(g == 0, r == 0)
                ref, ct = outs[n_x + i], cts[n_x + i]

                @pl.when(first)
                def _(ref=ref, ct=ct):
                    ref[...] = ct

                @pl.when(jnp.logical_not(first))
                def _(ref=ref, ct=ct):
                    ref[...] += ct

        return _pcall(
            body, name=name + "_bwd", grid=(G, S // t), in_specs=xsp + ndsp + pspecs(ps) + osp,
            out_specs=xsp + pspecs(ps),
            out_shape=[jax.ShapeDtypeStruct(x.shape, x.dtype) for x in xs] + [jax.ShapeDtypeStruct(p.shape, F32) for p in ps],
            compiler_params=_cp(("arbitrary", "arbitrary")),
        )(*xs, *nds, *ps, *douts)

    @jax.custom_vjp
    def op(xs, nds, ps):
        return tuple(fwd_call(xs, nds, ps))

    def op_fwd(xs, nds, ps):
        return op(xs, nds, ps), (xs, nds, ps)

    def op_bwd(res, douts):
        xs, nds, ps = res
        out = bwd_call(xs, nds, ps, douts)
        return tuple(out[:n_x]), tuple(jnp.zeros_like(n) for n in nds), tuple(out[n_x:])

    op.defvjp(op_fwd, op_bwd)
    return op


def _rms_rows(x):
    return x * lax.rsqrt(jnp.mean(x * x, axis=-1, keepdims=True) + EPS)


def _f_norm_mod(x, g, shift, scale):
    return ((_rms_rows(x) * g) * (1.0 + scale) + shift,)


def _f_resid_norm_mod(x, mix, gate, g, shift, scale):
    xn = x + gate * mix
    return xn, (_rms_rows(xn) * g) * (1.0 + scale) + shift


def _f_resid(x, mix, gate):
    return (x + gate * mix,)


def _f_rms(x, g):
    return (_rms_rows(x) * g,)


def _f_glu(a, g):
    return (a * jax.nn.sigmoid(g),)


def _f_ln_silu(z, g, b):
    mu = jnp.mean(z, axis=-1, keepdims=True)
    zc = z - mu
    var = jnp.mean(zc * zc, axis=-1, keepdims=True)
    y = zc * lax.rsqrt(var + EPS) * g + b
    return (y * jax.nn.sigmoid(y),)


def _f_silu(x):
    return (x * jax.nn.sigmoid(x),)


@jax.custom_vjp
def _bdot_chunks(a, b):
    return lax.dot_general(a.astype(BF16), b.astype(BF16), (((2,), (1,)), ((0,), (0,))), preferred_element_type=F32)


def _bdot_chunks_fwd(a, b):
    return _bdot_chunks(a, b), (a, b)


def _bdot_chunks_bwd(res, ct):
    a, b = res
    c16 = ct.astype(BF16)
    da = lax.dot_general(c16, b.astype(BF16), (((2,), (2,)), ((0,), (0,))), preferred_element_type=F32)
    db = lax.dot_general(a.astype(BF16), c16, (((1,), (1,)), ((0,), (0,))), preferred_element_type=F32)
    return da, db


_bdot_chunks.defvjp(_bdot_chunks_fwd, _bdot_chunks_bwd)


def _f_sgu(u, v, ng, w, bexp):
    n = u.shape[0] // CHUNK
    vn = (_rms_rows(jax.nn.gelu(v)) * ng).reshape(n, CHUNK, GROUP)
    row = lax.broadcasted_iota(jnp.int32, w.shape, 0)
    col = lax.broadcasted_iota(jnp.int32, w.shape, 1)
    wm = jnp.broadcast_to(jnp.where(row >= col, w, 0.0)[None], (n, CHUNK, CHUNK))
    mixed = _bdot_chunks(wm, vn) + bexp[None]
    return (jax.nn.gelu(u) * mixed.reshape(n * CHUNK, GROUP),)


def _lo_mask():
    return lax.broadcasted_iota(jnp.int32, (1, HEAD_W), 1) < NOPE


def _f_qhead(x, tab, g):
    lo = _lo_mask()
    x2 = x * x
    ms_lo = jnp.sum(jnp.where(lo, x2, 0.0), axis=-1, keepdims=True) * (1.0 / NOPE)
    ms_hi = jnp.sum(jnp.where(lo, 0.0, x2), axis=-1, keepdims=True) * (1.0 / (HEAD_W - NOPE))
    r = jnp.where(lo, lax.rsqrt(ms_lo + EPS), lax.rsqrt(ms_hi + EPS))
    return (((x * r) * g) * tab,)


def _f_kvhead(x, g):
    lo = _lo_mask()
    ms = jnp.sum(jnp.where(lo, x * x, 0.0), axis=-1, keepdims=True) * (1.0 / NOPE)
    return (jnp.where(lo, (x * lax.rsqrt(ms + EPS)) * g, x),)


def _f_krope(a, b, ta, tb, ga, gb):
    r = lax.rsqrt(jnp.mean(a * a, axis=-1, keepdims=True) + EPS)
    return (((a * r) * ga) * ta + ((b * r) * gb) * tb,)


def _conv_fwd_call(y, w, b, name):
    S, C = y.shape
    cw = 128
    rt = _pick(S, (128,))
    w = jnp.pad(w, ((0, CONV_PAD - CONV_W), (0, 0)))

    def body(y_ref, w_ref, b_ref, z_ref, pad_ref):
        pad_ref[pl.ds(0, CONV_PAD), :] = jnp.zeros((CONV_PAD, cw), F32)
        pad_ref[pl.ds(CONV_PAD, S), :] = y_ref[...]
        wv = w_ref[...]
        bv = b_ref[...]

        def chunk(ci, carry):
            r0 = pl.multiple_of(ci * rt, rt)
            win = pad_ref[pl.ds(r0, rt + CONV_PAD), :]
            acc = jnp.broadcast_to(bv, (rt, cw))
            for k in range(CONV_W):
                off = CONV_PAD - (CONV_W - 1) + k
                sh = win if off == 0 else pltpu.roll(win, rt + CONV_PAD - off, axis=0)
                acc = acc + wv[k:k + 1, :] * sh[:rt, :]
            z_ref[pl.ds(r0, rt), :] = acc
            return carry

        lax.fori_loop(0, S // rt, chunk, 0)

    return _pcall(
        body, name=name, grid=(C // cw,),
        in_specs=[pl.BlockSpec((S, cw), lambda j: (0, j)), pl.BlockSpec((CONV_PAD, cw), lambda j: (0, j)),
                  pl.BlockSpec((1, cw), lambda j: (0, j))],
        out_specs=pl.BlockSpec((S, cw), lambda j: (0, j)),
        out_shape=jax.ShapeDtypeStruct((S, C), F32),
        scratch_shapes=[pltpu.VMEM((S + CONV_PAD, cw), F32)],
        compiler_params=_cp(("parallel",)),
    )(y, w, b)


def _conv_bwd_call(y, w, dz, name):
    S, C = y.shape
    cw = 128
    rt = _pick(S, (128,))
    w = jnp.pad(w, ((0, CONV_PAD - CONV_W), (0, 0)))

    def body(y_ref, w_ref, dz_ref, dy_ref, dw_ref, db_ref, ypad_ref, zpad_ref):
        ypad_ref[pl.ds(0, CONV_PAD), :] = jnp.zeros((CONV_PAD, cw), F32)
        ypad_ref[pl.ds(CONV_PAD, S), :] = y_ref[...]
        zpad_ref[pl.ds(0, S), :] = dz_ref[...]
        zpad_ref[pl.ds(S, CONV_PAD), :] = jnp.zeros((CONV_PAD, cw), F32)
        dw_ref[...] = jnp.zeros_like(dw_ref)
        wv = w_ref[...]

        def chunk(ci, dbacc):
            r0 = pl.multiple_of(ci * rt, rt)
            ywin = ypad_ref[pl.ds(r0, rt + CONV_PAD), :]
            zwin = zpad_ref[pl.ds(r0, rt + CONV_PAD), :]
            dzc = zwin[:rt, :]
            acc = jnp.zeros((rt, cw), F32)
            for k in range(CONV_W):
                off_z = (CONV_W - 1) - k
                zs = zwin if off_z == 0 else pltpu.roll(zwin, rt + CONV_PAD - off_z, axis=0)
                acc = acc + wv[k:k + 1, :] * zs[:rt, :]
                off_y = CONV_PAD - (CONV_W - 1) + k
                ys = pltpu.roll(ywin, rt + CONV_PAD - off_y, axis=0)
                dw_ref[k:k + 1, :] += jnp.sum(dzc * ys[:rt, :], axis=0, keepdims=True)
            dy_ref[pl.ds(r0, rt), :] = acc
            return dbacc + jnp.sum(dzc, axis=0, keepdims=True)

        db_ref[...] = lax.fori_loop(0, S // rt, chunk, jnp.zeros((1, cw), F32))

    dy, dw, db = _pcall(
        body, name=name, grid=(C // cw,),
        in_specs=[pl.BlockSpec((S, cw), lambda j: (0, j)), pl.BlockSpec((CONV_PAD, cw), lambda j: (0, j)),
                  pl.BlockSpec((S, cw), lambda j: (0, j))],
        out_specs=[pl.BlockSpec((S, cw), lambda j: (0, j)), pl.BlockSpec((CONV_PAD, cw), lambda j: (0, j)),
                   pl.BlockSpec((1, cw), lambda j: (0, j))],
        out_shape=[jax.ShapeDtypeStruct((S, C), F32), jax.ShapeDtypeStruct((CONV_PAD, C), F32),
                   jax.ShapeDtypeStruct((1, C), F32)],
        scratch_shapes=[pltpu.VMEM((S + CONV_PAD, cw), F32), pltpu.VMEM((S + CONV_PAD, cw), F32)],
        compiler_params=_cp(("parallel",)),
    )(y, w, dz)
    return dy, dw[:CONV_W], db


def _conv_op(name):
    @jax.custom_vjp
    def conv(y, w, b):
        return _conv_fwd_call(y, w, b, name + "_fwd")

    def fwd(y, w, b):
        return conv(y, w, b), (y, w)

    def bwd(res, dz):
        y, w = res
        return _conv_bwd_call(y, w, dz, name + "_bwd")

    conv.defvjp(fwd, bwd)
    return conv


def _attn_tile(S):
    return _pick(S, (512, 256, 128))


def _attn_fwd_call(q, kv, kr, name):
    S = q.shape[0]
    t = _attn_tile(S)

    def body(q_ref, kv_ref, kr_ref, o_ref, lse_ref):
        i = pl.program_id(1)
        qv = q_ref[...]

        def update(jb, carry, diagonal):
            m, l, acc = carry
            off = pl.multiple_of(jb * t, t)
            kc = jnp.concatenate([kv_ref[pl.ds(off, t), pl.ds(0, NOPE)], kr_ref[pl.ds(off, t), :]], axis=-1)
            vv = kv_ref[pl.ds(off, t), pl.ds(NOPE, VDIM)]
            s = lax.dot_general(qv, kc, (((1,), (1,)), ((), ())), preferred_element_type=F32)
            if diagonal:
                row = lax.broadcasted_iota(jnp.int32, (t, t), 0)
                col = lax.broadcasted_iota(jnp.int32, (t, t), 1)
                s = jnp.where(col <= row, s, -jnp.inf)
            mn = jnp.maximum(m, jnp.max(s, axis=-1, keepdims=True))
            p = jnp.exp2((s - mn) * ATTN_SCALE_LOG2E)
            al = jnp.exp2((m - mn) * ATTN_SCALE_LOG2E)
            l = al * l + jnp.sum(p, axis=-1, keepdims=True)
            acc = al * acc + jnp.dot(p.astype(BF16), vv, preferred_element_type=F32)
            return mn, l, acc

        init = (jnp.full((t, 1), -jnp.inf, F32), jnp.zeros((t, 1), F32), jnp.zeros((t, VDIM), F32))
        carry = lax.fori_loop(0, i, lambda jb, cr: update(jb, cr, False), init)
        m, l, acc = update(i, carry, True)
        o_ref[...] = (acc / l).astype(o_ref.dtype)
        lse_ref[...] = m * ATTN_SCALE + jnp.log(l)

    return _pcall(
        body, name=name, grid=(N_HEADS, S // t),
        in_specs=[pl.BlockSpec((t, HEAD_W), lambda h, i: (i, h)), pl.BlockSpec((S, HEAD_W), lambda h, i: (0, h)),
                  pl.BlockSpec((S, 128), lambda h, i: (0, 0))],
        out_specs=[pl.BlockSpec((t, VDIM), lambda h, i: (i, h)), pl.BlockSpec((None, t, 1), lambda h, i: (h, i, 0))],
        out_shape=[jax.ShapeDtypeStruct((S, N_HEADS * VDIM), BF16), jax.ShapeDtypeStruct((N_HEADS, S, 1), F32)],
        compiler_params=_cp(("parallel", "parallel")),
    )(q, kv, kr)


def _attn_dd_call(o, do, name):
    S = o.shape[0]
    t = _attn_tile(S)

    def body(o_ref, do_ref, dd_ref):
        dd_ref[...] = jnp.sum(do_ref[...].astype(F32) * o_ref[...].astype(F32), axis=-1, keepdims=True)

    return _pcall(
        body, name=name, grid=(N_HEADS, S // t),
        in_specs=[pl.BlockSpec((t, VDIM), lambda h, i: (i, h)), pl.BlockSpec((t, VDIM), lambda h, i: (i, h))],
        out_specs=pl.BlockSpec((None, t, 1), lambda h, i: (h, i, 0)),
        out_shape=jax.ShapeDtypeStruct((N_HEADS, S, 1), F32), compiler_params=_cp(("parallel", "parallel")),
    )(o, do)


def _attn_bwd_call(q, kv, kr, do, lse, dd, name):
    S = q.shape[0]
    t = _attn_tile(S)
    nq = S // t

    def body(q_ref, kv_ref, kr_ref, do_ref, lse_ref, dd_ref, dq_ref, dkv_ref, dkr_ref, dq_acc):
        j = pl.program_id(1)

        @pl.when(j == 0)
        def _():
            dq_acc[...] = jnp.zeros_like(dq_acc)

        kc = jnp.concatenate([kv_ref[:, pl.ds(0, NOPE)], kr_ref[...]], axis=-1)
        vv = kv_ref[:, pl.ds(NOPE, VDIM)]

        def update(ib, carry, diagonal):
            dkc, dv = carry
            off = pl.multiple_of(ib * t, t)
            qv = q_ref[pl.ds(off, t), :]
            dov = do_ref[pl.ds(off, t), :]
            st = lax.dot_general(kc, qv, (((1,), (1,)), ((), ())), preferred_element_type=F32)
            if diagonal:
                key = lax.broadcasted_iota(jnp.int32, (t, t), 0)
                qry = lax.broadcasted_iota(jnp.int32, (t, t), 1)
                st = jnp.where(key <= qry, st, -jnp.inf)
            pt = jnp.exp2(st * ATTN_SCALE_LOG2E - lse_ref[pl.ds(ib, 1), :])
            dv = dv + jnp.dot(pt.astype(BF16), dov, preferred_element_type=F32)
            dpt = lax.dot_general(vv, dov, (((1,), (1,)), ((), ())), preferred_element_type=F32)
            dst = (pt * (dpt - dd_ref[pl.ds(ib, 1), :]) * ATTN_SCALE).astype(BF16)
            dkc = dkc + jnp.dot(dst, qv, preferred_element_type=F32)
            dq_acc[pl.ds(off, t), :] += lax.dot_general(dst, kc, (((0,), (0,)), ((), ())), preferred_element_type=F32)
            return dkc, dv

        carry = update(j, (jnp.zeros((t, HEAD_W), F32), jnp.zeros((t, VDIM), F32)), True)
        dkc, dv = lax.fori_loop(j + 1, nq, lambda ib, cr: update(ib, cr, False), carry)
        dkv_ref[:, pl.ds(0, NOPE)] = dkc[:, :NOPE].astype(dkv_ref.dtype)
        dkv_ref[:, pl.ds(NOPE, VDIM)] = dv.astype(dkv_ref.dtype)
        dkr_ref[...] = dkc[:, NOPE:]

        @pl.when(j == nq - 1)
        def _():
            dq_ref[...] = dq_acc[...].astype(dq_ref.dtype)

    dq, dkv, dkr_heads = _pcall(
        body, name=name, grid=(N_HEADS, nq),
        in_specs=[pl.BlockSpec((S, HEAD_W), lambda h, j: (0, h)), pl.BlockSpec((t, HEAD_W), lambda h, j: (j, h)),
                  pl.BlockSpec((t, 128), lambda h, j: (j, 0)), pl.BlockSpec((S, VDIM), lambda h, j: (0, h)),
                  pl.BlockSpec((None, nq, t), lambda h, j: (h, 0, 0)), pl.BlockSpec((None, nq, t), lambda h, j: (h, 0, 0))],
        out_specs=[pl.BlockSpec((S, HEAD_W), lambda h, j: (0, h)), pl.BlockSpec((t, HEAD_W), lambda h, j: (j, h)),
                   pl.BlockSpec((None, t, 128), lambda h, j: (h, j, 0))],
        out_shape=[jax.ShapeDtypeStruct(q.shape, q.dtype), jax.ShapeDtypeStruct(kv.shape, kv.dtype),
                   jax.ShapeDtypeStruct((N_HEADS, S, 128), F32)],
        scratch_shapes=[pltpu.VMEM((S, HEAD_W), F32)],
        compiler_params=_cp(("parallel", "arbitrary")),
    )(q, kv, kr, do, (lse * LOG2E).reshape(N_HEADS, nq, t), dd.reshape(N_HEADS, nq, t))

    def sum_body(p_ref, o_ref):
        acc = p_ref[0]
        for h in range(1, N_HEADS):
            acc = acc + p_ref[h]
        o_ref[...] = acc.astype(o_ref.dtype)

    dkr = _pcall(
        sum_body, name=name + "_rope_sum", grid=(nq,),
        in_specs=[pl.BlockSpec((N_HEADS, t, 128), lambda i: (0, i, 0))], out_specs=pl.BlockSpec((t, 128), lambda i: (i, 0)),
        out_shape=jax.ShapeDtypeStruct((S, 128), kr.dtype), compiler_params=_cp(("parallel",)),
    )(dkr_heads)
    return dq, dkv, dkr


def _attn_op(name):
    @jax.custom_vjp
    def attn(q, kv, kr):
        return _attn_fwd_call(q, kv, kr, name + "_fwd")[0]

    def fwd(q, kv, kr):
        o, lse = _attn_fwd_call(q, kv, kr, name + "_fwd")
        return o, (q, kv, kr, o, lse)

    def bwd(res, do):
        q, kv, kr, o, lse = res
        dd = _attn_dd_call(o, do, name + "_dd")
        return _attn_bwd_call(q, kv, kr, do, lse, dd, name + "_bwd")

    attn.defvjp(fwd, bwd)
    return attn


def _loss_call(y, target):
    S, D = y.shape
    t = _pick(S, (256, 128))

    def body(y_ref, t_ref, dy_ref, loss_ref):
        @pl.when(pl.program_id(0) == 0)
        def _():
            loss_ref[...] = jnp.zeros_like(loss_ref)

        e = y_ref[...] - t_ref[...]
        dy_ref[...] = e * (1.0 / D)
        loss_ref[...] += 0.5 * jnp.sum(jnp.mean(e * e, axis=-1, keepdims=True), axis=0, keepdims=True)

    return _pcall(
        body, name="loss_head", grid=(S // t,),
        in_specs=[pl.BlockSpec((t, D), lambda i: (i, 0)), pl.BlockSpec((t, D), lambda i: (i, 0))],
        out_specs=[pl.BlockSpec((t, D), lambda i: (i, 0)), pl.BlockSpec((1, 1), lambda i: (0, 0))],
        out_shape=[jax.ShapeDtypeStruct((S, D), F32), jax.ShapeDtypeStruct((1, 1), F32)],
        compiler_params=_cp(("arbitrary",)),
    )(y, target)


def _adamw_call(w, g, m, v, name):
    shape = w.shape
    C = shape[-1]
    R = int(np.prod(shape[:-1]))
    tr = R
    for cand in (512, 256, 128, 64, 32, 16, 8):
        if R % cand == 0 and cand * C * 4 <= 2 * 1024 * 1024:
            tr = cand
            break
    c1 = 1.0 - ADAM_B1 ** ADAM_STEP
    c2 = 1.0 - ADAM_B2 ** ADAM_STEP

    def body(w_ref, g_ref, m_ref, v_ref, d_ref, mo_ref, vo_ref):
        gv = g_ref[...]
        mn = ADAM_B1 * m_ref[...] + (1.0 - ADAM_B1) * gv
        vn = ADAM_B2 * v_ref[...] + (1.0 - ADAM_B2) * (gv * gv)
        d_ref[...] = -ADAM_LR * ((mn / c1) / (jnp.sqrt(vn / c2) + ADAM_EPS) + ADAM_WD * w_ref[...])
        mo_ref[...] = mn
        vo_ref[...] = vn

    spec = pl.BlockSpec((tr, C), lambda i: (i, 0))
    outs = _pcall(
        body, name=name, grid=(R // tr,), in_specs=[spec] * 4, out_specs=[spec] * 3,
        out_shape=[jax.ShapeDtypeStruct((R, C), F32)] * 3, compiler_params=_cp(("parallel",)),
    )(*[a.reshape(R, C) for a in (w, g, m, v)])
    return [o.reshape(shape) for o in outs]


def _gather_small(x2d, *, reduce, name):
    R, C = x2d.shape

    def body(x_ref, out_ref, *scratch):
        if reduce:
            buf_ref, send_sems, recv_sems, local_sem = scratch
        else:
            buf_ref = out_ref
            send_sems, recv_sems, local_sem = scratch
        x, y, c = lax.axis_index("x"), lax.axis_index("y"), lax.axis_index("c")
        me, sibling = (x, y, c), (x, y, 1 - c)
        chips = [(1 - x, y), (x, 1 - y), (1 - x, 1 - y)]

        def rows(px, py, pc):
            return buf_ref.at[pl.ds((4 * px + 2 * py + pc) * R, R), :]

        def copy(k, block, to, src=None):
            return pltpu.make_async_remote_copy(
                src_ref=rows(*block) if src is None else src, dst_ref=rows(*block),
                send_sem=send_sems.at[k], recv_sem=recv_sems.at[k], device_id=to, device_id_type=MESH)

        mine = pltpu.make_async_copy(x_ref, rows(*me), local_sem)
        mine.start()
        first = [copy(0, me, sibling, src=x_ref)]
        first += [copy(1 + j, me, (*chip, c), src=x_ref) for j, chip in enumerate(chips)]
        for cp in first:
            cp.start()
        passed = [copy(4 + j, (*chip, c), sibling) for j, chip in enumerate(chips)]
        for j, chip in enumerate(chips):
            copy(1 + j, (*chip, c), me).wait_recv()
            passed[j].start()
        copy(0, sibling, me).wait_recv()
        for j, chip in enumerate(chips):
            copy(4 + j, (*chip, 1 - c), me).wait_recv()
        for cp in first + passed:
            cp.wait_send()
        mine.wait()
        if reduce:
            acc = buf_ref[pl.ds(0, R), :]
            for d in range(1, N_DEV):
                acc = acc + buf_ref[pl.ds(d * R, R), :]
            out_ref[...] = acc

    scratch = [pltpu.SemaphoreType.DMA((7,)), pltpu.SemaphoreType.DMA((7,)), pltpu.SemaphoreType.DMA]
    if reduce:
        scratch = [pltpu.VMEM((N_DEV * R, C), F32)] + scratch
    return _pcall(
        body, name=name, out_shape=jax.ShapeDtypeStruct((R if reduce else N_DEV * R, C), F32),
        in_specs=[pl.BlockSpec(memory_space=pltpu.VMEM)], out_specs=pl.BlockSpec(memory_space=pltpu.VMEM),
        scratch_shapes=scratch, compiler_params=pltpu.CompilerParams(vmem_limit_bytes=VMEM_LIMIT),
    )(x2d)


def _pack(arrs):
    flat = jnp.concatenate([a.reshape(-1).astype(F32) for a in arrs])
    n = flat.shape[0]
    unit = 8 * SMALL_COLS
    flat = jnp.pad(flat, (0, (-n) % unit))
    return flat.reshape(-1, SMALL_COLS)


def _unpack(flat, shapes):
    out, o = [], 0
    for s in shapes:
        n = int(np.prod(s))
        out.append(flat[o:o + n].reshape(s))
        o += n
    return out


def _all_gather_small(arrs, name):
    p = _pack(arrs)
    g = _gather_small(p, reduce=False, name=name).reshape(N_DEV, -1)
    out, o = [], 0
    for a in arrs:
        n = int(np.prod(a.shape))
        out.append(g[:, o:o + n].reshape((N_DEV,) + a.shape))
        o += n
    return out


def _all_reduce_small(arrs, name):
    p = _pack(arrs)
    return _unpack(_gather_small(p, reduce=True, name=name).reshape(-1), [a.shape for a in arrs])


def _half_rows(shard_shape):
    return shard_shape[0] // 2


def _half_slot(ref, kind, rh, cols, half, slot):
    if kind == "col":
        return ref.at[pl.ds(half * rh, rh), pl.ds(slot * cols, cols)]
    return ref.at[pl.ds((slot * 2 + half) * rh, rh), :]


def _full_shape(kind, shard_shape):
    r, c = shard_shape
    return (r, c * N_CHIPS) if kind == "col" else (r * N_CHIPS, c)


def _mesh_pos():
    x, y, c = lax.axis_index("x"), lax.axis_index("y"), lax.axis_index("c")
    return x, y, c


def _handshake_chip_peers(x, y, c):
    barrier = pltpu.get_barrier_semaphore()
    for peer in [(x, y, 1 - c)] + [(x ^ fx, y ^ fy, c) for fx, fy in CHIP_RELS]:
        pl.semaphore_signal(barrier, inc=1, device_id=peer, device_id_type=MESH)
    pl.semaphore_wait(barrier, 1 + len(CHIP_RELS))


def _gather_big(shards, kinds, name):
    n = len(shards)

    def body(*refs):
        srcs, outs = refs[:n], refs[n:2 * n]
        ici_send, ici_recv, d2d_send, d2d_recv, own_send, own_recv = refs[2 * n:]
        x, y, c = _mesh_pos()
        _handshake_chip_peers(x, y, c)
        my = 2 * x + y
        geo = [(_half_rows(s.shape), s.shape[1]) for s in shards]

        def region(a, half, slot):
            return _half_slot(outs[a], kinds[a], geo[a][0], geo[a][1], half, slot)

        def slot_of(a, slot):
            rh, cols = geo[a]
            if kinds[a] == "col":
                return outs[a].at[:, pl.ds(slot * cols, cols)]
            return outs[a].at[pl.ds(slot * 2 * rh, 2 * rh), :]

        sends = []
        for a in range(n):
            cp = pltpu.make_async_remote_copy(
                src_ref=srcs[a], dst_ref=slot_of(a, my), send_sem=own_send.at[a], recv_sem=own_recv.at[a],
                device_id=(x, y, 1 - c), device_id_type=MESH)
            cp.start()
            sends.append(cp)
        for a in range(n):
            rh = geo[a][0]
            for r, (fx, fy) in enumerate(CHIP_RELS):
                cp = pltpu.make_async_remote_copy(
                    src_ref=srcs[a].at[pl.ds(c * rh, rh), :], dst_ref=region(a, c, my),
                    send_sem=ici_send.at[3 * a + r], recv_sem=ici_recv.at[3 * a + r],
                    device_id=(x ^ fx, y ^ fy, c), device_id_type=MESH)
                cp.start()
                sends.append(cp)
        passed = []
        for a in range(n):
            for r, (fx, fy) in enumerate(CHIP_RELS):
                frm = 2 * (x ^ fx) + (y ^ fy)
                landed = region(a, c, frm)
                pltpu.make_async_remote_copy(
                    src_ref=landed, dst_ref=landed, send_sem=ici_send.at[3 * a + r], recv_sem=ici_recv.at[3 * a + r],
                    device_id=(x, y, c), device_id_type=MESH).wait_recv()
                cp = pltpu.make_async_remote_copy(
                    src_ref=landed, dst_ref=landed, send_sem=d2d_send.at[3 * a + r], recv_sem=d2d_recv.at[3 * a + r],
                    device_id=(x, y, 1 - c), device_id_type=MESH)
                cp.start()
                passed.append(cp)
        for a in range(n):
            for r, (fx, fy) in enumerate(CHIP_RELS):
                frm = 2 * (x ^ fx) + (y ^ fy)
                other = region(a, 1 - c, frm)
                pltpu.make_async_remote_copy(
                    src_ref=other, dst_ref=other, send_sem=d2d_send.at[3 * a + r], recv_sem=d2d_recv.at[3 * a + r],
                    device_id=(x, y, c), device_id_type=MESH).wait_recv()
        for a in range(n):
            pltpu.make_async_remote_copy(
                src_ref=srcs[a], dst_ref=slot_of(a, my), send_sem=own_send.at[a], recv_sem=own_recv.at[a],
                device_id=(x, y, c), device_id_type=MESH).wait_recv()
        for cp in sends + passed:
            cp.wait_send()

    return pl.kernel(
        body, name=name,
        out_type=[jax.ShapeDtypeStruct(_full_shape(k, s.shape), s.dtype) for s, k in zip(shards, kinds)],
        mesh=plsc.ScalarSubcoreMesh(axis_name="sequencer", num_cores=1),
        scratch_types=[pltpu.SemaphoreType.DMA((3 * n,))] * 4 + [pltpu.SemaphoreType.DMA((n,))] * 2,
        compiler_params=pltpu.CompilerParams(collective_id=GATHER_COLLECTIVE_ID),
    )(*shards)


def _swap_halves(fulls, kinds, shard_shapes, name):
    n = len(fulls)
    geo = [(_half_rows(s), s[1]) for s in shard_shapes]

    def body(*refs):
        srcs, outs = refs[:n], refs[n:2 * n]
        send_sems, recv_sems = refs[2 * n:]
        x, y, c = _mesh_pos()
        barrier = pltpu.get_barrier_semaphore()
        pl.semaphore_signal(barrier, inc=1, device_id=(x, y, 1 - c), device_id_type=MESH)
        pl.semaphore_wait(barrier, 1)
        cps = []
        for a in range(n):
            for s in range(N_CHIPS):
                cp = pltpu.make_async_remote_copy(
                    src_ref=_half_slot(srcs[a], kinds[a], geo[a][0], geo[a][1], 1 - c, s), dst_ref=outs[a].at[s],
                    send_sem=send_sems.at[N_CHIPS * a + s], recv_sem=recv_sems.at[N_CHIPS * a + s],
                    device_id=(x, y, 1 - c), device_id_type=MESH)
                cp.start()
                cps.append(cp)
        for cp in cps:
            cp.wait()

    return pl.kernel(
        body, name=name,
        out_type=[jax.ShapeDtypeStruct((N_CHIPS,) + g, f.dtype) for g, f in zip(geo, fulls)],
        mesh=plsc.ScalarSubcoreMesh(axis_name="sequencer", num_cores=1),
        scratch_types=[pltpu.SemaphoreType.DMA((N_CHIPS * n,))] * 2,
        compiler_params=pltpu.CompilerParams(collective_id=SWAP_COLLECTIVE_ID),
    )(*fulls)


def _add_own_half(full, recv, kind, shard_shape, cidx, name):
    rh, cols = _half_rows(shard_shape), shard_shape[1]
    tr = _pick(rh, (256, 128, 64, 32, 16))
    nb = rh // tr

    def body(c_ref, f_ref, r_ref, o_ref):
        o_ref[...] = (f_ref[...].astype(F32) + r_ref[...].astype(F32)).astype(o_ref.dtype)

    if kind == "col":
        f_spec = pl.BlockSpec((tr, cols), lambda s, i, c_ref: (c_ref[0] * nb + i, s))
    else:
        f_spec = pl.BlockSpec((tr, cols), lambda s, i, c_ref: ((s * 2 + c_ref[0]) * nb + i, 0))
    blk = pl.BlockSpec((None, tr, cols), lambda s, i, c_ref: (s, i, 0))
    return _pcall(
        body, name=name,
        grid_spec=pltpu.PrefetchScalarGridSpec(num_scalar_prefetch=1, grid=(N_CHIPS, nb), in_specs=[f_spec, blk], out_specs=blk),
        out_shape=jax.ShapeDtypeStruct((N_CHIPS, rh, cols), BF16),
        compiler_params=_cp(("arbitrary", "arbitrary")),
    )(cidx, full, recv)


def _scatter_partials(partials, name):
    n = len(partials)

    def body(*refs):
        srcs, outs = refs[:n], refs[n:2 * n]
        send_sems, recv_sems = refs[2 * n:]
        x, y, c = _mesh_pos()
        barrier = pltpu.get_barrier_semaphore()
        for fx, fy in CHIP_RELS:
            pl.semaphore_signal(barrier, inc=1, device_id=(x ^ fx, y ^ fy, c), device_id_type=MESH)
        pl.semaphore_wait(barrier, len(CHIP_RELS))
        my = 2 * x + y
        cps = []
        for a in range(n):
            for r, (fx, fy) in enumerate(CHIP_RELS):
                to = 2 * (x ^ fx) + (y ^ fy)
                cp = pltpu.make_async_remote_copy(
                    src_ref=srcs[a].at[to], dst_ref=outs[a].at[my],
                    send_sem=send_sems.at[3 * a + r], recv_sem=recv_sems.at[3 * a + r],
                    device_id=(x ^ fx, y ^ fy, c), device_id_type=MESH)
                cp.start()
                cps.append(cp)
        for a in range(n):
            for r, (fx, fy) in enumerate(CHIP_RELS):
                frm = 2 * (x ^ fx) + (y ^ fy)
                pltpu.make_async_remote_copy(
                    src_ref=outs[a].at[frm], dst_ref=outs[a].at[frm],
                    send_sem=send_sems.at[3 * a + r], recv_sem=recv_sems.at[3 * a + r],
                    device_id=(x, y, c), device_id_type=MESH).wait_recv()
        for cp in cps:
            cp.wait_send()

    return pl.kernel(
        body, name=name,
        out_type=[jax.ShapeDtypeStruct(p.shape, p.dtype) for p in partials],
        mesh=plsc.ScalarSubcoreMesh(axis_name="sequencer", num_cores=1),
        scratch_types=[pltpu.SemaphoreType.DMA((3 * n,))] * 2,
        compiler_params=pltpu.CompilerParams(collective_id=SCATTER_COLLECTIVE_ID),
    )(*partials)


def _sum_chips_into(landed, partial, buf, layer, n_layers, pos, name):
    _, rh, cols = landed.shape
    tr = _pick(rh, (256, 128, 64, 32, 16))
    nb = rh // tr

    def body(pos_ref, l_ref, own_ref, *rest):
        o_ref = rest[-1]
        my = pos_ref[1]
        own = own_ref[...].astype(F32)
        acc = jnp.where(my == 0, own, l_ref[0].astype(F32))
        for j in range(1, N_CHIPS):
            acc = acc + jnp.where(my == j, own, l_ref[j].astype(F32))
        o_ref[...] = acc

    in_specs = [pl.BlockSpec((N_CHIPS, tr, cols), lambda i, pos_ref: (0, i, 0)),
                pl.BlockSpec((None, tr, cols), lambda i, pos_ref: (pos_ref[1], i, 0))]
    args = [pos, landed, partial]
    aliases = {}
    if buf is not None:
        in_specs.append(pl.BlockSpec(memory_space=pl.ANY))
        args.append(buf)
        aliases = {3: 0}
    return _pcall(
        body, name=name,
        grid_spec=pltpu.PrefetchScalarGridSpec(
            num_scalar_prefetch=1, grid=(nb,), in_specs=in_specs,
            out_specs=pl.BlockSpec((None, tr, cols), lambda i, pos_ref: (layer, pos_ref[0] * nb + i, 0))),
        out_shape=jax.ShapeDtypeStruct((n_layers, 2 * rh, cols), F32), input_output_aliases=aliases,
        compiler_params=_cp(("arbitrary",)),
    )(*args)


def _exchange_final(bufs):
    n = len(bufs)
    HBM = pl.BlockSpec(memory_space=pl.ANY)
    n_layers = [b.shape[0] for b in bufs]
    base = np.concatenate([[0], np.cumsum(n_layers)])

    def body(*refs):
        outs = refs[n:2 * n]
        send_sems, recv_sems = refs[2 * n:]
        x, y, c = _mesh_pos()
        cps = []
        for w in range(n):
            rh = bufs[w].shape[1] // 2
            for l in range(n_layers[w]):
                k = int(base[w]) + l
                mine = outs[w].at[l, pl.ds(c * rh, rh), :]
                cp = pltpu.make_async_remote_copy(
                    src_ref=mine, dst_ref=mine, send_sem=send_sems.at[k], recv_sem=recv_sems.at[k],
                    device_id=(x, y, 1 - c), device_id_type=MESH)
                cp.start()
                cps.append(cp)
        for w in range(n):
            rh = bufs[w].shape[1] // 2
            for l in range(n_layers[w]):
                k = int(base[w]) + l
                other = outs[w].at[l, pl.ds((1 - c) * rh, rh), :]
                pltpu.make_async_remote_copy(
                    src_ref=other, dst_ref=other, send_sem=send_sems.at[k], recv_sem=recv_sems.at[k],
                    device_id=(x, y, c), device_id_type=MESH).wait_recv()
        for cp in cps:
            cp.wait_send()

    return _pcall(
        body, name="grad_exchange_final", out_shape=[jax.ShapeDtypeStruct(b.shape, b.dtype) for b in bufs],
        in_specs=[HBM] * n, out_specs=[HBM] * n, input_output_aliases={i: i for i in range(n)},
        scratch_shapes=[pltpu.SemaphoreType.DMA((int(base[-1]),))] * 2,
        compiler_params=pltpu.CompilerParams(vmem_limit_bytes=VMEM_LIMIT),
    )(*bufs)


def _swap_halves_last(z):
    h = z.shape[-1] // 2
    return jnp.concatenate([z[..., h:], z[..., :h]], axis=-1)


def _ext_uq(w):
    lead = w.shape[:-1]
    wh = w.reshape(lead + (-1, QK_DIM))
    rope = wh[..., NOPE:]
    return jnp.concatenate([wh, _swap_halves_last(rope)], axis=-1).reshape(lead + (-1,))


def _fold_uq(d):
    lead = d.shape[:-1]
    dh = d.reshape(lead + (-1, HEAD_W))
    rope = dh[..., NOPE:QK_DIM] + _swap_halves_last(dh[..., QK_DIM:])
    return jnp.concatenate([dh[..., :NOPE], rope], axis=-1).reshape(lead + (-1,))


def _ext_win(w):
    base, kr = w[..., :-ROPE], w[..., -ROPE:]
    ks = _swap_halves_last(kr)
    return jnp.concatenate([base, kr, ks, ks, kr], axis=-1)


def _fold_win(d):
    n = d.shape[-1] - 4 * ROPE
    a, b, c2, e = [d[..., n + i * ROPE:n + (i + 1) * ROPE] for i in range(4)]
    return jnp.concatenate([d[..., :n], a + e + _swap_halves_last(b + c2)], axis=-1)


ROW_TILE = 256


def _stage(st, xcur, mix, big, small, tabs):
    D = xcur.shape[1]
    l = st // 2
    mod = [[small["mod"][k][i].reshape(1, 1, D) for i in range(6)] for k in range(l + 1)]
    sh1, sc1, g1, sh2, sc2, _ = mod[l]
    if st % 2 == 0:
        n1 = small["norm1_g"][l].reshape(1, 1, D)
        if l == 0:
            (h,) = _rowop("norm_mod_0", _f_norm_mod, n_x=1, n_nd=0, n_p=3, x_w=[D], nd_w=[], nd_shared=[], p_per_group=[False] * 3,
                          out_w=[D], out_dtypes=[BF16], tile=ROW_TILE)((xcur,), (), (n1, sh1, sc1))
        else:
            op = _rowop(f"resid_norm_mod_a{l}", _f_resid_norm_mod, n_x=2, n_nd=0, n_p=4, x_w=[D, D], nd_w=[], nd_shared=[],
                        p_per_group=[False] * 4, out_w=[D, D], out_dtypes=[F32, BF16], tile=ROW_TILE)
            xcur, h = op((xcur, mix), (), (mod[l - 1][5], n1, sh1, sc1))
        if l % 2 == 0:
            return xcur, _even_mixer(h, big, small, l // 2, ROW_TILE)
        return xcur, _mla_mixer(h, big, small, l // 2, ROW_TILE, *tabs)
    n2 = small["norm2_g"][l].reshape(1, 1, D)
    op = _rowop(f"resid_norm_mod_b{l}", _f_resid_norm_mod, n_x=2, n_nd=0, n_p=4, x_w=[D, D], nd_w=[], nd_shared=[],
                p_per_group=[False] * 4, out_w=[D, D], out_dtypes=[F32, BF16], tile=ROW_TILE)
    xcur, h = op((xcur, mix), (), (g1, n2, sh2, sc2))
    return xcur, _mlp(f"mlp_{l}", l >= 2)(h, big["mlp_w1"], big["mlp_w2"])


def _last_residual(xcur, mix, small):
    D = xcur.shape[1]
    gate = small["mod"][-1][5].reshape(1, 1, D)
    return _rowop("resid_last", _f_resid, n_x=2, n_nd=0, n_p=1, x_w=[D, D], nd_w=[], nd_shared=[], p_per_group=[False],
                  out_w=[D], out_dtypes=[F32], tile=ROW_TILE)((xcur, mix), (), (gate,))[0]


def _even_mixer(h, big, wts, e, tile):
    proj = _linear(f"ab_in_{e}", BF16)(h, big["ab_w_in"])
    da = proj.shape[1] // 4
    u, v, a, g = [proj[:, i * da:(i + 1) * da] for i in range(4)]
    ng = da // GROUP
    sgu = _rowop(f"sgu_{e}", _f_sgu, n_x=2, n_nd=0, n_p=3, x_w=[GROUP, GROUP], nd_w=[], nd_shared=[], p_per_group=[True] * 3,
                 out_w=[GROUP], out_dtypes=[BF16], tile=SGU_CHUNKS_PER_STEP * CHUNK, groups=ng)
    bexp = jnp.broadcast_to(wts["sgu_b"][e][:, :, None], (ng, CHUNK, GROUP))
    (out_a,) = sgu((u, v), (), (wts["sgu_norm_g"][e].reshape(ng, 1, GROUP), wts["sgu_w"][e], bexp))
    (yglu,) = _rowop(f"glu_{e}", _f_glu, n_x=2, n_nd=0, n_p=0, x_w=[da, da], nd_w=[], nd_shared=[], p_per_group=[],
                     out_w=[da], out_dtypes=[F32], tile=tile)((a, g), (), ())
    z = _conv_op(f"conv_{e}")(yglu, wts["conv_w"][e], wts["conv_b"][e].reshape(1, da))
    (out_b,) = _rowop(f"ln_silu_{e}", _f_ln_silu, n_x=1, n_nd=0, n_p=2, x_w=[da], nd_w=[], nd_shared=[], p_per_group=[False] * 2,
                      out_w=[da], out_dtypes=[BF16], tile=tile)(
        (z,), (), (wts["conv_ln_g"][e].reshape(1, 1, da), wts["conv_ln_b"][e].reshape(1, 1, da)))
    return _linear(f"ab_out_{e}")(jnp.concatenate([out_a, out_b], axis=-1), big["ab_w_out"])


def _mla_mixer(h, big, wts, o, tile, tab_q, tab_ka, tab_kb):
    proj = _linear(f"mla_in_{o}", BF16)(h, big["mla_w_in"])
    rank = (proj.shape[1] - 4 * ROPE) // 2
    c_q, c_kv = proj[:, :rank], proj[:, rank:2 * rank]
    kr_a, kr_b = proj[:, 2 * rank:2 * rank + 2 * ROPE], proj[:, 2 * rank + 2 * ROPE:]

    def rms(name, xx, gg):
        return _rowop(name, _f_rms, n_x=1, n_nd=0, n_p=1, x_w=[rank], nd_w=[], nd_shared=[], p_per_group=[False],
                      out_w=[rank], out_dtypes=[BF16], tile=tile)((xx,), (), (gg.reshape(1, 1, rank),))[0]

    q_raw = _linear(f"mla_uq_{o}", BF16)(rms(f"rms_q_{o}", c_q, wts["mla_q_norm_g"][o]), big["mla_w_uq"])
    kv_raw = _linear(f"mla_ukv_{o}", BF16)(rms(f"rms_kv_{o}", c_kv, wts["mla_kv_norm_g"][o]), big["mla_w_ukv"])
    gq, gk = wts["mla_q_head_g"][o], wts["mla_k_head_g"][o]
    gk_rope = gk[NOPE:]
    gq_ext = jnp.concatenate([gq, _swap_halves_last(gq[NOPE:])]).reshape(1, 1, HEAD_W)
    gk_ext = jnp.concatenate([gk[:NOPE], jnp.ones((HEAD_W - NOPE,), F32)]).reshape(1, 1, HEAD_W)
    gk_a = jnp.concatenate([gk_rope, _swap_halves_last(gk_rope)]).reshape(1, 1, 2 * ROPE)
    gk_b = jnp.concatenate([_swap_halves_last(gk_rope), gk_rope]).reshape(1, 1, 2 * ROPE)
    head_tile = 4 * tile
    (q,) = _rowop(f"q_head_{o}", _f_qhead, n_x=1, n_nd=1, n_p=1, x_w=[HEAD_W], nd_w=[HEAD_W], nd_shared=[True], p_per_group=[False],
                  out_w=[HEAD_W], out_dtypes=[BF16], tile=head_tile, groups=N_HEADS)((q_raw,), (tab_q,), (gq_ext,))
    (kv,) = _rowop(f"kv_head_{o}", _f_kvhead, n_x=1, n_nd=0, n_p=1, x_w=[HEAD_W], nd_w=[], nd_shared=[], p_per_group=[False],
                   out_w=[HEAD_W], out_dtypes=[BF16], tile=head_tile, groups=N_HEADS)((kv_raw,), (), (gk_ext,))
    (kr,) = _rowop(f"k_rope_{o}", _f_krope, n_x=2, n_nd=2, n_p=2, x_w=[2 * ROPE] * 2, nd_w=[2 * ROPE] * 2, nd_shared=[True] * 2,
                   p_per_group=[False] * 2, out_w=[2 * ROPE], out_dtypes=[BF16], tile=tile)((kr_a, kr_b), (tab_ka, tab_kb), (gk_a, gk_b))
    att = _attn_op(f"attn_{o}")(q, kv, kr)
    return _linear(f"mla_out_{o}")(att, big["mla_w_out"])


def _rope_tabs(S):
    pos = jnp.arange(S, dtype=F32)
    inv = ROPE_THETA ** (-jnp.arange(0, ROPE, 2, dtype=F32) / ROPE)
    ang = pos[:, None] * inv[None, :]
    cos, sin = jnp.cos(ang), jnp.sin(ang)
    cc = jnp.concatenate([cos, cos], axis=-1)
    sg = jnp.concatenate([-sin, sin], axis=-1)
    tab_q = jnp.concatenate([jnp.ones((S, NOPE), F32), cc, sg], axis=-1)
    return tab_q, jnp.concatenate([cc, sg], axis=-1), jnp.concatenate([sg, cc], axis=-1)


def kernel(x, c, norm1_g, norm2_g, ada_w, ada_b, mlp_w1, mlp_w2, ab_w_in, sgu_norm_g, sgu_w, sgu_b, conv_w, conv_b, conv_ln_g, conv_ln_b, ab_w_out, mla_w_in, mla_q_norm_g, mla_kv_norm_g, mla_w_uq, mla_w_ukv, mla_q_head_g, mla_k_head_g, mla_w_out, loss_target, m_norm1_g, m_norm2_g, m_ada_w, m_ada_b, m_mlp_w1, m_mlp_w2, m_ab_w_in, m_sgu_norm_g, m_sgu_w, m_sgu_b, m_conv_w, m_conv_b, m_conv_ln_g, m_conv_ln_b, m_ab_w_out, m_mla_w_in, m_mla_q_norm_g, m_mla_kv_norm_g, m_mla_w_uq, m_mla_w_ukv, m_mla_q_head_g, m_mla_k_head_g, m_mla_w_out, v_norm1_g, v_norm2_g, v_ada_w, v_ada_b, v_mlp_w1, v_mlp_w2, v_ab_w_in, v_sgu_norm_g, v_sgu_w, v_sgu_b, v_conv_w, v_conv_b, v_conv_ln_g, v_conv_ln_b, v_ab_w_out, v_mla_w_in, v_mla_q_norm_g, v_mla_kv_norm_g, v_mla_w_uq, v_mla_w_ukv, v_mla_q_head_g, v_mla_k_head_g, v_mla_w_out):
    names = ["norm1_g", "norm2_g", "ada_w", "ada_b", "mlp_w1", "mlp_w2", "ab_w_in", "sgu_norm_g", "sgu_w", "sgu_b", "conv_w",
             "conv_b", "conv_ln_g", "conv_ln_b", "ab_w_out", "mla_w_in", "mla_q_norm_g", "mla_kv_norm_g", "mla_w_uq", "mla_w_ukv",
             "mla_q_head_g", "mla_k_head_g", "mla_w_out"]
    W = dict(zip(names, [norm1_g, norm2_g, ada_w, ada_b, mlp_w1, mlp_w2, ab_w_in, sgu_norm_g, sgu_w, sgu_b, conv_w, conv_b, conv_ln_g,
                         conv_ln_b, ab_w_out, mla_w_in, mla_q_norm_g, mla_kv_norm_g, mla_w_uq, mla_w_ukv, mla_q_head_g, mla_k_head_g,
                         mla_w_out]))
    M = dict(zip(names, [m_norm1_g, m_norm2_g, m_ada_w, m_ada_b, m_mlp_w1, m_mlp_w2, m_ab_w_in, m_sgu_norm_g, m_sgu_w, m_sgu_b, m_conv_w,
                         m_conv_b, m_conv_ln_g, m_conv_ln_b, m_ab_w_out, m_mla_w_in, m_mla_q_norm_g, m_mla_kv_norm_g, m_mla_w_uq,
                         m_mla_w_ukv, m_mla_q_head_g, m_mla_k_head_g, m_mla_w_out]))
    V = dict(zip(names, [v_norm1_g, v_norm2_g, v_ada_w, v_ada_b, v_mlp_w1, v_mlp_w2, v_ab_w_in, v_sgu_norm_g, v_sgu_w, v_sgu_b, v_conv_w,
                         v_conv_b, v_conv_ln_g, v_conv_ln_b, v_ab_w_out, v_mla_w_in, v_mla_q_norm_g, v_mla_kv_norm_g, v_mla_w_uq,
                         v_mla_w_ukv, v_mla_q_head_g, v_mla_k_head_g, v_mla_w_out]))
    xi, yi, ci = lax.axis_index("x"), lax.axis_index("y"), lax.axis_index("c")
    chip = 2 * xi + yi
    dev = 2 * chip + ci
    pos = jnp.stack([ci, chip]).astype(jnp.int32)
    S, D = x.shape[1], x.shape[2]
    depth = norm1_g.shape[0]

    c_all, conv_w_all, qn_all, kvn_all = _all_gather_small([c, conv_w, mla_q_norm_g, mla_kv_norm_g], "gather_small_inputs")
    c_all = c_all.reshape(N_DEV, D)
    by_chip = lambda a: jnp.concatenate([a[2 * j] for j in range(N_CHIPS)], axis=-1)
    conv_w_full, qn_full, kvn_full = by_chip(conv_w_all), by_chip(qn_all), by_chip(kvn_all)

    (c_act,) = _rowop("silu_c", _f_silu, n_x=1, n_nd=0, n_p=0, x_w=[D], nd_w=[], nd_shared=[], p_per_group=[], out_w=[D],
                      out_dtypes=[F32], tile=N_DEV)((c_all,), (), ())
    c_act_pad = jnp.pad(c_act, ((0, 128 - N_DEV), (0, 0)))
    mod_cols = jnp.stack([_matmul(c_act_pad, ada_w[l], name=f"ada_fwd_{l}")[:N_DEV] for l in range(depth)])
    (mod_all,) = _all_gather_small([mod_cols], "gather_mod")
    mod_mine = jnp.concatenate([lax.dynamic_index_in_dim(mod_all[2 * j], dev, axis=1, keepdims=False) for j in range(N_CHIPS)], axis=-1)
    mod_mine = (mod_mine + ada_b).reshape(depth, 6, D)

    big = {"mlp_w1": "col", "mlp_w2": "row", "ab_w_in": "col", "ab_w_out": "row", "mla_w_in": "row", "mla_w_uq": "col",
           "mla_w_ukv": "col", "mla_w_out": "row"}
    src = dict(W)
    src["mla_w_in"] = _ext_win(mla_w_in)
    src["mla_w_uq"] = _ext_uq(mla_w_uq)
    shards, kinds, owner = [], [], []
    for nme, kind in big.items():
        for l in range(src[nme].shape[0]):
            shards.append(src[nme][l].astype(BF16))
            kinds.append(kind)
            owner.append((nme, l))
    def stage_of(nme, l):
        return 2 * l + 1 if nme.startswith("mlp") else (4 * l if nme.startswith("ab") else 4 * l + 2)

    fwd_stages = [[a for a, (nme, l) in enumerate(owner) if stage_of(nme, l) == st] for st in range(2 * depth)]
    fulls = [None] * len(shards)
    for st, idx in enumerate(fwd_stages):
        got = _gather_big([shards[a] for a in idx], [kinds[a] for a in idx], f"gather_weights_{st}")
        for a, f in zip(idx, got):
            fulls[a] = f
    small = dict(norm1_g=norm1_g, norm2_g=norm2_g, mod=mod_mine, sgu_norm_g=sgu_norm_g, sgu_w=sgu_w, sgu_b=sgu_b, conv_w=conv_w_full,
                 conv_b=conv_b, conv_ln_g=conv_ln_g, conv_ln_b=conv_ln_b, mla_q_norm_g=qn_full, mla_kv_norm_g=kvn_full,
                 mla_q_head_g=mla_q_head_g, mla_k_head_g=mla_k_head_g)
    groups, a0 = [], 0
    for nme in big:
        groups.append(list(range(a0, a0 + src[nme].shape[0])))
        a0 += src[nme].shape[0]
    shard_shapes = [s.shape for s in shards]
    where = {a: (wi, l) for wi, idxs in enumerate(groups) for l, a in enumerate(idxs)}

    tabs = _rope_tabs(S)
    n_stages = 2 * depth
    xcur, mix, vjps = x[0], None, []
    for st, idx in enumerate(fwd_stages):
        stage_big = {owner[a][0]: fulls[a] for a in idx}
        if st == 0:
            (xcur, mix), vjp = jax.vjp(lambda xx, bb, ss: _stage(0, xx, None, bb, ss, tabs), xcur, stage_big, small)
        else:
            (xcur, mix), vjp = jax.vjp(lambda xx, mm, bb, ss, st=st: _stage(st, xx, mm, bb, ss, tabs), xcur, mix, stage_big, small)
        vjps.append(vjp)
    y, vjp_last = jax.vjp(_last_residual, xcur, mix, small)
    dy, loss_mine = _loss_call(y, loss_target[0])
    loss = lax.psum(loss_mine[0, 0], ("x", "y", "c"))

    dxc, dmix, dsmall = vjp_last(dy)
    bufs = [None] * len(groups)
    swapped, scattered = {}, {}

    def add_and_scatter(st):
        idx, gr, recv = swapped.pop(st)
        partials = [_add_own_half(gr[a], r, kinds[a], shard_shapes[a], pos, f"grad_add_sibling_{a}") for a, r in zip(idx, recv)]
        scattered[st] = (idx, partials, _scatter_partials(partials, f"grad_scatter_partials_{st}"))

    def sum_chips(st):
        idx, partials, landed = scattered.pop(st)
        for a, part, land in zip(idx, partials, landed):
            wi, l = where[a]
            bufs[wi] = _sum_chips_into(land, part, bufs[wi], l, len(groups[wi]), pos, f"grad_sum_chips_{a}")

    for st in reversed(range(n_stages)):
        idx = fwd_stages[st]
        if st == 0:
            dxc, dbig, ds = vjps[st]((dxc, dmix))
        else:
            dxc, dmix, dbig, ds = vjps[st]((dxc, dmix))
        dsmall = jax.tree.map(jnp.add, dsmall, ds)
        gr = {a: dbig[owner[a][0]] for a in idx}
        swapped[st] = (idx, gr, _swap_halves([gr[a] for a in idx], [kinds[a] for a in idx], [shard_shapes[a] for a in idx],
                                             f"grad_swap_halves_{st}"))
        started = [s for s in ((st + 1, st) if st == 1 else (st + 1,)) if s in swapped]
        for s in started:
            add_and_scatter(s)
        if st + 2 in scattered:
            sum_chips(st + 2)
        if st > 0:
            pinned = {s: scattered[s][1] for s in started}
            (dxc, dmix), pinned, bufs = lax.optimization_barrier(((dxc, dmix), pinned, bufs))
            for s in started:
                scattered[s] = (scattered[s][0], pinned[s], scattered[s][2])
    dx = dxc
    for st in sorted(swapped, reverse=True):
        add_and_scatter(st)
    for st in sorted(scattered, reverse=True):
        sum_chips(st)
    gsh = _exchange_final(bufs)

    dw = dsmall
    dmod = dw["mod"].reshape(depth, 6 * D)
    small_names = ["norm1_g", "norm2_g", "sgu_norm_g", "sgu_w", "sgu_b", "conv_w", "conv_b", "conv_ln_g", "conv_ln_b", "mla_q_norm_g",
                   "mla_kv_norm_g", "mla_q_head_g", "mla_k_head_g"]
    red = _all_reduce_small([dmod] + [dw[nme] for nme in small_names], "reduce_small_grads")
    G = dict(zip(["ada_b"] + small_names, red))
    own_cols = lambda a: lax.dynamic_slice_in_dim(a, chip * (a.shape[-1] // N_CHIPS), a.shape[-1] // N_CHIPS, axis=-1)
    for nme in ("conv_w", "mla_q_norm_g", "mla_kv_norm_g"):
        G[nme] = own_cols(G[nme])

    (dmod_all,) = _all_gather_small([dmod], "gather_dmod")
    dmod_cols = own_cols(dmod_all)
    dmod_pad = jnp.pad(dmod_cols, ((0, 128 - N_DEV), (0, 0), (0, 0)))
    G["ada_w"] = _ada_wgrad_call(c_act_pad, dmod_pad)

    for nme, g in zip(big, gsh):
        G[nme] = g
    G["mla_w_in"] = _fold_win(G["mla_w_in"])
    G["mla_w_uq"] = _fold_uq(G["mla_w_uq"])

    deltas, new_m, new_v = [], [], []
    for nme in names:
        d, mn, vn = _adamw_call(W[nme], G[nme], M[nme], V[nme], f"adamw_{nme}")
        deltas.append(d)
        new_m.append(mn)
        new_v.append(vn)
    return (loss, dx[None], *[G[nme] for nme in names], *deltas, *new_m, *new_v)
```

```python
import functools
import numpy as np
import jax
import jax.numpy as jnp
from jax import lax
from jax.experimental import pallas as pl
from jax.experimental.pallas import tpu as pltpu
from jax.experimental.pallas import tpu_sc as plsc

F32 = jnp.float32
BF16 = jnp.bfloat16
MESH = pl.DeviceIdType.MESH

EPS = 1e-6
N_HEADS = 16
NOPE = 128
ROPE = 64
VDIM = 128
QK_DIM = NOPE + ROPE
HEAD_W = 256
CHUNK = 128
GROUP = 128
SGU_CHUNKS_PER_STEP = 4
CONV_W = 31
CONV_PAD = 32
ROPE_THETA = 10000.0
ATTN_SCALE = QK_DIM ** -0.5
LOG2E = 1.4426950408889634
ATTN_SCALE_LOG2E = ATTN_SCALE * LOG2E
ADAM_LR, ADAM_B1, ADAM_B2, ADAM_EPS, ADAM_WD, ADAM_STEP = 0.001, 0.9, 0.999, 1e-08, 0.01, 10
N_CHIPS = 4
N_DEV = 8
VMEM_LIMIT = 56 * 1024 * 1024
SMALL_COLS = 1024
CHIP_RELS = ((1, 0), (0, 1), (1, 1))
GATHER_COLLECTIVE_ID = 0
SWAP_COLLECTIVE_ID = 1
SCATTER_COLLECTIVE_ID = 2


def _pcall(body, **kw):
    return pl.pallas_call(body, **kw)


def _cp(sem=None, **kw):
    return pltpu.CompilerParams(dimension_semantics=sem, vmem_limit_bytes=VMEM_LIMIT, **kw)


def _pick(n, cands):
    for c in cands:
        if n % c == 0:
            return c
    return n


def _matmul(a, b, *, ta=False, tb=False, out_dtype=F32, name, extra=(), epilogue=None, out_dtypes=None, tiles=None):
    if ta:
        K, M = a.shape
    else:
        M, K = a.shape
    if tb:
        N, K2 = b.shape
    else:
        K2, N = b.shape
    assert K == K2, (a.shape, b.shape, ta, tb)
    tm = _pick(M, (1024, 512, 256, 128))
    tn = N if N <= 1536 else _pick(N, (1024, 512, 256, 128))
    tk = _pick(K, (2048, 1024, 512, 256, 128))
    if tiles is not None and M % tiles[0] == 0 and N % tiles[1] == 0 and K % tiles[2] == 0:
        tm, tn, tk = tiles
    nk = K // tk
    dn = (((0 if ta else 1,), (1 if tb else 0,)), ((), ()))

    n_extra = len(extra)
    single = epilogue is None
    if single:
        out_dtypes = [out_dtype]

    n_out = len(out_dtypes)

    def body(a_ref, b_ref, *rest):
        extra_refs, o_refs = rest[:n_extra], rest[n_extra:n_extra + n_out]

        def dot():
            return lax.dot_general(a_ref[...].astype(BF16), b_ref[...].astype(BF16), dn, preferred_element_type=F32)

        def finish(acc):
            res = (acc,) if single else epilogue(acc, *[r[...] for r in extra_refs])
            for o_ref, val in zip(o_refs, res):
                o_ref[...] = val.astype(o_ref.dtype)

        if nk == 1:
            finish(dot())
            return
        acc_ref = rest[-1]
        k = pl.program_id(2)

        @pl.when(k == 0)
        def _():
            acc_ref[...] = jnp.zeros_like(acc_ref)

        acc_ref[...] += dot()

        @pl.when(k == nk - 1)
        def _():
            finish(acc_ref[...])

    a_spec = pl.BlockSpec((tk, tm), lambda i, j, k: (k, i)) if ta else pl.BlockSpec((tm, tk), lambda i, j, k: (i, k))
    b_spec = pl.BlockSpec((tn, tk), lambda i, j, k: (j, k)) if tb else pl.BlockSpec((tk, tn), lambda i, j, k: (k, j))
    mn_spec = pl.BlockSpec((tm, tn), lambda i, j, k: (i, j))
    outs = _pcall(
        body, name=name, grid=(M // tm, N // tn, nk), in_specs=[a_spec, b_spec] + [mn_spec] * n_extra,
        out_specs=[mn_spec] * len(out_dtypes),
        out_shape=[jax.ShapeDtypeStruct((M, N), d) for d in out_dtypes],
        scratch_shapes=[pltpu.VMEM((tm, tn), F32)] if nk > 1 else [],
        compiler_params=_cp(("parallel", "parallel", "arbitrary")),
    )(a, b, *extra)
    return outs[0] if single else outs


def _linear(name, out_dtype=F32):
    @jax.custom_vjp
    def mm(a, w):
        return _matmul(a, w, out_dtype=out_dtype, name=name + "_fwd")

    def fwd(a, w):
        return mm(a, w), (a, w)

    def bwd(res, dy):
        a, w = res
        da = _matmul(dy, w, tb=True, out_dtype=a.dtype, name=name + "_dgrad")
        dw = _matmul(a, dy, ta=True, out_dtype=w.dtype, name=name + "_wgrad")
        return da, dw

    mm.defvjp(fwd, bwd)
    return mm


def _ada_wgrad_call(c_act, dmod):
    P, D = c_act.shape
    _, L, C = dmod.shape
    tm = _pick(D, (1024, 512, 256, 128))
    tn = _pick(C, (1024, 512, 256, 128))
    nj = C // tn

    def body(a_ref, b_ref, o_ref):
        o_ref[...] = lax.dot_general(a_ref[...].astype(BF16), b_ref[...].astype(BF16), (((0,), (0,)), ((), ())),
                                     preferred_element_type=F32)

    return _pcall(
        body, name="ada_wgrad", grid=(L, D // tm, nj),
        in_specs=[pl.BlockSpec((P, tm), lambda l, i, j: (0, i)), pl.BlockSpec((P, tn), lambda l, i, j: (0, l * nj + j))],
        out_specs=pl.BlockSpec((None, tm, tn), lambda l, i, j: (l, i, j)),
        out_shape=jax.ShapeDtypeStruct((L, D, C), F32), compiler_params=_cp(("parallel", "parallel", "parallel")),
    )(c_act, dmod.reshape(P, L * C))


def _relu2_epilogue(acc):
    return acc, jnp.square(jnp.maximum(acc, 0.0))


def _relu2_grad_epilogue(acc, h1):
    return (acc * (2.0 * jnp.maximum(h1.astype(F32), 0.0)),)


def _mlp(name, whole_k):
    t_upw = (1024, 1024, 4096) if whole_k else None

    def run(h, w1, w2):
        h1, act = _matmul(h, w1, name=name + "_up_fwd", epilogue=_relu2_epilogue, out_dtypes=[BF16, BF16])
        return _matmul(act, w2, name=name + "_down_fwd"), h1, act

    @jax.custom_vjp
    def mlp(h, w1, w2):
        return run(h, w1, w2)[0]

    def fwd(h, w1, w2):
        y, h1, act = run(h, w1, w2)
        return y, (h, w1, w2, h1, act)

    def bwd(res, dy):
        h, w1, w2, h1, act = res
        dw2 = _matmul(act, dy, ta=True, out_dtype=w2.dtype, name=name + "_down_wgrad")
        (dh1,) = _matmul(dy, w2, tb=True, name=name + "_down_dgrad", extra=(h1,), epilogue=_relu2_grad_epilogue, out_dtypes=[BF16])
        dw1 = _matmul(h, dh1, ta=True, out_dtype=w1.dtype, name=name + "_up_wgrad", tiles=t_upw)
        dh = _matmul(dh1, w1, tb=True, out_dtype=h.dtype, name=name + "_up_dgrad")
        return dh, dw1, dw2

    mlp.defvjp(fwd, bwd)
    return mlp


def _rowop(name, f, *, n_x, n_nd, n_p, x_w, nd_w, nd_shared, p_per_group, out_w, out_dtypes, tile, groups=1):
    G = groups

    def specs(S):
        t = min(tile, S)
        xs = [pl.BlockSpec((t, w), lambda g, r: (r, g)) for w in x_w]
        nds = [pl.BlockSpec((t, w), (lambda g, r: (r, 0)) if sh else (lambda g, r: (r, g))) for w, sh in zip(nd_w, nd_shared)]
        outs = [pl.BlockSpec((t, w), lambda g, r: (r, g)) for w in out_w]
        return t, xs, nds, outs

    def pspecs(ps):
        return [pl.BlockSpec((None,) + p.shape[1:], (lambda g, r: (g, 0, 0)) if pg else (lambda g, r: (0, 0, 0)))
                for p, pg in zip(ps, p_per_group)]

    def fwd_call(xs, nds, ps):
        S = xs[0].shape[0]
        t, xsp, ndsp, osp = specs(S)

        def body(*refs):
            ins, outs = refs[:n_x + n_nd + n_p], refs[n_x + n_nd + n_p:]
            vals = [r[...].astype(F32) for r in ins]
            res = f(*vals)
            for o, r in zip(res, outs):
                r[...] = o.astype(r.dtype)

        return _pcall(
            body, name=name + "_fwd", grid=(G, S // t), in_specs=xsp + ndsp + pspecs(ps), out_specs=osp,
            out_shape=[jax.ShapeDtypeStruct((S, G * w), d) for w, d in zip(out_w, out_dtypes)],
            compiler_params=_cp(("parallel", "parallel")),
        )(*xs, *nds, *ps)

    def bwd_call(xs, nds, ps, douts):
        S = xs[0].shape[0]
        t, xsp, ndsp, osp = specs(S)
        n_in = n_x + n_nd + n_p + len(out_w)

        def body(*refs):
            ins, outs = refs[:n_in], refs[n_in:]
            xv = [r[...].astype(F32) for r in ins[:n_x]]
            ndv = [r[...].astype(F32) for r in ins[n_x:n_x + n_nd]]
            pv = [r[...].astype(F32) for r in ins[n_x + n_nd:n_x + n_nd + n_p]]
            dov = tuple(r[...].astype(F32) for r in ins[n_x + n_nd + n_p:])
            _, vjp = jax.vjp(lambda *a: tuple(f(*a[:n_x], *ndv, *a[n_x:])), *xv, *pv)
            cts = vjp(dov)
            for i in range(n_x):
                outs[i][...] = cts[i].astype(outs[i].dtype)
            g, r = pl.program_id(0), pl.program_id(1)
            for i in range(n_p):
                first = (r == 0) if p_per_group[i] else jnp.logical_and(g == 0, r == 0)
                ref, ct = outs[n_x + i], cts[n_x + i]

                @pl.when(first)
                def _(ref=ref, ct=ct):
                    ref[...] = ct

                @pl.when(jnp.logical_not(first))
                def _(ref=ref, ct=ct):
                    ref[...] += ct

        return _pcall(
            body, name=name + "_bwd", grid=(G, S // t), in_specs=xsp + ndsp + pspecs(ps) + osp,
            out_specs=xsp + pspecs(ps),
            out_shape=[jax.ShapeDtypeStruct(x.shape, x.dtype) for x in xs] + [jax.ShapeDtypeStruct(p.shape, F32) for p in ps],
            compiler_params=_cp(("arbitrary", "arbitrary")),
        )(*xs, *nds, *ps, *douts)

    @jax.custom_vjp
    def op(xs, nds, ps):
        return tuple(fwd_call(xs, nds, ps))

    def op_fwd(xs, nds, ps):
        return op(xs, nds, ps), (xs, nds, ps)

    def op_bwd(res, douts):
        xs, nds, ps = res
        out = bwd_call(xs, nds, ps, douts)
        return tuple(out[:n_x]), tuple(jnp.zeros_like(n) for n in nds), tuple(out[n_x:])

    op.defvjp(op_fwd, op_bwd)
    return op


def _rms_rows(x):
    return x * lax.rsqrt(jnp.mean(x * x, axis=-1, keepdims=True) + EPS)


def _f_norm_mod(x, g, shift, scale):
    return ((_rms_rows(x) * g) * (1.0 + scale) + shift,)


def _f_resid_norm_mod(x, mix, gate, g, shift, scale):
    xn = x + gate * mix
    return xn, (_rms_rows(xn) * g) * (1.0 + scale) + shift


def _f_resid(x, mix, gate):
    return (x + gate * mix,)


def _f_rms(x, g):
    return (_rms_rows(x) * g,)


def _f_glu(a, g):
    return (a * jax.nn.sigmoid(g),)


def _f_ln_silu(z, g, b):
    mu = jnp.mean(z, axis=-1, keepdims=True)
    zc = z - mu
    var = jnp.mean(zc * zc, axis=-1, keepdims=True)
    y = zc * lax.rsqrt(var + EPS) * g + b
    return (y * jax.nn.sigmoid(y),)


def _f_silu(x):
    return (x * jax.nn.sigmoid(x),)


@jax.custom_vjp
def _bdot_chunks(a, b):
    return lax.dot_general(a.astype(BF16), b.astype(BF16), (((2,), (1,)), ((0,), (0,))), preferred_element_type=F32)


def _bdot_chunks_fwd(a, b):
    return _bdot_chunks(a, b), (a, b)


def _bdot_chunks_bwd(res, ct):
    a, b = res
    c16 = ct.astype(BF16)
    da = lax.dot_general(c16, b.astype(BF16), (((2,), (2,)), ((0,), (0,))), preferred_element_type=F32)
    db = lax.dot_general(a.astype(BF16), c16, (((1,), (1,)), ((0,), (0,))), preferred_element_type=F32)
    return da, db


_bdot_chunks.defvjp(_bdot_chunks_fwd, _bdot_chunks_bwd)


def _f_sgu(u, v, ng, w, bexp):
    n = u.shape[0] // CHUNK
    vn = (_rms_rows(jax.nn.gelu(v)) * ng).reshape(n, CHUNK, GROUP)
    row = lax.broadcasted_iota(jnp.int32, w.shape, 0)
    col = lax.broadcasted_iota(jnp.int32, w.shape, 1)
    wm = jnp.broadcast_to(jnp.where(row >= col, w, 0.0)[None], (n, CHUNK, CHUNK))
    mixed = _bdot_chunks(wm, vn) + bexp[None]
    return (jax.nn.gelu(u) * mixed.reshape(n * CHUNK, GROUP),)


def _lo_mask():
    return lax.broadcasted_iota(jnp.int32, (1, HEAD_W), 1) < NOPE


def _f_qhead(x, tab, g):
    lo = _lo_mask()
    x2 = x * x
    ms_lo = jnp.sum(jnp.where(lo, x2, 0.0), axis=-1, keepdims=True) * (1.0 / NOPE)
    ms_hi = jnp.sum(jnp.where(lo, 0.0, x2), axis=-1, keepdims=True) * (1.0 / (HEAD_W - NOPE))
    r = jnp.where(lo, lax.rsqrt(ms_lo + EPS), lax.rsqrt(ms_hi + EPS))
    return (((x * r) * g) * tab,)


def _f_kvhead(x, g):
    lo = _lo_mask()
    ms = jnp.sum(jnp.where(lo, x * x, 0.0), axis=-1, keepdims=True) * (1.0 / NOPE)
    return (jnp.where(lo, (x * lax.rsqrt(ms + EPS)) * g, x),)


def _f_krope(a, b, ta, tb, ga, gb):
    r = lax.rsqrt(jnp.mean(a * a, axis=-1, keepdims=True) + EPS)
    return (((a * r) * ga) * ta + ((b * r) * gb) * tb,)


def _conv_fwd_call(y, w, b, name):
    S, C = y.shape
    cw = 128
    rt = _pick(S, (128,))
    w = jnp.pad(w, ((0, CONV_PAD - CONV_W), (0, 0)))

    def body(y_ref, w_ref, b_ref, z_ref, pad_ref):
        pad_ref[pl.ds(0, CONV_PAD), :] = jnp.zeros((CONV_PAD, cw), F32)
        pad_ref[pl.ds(CONV_PAD, S), :] = y_ref[...]
        wv = w_ref[...]
        bv = b_ref[...]

        def chunk(ci, carry):
            r0 = pl.multiple_of(ci * rt, rt)
            win = pad_ref[pl.ds(r0, rt + CONV_PAD), :]
            acc = jnp.broadcast_to(bv, (rt, cw))
            for k in range(CONV_W):
                off = CONV_PAD - (CONV_W - 1) + k
                sh = win if off == 0 else pltpu.roll(win, rt + CONV_PAD - off, axis=0)
                acc = acc + wv[k:k + 1, :] * sh[:rt, :]
            z_ref[pl.ds(r0, rt), :] = acc
            return carry

        lax.fori_loop(0, S // rt, chunk, 0)

    return _pcall(
        body, name=name, grid=(C // cw,),
        in_specs=[pl.BlockSpec((S, cw), lambda j: (0, j)), pl.BlockSpec((CONV_PAD, cw), lambda j: (0, j)),
                  pl.BlockSpec((1, cw), lambda j: (0, j))],
        out_specs=pl.BlockSpec((S, cw), lambda j: (0, j)),
        out_shape=jax.ShapeDtypeStruct((S, C), F32),
        scratch_shapes=[pltpu.VMEM((S + CONV_PAD, cw), F32)],
        compiler_params=_cp(("parallel",)),
    )(y, w, b)


def _conv_bwd_call(y, w, dz, name):
    S, C = y.shape
    cw = 128
    rt = _pick(S, (128,))
    w = jnp.pad(w, ((0, CONV_PAD - CONV_W), (0, 0)))

    def body(y_ref, w_ref, dz_ref, dy_ref, dw_ref, db_ref, ypad_ref, zpad_ref):
        ypad_ref[pl.ds(0, CONV_PAD), :] = jnp.zeros((CONV_PAD, cw), F32)
        ypad_ref[pl.ds(CONV_PAD, S), :] = y_ref[...]
        zpad_ref[pl.ds(0, S), :] = dz_ref[...]
        zpad_ref[pl.ds(S, CONV_PAD), :] = jnp.zeros((CONV_PAD, cw), F32)
        dw_ref[...] = jnp.zeros_like(dw_ref)
        wv = w_ref[...]

        def chunk(ci, dbacc):
            r0 = pl.multiple_of(ci * rt, rt)
            ywin = ypad_ref[pl.ds(r0, rt + CONV_PAD), :]
            zwin = zpad_ref[pl.ds(r0, rt + CONV_PAD), :]
            dzc = zwin[:rt, :]
            acc = jnp.zeros((rt, cw), F32)
            for k in range(CONV_W):
                off_z = (CONV_W - 1) - k
                zs = zwin if off_z == 0 else pltpu.roll(zwin, rt + CONV_PAD - off_z, axis=0)
                acc = acc + wv[k:k + 1, :] * zs[:rt, :]
                off_y = CONV_PAD - (CONV_W - 1) + k
                ys = pltpu.roll(ywin, rt + CONV_PAD - off_y, axis=0)
                dw_ref[k:k + 1, :] += jnp.sum(dzc * ys[:rt, :], axis=0, keepdims=True)
            dy_ref[pl.ds(r0, rt), :] = acc
            return dbacc + jnp.sum(dzc, axis=0, keepdims=True)

        db_ref[...] = lax.fori_loop(0, S // rt, chunk, jnp.zeros((1, cw), F32))

    dy, dw, db = _pcall(
        body, name=name, grid=(C // cw,),
        in_specs=[pl.BlockSpec((S, cw), lambda j: (0, j)), pl.BlockSpec((CONV_PAD, cw), lambda j: (0, j)),
                  pl.BlockSpec((S, cw), lambda j: (0, j))],
        out_specs=[pl.BlockSpec((S, cw), lambda j: (0, j)), pl.BlockSpec((CONV_PAD, cw), lambda j: (0, j)),
                   pl.BlockSpec((1, cw), lambda j: (0, j))],
        out_shape=[jax.ShapeDtypeStruct((S, C), F32), jax.ShapeDtypeStruct((CONV_PAD, C), F32),
                   jax.ShapeDtypeStruct((1, C), F32)],
        scratch_shapes=[pltpu.VMEM((S + CONV_PAD, cw), F32), pltpu.VMEM((S + CONV_PAD, cw), F32)],
        compiler_params=_cp(("parallel",)),
    )(y, w, dz)
    return dy, dw[:CONV_W], db


def _conv_op(name):
    @jax.custom_vjp
    def conv(y, w, b):
        return _conv_fwd_call(y, w, b, name + "_fwd")

    def fwd(y, w, b):
        return conv(y, w, b), (y, w)

    def bwd(res, dz):
        y, w = res
        return _conv_bwd_call(y, w, dz, name + "_bwd")

    conv.defvjp(fwd, bwd)
    return conv


def _attn_tile(S):
    return _pick(S, (512, 256, 128))


def _attn_fwd_call(q, kv, kr, name):
    S = q.shape[0]
    t = _attn_tile(S)

    def body(q_ref, kv_ref, kr_ref, o_ref, lse_ref):
        i = pl.program_id(1)
        qv = q_ref[...]

        def update(jb, carry, diagonal):
            m, l, acc = carry
            off = pl.multiple_of(jb * t, t)
            kc = jnp.concatenate([kv_ref[pl.ds(off, t), pl.ds(0, NOPE)], kr_ref[pl.ds(off, t), :]], axis=-1)
            vv = kv_ref[pl.ds(off, t), pl.ds(NOPE, VDIM)]
            s = lax.dot_general(qv, kc, (((1,), (1,)), ((), ())), preferred_element_type=F32)
            if diagonal:
                row = lax.broadcasted_iota(jnp.int32, (t, t), 0)
                col = lax.broadcasted_iota(jnp.int32, (t, t), 1)
                s = jnp.where(col <= row, s, -jnp.inf)
            mn = jnp.maximum(m, jnp.max(s, axis=-1, keepdims=True))
            p = jnp.exp2((s - mn) * ATTN_SCALE_LOG2E)
            al = jnp.exp2((m - mn) * ATTN_SCALE_LOG2E)
            l = al * l + jnp.sum(p, axis=-1, keepdims=True)
            acc = al * acc + jnp.dot(p.astype(BF16), vv, preferred_element_type=F32)
            return mn, l, acc

        init = (jnp.full((t, 1), -jnp.inf, F32), jnp.zeros((t, 1), F32), jnp.zeros((t, VDIM), F32))
        carry = lax.fori_loop(0, i, lambda jb, cr: update(jb, cr, False), init)
        m, l, acc = update(i, carry, True)
        o_ref[...] = (acc / l).astype(o_ref.dtype)
        lse_ref[...] = m * ATTN_SCALE + jnp.log(l)

    return _pcall(
        body, name=name, grid=(N_HEADS, S // t),
        in_specs=[pl.BlockSpec((t, HEAD_W), lambda h, i: (i, h)), pl.BlockSpec((S, HEAD_W), lambda h, i: (0, h)),
                  pl.BlockSpec((S, 128), lambda h, i: (0, 0))],
        out_specs=[pl.BlockSpec((t, VDIM), lambda h, i: (i, h)), pl.BlockSpec((None, t, 1), lambda h, i: (h, i, 0))],
        out_shape=[jax.ShapeDtypeStruct((S, N_HEADS * VDIM), BF16), jax.ShapeDtypeStruct((N_HEADS, S, 1), F32)],
        compiler_params=_cp(("parallel", "parallel")),
    )(q, kv, kr)


def _attn_dd_call(o, do, name):
    S = o.shape[0]
    t = _attn_tile(S)

    def body(o_ref, do_ref, dd_ref):
        dd_ref[...] = jnp.sum(do_ref[...].astype(F32) * o_ref[...].astype(F32), axis=-1, keepdims=True)

    return _pcall(
        body, name=name, grid=(N_HEADS, S // t),
        in_specs=[pl.BlockSpec((t, VDIM), lambda h, i: (i, h)), pl.BlockSpec((t, VDIM), lambda h, i: (i, h))],
        out_specs=pl.BlockSpec((None, t, 1), lambda h, i: (h, i, 0)),
        out_shape=jax.ShapeDtypeStruct((N_HEADS, S, 1), F32), compiler_params=_cp(("parallel", "parallel")),
    )(o, do)


def _attn_bwd_call(q, kv, kr, do, lse, dd, name):
    S = q.shape[0]
    t = _attn_tile(S)
    nq = S // t

    def body(q_ref, kv_ref, kr_ref, do_ref, lse_ref, dd_ref, dq_ref, dkv_ref, dkr_ref, dq_acc):
        j = pl.program_id(1)

        @pl.when(j == 0)
        def _():
            dq_acc[...] = jnp.zeros_like(dq_acc)

        kc = jnp.concatenate([kv_ref[:, pl.ds(0, NOPE)], kr_ref[...]], axis=-1)
        vv = kv_ref[:, pl.ds(NOPE, VDIM)]

        def update(ib, carry, diagonal):
            dkc, dv = carry
            off = pl.multiple_of(ib * t, t)
            qv = q_ref[pl.ds(off, t), :]
            dov = do_ref[pl.ds(off, t), :]
            st = lax.dot_general(kc, qv, (((1,), (1,)), ((), ())), preferred_element_type=F32)
            if diagonal:
                key = lax.broadcasted_iota(jnp.int32, (t, t), 0)
                qry = lax.broadcasted_iota(jnp.int32, (t, t), 1)
                st = jnp.where(key <= qry, st, -jnp.inf)
            pt = jnp.exp2(st * ATTN_SCALE_LOG2E - lse_ref[pl.ds(ib, 1), :])
            dv = dv + jnp.dot(pt.astype(BF16), dov, preferred_element_type=F32)
            dpt = lax.dot_general(vv, dov, (((1,), (1,)), ((), ())), preferred_element_type=F32)
            dst = (pt * (dpt - dd_ref[pl.ds(ib, 1), :]) * ATTN_SCALE).astype(BF16)
            dkc = dkc + jnp.dot(dst, qv, preferred_element_type=F32)
            dq_acc[pl.ds(off, t), :] += lax.dot_general(dst, kc, (((0,), (0,)), ((), ())), preferred_element_type=F32)
            return dkc, dv

        carry = update(j, (jnp.zeros((t, HEAD_W), F32), jnp.zeros((t, VDIM), F32)), True)
        dkc, dv = lax.fori_loop(j + 1, nq, lambda ib, cr: update(ib, cr, False), carry)
        dkv_ref[:, pl.ds(0, NOPE)] = dkc[:, :NOPE].astype(dkv_ref.dtype)
        dkv_ref[:, pl.ds(NOPE, VDIM)] = dv.astype(dkv_ref.dtype)
        dkr_ref[...] = dkc[:, NOPE:]

        @pl.when(j == nq - 1)
        def _():
            dq_ref[...] = dq_acc[...].astype(dq_ref.dtype)

    dq, dkv, dkr_heads = _pcall(
        body, name=name, grid=(N_HEADS, nq),
        in_specs=[pl.BlockSpec((S, HEAD_W), lambda h, j: (0, h)), pl.BlockSpec((t, HEAD_W), lambda h, j: (j, h)),
                  pl.BlockSpec((t, 128), lambda h, j: (j, 0)), pl.BlockSpec((S, VDIM), lambda h, j: (0, h)),
                  pl.BlockSpec((None, nq, t), lambda h, j: (h, 0, 0)), pl.BlockSpec((None, nq, t), lambda h, j: (h, 0, 0))],
        out_specs=[pl.BlockSpec((S, HEAD_W), lambda h, j: (0, h)), pl.BlockSpec((t, HEAD_W), lambda h, j: (j, h)),
                   pl.BlockSpec((None, t, 128), lambda h, j: (h, j, 0))],
        out_shape=[jax.ShapeDtypeStruct(q.shape, q.dtype), jax.ShapeDtypeStruct(kv.shape, kv.dtype),
                   jax.ShapeDtypeStruct((N_HEADS, S, 128), F32)],
        scratch_shapes=[pltpu.VMEM((S, HEAD_W), F32)],
        compiler_params=_cp(("parallel", "arbitrary")),
    )(q, kv, kr, do, (lse * LOG2E).reshape(N_HEADS, nq, t), dd.reshape(N_HEADS, nq, t))

    def sum_body(p_ref, o_ref):
        acc = p_ref[0]
        for h in range(1, N_HEADS):
            acc = acc + p_ref[h]
        o_ref[...] = acc.astype(o_ref.dtype)

    dkr = _pcall(
        sum_body, name=name + "_rope_sum", grid=(nq,),
        in_specs=[pl.BlockSpec((N_HEADS, t, 128), lambda i: (0, i, 0))], out_specs=pl.BlockSpec((t, 128), lambda i: (i, 0)),
        out_shape=jax.ShapeDtypeStruct((S, 128), kr.dtype), compiler_params=_cp(("parallel",)),
    )(dkr_heads)
    return dq, dkv, dkr


def _attn_op(name):
    @jax.custom_vjp
    def attn(q, kv, kr):
        return _attn_fwd_call(q, kv, kr, name + "_fwd")[0]

    def fwd(q, kv, kr):
        o, lse = _attn_fwd_call(q, kv, kr, name + "_fwd")
        return o, (q, kv, kr, o, lse)

    def bwd(res, do):
        q, kv, kr, o, lse = res
        dd = _attn_dd_call(o, do, name + "_dd")
        return _attn_bwd_call(q, kv, kr, do, lse, dd, name + "_bwd")

    attn.defvjp(fwd, bwd)
    return attn


def _loss_call(y, target):
    S, D = y.shape
    t = _pick(S, (256, 128))

    def body(y_ref, t_ref, dy_ref, loss_ref):
        @pl.when(pl.program_id(0) == 0)
        def _():
            loss_ref[...] = jnp.zeros_like(loss_ref)

        e = y_ref[...] - t_ref[...]
        dy_ref[...] = e * (1.0 / D)
        loss_ref[...] += 0.5 * jnp.sum(jnp.mean(e * e, axis=-1, keepdims=True), axis=0, keepdims=True)

    return _pcall(
        body, name="loss_head", grid=(S // t,),
        in_specs=[pl.BlockSpec((t, D), lambda i: (i, 0)), pl.BlockSpec((t, D), lambda i: (i, 0))],
        out_specs=[pl.BlockSpec((t, D), lambda i: (i, 0)), pl.BlockSpec((1, 1), lambda i: (0, 0))],
        out_shape=[jax.ShapeDtypeStruct((S, D), F32), jax.ShapeDtypeStruct((1, 1), F32)],
        compiler_params=_cp(("arbitrary",)),
    )(y, target)


def _adamw_call(w, g, m, v, name):
    shape = w.shape
    C = shape[-1]
    R = int(np.prod(shape[:-1]))
    tr = R
    for cand in (512, 256, 128, 64, 32, 16, 8):
        if R % cand == 0 and cand * C * 4 <= 2 * 1024 * 1024:
            tr = cand
            break
    c1 = 1.0 - ADAM_B1 ** ADAM_STEP
    c2 = 1.0 - ADAM_B2 ** ADAM_STEP

    def body(w_ref, g_ref, m_ref, v_ref, d_ref, mo_ref, vo_ref):
        gv = g_ref[...]
        mn = ADAM_B1 * m_ref[...] + (1.0 - ADAM_B1) * gv
        vn = ADAM_B2 * v_ref[...] + (1.0 - ADAM_B2) * (gv * gv)
        d_ref[...] = -ADAM_LR * ((mn / c1) / (jnp.sqrt(vn / c2) + ADAM_EPS) + ADAM_WD * w_ref[...])
        mo_ref[...] = mn
        vo_ref[...] = vn

    spec = pl.BlockSpec((tr, C), lambda i: (i, 0))
    outs = _pcall(
        body, name=name, grid=(R // tr,), in_specs=[spec] * 4, out_specs=[spec] * 3,
        out_shape=[jax.ShapeDtypeStruct((R, C), F32)] * 3, compiler_params=_cp(("parallel",)),
    )(*[a.reshape(R, C) for a in (w, g, m, v)])
    return [o.reshape(shape) for o in outs]


def _gather_small(x2d, *, reduce, name):
    R, C = x2d.shape

    def body(x_ref, out_ref, *scratch):
        if reduce:
            buf_ref, send_sems, recv_sems, local_sem = scratch
        else:
            buf_ref = out_ref
            send_sems, recv_sems, local_sem = scratch
        x, y, c = lax.axis_index("x"), lax.axis_index("y"), lax.axis_index("c")
        me, sibling = (x, y, c), (x, y, 1 - c)
        chips = [(1 - x, y), (x, 1 - y), (1 - x, 1 - y)]

        def rows(px, py, pc):
            return buf_ref.at[pl.ds((4 * px + 2 * py + pc) * R, R), :]

        def copy(k, block, to, src=None):
            return pltpu.make_async_remote_copy(
                src_ref=rows(*block) if src is None else src, dst_ref=rows(*block),
                send_sem=send_sems.at[k], recv_sem=recv_sems.at[k], device_id=to, device_id_type=MESH)

        mine = pltpu.make_async_copy(x_ref, rows(*me), local_sem)
        mine.start()
        first = [copy(0, me, sibling, src=x_ref)]
        first += [copy(1 + j, me, (*chip, c), src=x_ref) for j, chip in enumerate(chips)]
        for cp in first:
            cp.start()
        passed = [copy(4 + j, (*chip, c), sibling) for j, chip in enumerate(chips)]
        for j, chip in enumerate(chips):
            copy(1 + j, (*chip, c), me).wait_recv()
            passed[j].start()
        copy(0, sibling, me).wait_recv()
        for j, chip in enumerate(chips):
            copy(4 + j, (*chip, 1 - c), me).wait_recv()
        for cp in first + passed:
            cp.wait_send()
        mine.wait()
        if reduce:
            acc = buf_ref[pl.ds(0, R), :]
            for d in range(1, N_DEV):
                acc = acc + buf_ref[pl.ds(d * R, R), :]
            out_ref[...] = acc

    scratch = [pltpu.SemaphoreType.DMA((7,)), pltpu.SemaphoreType.DMA((7,)), pltpu.SemaphoreType.DMA]
    if reduce:
        scratch = [pltpu.VMEM((N_DEV * R, C), F32)] + scratch
    return _pcall(
        body, name=name, out_shape=jax.ShapeDtypeStruct((R if reduce else N_DEV * R, C), F32),
        in_specs=[pl.BlockSpec(memory_space=pltpu.VMEM)], out_specs=pl.BlockSpec(memory_space=pltpu.VMEM),
        scratch_shapes=scratch, compiler_params=pltpu.CompilerParams(vmem_limit_bytes=VMEM_LIMIT),
    )(x2d)


def _pack(arrs):
    flat = jnp.concatenate([a.reshape(-1).astype(F32) for a in arrs])
    n = flat.shape[0]
    unit = 8 * SMALL_COLS
    flat = jnp.pad(flat, (0, (-n) % unit))
    return flat.reshape(-1, SMALL_COLS)


def _unpack(flat, shapes):
    out, o = [], 0
    for s in shapes:
        n = int(np.prod(s))
        out.append(flat[o:o + n].reshape(s))
        o += n
    return out


def _all_gather_small(arrs, name):
    p = _pack(arrs)
    g = _gather_small(p, reduce=False, name=name).reshape(N_DEV, -1)
    out, o = [], 0
    for a in arrs:
        n = int(np.prod(a.shape))
        out.append(g[:, o:o + n].reshape((N_DEV,) + a.shape))
        o += n
    return out


def _all_reduce_small(arrs, name):
    p = _pack(arrs)
    return _unpack(_gather_small(p, reduce=True, name=name).reshape(-1), [a.shape for a in arrs])


def _half_rows(shard_shape):
    return shard_shape[0] // 2


def _half_slot(ref, kind, rh, cols, half, slot):
    if kind == "col":
        return ref.at[pl.ds(half * rh, rh), pl.ds(slot * cols, cols)]
    return ref.at[pl.ds((slot * 2 + half) * rh, rh), :]


def _full_shape(kind, shard_shape):
    r, c = shard_shape
    return (r, c * N_CHIPS) if kind == "col" else (r * N_CHIPS, c)


def _mesh_pos():
    x, y, c = lax.axis_index("x"), lax.axis_index("y"), lax.axis_index("c")
    return x, y, c


def _handshake_chip_peers(x, y, c):
    barrier = pltpu.get_barrier_semaphore()
    for peer in [(x, y, 1 - c)] + [(x ^ fx, y ^ fy, c) for fx, fy in CHIP_RELS]:
        pl.semaphore_signal(barrier, inc=1, device_id=peer, device_id_type=MESH)
    pl.semaphore_wait(barrier, 1 + len(CHIP_RELS))


def _gather_big(shards, kinds, name):
    n = len(shards)

    def body(*refs):
        srcs, outs = refs[:n], refs[n:2 * n]
        ici_send, ici_recv, d2d_send, d2d_recv, own_send, own_recv = refs[2 * n:]
        x, y, c = _mesh_pos()
        _handshake_chip_peers(x, y, c)
        my = 2 * x + y
        geo = [(_half_rows(s.shape), s.shape[1]) for s in shards]

        def region(a, half, slot):
            return _half_slot(outs[a], kinds[a], geo[a][0], geo[a][1], half, slot)

        def slot_of(a, slot):
            rh, cols = geo[a]
            if kinds[a] == "col":
                return outs[a].at[:, pl.ds(slot * cols, cols)]
            return outs[a].at[pl.ds(slot * 2 * rh, 2 * rh), :]

        sends = []
        for a in range(n):
            cp = pltpu.make_async_remote_copy(
                src_ref=srcs[a], dst_ref=slot_of(a, my), send_sem=own_send.at[a], recv_sem=own_recv.at[a],
                device_id=(x, y, 1 - c), device_id_type=MESH)
            cp.start()
            sends.append(cp)
        for a in range(n):
            rh = geo[a][0]
            for r, (fx, fy) in enumerate(CHIP_RELS):
                cp = pltpu.make_async_remote_copy(
                    src_ref=srcs[a].at[pl.ds(c * rh, rh), :], dst_ref=region(a, c, my),
                    send_sem=ici_send.at[3 * a + r], recv_sem=ici_recv.at[3 * a + r],
                    device_id=(x ^ fx, y ^ fy, c), device_id_type=MESH)
                cp.start()
                sends.append(cp)
        passed = []
        for a in range(n):
            for r, (fx, fy) in enumerate(CHIP_RELS):
                frm = 2 * (x ^ fx) + (y ^ fy)
                landed = region(a, c, frm)
                pltpu.make_async_remote_copy(
                    src_ref=landed, dst_ref=landed, send_sem=ici_send.at[3 * a + r], recv_sem=ici_recv.at[3 * a + r],
                    device_id=(x, y, c), device_id_type=MESH).wait_recv()
                cp = pltpu.make_async_remote_copy(
                    src_ref=landed, dst_ref=landed, send_sem=d2d_send.at[3 * a + r], recv_sem=d2d_recv.at[3 * a + r],
                    device_id=(x, y, 1 - c), device_id_type=MESH)
                cp.start()
                passed.append(cp)
        for a in range(n):
            for r, (fx, fy) in enumerate(CHIP_RELS):
                frm = 2 * (x ^ fx) + (y ^ fy)
                other = region(a, 1 - c, frm)
                pltpu.make_async_remote_copy(
                    src_ref=other, dst_ref=other, send_sem=d2d_send.at[3 * a + r], recv_sem=d2d_recv.at[3 * a + r],
                    device_id=(x, y, c), device_id_type=MESH).wait_recv()
        for a in range(n):
            pltpu.make_async_remote_copy(
                src_ref=srcs[a], dst_ref=slot_of(a, my), send_sem=own_send.at[a], recv_sem=own_recv.at[a],
                device_id=(x, y, c), device_id_type=MESH).wait_recv()
        for cp in sends + passed:
            cp.wait_send()

    return pl.kernel(
        body, name=name,
        out_type=[jax.ShapeDtypeStruct(_full_shape(k, s.shape), s.dtype) for s, k in zip(shards, kinds)],
        mesh=plsc.ScalarSubcoreMesh(axis_name="sequencer", num_cores=1),
        scratch_types=[pltpu.SemaphoreType.DMA((3 * n,))] * 4 + [pltpu.SemaphoreType.DMA((n,))] * 2,
        compiler_params=pltpu.CompilerParams(collective_id=GATHER_COLLECTIVE_ID),
    )(*shards)


def _swap_halves(fulls, kinds, shard_shapes, name):
    n = len(fulls)
    geo = [(_half_rows(s), s[1]) for s in shard_shapes]

    def body(*refs):
        srcs, outs = refs[:n], refs[n:2 * n]
        send_sems, recv_sems = refs[2 * n:]
        x, y, c = _mesh_pos()
        barrier = pltpu.get_barrier_semaphore()
        pl.semaphore_signal(barrier, inc=1, device_id=(x, y, 1 - c), device_id_type=MESH)
        pl.semaphore_wait(barrier, 1)
        cps = []
        for a in range(n):
            for s in range(N_CHIPS):
                cp = pltpu.make_async_remote_copy(
                    src_ref=_half_slot(srcs[a], kinds[a], geo[a][0], geo[a][1], 1 - c, s), dst_ref=outs[a].at[s],
                    send_sem=send_sems.at[N_CHIPS * a + s], recv_sem=recv_sems.at[N_CHIPS * a + s],
                    device_id=(x, y, 1 - c), device_id_type=MESH)
                cp.start()
                cps.append(cp)
        for cp in cps:
            cp.wait()

    return pl.kernel(
        body, name=name,
        out_type=[jax.ShapeDtypeStruct((N_CHIPS,) + g, f.dtype) for g, f in zip(geo, fulls)],
        mesh=plsc.ScalarSubcoreMesh(axis_name="sequencer", num_cores=1),
        scratch_types=[pltpu.SemaphoreType.DMA((N_CHIPS * n,))] * 2,
        compiler_params=pltpu.CompilerParams(collective_id=SWAP_COLLECTIVE_ID),
    )(*fulls)


def _add_own_half(full, recv, kind, shard_shape, cidx, name):
    rh, cols = _half_rows(shard_shape), shard_shape[1]
    tr = _pick(rh, (256, 128, 64, 32, 16))
    nb = rh // tr

    def body(c_ref, f_ref, r_ref, o_ref):
        o_ref[...] = (f_ref[...].astype(F32) + r_ref[...].astype(F32)).astype(o_ref.dtype)

    if kind == "col":
        f_spec = pl.BlockSpec((tr, cols), lambda s, i, c_ref: (c_ref[0] * nb + i, s))
    else:
        f_spec = pl.BlockSpec((tr, cols), lambda s, i, c_ref: ((s * 2 + c_ref[0]) * nb + i, 0))
    blk = pl.BlockSpec((None, tr, cols), lambda s, i, c_ref: (s, i, 0))
    return _pcall(
        body, name=name,
        grid_spec=pltpu.PrefetchScalarGridSpec(num_scalar_prefetch=1, grid=(N_CHIPS, nb), in_specs=[f_spec, blk], out_specs=blk),
        out_shape=jax.ShapeDtypeStruct((N_CHIPS, rh, cols), BF16),
        compiler_params=_cp(("arbitrary", "arbitrary")),
    )(cidx, full, recv)


def _scatter_partials(partials, name):
    n = len(partials)

    def body(*refs):
        srcs, outs = refs[:n], refs[n:2 * n]
        send_sems, recv_sems = refs[2 * n:]
        x, y, c = _mesh_pos()
        barrier = pltpu.get_barrier_semaphore()
        for fx, fy in CHIP_RELS:
            pl.semaphore_signal(barrier, inc=1, device_id=(x ^ fx, y ^ fy, c), device_id_type=MESH)
        pl.semaphore_wait(barrier, len(CHIP_RELS))
        my = 2 * x + y
        cps = []
        for a in range(n):
            for r, (fx, fy) in enumerate(CHIP_RELS):
                to = 2 * (x ^ fx) + (y ^ fy)
                cp = pltpu.make_async_remote_copy(
                    src_ref=srcs[a].at[to], dst_ref=outs[a].at[my],
                    send_sem=send_sems.at[3 * a + r], recv_sem=recv_sems.at[3 * a + r],
                    device_id=(x ^ fx, y ^ fy, c), device_id_type=MESH)
                cp.start()
                cps.append(cp)
        for a in range(n):
            for r, (fx, fy) in enumerate(CHIP_RELS):
                frm = 2 * (x ^ fx) + (y ^ fy)
                pltpu.make_async_remote_copy(
                    src_ref=outs[a].at[frm], dst_ref=outs[a].at[frm],
                    send_sem=send_sems.at[3 * a + r], recv_sem=recv_sems.at[3 * a + r],
                    device_id=(x, y, c), device_id_type=MESH).wait_recv()
        for cp in cps:
            cp.wait_send()

    return pl.kernel(
        body, name=name,
        out_type=[jax.ShapeDtypeStruct(p.shape, p.dtype) for p in partials],
        mesh=plsc.ScalarSubcoreMesh(axis_name="sequencer", num_cores=1),
        scratch_types=[pltpu.SemaphoreType.DMA((3 * n,))] * 2,
        compiler_params=pltpu.CompilerParams(collective_id=SCATTER_COLLECTIVE_ID),
    )(*partials)


def _sum_chips_into(landed, partial, buf, layer, n_layers, pos, name):
    _, rh, cols = landed.shape
    tr = _pick(rh, (256, 128, 64, 32, 16))
    nb = rh // tr

    def body(pos_ref, l_ref, own_ref, *rest):
        o_ref = rest[-1]
        my = pos_ref[1]
        own = own_ref[...].astype(F32)
        acc = jnp.where(my == 0, own, l_ref[0].astype(F32))
        for j in range(1, N_CHIPS):
            acc = acc + jnp.where(my == j, own, l_ref[j].astype(F32))
        o_ref[...] = acc

    in_specs = [pl.BlockSpec((N_CHIPS, tr, cols), lambda i, pos_ref: (0, i, 0)),
                pl.BlockSpec((None, tr, cols), lambda i, pos_ref: (pos_ref[1], i, 0))]
    args = [pos, landed, partial]
    aliases = {}
    if buf is not None:
        in_specs.append(pl.BlockSpec(memory_space=pl.ANY))
        args.append(buf)
        aliases = {3: 0}
    return _pcall(
        body, name=name,
        grid_spec=pltpu.PrefetchScalarGridSpec(
            num_scalar_prefetch=1, grid=(nb,), in_specs=in_specs,
            out_specs=pl.BlockSpec((None, tr, cols), lambda i, pos_ref: (layer, pos_ref[0] * nb + i, 0))),
        out_shape=jax.ShapeDtypeStruct((n_layers, 2 * rh, cols), F32), input_output_aliases=aliases,
        compiler_params=_cp(("arbitrary",)),
    )(*args)


def _exchange_final(bufs):
    n = len(bufs)
    HBM = pl.BlockSpec(memory_space=pl.ANY)
    n_layers = [b.shape[0] for b in bufs]
    base = np.concatenate([[0], np.cumsum(n_layers)])

    def body(*refs):
        outs = refs[n:2 * n]
        send_sems, recv_sems = refs[2 * n:]
        x, y, c = _mesh_pos()
        cps = []
        for w in range(n):
            rh = bufs[w].shape[1] // 2
            for l in range(n_layers[w]):
                k = int(base[w]) + l
                mine = outs[w].at[l, pl.ds(c * rh, rh), :]
                cp = pltpu.make_async_remote_copy(
                    src_ref=mine, dst_ref=mine, send_sem=send_sems.at[k], recv_sem=recv_sems.at[k],
                    device_id=(x, y, 1 - c), device_id_type=MESH)
                cp.start()
                cps.append(cp)
        for w in range(n):
            rh = bufs[w].shape[1] // 2
            for l in range(n_layers[w]):
                k = int(base[w]) + l
                other = outs[w].at[l, pl.ds((1 - c) * rh, rh), :]
                pltpu.make_async_remote_copy(
                    src_ref=other, dst_ref=other, send_sem=send_sems.at[k], recv_sem=recv_sems.at[k],
                    device_id=(x, y, c), device_id_type=MESH).wait_recv()
        for cp in cps:
            cp.wait_send()

    return _pcall(
        body, name="grad_exchange_final", out_shape=[jax.ShapeDtypeStruct(b.shape, b.dtype) for b in bufs],
        in_specs=[HBM] * n, out_specs=[HBM] * n, input_output_aliases={i: i for i in range(n)},
        scratch_shapes=[pltpu.SemaphoreType.DMA((int(base[-1]),))] * 2,
        compiler_params=pltpu.CompilerParams(vmem_limit_bytes=VMEM_LIMIT),
    )(*bufs)


def _swap_halves_last(z):
    h = z.shape[-1] // 2
    return jnp.concatenate([z[..., h:], z[..., :h]], axis=-1)


def _ext_uq(w):
    lead = w.shape[:-1]
    wh = w.reshape(lead + (-1, QK_DIM))
    rope = wh[..., NOPE:]
    return jnp.concatenate([wh, _swap_halves_last(rope)], axis=-1).reshape(lead + (-1,))


def _fold_uq(d):
    lead = d.shape[:-1]
    dh = d.reshape(lead + (-1, HEAD_W))
    rope = dh[..., NOPE:QK_DIM] + _swap_halves_last(dh[..., QK_DIM:])
    return jnp.concatenate([dh[..., :NOPE], rope], axis=-1).reshape(lead + (-1,))


def _ext_win(w):
    base, kr = w[..., :-ROPE], w[..., -ROPE:]
    ks = _swap_halves_last(kr)
    return jnp.concatenate([base, kr, ks, ks, kr], axis=-1)


def _fold_win(d):
    n = d.shape[-1] - 4 * ROPE
    a, b, c2, e = [d[..., n + i * ROPE:n + (i + 1) * ROPE] for i in range(4)]
    return jnp.concatenate([d[..., :n], a + e + _swap_halves_last(b + c2)], axis=-1)


ROW_TILE = 256


def _stage(st, xcur, mix, big, small, tabs):
    D = xcur.shape[1]
    l = st // 2
    mod = [[small["mod"][k][i].reshape(1, 1, D) for i in range(6)] for k in range(l + 1)]
    sh1, sc1, g1, sh2, sc2, _ = mod[l]
    if st % 2 == 0:
        n1 = small["norm1_g"][l].reshape(1, 1, D)
        if l == 0:
            (h,) = _rowop("norm_mod_0", _f_norm_mod, n_x=1, n_nd=0, n_p=3, x_w=[D], nd_w=[], nd_shared=[], p_per_group=[False] * 3,
                          out_w=[D], out_dtypes=[BF16], tile=ROW_TILE)((xcur,), (), (n1, sh1, sc1))
        else:
            op = _rowop(f"resid_norm_mod_a{l}", _f_resid_norm_mod, n_x=2, n_nd=0, n_p=4, x_w=[D, D], nd_w=[], nd_shared=[],
                        p_per_group=[False] * 4, out_w=[D, D], out_dtypes=[F32, BF16], tile=ROW_TILE)
            xcur, h = op((xcur, mix), (), (mod[l - 1][5], n1, sh1, sc1))
        if l % 2 == 0:
            return xcur, _even_mixer(h, big, small, l // 2, ROW_TILE)
        return xcur, _mla_mixer(h, big, small, l // 2, ROW_TILE, *tabs)
    n2 = small["norm2_g"][l].reshape(1, 1, D)
    op = _rowop(f"resid_norm_mod_b{l}", _f_resid_norm_mod, n_x=2, n_nd=0, n_p=4, x_w=[D, D], nd_w=[], nd_shared=[],
                p_per_group=[False] * 4, out_w=[D, D], out_dtypes=[F32, BF16], tile=ROW_TILE)
    xcur, h = op((xcur, mix), (), (g1, n2, sh2, sc2))
    return xcur, _mlp(f"mlp_{l}", l >= 2)(h, big["mlp_w1"], big["mlp_w2"])


def _last_residual(xcur, mix, small):
    D = xcur.shape[1]
    gate = small["mod"][-1][5].reshape(1, 1, D)
    return _rowop("resid_last", _f_resid, n_x=2, n_nd=0, n_p=1, x_w=[D, D], nd_w=[], nd_shared=[], p_per_group=[False],
                  out_w=[D], out_dtypes=[F32], tile=ROW_TILE)((xcur, mix), (), (gate,))[0]


def _even_mixer(h, big, wts, e, tile):
    proj = _linear(f"ab_in_{e}", BF16)(h, big["ab_w_in"])
    da = proj.shape[1] // 4
    u, v, a, g = [proj[:, i * da:(i + 1) * da] for i in range(4)]
    ng = da // GROUP
    sgu = _rowop(f"sgu_{e}", _f_sgu, n_x=2, n_nd=0, n_p=3, x_w=[GROUP, GROUP], nd_w=[], nd_shared=[], p_per_group=[True] * 3,
                 out_w=[GROUP], out_dtypes=[BF16], tile=SGU_CHUNKS_PER_STEP * CHUNK, groups=ng)
    bexp = jnp.broadcast_to(wts["sgu_b"][e][:, :, None], (ng, CHUNK, GROUP))
    (out_a,) = sgu((u, v), (), (wts["sgu_norm_g"][e].reshape(ng, 1, GROUP), wts["sgu_w"][e], bexp))
    (yglu,) = _rowop(f"glu_{e}", _f_glu, n_x=2, n_nd=0, n_p=0, x_w=[da, da], nd_w=[], nd_shared=[], p_per_group=[],
                     out_w=[da], out_dtypes=[F32], tile=tile)((a, g), (), ())
    z = _conv_op(f"conv_{e}")(yglu, wts["conv_w"][e], wts["conv_b"][e].reshape(1, da))
    (out_b,) = _rowop(f"ln_silu_{e}", _f_ln_silu, n_x=1, n_nd=0, n_p=2, x_w=[da], nd_w=[], nd_shared=[], p_per_group=[False] * 2,
                      out_w=[da], out_dtypes=[BF16], tile=tile)(
        (z,), (), (wts["conv_ln_g"][e].reshape(1, 1, da), wts["conv_ln_b"][e].reshape(1, 1, da)))
    return _linear(f"ab_out_{e}")(jnp.concatenate([out_a, out_b], axis=-1), big["ab_w_out"])


def _mla_mixer(h, big, wts, o, tile, tab_q, tab_ka, tab_kb):
    proj = _linear(f"mla_in_{o}", BF16)(h, big["mla_w_in"])
    rank = (proj.shape[1] - 4 * ROPE) // 2
    c_q, c_kv = proj[:, :rank], proj[:, rank:2 * rank]
    kr_a, kr_b = proj[:, 2 * rank:2 * rank + 2 * ROPE], proj[:, 2 * rank + 2 * ROPE:]

    def rms(name, xx, gg):
        return _rowop(name, _f_rms, n_x=1, n_nd=0, n_p=1, x_w=[rank], nd_w=[], nd_shared=[], p_per_group=[False],
                      out_w=[rank], out_dtypes=[BF16], tile=tile)((xx,), (), (gg.reshape(1, 1, rank),))[0]

    q_raw = _linear(f"mla_uq_{o}", BF16)(rms(f"rms_q_{o}", c_q, wts["mla_q_norm_g"][o]), big["mla_w_uq"])
    kv_raw = _linear(f"mla_ukv_{o}", BF16)(rms(f"rms_kv_{o}", c_kv, wts["mla_kv_norm_g"][o]), big["mla_w_ukv"])
    gq, gk = wts["mla_q_head_g"][o], wts["mla_k_head_g"][o]
    gk_rope = gk[NOPE:]
    gq_ext = jnp.concatenate([gq, _swap_halves_last(gq[NOPE:])]).reshape(1, 1, HEAD_W)
    gk_ext = jnp.concatenate([gk[:NOPE], jnp.ones((HEAD_W - NOPE,), F32)]).reshape(1, 1, HEAD_W)
    gk_a = jnp.concatenate([gk_rope, _swap_halves_last(gk_rope)]).reshape(1, 1, 2 * ROPE)
    gk_b = jnp.concatenate([_swap_halves_last(gk_rope), gk_rope]).reshape(1, 1, 2 * ROPE)
    head_tile = 4 * tile
    (q,) = _rowop(f"q_head_{o}", _f_qhead, n_x=1, n_nd=1, n_p=1, x_w=[HEAD_W], nd_w=[HEAD_W], nd_shared=[True], p_per_group=[False],
                  out_w=[HEAD_W], out_dtypes=[BF16], tile=head_tile, groups=N_HEADS)((q_raw,), (tab_q,), (gq_ext,))
    (kv,) = _rowop(f"kv_head_{o}", _f_kvhead, n_x=1, n_nd=0, n_p=1, x_w=[HEAD_W], nd_w=[], nd_shared=[], p_per_group=[False],
                   out_w=[HEAD_W], out_dtypes=[BF16], tile=head_tile, groups=N_HEADS)((kv_raw,), (), (gk_ext,))
    (kr,) = _rowop(f"k_rope_{o}", _f_krope, n_x=2, n_nd=2, n_p=2, x_w=[2 * ROPE] * 2, nd_w=[2 * ROPE] * 2, nd_shared=[True] * 2,
                   p_per_group=[False] * 2, out_w=[2 * ROPE], out_dtypes=[BF16], tile=tile)((kr_a, kr_b), (tab_ka, tab_kb), (gk_a, gk_b))
    att = _attn_op(f"attn_{o}")(q, kv, kr)
    return _linear(f"mla_out_{o}")(att, big["mla_w_out"])


def _rope_tabs(S):
    pos = jnp.arange(S, dtype=F32)
    inv = ROPE_THETA ** (-jnp.arange(0, ROPE, 2, dtype=F32) / ROPE)
    ang = pos[:, None] * inv[None, :]
    cos, sin = jnp.cos(ang), jnp.sin(ang)
    cc = jnp.concatenate([cos, cos], axis=-1)
    sg = jnp.concatenate([-sin, sin], axis=-1)
    tab_q = jnp.concatenate([jnp.ones((S, NOPE), F32), cc, sg], axis=-1)
    return tab_q, jnp.concatenate([cc, sg], axis=-1), jnp.concatenate([sg, cc], axis=-1)


def kernel(x, c, norm1_g, norm2_g, ada_w, ada_b, mlp_w1, mlp_w2, ab_w_in, sgu_norm_g, sgu_w, sgu_b, conv_w, conv_b, conv_ln_g, conv_ln_b, ab_w_out, mla_w_in, mla_q_norm_g, mla_kv_norm_g, mla_w_uq, mla_w_ukv, mla_q_head_g, mla_k_head_g, mla_w_out, loss_target, m_norm1_g, m_norm2_g, m_ada_w, m_ada_b, m_mlp_w1, m_mlp_w2, m_ab_w_in, m_sgu_norm_g, m_sgu_w, m_sgu_b, m_conv_w, m_conv_b, m_conv_ln_g, m_conv_ln_b, m_ab_w_out, m_mla_w_in, m_mla_q_norm_g, m_mla_kv_norm_g, m_mla_w_uq, m_mla_w_ukv, m_mla_q_head_g, m_mla_k_head_g, m_mla_w_out, v_norm1_g, v_norm2_g, v_ada_w, v_ada_b, v_mlp_w1, v_mlp_w2, v_ab_w_in, v_sgu_norm_g, v_sgu_w, v_sgu_b, v_conv_w, v_conv_b, v_conv_ln_g, v_conv_ln_b, v_ab_w_out, v_mla_w_in, v_mla_q_norm_g, v_mla_kv_norm_g, v_mla_w_uq, v_mla_w_ukv, v_mla_q_head_g, v_mla_k_head_g, v_mla_w_out):
    names = ["norm1_g", "norm2_g", "ada_w", "ada_b", "mlp_w1", "mlp_w2", "ab_w_in", "sgu_norm_g", "sgu_w", "sgu_b", "conv_w",
             "conv_b", "conv_ln_g", "conv_ln_b", "ab_w_out", "mla_w_in", "mla_q_norm_g", "mla_kv_norm_g", "mla_w_uq", "mla_w_ukv",
             "mla_q_head_g", "mla_k_head_g", "mla_w_out"]
    W = dict(zip(names, [norm1_g, norm2_g, ada_w, ada_b, mlp_w1, mlp_w2, ab_w_in, sgu_norm_g, sgu_w, sgu_b, conv_w, conv_b, conv_ln_g,
                         conv_ln_b, ab_w_out, mla_w_in, mla_q_norm_g, mla_kv_norm_g, mla_w_uq, mla_w_ukv, mla_q_head_g, mla_k_head_g,
                         mla_w_out]))
    M = dict(zip(names, [m_norm1_g, m_norm2_g, m_ada_w, m_ada_b, m_mlp_w1, m_mlp_w2, m_ab_w_in, m_sgu_norm_g, m_sgu_w, m_sgu_b, m_conv_w,
                         m_conv_b, m_conv_ln_g, m_conv_ln_b, m_ab_w_out, m_mla_w_in, m_mla_q_norm_g, m_mla_kv_norm_g, m_mla_w_uq,
                         m_mla_w_ukv, m_mla_q_head_g, m_mla_k_head_g, m_mla_w_out]))
    V = dict(zip(names, [v_norm1_g, v_norm2_g, v_ada_w, v_ada_b, v_mlp_w1, v_mlp_w2, v_ab_w_in, v_sgu_norm_g, v_sgu_w, v_sgu_b, v_conv_w,
                         v_conv_b, v_conv_ln_g, v_conv_ln_b, v_ab_w_out, v_mla_w_in, v_mla_q_norm_g, v_mla_kv_norm_g, v_mla_w_uq,
                         v_mla_w_ukv, v_mla_q_head_g, v_mla_k_head_g, v_mla_w_out]))
    xi, yi, ci = lax.axis_index("x"), lax.axis_index("y"), lax.axis_index("c")
    chip = 2 * xi + yi
    dev = 2 * chip + ci
    pos = jnp.stack([ci, chip]).astype(jnp.int32)
    S, D = x.shape[1], x.shape[2]
    depth = norm1_g.shape[0]

    c_all, conv_w_all, qn_all, kvn_all = _all_gather_small([c, conv_w, mla_q_norm_g, mla_kv_norm_g], "gather_small_inputs")
    c_all = c_all.reshape(N_DEV, D)
    by_chip = lambda a: jnp.concatenate([a[2 * j] for j in range(N_CHIPS)], axis=-1)
    conv_w_full, qn_full, kvn_full = by_chip(conv_w_all), by_chip(qn_all), by_chip(kvn_all)

    (c_act,) = _rowop("silu_c", _f_silu, n_x=1, n_nd=0, n_p=0, x_w=[D], nd_w=[], nd_shared=[], p_per_group=[], out_w=[D],
                      out_dtypes=[F32], tile=N_DEV)((c_all,), (), ())
    c_act_pad = jnp.pad(c_act, ((0, 128 - N_DEV), (0, 0)))
    mod_cols = jnp.stack([_matmul(c_act_pad, ada_w[l], name=f"ada_fwd_{l}")[:N_DEV] for l in range(depth)])
    (mod_all,) = _all_gather_small([mod_cols], "gather_mod")
    mod_mine = jnp.concatenate([lax.dynamic_index_in_dim(mod_all[2 * j], dev, axis=1, keepdims=False) for j in range(N_CHIPS)], axis=-1)
    mod_mine = (mod_mine + ada_b).reshape(depth, 6, D)

    big = {"mlp_w1": "col", "mlp_w2": "row", "ab_w_in": "col", "ab_w_out": "row", "mla_w_in": "row", "mla_w_uq": "col",
           "mla_w_ukv": "col", "mla_w_out": "row"}
    src = dict(W)
    src["mla_w_in"] = _ext_win(mla_w_in)
    src["mla_w_uq"] = _ext_uq(mla_w_uq)
    shards, kinds, owner = [], [], []
    for nme, kind in big.items():
        for l in range(src[nme].shape[0]):
            shards.append(src[nme][l].astype(BF16))
            kinds.append(kind)
            owner.append((nme, l))
    def stage_of(nme, l):
        return 2 * l + 1 if nme.startswith("mlp") else (4 * l if nme.startswith("ab") else 4 * l + 2)

    fwd_stages = [[a for a, (nme, l) in enumerate(owner) if stage_of(nme, l) == st] for st in range(2 * depth)]
    fulls = [None] * len(shards)
    for st, idx in enumerate(fwd_stages):
        got = _gather_big([shards[a] for a in idx], [kinds[a] for a in idx], f"gather_weights_{st}")
        for a, f in zip(idx, got):
            fulls[a] = f
    small = dict(norm1_g=norm1_g, norm2_g=norm2_g, mod=mod_mine, sgu_norm_g=sgu_norm_g, sgu_w=sgu_w, sgu_b=sgu_b, conv_w=conv_w_full,
                 conv_b=conv_b, conv_ln_g=conv_ln_g, conv_ln_b=conv_ln_b, mla_q_norm_g=qn_full, mla_kv_norm_g=kvn_full,
                 mla_q_head_g=mla_q_head_g, mla_k_head_g=mla_k_head_g)
    groups, a0 = [], 0
    for nme in big:
        groups.append(list(range(a0, a0 + src[nme].shape[0])))
        a0 += src[nme].shape[0]
    shard_shapes = [s.shape for s in shards]
    where = {a: (wi, l) for wi, idxs in enumerate(groups) for l, a in enumerate(idxs)}

    tabs = _rope_tabs(S)
    n_stages = 2 * depth
    xcur, mix, vjps = x[0], None, []
    for st, idx in enumerate(fwd_stages):
        stage_big = {owner[a][0]: fulls[a] for a in idx}
        if st == 0:
            (xcur, mix), vjp = jax.vjp(lambda xx, bb, ss: _stage(0, xx, None, bb, ss, tabs), xcur, stage_big, small)
        else:
            (xcur, mix), vjp = jax.vjp(lambda xx, mm, bb, ss, st=st: _stage(st, xx, mm, bb, ss, tabs), xcur, mix, stage_big, small)
        vjps.append(vjp)
    y, vjp_last = jax.vjp(_last_residual, xcur, mix, small)
    dy, loss_mine = _loss_call(y, loss_target[0])
    loss = lax.psum(loss_mine[0, 0], ("x", "y", "c"))

    dxc, dmix, dsmall = vjp_last(dy)
    bufs = [None] * len(groups)
    swapped, scattered = {}, {}

    def add_and_scatter(st):
        idx, gr, recv = swapped.pop(st)
        partials = [_add_own_half(gr[a], r, kinds[a], shard_shapes[a], pos, f"grad_add_sibling_{a}") for a, r in zip(idx, recv)]
        scattered[st] = (idx, partials, _scatter_partials(partials, f"grad_scatter_partials_{st}"))

    def sum_chips(st):
        idx, partials, landed = scattered.pop(st)
        for a, part, land in zip(idx, partials, landed):
            wi, l = where[a]
            bufs[wi] = _sum_chips_into(land, part, bufs[wi], l, len(groups[wi]), pos, f"grad_sum_chips_{a}")

    for st in reversed(range(n_stages)):
        idx = fwd_stages[st]
        if st == 0:
            dxc, dbig, ds = vjps[st]((dxc, dmix))
        else:
            dxc, dmix, dbig, ds = vjps[st]((dxc, dmix))
        dsmall = jax.tree.map(jnp.add, dsmall, ds)
        gr = {a: dbig[owner[a][0]] for a in idx}
        swapped[st] = (idx, gr, _swap_halves([gr[a] for a in idx], [kinds[a] for a in idx], [shard_shapes[a] for a in idx],
                                             f"grad_swap_halves_{st}"))
        started = [s for s in ((st + 1, st) if st == 1 else (st + 1,)) if s in swapped]
        for s in started:
            add_and_scatter(s)
        if st + 2 in scattered:
            sum_chips(st + 2)
        if st > 0:
            pinned = {s: scattered[s][1] for s in started}
            (dxc, dmix), pinned, bufs = lax.optimization_barrier(((dxc, dmix), pinned, bufs))
            for s in started:
                scattered[s] = (scattered[s][0], pinned[s], scattered[s][2])
    dx = dxc
    for st in sorted(swapped, reverse=True):
        add_and_scatter(st)
    for st in sorted(scattered, reverse=True):
        sum_chips(st)
    gsh = _exchange_final(bufs)

    dw = dsmall
    dmod = dw["mod"].reshape(depth, 6 * D)
    small_names = ["norm1_g", "norm2_g", "sgu_norm_g", "sgu_w", "sgu_b", "conv_w", "conv_b", "conv_ln_g", "conv_ln_b", "mla_q_norm_g",
                   "mla_kv_norm_g", "mla_q_head_g", "mla_k_head_g"]
    red = _all_reduce_small([dmod] + [dw[nme] for nme in small_names], "reduce_small_grads")
    G = dict(zip(["ada_b"] + small_names, red))
    own_cols = lambda a: lax.dynamic_slice_in_dim(a, chip * (a.shape[-1] // N_CHIPS), a.shape[-1] // N_CHIPS, axis=-1)
    for nme in ("conv_w", "mla_q_norm_g", "mla_kv_norm_g"):
        G[nme] = own_cols(G[nme])

    (dmod_all,) = _all_gather_small([dmod], "gather_dmod")
    dmod_cols = own_cols(dmod_all)
    dmod_pad = jnp.pad(dmod_cols, ((0, 128 - N_DEV), (0, 0), (0, 0)))
    G["ada_w"] = _ada_wgrad_call(c_act_pad, dmod_pad)

    for nme, g in zip(big, gsh):
        G[nme] = g
    G["mla_w_in"] = _fold_win(G["mla_w_in"])
    G["mla_w_uq"] = _fold_uq(G["mla_w_uq"])

    deltas, new_m, new_v = [], [], []
    for nme in names:
        d, mn, vn = _adamw_call(W[nme], G[nme], M[nme], V[nme], f"adamw_{nme}")
        deltas.append(d)
        new_m.append(mn)
        new_v.append(vn)
    return (loss, dx[None], *[G[nme] for nme in names], *deltas, *new_m, *new_v)
```

```python
import functools
import numpy as np
import jax
import jax.numpy as jnp
from jax import lax
from jax.experimental import pallas as pl
from jax.experimental.pallas import tpu as pltpu
from jax.experimental.pallas import tpu_sc as plsc

F32 = jnp.float32
BF16 = jnp.bfloat16
MESH = pl.DeviceIdType.MESH

EPS = 1e-6
N_HEADS = 16
NOPE = 128
ROPE = 64
VDIM = 128
QK_DIM = NOPE + ROPE
HEAD_W = 256
CHUNK = 128
GROUP = 128
SGU_CHUNKS_PER_STEP = 4
CONV_W = 31
CONV_PAD = 32
ROPE_THETA = 10000.0
ATTN_SCALE = QK_DIM ** -0.5
LOG2E = 1.4426950408889634
ATTN_SCALE_LOG2E = ATTN_SCALE * LOG2E
ADAM_LR, ADAM_B1, ADAM_B2, ADAM_EPS, ADAM_WD, ADAM_STEP = 0.001, 0.9, 0.999, 1e-08, 0.01, 10
N_CHIPS = 4
N_DEV = 8
VMEM_LIMIT = 56 * 1024 * 1024
WHOLE_K_LIMIT = 4096
SMALL_COLS = 1024
CHIP_RELS = ((1, 0), (0, 1), (1, 1))
GATHER_COLLECTIVE_ID = 0
SWAP_COLLECTIVE_ID = 1
SCATTER_COLLECTIVE_ID = 2


def _pcall(body, **kw):
    return pl.pallas_call(body, **kw)


def _cp(sem=None, **kw):
    return pltpu.CompilerParams(dimension_semantics=sem, vmem_limit_bytes=VMEM_LIMIT, **kw)


def _pick(n, cands):
    for c in cands:
        if n % c == 0:
            return c
    return n


def _matmul(a, b, *, ta=False, tb=False, out_dtype=F32, name, extra=(), epilogue=None, out_dtypes=None):
    if ta:
        K, M = a.shape
    else:
        M, K = a.shape
    if tb:
        N, K2 = b.shape
    else:
        K2, N = b.shape
    assert K == K2, (a.shape, b.shape, ta, tb)
    tm = _pick(M, (1024, 512, 256, 128))
    tn = N if N <= 1536 else _pick(N, (1024, 512, 256, 128))
    tk = _pick(K, (2048, 1024, 512, 256, 128))
    if ta and K <= WHOLE_K_LIMIT and max(a.dtype.itemsize, b.dtype.itemsize) == 2:
        tk = K
    nk = K // tk
    dn = (((0 if ta else 1,), (1 if tb else 0,)), ((), ()))

    n_extra = len(extra)
    single = epilogue is None
    if single:
        out_dtypes = [out_dtype]

    n_out = len(out_dtypes)

    def body(a_ref, b_ref, *rest):
        extra_refs, o_refs = rest[:n_extra], rest[n_extra:n_extra + n_out]

        def dot():
            return lax.dot_general(a_ref[...].astype(BF16), b_ref[...].astype(BF16), dn, preferred_element_type=F32)

        def finish(acc):
            res = (acc,) if single else epilogue(acc, *[r[...] for r in extra_refs])
            for o_ref, val in zip(o_refs, res):
                o_ref[...] = val.astype(o_ref.dtype)

        if nk == 1:
            finish(dot())
            return
        acc_ref = rest[-1]
        k = pl.program_id(2)

        @pl.when(k == 0)
        def _():
            acc_ref[...] = jnp.zeros_like(acc_ref)

        acc_ref[...] += dot()

        @pl.when(k == nk - 1)
        def _():
            finish(acc_ref[...])

    a_spec = pl.BlockSpec((tk, tm), lambda i, j, k: (k, i)) if ta else pl.BlockSpec((tm, tk), lambda i, j, k: (i, k))
    b_spec = pl.BlockSpec((tn, tk), lambda i, j, k: (j, k)) if tb else pl.BlockSpec((tk, tn), lambda i, j, k: (k, j))
    mn_spec = pl.BlockSpec((tm, tn), lambda i, j, k: (i, j))
    outs = _pcall(
        body, name=name, grid=(M // tm, N // tn, nk), in_specs=[a_spec, b_spec] + [mn_spec] * n_extra,
        out_specs=[mn_spec] * len(out_dtypes),
        out_shape=[jax.ShapeDtypeStruct((M, N), d) for d in out_dtypes],
        scratch_shapes=[pltpu.VMEM((tm, tn), F32)] if nk > 1 else [],
        compiler_params=_cp(("parallel", "parallel", "arbitrary")),
    )(a, b, *extra)
    return outs[0] if single else outs


def _linear(name, out_dtype=F32):
    @jax.custom_vjp
    def mm(a, w):
        return _matmul(a, w, out_dtype=out_dtype, name=name + "_fwd")

    def fwd(a, w):
        return mm(a, w), (a, w)

    def bwd(res, dy):
        a, w = res
        da = _matmul(dy, w, tb=True, out_dtype=a.dtype, name=name + "_dgrad")
        dw = _matmul(a, dy, ta=True, out_dtype=w.dtype, name=name + "_wgrad")
        return da, dw

    mm.defvjp(fwd, bwd)
    return mm


def _ada_wgrad_call(c_act, dmod):
    P, D = c_act.shape
    _, L, C = dmod.shape
    tm = _pick(D, (1024, 512, 256, 128))
    tn = _pick(C, (1024, 512, 256, 128))
    nj = C // tn

    def body(a_ref, b_ref, o_ref):
        o_ref[...] = lax.dot_general(a_ref[...].astype(BF16), b_ref[...].astype(BF16), (((0,), (0,)), ((), ())),
                                     preferred_element_type=F32)

    return _pcall(
        body, name="ada_wgrad", grid=(L, D // tm, nj),
        in_specs=[pl.BlockSpec((P, tm), lambda l, i, j: (0, i)), pl.BlockSpec((P, tn), lambda l, i, j: (0, l * nj + j))],
        out_specs=pl.BlockSpec((None, tm, tn), lambda l, i, j: (l, i, j)),
        out_shape=jax.ShapeDtypeStruct((L, D, C), F32), compiler_params=_cp(("parallel", "parallel", "parallel")),
    )(c_act, dmod.reshape(P, L * C))


def _relu2_epilogue(acc):
    return acc, jnp.square(jnp.maximum(acc, 0.0))


def _relu2_grad_epilogue(acc, h1):
    return (acc * (2.0 * jnp.maximum(h1.astype(F32), 0.0)),)


def _mlp(name):
    def run(h, w1, w2):
        h1, act = _matmul(h, w1, name=name + "_up_fwd", epilogue=_relu2_epilogue, out_dtypes=[BF16, BF16])
        return _matmul(act, w2, name=name + "_down_fwd"), h1, act

    @jax.custom_vjp
    def mlp(h, w1, w2):
        return run(h, w1, w2)[0]

    def fwd(h, w1, w2):
        y, h1, act = run(h, w1, w2)
        return y, (h, w1, w2, h1, act)

    def bwd(res, dy):
        h, w1, w2, h1, act = res
        dw2 = _matmul(act, dy, ta=True, out_dtype=w2.dtype, name=name + "_down_wgrad")
        (dh1,) = _matmul(dy, w2, tb=True, name=name + "_down_dgrad", extra=(h1,), epilogue=_relu2_grad_epilogue, out_dtypes=[BF16])
        dw1 = _matmul(h, dh1, ta=True, out_dtype=w1.dtype, name=name + "_up_wgrad")
        dh = _matmul(dh1, w1, tb=True, out_dtype=h.dtype, name=name + "_up_dgrad")
        return dh, dw1, dw2

    mlp.defvjp(fwd, bwd)
    return mlp


def _rowop(name, f, *, n_x, n_nd, n_p, x_w, nd_w, nd_shared, p_per_group, out_w, out_dtypes, tile, groups=1):
    G = groups

    def specs(S):
        t = min(tile, S)
        xs = [pl.BlockSpec((t, w), lambda g, r: (r, g)) for w in x_w]
        nds = [pl.BlockSpec((t, w), (lambda g, r: (r, 0)) if sh else (lambda g, r: (r, g))) for w, sh in zip(nd_w, nd_shared)]
        outs = [pl.BlockSpec((t, w), lambda g, r: (r, g)) for w in out_w]
        return t, xs, nds, outs

    def pspecs(ps):
        return [pl.BlockSpec((None,) + p.shape[1:], (lambda g, r: (g, 0, 0)) if pg else (lambda g, r: (0, 0, 0)))
                for p, pg in zip(ps, p_per_group)]

    def fwd_call(xs, nds, ps):
        S = xs[0].shape[0]
        t, xsp, ndsp, osp = specs(S)

        def body(*refs):
            ins, outs = refs[:n_x + n_nd + n_p], refs[n_x + n_nd + n_p:]
            vals = [r[...].astype(F32) for r in ins]
            res = f(*vals)
            for o, r in zip(res, outs):
                r[...] = o.astype(r.dtype)

        return _pcall(
            body, name=name + "_fwd", grid=(G, S // t), in_specs=xsp + ndsp + pspecs(ps), out_specs=osp,
            out_shape=[jax.ShapeDtypeStruct((S, G * w), d) for w, d in zip(out_w, out_dtypes)],
            compiler_params=_cp(("parallel", "parallel")),
        )(*xs, *nds, *ps)

    def bwd_call(xs, nds, ps, douts):
        S = xs[0].shape[0]
        t, xsp, ndsp, osp = specs(S)
        n_in = n_x + n_nd + n_p + len(out_w)

        def body(*refs):
            ins, outs = refs[:n_in], refs[n_in:]
            xv = [r[...].astype(F32) for r in ins[:n_x]]
            ndv = [r[...].astype(F32) for r in ins[n_x:n_x + n_nd]]
            pv = [r[...].astype(F32) for r in ins[n_x + n_nd:n_x + n_nd + n_p]]
            dov = tuple(r[...].astype(F32) for r in ins[n_x + n_nd + n_p:])
            _, vjp = jax.vjp(lambda *a: tuple(f(*a[:n_x], *ndv, *a[n_x:])), *xv, *pv)
            cts = vjp(dov)
            for i in range(n_x):
                outs[i][...] = cts[i].astype(outs[i].dtype)
            g, r = pl.program_id(0), pl.program_id(1)
            for i in range(n_p):
                first = (r == 0) if p_per_group[i] else jnp.logical_and(g == 0, r == 0)
                ref, ct = outs[n_x + i], cts[n_x + i]

                @pl.when(first)
                def _(ref=ref, ct=ct):
                    ref[...] = ct

                @pl.when(jnp.logical_not(first))
                def _(ref=ref, ct=ct):
                    ref[...] += ct

        return _pcall(
            body, name=name + "_bwd", grid=(G, S // t), in_specs=xsp + ndsp + pspecs(ps) + osp,
            out_specs=xsp + pspecs(ps),
            out_shape=[jax.ShapeDtypeStruct(x.shape, x.dtype) for x in xs] + [jax.ShapeDtypeStruct(p.shape, F32) for p in ps],
            compiler_params=_cp(("arbitrary", "arbitrary")),
        )(*xs, *nds, *ps, *douts)

    @jax.custom_vjp
    def op(xs, nds, ps):
        return tuple(fwd_call(xs, nds, ps))

    def op_fwd(xs, nds, ps):
        return op(xs, nds, ps), (xs, nds, ps)

    def op_bwd(res, douts):
        xs, nds, ps = res
        out = bwd_call(xs, nds, ps, douts)
        return tuple(out[:n_x]), tuple(jnp.zeros_like(n) for n in nds), tuple(out[n_x:])

    op.defvjp(op_fwd, op_bwd)
    return op


def _rms_rows(x):
    return x * lax.rsqrt(jnp.mean(x * x, axis=-1, keepdims=True) + EPS)


def _f_norm_mod(x, g, shift, scale):
    return ((_rms_rows(x) * g) * (1.0 + scale) + shift,)


def _f_resid_norm_mod(x, mix, gate, g, shift, scale):
    xn = x + gate * mix
    return xn, (_rms_rows(xn) * g) * (1.0 + scale) + shift


def _f_resid(x, mix, gate):
    return (x + gate * mix,)


def _f_rms(x, g):
    return (_rms_rows(x) * g,)


def _f_glu(a, g):
    return (a * jax.nn.sigmoid(g),)


def _f_ln_silu(z, g, b):
    mu = jnp.mean(z, axis=-1, keepdims=True)
    zc = z - mu
    var = jnp.mean(zc * zc, axis=-1, keepdims=True)
    y = zc * lax.rsqrt(var + EPS) * g + b
    return (y * jax.nn.sigmoid(y),)


def _f_silu(x):
    return (x * jax.nn.sigmoid(x),)


@jax.custom_vjp
def _bdot_chunks(a, b):
    return lax.dot_general(a.astype(BF16), b.astype(BF16), (((2,), (1,)), ((0,), (0,))), preferred_element_type=F32)


def _bdot_chunks_fwd(a, b):
    return _bdot_chunks(a, b), (a, b)


def _bdot_chunks_bwd(res, ct):
    a, b = res
    c16 = ct.astype(BF16)
    da = lax.dot_general(c16, b.astype(BF16), (((2,), (2,)), ((0,), (0,))), preferred_element_type=F32)
    db = lax.dot_general(a.astype(BF16), c16, (((1,), (1,)), ((0,), (0,))), preferred_element_type=F32)
    return da, db


_bdot_chunks.defvjp(_bdot_chunks_fwd, _bdot_chunks_bwd)


def _f_sgu(u, v, ng, w, bexp):
    n = u.shape[0] // CHUNK
    vn = (_rms_rows(jax.nn.gelu(v)) * ng).reshape(n, CHUNK, GROUP)
    row = lax.broadcasted_iota(jnp.int32, w.shape, 0)
    col = lax.broadcasted_iota(jnp.int32, w.shape, 1)
    wm = jnp.broadcast_to(jnp.where(row >= col, w, 0.0)[None], (n, CHUNK, CHUNK))
    mixed = _bdot_chunks(wm, vn) + bexp[None]
    return (jax.nn.gelu(u) * mixed.reshape(n * CHUNK, GROUP),)


def _lo_mask():
    return lax.broadcasted_iota(jnp.int32, (1, HEAD_W), 1) < NOPE


def _f_qhead(x, tab, g):
    lo = _lo_mask()
    x2 = x * x
    ms_lo = jnp.sum(jnp.where(lo, x2, 0.0), axis=-1, keepdims=True) * (1.0 / NOPE)
    ms_hi = jnp.sum(jnp.where(lo, 0.0, x2), axis=-1, keepdims=True) * (1.0 / (HEAD_W - NOPE))
    r = jnp.where(lo, lax.rsqrt(ms_lo + EPS), lax.rsqrt(ms_hi + EPS))
    return (((x * r) * g) * tab,)


def _f_kvhead(x, g):
    lo = _lo_mask()
    ms = jnp.sum(jnp.where(lo, x * x, 0.0), axis=-1, keepdims=True) * (1.0 / NOPE)
    return (jnp.where(lo, (x * lax.rsqrt(ms + EPS)) * g, x),)


def _f_krope(a, b, ta, tb, ga, gb):
    r = lax.rsqrt(jnp.mean(a * a, axis=-1, keepdims=True) + EPS)
    return (((a * r) * ga) * ta + ((b * r) * gb) * tb,)


def _conv_fwd_call(y, w, b, name):
    S, C = y.shape
    cw = 128
    rt = _pick(S, (128,))
    w = jnp.pad(w, ((0, CONV_PAD - CONV_W), (0, 0)))

    def body(y_ref, w_ref, b_ref, z_ref, pad_ref):
        pad_ref[pl.ds(0, CONV_PAD), :] = jnp.zeros((CONV_PAD, cw), F32)
        pad_ref[pl.ds(CONV_PAD, S), :] = y_ref[...]
        wv = w_ref[...]
        bv = b_ref[...]

        def chunk(ci, carry):
            r0 = pl.multiple_of(ci * rt, rt)
            win = pad_ref[pl.ds(r0, rt + CONV_PAD), :]
            acc = jnp.broadcast_to(bv, (rt, cw))
            for k in range(CONV_W):
                off = CONV_PAD - (CONV_W - 1) + k
                sh = win if off == 0 else pltpu.roll(win, rt + CONV_PAD - off, axis=0)
                acc = acc + wv[k:k + 1, :] * sh[:rt, :]
            z_ref[pl.ds(r0, rt), :] = acc
            return carry

        lax.fori_loop(0, S // rt, chunk, 0)

    return _pcall(
        body, name=name, grid=(C // cw,),
        in_specs=[pl.BlockSpec((S, cw), lambda j: (0, j)), pl.BlockSpec((CONV_PAD, cw), lambda j: (0, j)),
                  pl.BlockSpec((1, cw), lambda j: (0, j))],
        out_specs=pl.BlockSpec((S, cw), lambda j: (0, j)),
        out_shape=jax.ShapeDtypeStruct((S, C), F32),
        scratch_shapes=[pltpu.VMEM((S + CONV_PAD, cw), F32)],
        compiler_params=_cp(("parallel",)),
    )(y, w, b)


def _conv_bwd_call(y, w, dz, name):
    S, C = y.shape
    cw = 128
    rt = _pick(S, (128,))
    w = jnp.pad(w, ((0, CONV_PAD - CONV_W), (0, 0)))

    def body(y_ref, w_ref, dz_ref, dy_ref, dw_ref, db_ref, ypad_ref, zpad_ref):
        ypad_ref[pl.ds(0, CONV_PAD), :] = jnp.zeros((CONV_PAD, cw), F32)
        ypad_ref[pl.ds(CONV_PAD, S), :] = y_ref[...]
        zpad_ref[pl.ds(0, S), :] = dz_ref[...]
        zpad_ref[pl.ds(S, CONV_PAD), :] = jnp.zeros((CONV_PAD, cw), F32)
        dw_ref[...] = jnp.zeros_like(dw_ref)
        wv = w_ref[...]

        def chunk(ci, dbacc):
            r0 = pl.multiple_of(ci * rt, rt)
            ywin = ypad_ref[pl.ds(r0, rt + CONV_PAD), :]
            zwin = zpad_ref[pl.ds(r0, rt + CONV_PAD), :]
            dzc = zwin[:rt, :]
            acc = jnp.zeros((rt, cw), F32)
            for k in range(CONV_W):
                off_z = (CONV_W - 1) - k
                zs = zwin if off_z == 0 else pltpu.roll(zwin, rt + CONV_PAD - off_z, axis=0)
                acc = acc + wv[k:k + 1, :] * zs[:rt, :]
                off_y = CONV_PAD - (CONV_W - 1) + k
                ys = pltpu.roll(ywin, rt + CONV_PAD - off_y, axis=0)
                dw_ref[k:k + 1, :] += jnp.sum(dzc * ys[:rt, :], axis=0, keepdims=True)
            dy_ref[pl.ds(r0, rt), :] = acc
            return dbacc + jnp.sum(dzc, axis=0, keepdims=True)

        db_ref[...] = lax.fori_loop(0, S // rt, chunk, jnp.zeros((1, cw), F32))

    dy, dw, db = _pcall(
        body, name=name, grid=(C // cw,),
        in_specs=[pl.BlockSpec((S, cw), lambda j: (0, j)), pl.BlockSpec((CONV_PAD, cw), lambda j: (0, j)),
                  pl.BlockSpec((S, cw), lambda j: (0, j))],
        out_specs=[pl.BlockSpec((S, cw), lambda j: (0, j)), pl.BlockSpec((CONV_PAD, cw), lambda j: (0, j)),
                   pl.BlockSpec((1, cw), lambda j: (0, j))],
        out_shape=[jax.ShapeDtypeStruct((S, C), F32), jax.ShapeDtypeStruct((CONV_PAD, C), F32),
                   jax.ShapeDtypeStruct((1, C), F32)],
        scratch_shapes=[pltpu.VMEM((S + CONV_PAD, cw), F32), pltpu.VMEM((S + CONV_PAD, cw), F32)],
        compiler_params=_cp(("parallel",)),
    )(y, w, dz)
    return dy, dw[:CONV_W], db


def _conv_op(name):
    @jax.custom_vjp
    def conv(y, w, b):
        return _conv_fwd_call(y, w, b, name + "_fwd")

    def fwd(y, w, b):
        return conv(y, w, b), (y, w)

    def bwd(res, dz):
        y, w = res
        return _conv_bwd_call(y, w, dz, name + "_bwd")

    conv.defvjp(fwd, bwd)
    return conv


def _attn_tile(S):
    return _pick(S, (512, 256, 128))


def _attn_fwd_call(q, kv, kr, name):
    S = q.shape[0]
    t = _attn_tile(S)

    def body(q_ref, kv_ref, kr_ref, o_ref, lse_ref):
        i = pl.program_id(1)
        qv = q_ref[...]

        def update(jb, carry, diagonal):
            m, l, acc = carry
            off = pl.multiple_of(jb * t, t)
            kc = jnp.concatenate([kv_ref[pl.ds(off, t), pl.ds(0, NOPE)], kr_ref[pl.ds(off, t), :]], axis=-1)
            vv = kv_ref[pl.ds(off, t), pl.ds(NOPE, VDIM)]
            s = lax.dot_general(qv, kc, (((1,), (1,)), ((), ())), preferred_element_type=F32)
            if diagonal:
                row = lax.broadcasted_iota(jnp.int32, (t, t), 0)
                col = lax.broadcasted_iota(jnp.int32, (t, t), 1)
                s = jnp.where(col <= row, s, -jnp.inf)
            mn = jnp.maximum(m, jnp.max(s, axis=-1, keepdims=True))
            p = jnp.exp2((s - mn) * ATTN_SCALE_LOG2E)
            al = jnp.exp2((m - mn) * ATTN_SCALE_LOG2E)
            l = al * l + jnp.sum(p, axis=-1, keepdims=True)
            acc = al * acc + jnp.dot(p.astype(BF16), vv, preferred_element_type=F32)
            return mn, l, acc

        init = (jnp.full((t, 1), -jnp.inf, F32), jnp.zeros((t, 1), F32), jnp.zeros((t, VDIM), F32))
        carry = lax.fori_loop(0, i, lambda jb, cr: update(jb, cr, False), init)
        m, l, acc = update(i, carry, True)
        o_ref[...] = (acc / l).astype(o_ref.dtype)
        lse_ref[...] = m * ATTN_SCALE + jnp.log(l)

    return _pcall(
        body, name=name, grid=(N_HEADS, S // t),
        in_specs=[pl.BlockSpec((t, HEAD_W), lambda h, i: (i, h)), pl.BlockSpec((S, HEAD_W), lambda h, i: (0, h)),
                  pl.BlockSpec((S, 128), lambda h, i: (0, 0))],
        out_specs=[pl.BlockSpec((t, VDIM), lambda h, i: (i, h)), pl.BlockSpec((None, t, 1), lambda h, i: (h, i, 0))],
        out_shape=[jax.ShapeDtypeStruct((S, N_HEADS * VDIM), BF16), jax.ShapeDtypeStruct((N_HEADS, S, 1), F32)],
        compiler_params=_cp(("parallel", "parallel")),
    )(q, kv, kr)


def _attn_dd_call(o, do, name):
    S = o.shape[0]
    t = _attn_tile(S)

    def body(o_ref, do_ref, dd_ref):
        dd_ref[...] = jnp.sum(do_ref[...].astype(F32) * o_ref[...].astype(F32), axis=-1, keepdims=True)

    return _pcall(
        body, name=name, grid=(N_HEADS, S // t),
        in_specs=[pl.BlockSpec((t, VDIM), lambda h, i: (i, h)), pl.BlockSpec((t, VDIM), lambda h, i: (i, h))],
        out_specs=pl.BlockSpec((None, t, 1), lambda h, i: (h, i, 0)),
        out_shape=jax.ShapeDtypeStruct((N_HEADS, S, 1), F32), compiler_params=_cp(("parallel", "parallel")),
    )(o, do)


def _attn_bwd_call(q, kv, kr, do, lse, dd, name):
    S = q.shape[0]
    t = _attn_tile(S)
    nq = S // t

    def body(q_ref, kv_ref, kr_ref, do_ref, lse_ref, dd_ref, dq_ref, dkv_ref, dkr_ref, dq_acc):
        j = pl.program_id(1)

        @pl.when(j == 0)
        def _():
            dq_acc[...] = jnp.zeros_like(dq_acc)

        kc = jnp.concatenate([kv_ref[:, pl.ds(0, NOPE)], kr_ref[...]], axis=-1)
        vv = kv_ref[:, pl.ds(NOPE, VDIM)]

        def update(ib, carry, diagonal):
            dkc, dv = carry
            off = pl.multiple_of(ib * t, t)
            qv = q_ref[pl.ds(off, t), :]
            dov = do_ref[pl.ds(off, t), :]
            st = lax.dot_general(kc, qv, (((1,), (1,)), ((), ())), preferred_element_type=F32)
            if diagonal:
                key = lax.broadcasted_iota(jnp.int32, (t, t), 0)
                qry = lax.broadcasted_iota(jnp.int32, (t, t), 1)
                st = jnp.where(key <= qry, st, -jnp.inf)
            pt = jnp.exp2(st * ATTN_SCALE_LOG2E - lse_ref[pl.ds(ib, 1), :])
            dv = dv + jnp.dot(pt.astype(BF16), dov, preferred_element_type=F32)
            dpt = lax.dot_general(vv, dov, (((1,), (1,)), ((), ())), preferred_element_type=F32)
            dst = (pt * (dpt - dd_ref[pl.ds(ib, 1), :]) * ATTN_SCALE).astype(BF16)
            dkc = dkc + jnp.dot(dst, qv, preferred_element_type=F32)
            dq_acc[pl.ds(off, t), :] += lax.dot_general(dst, kc, (((0,), (0,)), ((), ())), preferred_element_type=F32)
            return dkc, dv

        carry = update(j, (jnp.zeros((t, HEAD_W), F32), jnp.zeros((t, VDIM), F32)), True)
        dkc, dv = lax.fori_loop(j + 1, nq, lambda ib, cr: update(ib, cr, False), carry)
        dkv_ref[:, pl.ds(0, NOPE)] = dkc[:, :NOPE].astype(dkv_ref.dtype)
        dkv_ref[:, pl.ds(NOPE, VDIM)] = dv.astype(dkv_ref.dtype)
        dkr_ref[...] = dkc[:, NOPE:]

        @pl.when(j == nq - 1)
        def _():
            dq_ref[...] = dq_acc[...].astype(dq_ref.dtype)

    dq, dkv, dkr_heads = _pcall(
        body, name=name, grid=(N_HEADS, nq),
        in_specs=[pl.BlockSpec((S, HEAD_W), lambda h, j: (0, h)), pl.BlockSpec((t, HEAD_W), lambda h, j: (j, h)),
                  pl.BlockSpec((t, 128), lambda h, j: (j, 0)), pl.BlockSpec((S, VDIM), lambda h, j: (0, h)),
                  pl.BlockSpec((None, nq, t), lambda h, j: (h, 0, 0)), pl.BlockSpec((None, nq, t), lambda h, j: (h, 0, 0))],
        out_specs=[pl.BlockSpec((S, HEAD_W), lambda h, j: (0, h)), pl.BlockSpec((t, HEAD_W), lambda h, j: (j, h)),
                   pl.BlockSpec((None, t, 128), lambda h, j: (h, j, 0))],
        out_shape=[jax.ShapeDtypeStruct(q.shape, q.dtype), jax.ShapeDtypeStruct(kv.shape, kv.dtype),
                   jax.ShapeDtypeStruct((N_HEADS, S, 128), F32)],
        scratch_shapes=[pltpu.VMEM((S, HEAD_W), F32)],
        compiler_params=_cp(("parallel", "arbitrary")),
    )(q, kv, kr, do, (lse * LOG2E).reshape(N_HEADS, nq, t), dd.reshape(N_HEADS, nq, t))

    def sum_body(p_ref, o_ref):
        acc = p_ref[0]
        for h in range(1, N_HEADS):
            acc = acc + p_ref[h]
        o_ref[...] = acc.astype(o_ref.dtype)

    dkr = _pcall(
        sum_body, name=name + "_rope_sum", grid=(nq,),
        in_specs=[pl.BlockSpec((N_HEADS, t, 128), lambda i: (0, i, 0))], out_specs=pl.BlockSpec((t, 128), lambda i: (i, 0)),
        out_shape=jax.ShapeDtypeStruct((S, 128), kr.dtype), compiler_params=_cp(("parallel",)),
    )(dkr_heads)
    return dq, dkv, dkr


def _attn_op(name):
    @jax.custom_vjp
    def attn(q, kv, kr):
        return _attn_fwd_call(q, kv, kr, name + "_fwd")[0]

    def fwd(q, kv, kr):
        o, lse = _attn_fwd_call(q, kv, kr, name + "_fwd")
        return o, (q, kv, kr, o, lse)

    def bwd(res, do):
        q, kv, kr, o, lse = res
        dd = _attn_dd_call(o, do, name + "_dd")
        return _attn_bwd_call(q, kv, kr, do, lse, dd, name + "_bwd")

    attn.defvjp(fwd, bwd)
    return attn


def _loss_call(y, target):
    S, D = y.shape
    t = _pick(S, (256, 128))

    def body(y_ref, t_ref, dy_ref, loss_ref):
        @pl.when(pl.program_id(0) == 0)
        def _():
            loss_ref[...] = jnp.zeros_like(loss_ref)

        e = y_ref[...] - t_ref[...]
        dy_ref[...] = e * (1.0 / D)
        loss_ref[...] += 0.5 * jnp.sum(jnp.mean(e * e, axis=-1, keepdims=True), axis=0, keepdims=True)

    return _pcall(
        body, name="loss_head", grid=(S // t,),
        in_specs=[pl.BlockSpec((t, D), lambda i: (i, 0)), pl.BlockSpec((t, D), lambda i: (i, 0))],
        out_specs=[pl.BlockSpec((t, D), lambda i: (i, 0)), pl.BlockSpec((1, 1), lambda i: (0, 0))],
        out_shape=[jax.ShapeDtypeStruct((S, D), F32), jax.ShapeDtypeStruct((1, 1), F32)],
        compiler_params=_cp(("arbitrary",)),
    )(y, target)


def _adamw_call(w, g, m, v, name):
    shape = w.shape
    C = shape[-1]
    R = int(np.prod(shape[:-1]))
    tr = R
    for cand in (512, 256, 128, 64, 32, 16, 8):
        if R % cand == 0 and cand * C * 4 <= 2 * 1024 * 1024:
            tr = cand
            break
    c1 = 1.0 - ADAM_B1 ** ADAM_STEP
    c2 = 1.0 - ADAM_B2 ** ADAM_STEP

    def body(w_ref, g_ref, m_ref, v_ref, d_ref, mo_ref, vo_ref):
        gv = g_ref[...]
        mn = ADAM_B1 * m_ref[...] + (1.0 - ADAM_B1) * gv
        vn = ADAM_B2 * v_ref[...] + (1.0 - ADAM_B2) * (gv * gv)
        d_ref[...] = -ADAM_LR * ((mn / c1) / (jnp.sqrt(vn / c2) + ADAM_EPS) + ADAM_WD * w_ref[...])
        mo_ref[...] = mn
        vo_ref[...] = vn

    spec = pl.BlockSpec((tr, C), lambda i: (i, 0))
    outs = _pcall(
        body, name=name, grid=(R // tr,), in_specs=[spec] * 4, out_specs=[spec] * 3,
        out_shape=[jax.ShapeDtypeStruct((R, C), F32)] * 3, compiler_params=_cp(("parallel",)),
    )(*[a.reshape(R, C) for a in (w, g, m, v)])
    return [o.reshape(shape) for o in outs]


def _gather_small(x2d, *, reduce, name):
    R, C = x2d.shape

    def body(x_ref, out_ref, *scratch):
        if reduce:
            buf_ref, send_sems, recv_sems, local_sem = scratch
        else:
            buf_ref = out_ref
            send_sems, recv_sems, local_sem = scratch
        x, y, c = lax.axis_index("x"), lax.axis_index("y"), lax.axis_index("c")
        me, sibling = (x, y, c), (x, y, 1 - c)
        chips = [(1 - x, y), (x, 1 - y), (1 - x, 1 - y)]

        def rows(px, py, pc):
            return buf_ref.at[pl.ds((4 * px + 2 * py + pc) * R, R), :]

        def copy(k, block, to, src=None):
            return pltpu.make_async_remote_copy(
                src_ref=rows(*block) if src is None else src, dst_ref=rows(*block),
                send_sem=send_sems.at[k], recv_sem=recv_sems.at[k], device_id=to, device_id_type=MESH)

        mine = pltpu.make_async_copy(x_ref, rows(*me), local_sem)
        mine.start()
        first = [copy(0, me, sibling, src=x_ref)]
        first += [copy(1 + j, me, (*chip, c), src=x_ref) for j, chip in enumerate(chips)]
        for cp in first:
            cp.start()
        passed = [copy(4 + j, (*chip, c), sibling) for j, chip in enumerate(chips)]
        for j, chip in enumerate(chips):
            copy(1 + j, (*chip, c), me).wait_recv()
            passed[j].start()
        copy(0, sibling, me).wait_recv()
        for j, chip in enumerate(chips):
            copy(4 + j, (*chip, 1 - c), me).wait_recv()
        for cp in first + passed:
            cp.wait_send()
        mine.wait()
        if reduce:
            acc = buf_ref[pl.ds(0, R), :]
            for d in range(1, N_DEV):
                acc = acc + buf_ref[pl.ds(d * R, R), :]
            out_ref[...] = acc

    scratch = [pltpu.SemaphoreType.DMA((7,)), pltpu.SemaphoreType.DMA((7,)), pltpu.SemaphoreType.DMA]
    if reduce:
        scratch = [pltpu.VMEM((N_DEV * R, C), F32)] + scratch
    return _pcall(
        body, name=name, out_shape=jax.ShapeDtypeStruct((R if reduce else N_DEV * R, C), F32),
        in_specs=[pl.BlockSpec(memory_space=pltpu.VMEM)], out_specs=pl.BlockSpec(memory_space=pltpu.VMEM),
        scratch_shapes=scratch, compiler_params=pltpu.CompilerParams(vmem_limit_bytes=VMEM_LIMIT),
    )(x2d)


def _pack(arrs):
    flat = jnp.concatenate([a.reshape(-1).astype(F32) for a in arrs])
    n = flat.shape[0]
    unit = 8 * SMALL_COLS
    flat = jnp.pad(flat, (0, (-n) % unit))
    return flat.reshape(-1, SMALL_COLS)


def _unpack(flat, shapes):
    out, o = [], 0
    for s in shapes:
        n = int(np.prod(s))
        out.append(flat[o:o + n].reshape(s))
        o += n
    return out


def _all_gather_small(arrs, name):
    p = _pack(arrs)
    g = _gather_small(p, reduce=False, name=name).reshape(N_DEV, -1)
    out, o = [], 0
    for a in arrs:
        n = int(np.prod(a.shape))
        out.append(g[:, o:o + n].reshape((N_DEV,) + a.shape))
        o += n
    return out


def _all_reduce_small(arrs, name):
    p = _pack(arrs)
    return _unpack(_gather_small(p, reduce=True, name=name).reshape(-1), [a.shape for a in arrs])


def _half_rows(shard_shape):
    return shard_shape[0] // 2


def _half_slot(ref, kind, rh, cols, half, slot):
    if kind == "col":
        return ref.at[pl.ds(half * rh, rh), pl.ds(slot * cols, cols)]
    return ref.at[pl.ds((slot * 2 + half) * rh, rh), :]


def _full_shape(kind, shard_shape):
    r, c = shard_shape
    return (r, c * N_CHIPS) if kind == "col" else (r * N_CHIPS, c)


def _mesh_pos():
    x, y, c = lax.axis_index("x"), lax.axis_index("y"), lax.axis_index("c")
    return x, y, c


def _handshake_chip_peers(x, y, c):
    barrier = pltpu.get_barrier_semaphore()
    for peer in [(x, y, 1 - c)] + [(x ^ fx, y ^ fy, c) for fx, fy in CHIP_RELS]:
        pl.semaphore_signal(barrier, inc=1, device_id=peer, device_id_type=MESH)
    pl.semaphore_wait(barrier, 1 + len(CHIP_RELS))


def _gather_big(shards, kinds, name):
    n = len(shards)

    def body(*refs):
        srcs, outs = refs[:n], refs[n:2 * n]
        ici_send, ici_recv, d2d_send, d2d_recv, own_send, own_recv = refs[2 * n:]
        x, y, c = _mesh_pos()
        _handshake_chip_peers(x, y, c)
        my = 2 * x + y
        geo = [(_half_rows(s.shape), s.shape[1]) for s in shards]

        def region(a, half, slot):
            return _half_slot(outs[a], kinds[a], geo[a][0], geo[a][1], half, slot)

        def slot_of(a, slot):
            rh, cols = geo[a]
            if kinds[a] == "col":
                return outs[a].at[:, pl.ds(slot * cols, cols)]
            return outs[a].at[pl.ds(slot * 2 * rh, 2 * rh), :]

        sends = []
        for a in range(n):
            cp = pltpu.make_async_remote_copy(
                src_ref=srcs[a], dst_ref=slot_of(a, my), send_sem=own_send.at[a], recv_sem=own_recv.at[a],
                device_id=(x, y, 1 - c), device_id_type=MESH)
            cp.start()
            sends.append(cp)
        for a in range(n):
            rh = geo[a][0]
            for r, (fx, fy) in enumerate(CHIP_RELS):
                cp = pltpu.make_async_remote_copy(
                    src_ref=srcs[a].at[pl.ds(c * rh, rh), :], dst_ref=region(a, c, my),
                    send_sem=ici_send.at[3 * a + r], recv_sem=ici_recv.at[3 * a + r],
                    device_id=(x ^ fx, y ^ fy, c), device_id_type=MESH)
                cp.start()
                sends.append(cp)
        passed = []
        for a in range(n):
            for r, (fx, fy) in enumerate(CHIP_RELS):
                frm = 2 * (x ^ fx) + (y ^ fy)
                landed = region(a, c, frm)
                pltpu.make_async_remote_copy(
                    src_ref=landed, dst_ref=landed, send_sem=ici_send.at[3 * a + r], recv_sem=ici_recv.at[3 * a + r],
                    device_id=(x, y, c), device_id_type=MESH).wait_recv()
                cp = pltpu.make_async_remote_copy(
                    src_ref=landed, dst_ref=landed, send_sem=d2d_send.at[3 * a + r], recv_sem=d2d_recv.at[3 * a + r],
                    device_id=(x, y, 1 - c), device_id_type=MESH)
                cp.start()
                passed.append(cp)
        for a in range(n):
            for r, (fx, fy) in enumerate(CHIP_RELS):
                frm = 2 * (x ^ fx) + (y ^ fy)
                other = region(a, 1 - c, frm)
                pltpu.make_async_remote_copy(
                    src_ref=other, dst_ref=other, send_sem=d2d_send.at[3 * a + r], recv_sem=d2d_recv.at[3 * a + r],
                    device_id=(x, y, c), device_id_type=MESH).wait_recv()
        for a in range(n):
            pltpu.make_async_remote_copy(
                src_ref=srcs[a], dst_ref=slot_of(a, my), send_sem=own_send.at[a], recv_sem=own_recv.at[a],
                device_id=(x, y, c), device_id_type=MESH).wait_recv()
        for cp in sends + passed:
            cp.wait_send()

    return pl.kernel(
        body, name=name,
        out_type=[jax.ShapeDtypeStruct(_full_shape(k, s.shape), s.dtype) for s, k in zip(shards, kinds)],
        mesh=plsc.ScalarSubcoreMesh(axis_name="sequencer", num_cores=1),
        scratch_types=[pltpu.SemaphoreType.DMA((3 * n,))] * 4 + [pltpu.SemaphoreType.DMA((n,))] * 2,
        compiler_params=pltpu.CompilerParams(collective_id=GATHER_COLLECTIVE_ID),
    )(*shards)


def _swap_halves(fulls, kinds, shard_shapes, name):
    n = len(fulls)
    geo = [(_half_rows(s), s[1]) for s in shard_shapes]

    def body(*refs):
        srcs, outs = refs[:n], refs[n:2 * n]
        send_sems, recv_sems = refs[2 * n:]
        x, y, c = _mesh_pos()
        barrier = pltpu.get_barrier_semaphore()
        pl.semaphore_signal(barrier, inc=1, device_id=(x, y, 1 - c), device_id_type=MESH)
        pl.semaphore_wait(barrier, 1)
        cps = []
        for a in range(n):
            for s in range(N_CHIPS):
                cp = pltpu.make_async_remote_copy(
                    src_ref=_half_slot(srcs[a], kinds[a], geo[a][0], geo[a][1], 1 - c, s), dst_ref=outs[a].at[s],
                    send_sem=send_sems.at[N_CHIPS * a + s], recv_sem=recv_sems.at[N_CHIPS * a + s],
                    device_id=(x, y, 1 - c), device_id_type=MESH)
                cp.start()
                cps.append(cp)
        for cp in cps:
            cp.wait()

    return pl.kernel(
        body, name=name,
        out_type=[jax.ShapeDtypeStruct((N_CHIPS,) + g, f.dtype) for g, f in zip(geo, fulls)],
        mesh=plsc.ScalarSubcoreMesh(axis_name="sequencer", num_cores=1),
        scratch_types=[pltpu.SemaphoreType.DMA((N_CHIPS * n,))] * 2,
        compiler_params=pltpu.CompilerParams(collective_id=SWAP_COLLECTIVE_ID),
    )(*fulls)


def _add_own_half(full, recv, kind, shard_shape, cidx, name):
    rh, cols = _half_rows(shard_shape), shard_shape[1]
    tr = _pick(rh, (256, 128, 64, 32, 16))
    nb = rh // tr

    def body(c_ref, f_ref, r_ref, o_ref):
        o_ref[...] = (f_ref[...].astype(F32) + r_ref[...].astype(F32)).astype(o_ref.dtype)

    if kind == "col":
        f_spec = pl.BlockSpec((tr, cols), lambda s, i, c_ref: (c_ref[0] * nb + i, s))
    else:
        f_spec = pl.BlockSpec((tr, cols), lambda s, i, c_ref: ((s * 2 + c_ref[0]) * nb + i, 0))
    blk = pl.BlockSpec((None, tr, cols), lambda s, i, c_ref: (s, i, 0))
    return _pcall(
        body, name=name,
        grid_spec=pltpu.PrefetchScalarGridSpec(num_scalar_prefetch=1, grid=(N_CHIPS, nb), in_specs=[f_spec, blk], out_specs=blk),
        out_shape=jax.ShapeDtypeStruct((N_CHIPS, rh, cols), BF16),
        compiler_params=_cp(("arbitrary", "arbitrary")),
    )(cidx, full, recv)


def _scatter_partials(partials, name):
    n = len(partials)

    def body(*refs):
        srcs, outs = refs[:n], refs[n:2 * n]
        send_sems, recv_sems = refs[2 * n:]
        x, y, c = _mesh_pos()
        barrier = pltpu.get_barrier_semaphore()
        for fx, fy in CHIP_RELS:
            pl.semaphore_signal(barrier, inc=1, device_id=(x ^ fx, y ^ fy, c), device_id_type=MESH)
        pl.semaphore_wait(barrier, len(CHIP_RELS))
        my = 2 * x + y
        cps = []
        for a in range(n):
            for r, (fx, fy) in enumerate(CHIP_RELS):
                to = 2 * (x ^ fx) + (y ^ fy)
                cp = pltpu.make_async_remote_copy(
                    src_ref=srcs[a].at[to], dst_ref=outs[a].at[my],
                    send_sem=send_sems.at[3 * a + r], recv_sem=recv_sems.at[3 * a + r],
                    device_id=(x ^ fx, y ^ fy, c), device_id_type=MESH)
                cp.start()
                cps.append(cp)
        for a in range(n):
            for r, (fx, fy) in enumerate(CHIP_RELS):
                frm = 2 * (x ^ fx) + (y ^ fy)
                pltpu.make_async_remote_copy(
                    src_ref=outs[a].at[frm], dst_ref=outs[a].at[frm],
                    send_sem=send_sems.at[3 * a + r], recv_sem=recv_sems.at[3 * a + r],
                    device_id=(x, y, c), device_id_type=MESH).wait_recv()
        for cp in cps:
            cp.wait_send()

    return pl.kernel(
        body, name=name,
        out_type=[jax.ShapeDtypeStruct(p.shape, p.dtype) for p in partials],
        mesh=plsc.ScalarSubcoreMesh(axis_name="sequencer", num_cores=1),
        scratch_types=[pltpu.SemaphoreType.DMA((3 * n,))] * 2,
        compiler_params=pltpu.CompilerParams(collective_id=SCATTER_COLLECTIVE_ID),
    )(*partials)


def _sum_chips_into(landed, partial, buf, layer, n_layers, pos, name):
    _, rh, cols = landed.shape
    tr = _pick(rh, (256, 128, 64, 32, 16))
    nb = rh // tr

    def body(pos_ref, l_ref, own_ref, *rest):
        o_ref = rest[-1]
        my = pos_ref[1]
        own = own_ref[...].astype(F32)
        acc = jnp.where(my == 0, own, l_ref[0].astype(F32))
        for j in range(1, N_CHIPS):
            acc = acc + jnp.where(my == j, own, l_ref[j].astype(F32))
        o_ref[...] = acc

    in_specs = [pl.BlockSpec((N_CHIPS, tr, cols), lambda i, pos_ref: (0, i, 0)),
                pl.BlockSpec((None, tr, cols), lambda i, pos_ref: (pos_ref[1], i, 0))]
    args = [pos, landed, partial]
    aliases = {}
    if buf is not None:
        in_specs.append(pl.BlockSpec(memory_space=pl.ANY))
        args.append(buf)
        aliases = {3: 0}
    return _pcall(
        body, name=name,
        grid_spec=pltpu.PrefetchScalarGridSpec(
            num_scalar_prefetch=1, grid=(nb,), in_specs=in_specs,
            out_specs=pl.BlockSpec((None, tr, cols), lambda i, pos_ref: (layer, pos_ref[0] * nb + i, 0))),
        out_shape=jax.ShapeDtypeStruct((n_layers, 2 * rh, cols), F32), input_output_aliases=aliases,
        compiler_params=_cp(("arbitrary",)),
    )(*args)


def _exchange_final(bufs):
    n = len(bufs)
    HBM = pl.BlockSpec(memory_space=pl.ANY)
    n_layers = [b.shape[0] for b in bufs]
    base = np.concatenate([[0], np.cumsum(n_layers)])

    def body(*refs):
        outs = refs[n:2 * n]
        send_sems, recv_sems = refs[2 * n:]
        x, y, c = _mesh_pos()
        cps = []
        for w in range(n):
            rh = bufs[w].shape[1] // 2
            for l in range(n_layers[w]):
                k = int(base[w]) + l
                mine = outs[w].at[l, pl.ds(c * rh, rh), :]
                cp = pltpu.make_async_remote_copy(
                    src_ref=mine, dst_ref=mine, send_sem=send_sems.at[k], recv_sem=recv_sems.at[k],
                    device_id=(x, y, 1 - c), device_id_type=MESH)
                cp.start()
                cps.append(cp)
        for w in range(n):
            rh = bufs[w].shape[1] // 2
            for l in range(n_layers[w]):
                k = int(base[w]) + l
                other = outs[w].at[l, pl.ds((1 - c) * rh, rh), :]
                pltpu.make_async_remote_copy(
                    src_ref=other, dst_ref=other, send_sem=send_sems.at[k], recv_sem=recv_sems.at[k],
                    device_id=(x, y, c), device_id_type=MESH).wait_recv()
        for cp in cps:
            cp.wait_send()

    return _pcall(
        body, name="grad_exchange_final", out_shape=[jax.ShapeDtypeStruct(b.shape, b.dtype) for b in bufs],
        in_specs=[HBM] * n, out_specs=[HBM] * n, input_output_aliases={i: i for i in range(n)},
        scratch_shapes=[pltpu.SemaphoreType.DMA((int(base[-1]),))] * 2,
        compiler_params=pltpu.CompilerParams(vmem_limit_bytes=VMEM_LIMIT),
    )(*bufs)


def _swap_halves_last(z):
    h = z.shape[-1] // 2
    return jnp.concatenate([z[..., h:], z[..., :h]], axis=-1)


def _ext_uq(w):
    lead = w.shape[:-1]
    wh = w.reshape(lead + (-1, QK_DIM))
    rope = wh[..., NOPE:]
    return jnp.concatenate([wh, _swap_halves_last(rope)], axis=-1).reshape(lead + (-1,))


def _fold_uq(d):
    lead = d.shape[:-1]
    dh = d.reshape(lead + (-1, HEAD_W))
    rope = dh[..., NOPE:QK_DIM] + _swap_halves_last(dh[..., QK_DIM:])
    return jnp.concatenate([dh[..., :NOPE], rope], axis=-1).reshape(lead + (-1,))


def _ext_win(w):
    base, kr = w[..., :-ROPE], w[..., -ROPE:]
    ks = _swap_halves_last(kr)
    return jnp.concatenate([base, kr, ks, ks, kr], axis=-1)


def _fold_win(d):
    n = d.shape[-1] - 4 * ROPE
    a, b, c2, e = [d[..., n + i * ROPE:n + (i + 1) * ROPE] for i in range(4)]
    return jnp.concatenate([d[..., :n], a + e + _swap_halves_last(b + c2)], axis=-1)


ROW_TILE = 256


def _stage(st, xcur, mix, big, small, tabs):
    D = xcur.shape[1]
    l = st // 2
    mod = [[small["mod"][k][i].reshape(1, 1, D) for i in range(6)] for k in range(l + 1)]
    sh1, sc1, g1, sh2, sc2, _ = mod[l]
    if st % 2 == 0:
        n1 = small["norm1_g"][l].reshape(1, 1, D)
        if l == 0:
            (h,) = _rowop("norm_mod_0", _f_norm_mod, n_x=1, n_nd=0, n_p=3, x_w=[D], nd_w=[], nd_shared=[], p_per_group=[False] * 3,
                          out_w=[D], out_dtypes=[BF16], tile=ROW_TILE)((xcur,), (), (n1, sh1, sc1))
        else:
            op = _rowop(f"resid_norm_mod_a{l}", _f_resid_norm_mod, n_x=2, n_nd=0, n_p=4, x_w=[D, D], nd_w=[], nd_shared=[],
                        p_per_group=[False] * 4, out_w=[D, D], out_dtypes=[F32, BF16], tile=ROW_TILE)
            xcur, h = op((xcur, mix), (), (mod[l - 1][5], n1, sh1, sc1))
        if l % 2 == 0:
            return xcur, _even_mixer(h, big, small, l // 2, ROW_TILE)
        return xcur, _mla_mixer(h, big, small, l // 2, ROW_TILE, *tabs)
    n2 = small["norm2_g"][l].reshape(1, 1, D)
    op = _rowop(f"resid_norm_mod_b{l}", _f_resid_norm_mod, n_x=2, n_nd=0, n_p=4, x_w=[D, D], nd_w=[], nd_shared=[],
                p_per_group=[False] * 4, out_w=[D, D], out_dtypes=[F32, BF16], tile=ROW_TILE)
    xcur, h = op((xcur, mix), (), (g1, n2, sh2, sc2))
    return xcur, _mlp(f"mlp_{l}")(h, big["mlp_w1"], big["mlp_w2"])


def _last_residual(xcur, mix, small):
    D = xcur.shape[1]
    gate = small["mod"][-1][5].reshape(1, 1, D)
    return _rowop("resid_last", _f_resid, n_x=2, n_nd=0, n_p=1, x_w=[D, D], nd_w=[], nd_shared=[], p_per_group=[False],
                  out_w=[D], out_dtypes=[F32], tile=ROW_TILE)((xcur, mix), (), (gate,))[0]


def _even_mixer(h, big, wts, e, tile):
    proj = _linear(f"ab_in_{e}", BF16)(h, big["ab_w_in"])
    da = proj.shape[1] // 4
    u, v, a, g = [proj[:, i * da:(i + 1) * da] for i in range(4)]
    ng = da // GROUP
    sgu = _rowop(f"sgu_{e}", _f_sgu, n_x=2, n_nd=0, n_p=3, x_w=[GROUP, GROUP], nd_w=[], nd_shared=[], p_per_group=[True] * 3,
                 out_w=[GROUP], out_dtypes=[BF16], tile=SGU_CHUNKS_PER_STEP * CHUNK, groups=ng)
    bexp = jnp.broadcast_to(wts["sgu_b"][e][:, :, None], (ng, CHUNK, GROUP))
    (out_a,) = sgu((u, v), (), (wts["sgu_norm_g"][e].reshape(ng, 1, GROUP), wts["sgu_w"][e], bexp))
    (yglu,) = _rowop(f"glu_{e}", _f_glu, n_x=2, n_nd=0, n_p=0, x_w=[da, da], nd_w=[], nd_shared=[], p_per_group=[],
                     out_w=[da], out_dtypes=[F32], tile=tile)((a, g), (), ())
    z = _conv_op(f"conv_{e}")(yglu, wts["conv_w"][e], wts["conv_b"][e].reshape(1, da))
    (out_b,) = _rowop(f"ln_silu_{e}", _f_ln_silu, n_x=1, n_nd=0, n_p=2, x_w=[da], nd_w=[], nd_shared=[], p_per_group=[False] * 2,
                      out_w=[da], out_dtypes=[BF16], tile=tile)(
        (z,), (), (wts["conv_ln_g"][e].reshape(1, 1, da), wts["conv_ln_b"][e].reshape(1, 1, da)))
    return _linear(f"ab_out_{e}")(jnp.concatenate([out_a, out_b], axis=-1), big["ab_w_out"])


def _mla_mixer(h, big, wts, o, tile, tab_q, tab_ka, tab_kb):
    proj = _linear(f"mla_in_{o}", BF16)(h, big["mla_w_in"])
    rank = (proj.shape[1] - 4 * ROPE) // 2
    c_q, c_kv = proj[:, :rank], proj[:, rank:2 * rank]
    kr_a, kr_b = proj[:, 2 * rank:2 * rank + 2 * ROPE], proj[:, 2 * rank + 2 * ROPE:]

    def rms(name, xx, gg):
        return _rowop(name, _f_rms, n_x=1, n_nd=0, n_p=1, x_w=[rank], nd_w=[], nd_shared=[], p_per_group=[False],
                      out_w=[rank], out_dtypes=[BF16], tile=tile)((xx,), (), (gg.reshape(1, 1, rank),))[0]

    q_raw = _linear(f"mla_uq_{o}", BF16)(rms(f"rms_q_{o}", c_q, wts["mla_q_norm_g"][o]), big["mla_w_uq"])
    kv_raw = _linear(f"mla_ukv_{o}", BF16)(rms(f"rms_kv_{o}", c_kv, wts["mla_kv_norm_g"][o]), big["mla_w_ukv"])
    gq, gk = wts["mla_q_head_g"][o], wts["mla_k_head_g"][o]
    gk_rope = gk[NOPE:]
    gq_ext = jnp.concatenate([gq, _swap_halves_last(gq[NOPE:])]).reshape(1, 1, HEAD_W)
    gk_ext = jnp.concatenate([gk[:NOPE], jnp.ones((HEAD_W - NOPE,), F32)]).reshape(1, 1, HEAD_W)
    gk_a = jnp.concatenate([gk_rope, _swap_halves_last(gk_rope)]).reshape(1, 1, 2 * ROPE)
    gk_b = jnp.concatenate([_swap_halves_last(gk_rope), gk_rope]).reshape(1, 1, 2 * ROPE)
    head_tile = 4 * tile
    (q,) = _rowop(f"q_head_{o}", _f_qhead, n_x=1, n_nd=1, n_p=1, x_w=[HEAD_W], nd_w=[HEAD_W], nd_shared=[True], p_per_group=[False],
                  out_w=[HEAD_W], out_dtypes=[BF16], tile=head_tile, groups=N_HEADS)((q_raw,), (tab_q,), (gq_ext,))
    (kv,) = _rowop(f"kv_head_{o}", _f_kvhead, n_x=1, n_nd=0, n_p=1, x_w=[HEAD_W], nd_w=[], nd_shared=[], p_per_group=[False],
                   out_w=[HEAD_W], out_dtypes=[BF16], tile=head_tile, groups=N_HEADS)((kv_raw,), (), (gk_ext,))
    (kr,) = _rowop(f"k_rope_{o}", _f_krope, n_x=2, n_nd=2, n_p=2, x_w=[2 * ROPE] * 2, nd_w=[2 * ROPE] * 2, nd_shared=[True] * 2,
                   p_per_group=[False] * 2, out_w=[2 * ROPE], out_dtypes=[BF16], tile=tile)((kr_a, kr_b), (tab_ka, tab_kb), (gk_a, gk_b))
    att = _attn_op(f"attn_{o}")(q, kv, kr)
    return _linear(f"mla_out_{o}")(att, big["mla_w_out"])


def _rope_tabs(S):
    pos = jnp.arange(S, dtype=F32)
    inv = ROPE_THETA ** (-jnp.arange(0, ROPE, 2, dtype=F32) / ROPE)
    ang = pos[:, None] * inv[None, :]
    cos, sin = jnp.cos(ang), jnp.sin(ang)
    cc = jnp.concatenate([cos, cos], axis=-1)
    sg = jnp.concatenate([-sin, sin], axis=-1)
    tab_q = jnp.concatenate([jnp.ones((S, NOPE), F32), cc, sg], axis=-1)
    return tab_q, jnp.concatenate([cc, sg], axis=-1), jnp.concatenate([sg, cc], axis=-1)


def kernel(x, c, norm1_g, norm2_g, ada_w, ada_b, mlp_w1, mlp_w2, ab_w_in, sgu_norm_g, sgu_w, sgu_b, conv_w, conv_b, conv_ln_g, conv_ln_b, ab_w_out, mla_w_in, mla_q_norm_g, mla_kv_norm_g, mla_w_uq, mla_w_ukv, mla_q_head_g, mla_k_head_g, mla_w_out, loss_target, m_norm1_g, m_norm2_g, m_ada_w, m_ada_b, m_mlp_w1, m_mlp_w2, m_ab_w_in, m_sgu_norm_g, m_sgu_w, m_sgu_b, m_conv_w, m_conv_b, m_conv_ln_g, m_conv_ln_b, m_ab_w_out, m_mla_w_in, m_mla_q_norm_g, m_mla_kv_norm_g, m_mla_w_uq, m_mla_w_ukv, m_mla_q_head_g, m_mla_k_head_g, m_mla_w_out, v_norm1_g, v_norm2_g, v_ada_w, v_ada_b, v_mlp_w1, v_mlp_w2, v_ab_w_in, v_sgu_norm_g, v_sgu_w, v_sgu_b, v_conv_w, v_conv_b, v_conv_ln_g, v_conv_ln_b, v_ab_w_out, v_mla_w_in, v_mla_q_norm_g, v_mla_kv_norm_g, v_mla_w_uq, v_mla_w_ukv, v_mla_q_head_g, v_mla_k_head_g, v_mla_w_out):
    names = ["norm1_g", "norm2_g", "ada_w", "ada_b", "mlp_w1", "mlp_w2", "ab_w_in", "sgu_norm_g", "sgu_w", "sgu_b", "conv_w",
             "conv_b", "conv_ln_g", "conv_ln_b", "ab_w_out", "mla_w_in", "mla_q_norm_g", "mla_kv_norm_g", "mla_w_uq", "mla_w_ukv",
             "mla_q_head_g", "mla_k_head_g", "mla_w_out"]
    W = dict(zip(names, [norm1_g, norm2_g, ada_w, ada_b, mlp_w1, mlp_w2, ab_w_in, sgu_norm_g, sgu_w, sgu_b, conv_w, conv_b, conv_ln_g,
                         conv_ln_b, ab_w_out, mla_w_in, mla_q_norm_g, mla_kv_norm_g, mla_w_uq, mla_w_ukv, mla_q_head_g, mla_k_head_g,
                         mla_w_out]))
    M = dict(zip(names, [m_norm1_g, m_norm2_g, m_ada_w, m_ada_b, m_mlp_w1, m_mlp_w2, m_ab_w_in, m_sgu_norm_g, m_sgu_w, m_sgu_b, m_conv_w,
                         m_conv_b, m_conv_ln_g, m_conv_ln_b, m_ab_w_out, m_mla_w_in, m_mla_q_norm_g, m_mla_kv_norm_g, m_mla_w_uq,
                         m_mla_w_ukv, m_mla_q_head_g, m_mla_k_head_g, m_mla_w_out]))
    V = dict(zip(names, [v_norm1_g, v_norm2_g, v_ada_w, v_ada_b, v_mlp_w1, v_mlp_w2, v_ab_w_in, v_sgu_norm_g, v_sgu_w, v_sgu_b, v_conv_w,
                         v_conv_b, v_conv_ln_g, v_conv_ln_b, v_ab_w_out, v_mla_w_in, v_mla_q_norm_g, v_mla_kv_norm_g, v_mla_w_uq,
                         v_mla_w_ukv, v_mla_q_head_g, v_mla_k_head_g, v_mla_w_out]))
    xi, yi, ci = lax.axis_index("x"), lax.axis_index("y"), lax.axis_index("c")
    chip = 2 * xi + yi
    dev = 2 * chip + ci
    pos = jnp.stack([ci, chip]).astype(jnp.int32)
    S, D = x.shape[1], x.shape[2]
    depth = norm1_g.shape[0]

    c_all, conv_w_all, qn_all, kvn_all = _all_gather_small([c, conv_w, mla_q_norm_g, mla_kv_norm_g], "gather_small_inputs")
    c_all = c_all.reshape(N_DEV, D)
    by_chip = lambda a: jnp.concatenate([a[2 * j] for j in range(N_CHIPS)], axis=-1)
    conv_w_full, qn_full, kvn_full = by_chip(conv_w_all), by_chip(qn_all), by_chip(kvn_all)

    (c_act,) = _rowop("silu_c", _f_silu, n_x=1, n_nd=0, n_p=0, x_w=[D], nd_w=[], nd_shared=[], p_per_group=[], out_w=[D],
                      out_dtypes=[F32], tile=N_DEV)((c_all,), (), ())
    c_act_pad = jnp.pad(c_act, ((0, 128 - N_DEV), (0, 0)))
    mod_cols = jnp.stack([_matmul(c_act_pad, ada_w[l], name=f"ada_fwd_{l}")[:N_DEV] for l in range(depth)])
    (mod_all,) = _all_gather_small([mod_cols], "gather_mod")
    mod_mine = jnp.concatenate([lax.dynamic_index_in_dim(mod_all[2 * j], dev, axis=1, keepdims=False) for j in range(N_CHIPS)], axis=-1)
    mod_mine = (mod_mine + ada_b).reshape(depth, 6, D)

    big = {"mlp_w1": "col", "mlp_w2": "row", "ab_w_in": "col", "ab_w_out": "row", "mla_w_in": "row", "mla_w_uq": "col",
           "mla_w_ukv": "col", "mla_w_out": "row"}
    src = dict(W)
    src["mla_w_in"] = _ext_win(mla_w_in)
    src["mla_w_uq"] = _ext_uq(mla_w_uq)
    shards, kinds, owner = [], [], []
    for nme, kind in big.items():
        for l in range(src[nme].shape[0]):
            shards.append(src[nme][l].astype(BF16))
            kinds.append(kind)
            owner.append((nme, l))
    def stage_of(nme, l):
        return 2 * l + 1 if nme.startswith("mlp") else (4 * l if nme.startswith("ab") else 4 * l + 2)

    fwd_stages = [[a for a, (nme, l) in enumerate(owner) if stage_of(nme, l) == st] for st in range(2 * depth)]
    fulls = [None] * len(shards)
    for st, idx in enumerate(fwd_stages):
        got = _gather_big([shards[a] for a in idx], [kinds[a] for a in idx], f"gather_weights_{st}")
        for a, f in zip(idx, got):
            fulls[a] = f
    small = dict(norm1_g=norm1_g, norm2_g=norm2_g, mod=mod_mine, sgu_norm_g=sgu_norm_g, sgu_w=sgu_w, sgu_b=sgu_b, conv_w=conv_w_full,
                 conv_b=conv_b, conv_ln_g=conv_ln_g, conv_ln_b=conv_ln_b, mla_q_norm_g=qn_full, mla_kv_norm_g=kvn_full,
                 mla_q_head_g=mla_q_head_g, mla_k_head_g=mla_k_head_g)
    groups, a0 = [], 0
    for nme in big:
        groups.append(list(range(a0, a0 + src[nme].shape[0])))
        a0 += src[nme].shape[0]
    shard_shapes = [s.shape for s in shards]
    where = {a: (wi, l) for wi, idxs in enumerate(groups) for l, a in enumerate(idxs)}

    tabs = _rope_tabs(S)
    n_stages = 2 * depth
    xcur, mix, vjps = x[0], None, []
    for st, idx in enumerate(fwd_stages):
        stage_big = {owner[a][0]: fulls[a] for a in idx}
        if st == 0:
            (xcur, mix), vjp = jax.vjp(lambda xx, bb, ss: _stage(0, xx, None, bb, ss, tabs), xcur, stage_big, small)
        else:
            (xcur, mix), vjp = jax.vjp(lambda xx, mm, bb, ss, st=st: _stage(st, xx, mm, bb, ss, tabs), xcur, mix, stage_big, small)
        vjps.append(vjp)
    y, vjp_last = jax.vjp(_last_residual, xcur, mix, small)
    dy, loss_mine = _loss_call(y, loss_target[0])
    loss = lax.psum(loss_mine[0, 0], ("x", "y", "c"))

    dxc, dmix, dsmall = vjp_last(dy)
    bufs = [None] * len(groups)
    swapped, scattered = {}, {}

    def add_and_scatter(st):
        idx, gr, recv = swapped.pop(st)
        partials = [_add_own_half(gr[a], r, kinds[a], shard_shapes[a], pos, f"grad_add_sibling_{a}") for a, r in zip(idx, recv)]
        scattered[st] = (idx, partials, _scatter_partials(partials, f"grad_scatter_partials_{st}"))

    def sum_chips(st):
        idx, partials, landed = scattered.pop(st)
        for a, part, land in zip(idx, partials, landed):
            wi, l = where[a]
            bufs[wi] = _sum_chips_into(land, part, bufs[wi], l, len(groups[wi]), pos, f"grad_sum_chips_{a}")

    for st in reversed(range(n_stages)):
        idx = fwd_stages[st]
        if st == 0:
            dxc, dbig, ds = vjps[st]((dxc, dmix))
        else:
            dxc, dmix, dbig, ds = vjps[st]((dxc, dmix))
        dsmall = jax.tree.map(jnp.add, dsmall, ds)
        gr = {a: dbig[owner[a][0]] for a in idx}
        swapped[st] = (idx, gr, _swap_halves([gr[a] for a in idx], [kinds[a] for a in idx], [shard_shapes[a] for a in idx],
                                             f"grad_swap_halves_{st}"))
        started = [s for s in ((st + 1, st) if st == 1 else (st + 1,)) if s in swapped]
        for s in started:
            add_and_scatter(s)
        if st + 2 in scattered:
            sum_chips(st + 2)
        if st > 0:
            pinned = {s: scattered[s][1] for s in started}
            (dxc, dmix), pinned, bufs = lax.optimization_barrier(((dxc, dmix), pinned, bufs))
            for s in started:
                scattered[s] = (scattered[s][0], pinned[s], scattered[s][2])
    dx = dxc
    for st in sorted(swapped, reverse=True):
        add_and_scatter(st)
    for st in sorted(scattered, reverse=True):
        sum_chips(st)
    gsh = _exchange_final(bufs)

    dw = dsmall
    dmod = dw["mod"].reshape(depth, 6 * D)
    small_names = ["norm1_g", "norm2_g", "sgu_norm_g", "sgu_w", "sgu_b", "conv_w", "conv_b", "conv_ln_g", "conv_ln_b", "mla_q_norm_g",
                   "mla_kv_norm_g", "mla_q_head_g", "mla_k_head_g"]
    red = _all_reduce_small([dmod] + [dw[nme] for nme in small_names], "reduce_small_grads")
    G = dict(zip(["ada_b"] + small_names, red))
    own_cols = lambda a: lax.dynamic_slice_in_dim(a, chip * (a.shape[-1] // N_CHIPS), a.shape[-1] // N_CHIPS, axis=-1)
    for nme in ("conv_w", "mla_q_norm_g", "mla_kv_norm_g"):
        G[nme] = own_cols(G[nme])

    (dmod_all,) = _all_gather_small([dmod], "gather_dmod")
    dmod_cols = own_cols(dmod_all)
    dmod_pad = jnp.pad(dmod_cols, ((0, 128 - N_DEV), (0, 0), (0, 0)))
    G["ada_w"] = _ada_wgrad_call(c_act_pad, dmod_pad)

    for nme, g in zip(big, gsh):
        G[nme] = g
    G["mla_w_in"] = _fold_win(G["mla_w_in"])
    G["mla_w_uq"] = _fold_uq(G["mla_w_uq"])

    deltas, new_m, new_v = [], [], []
    for nme in names:
        d, mn, vn = _adamw_call(W[nme], G[nme], M[nme], V[nme], f"adamw_{nme}")
        deltas.append(d)
        new_m.append(mn)
        new_v.append(vn)
    return (loss, dx[None], *[G[nme] for nme in names], *deltas, *new_m, *new_v)
```

```python
import functools
import numpy as np
import jax
import jax.numpy as jnp
from jax import lax
from jax.experimental import pallas as pl
from jax.experimental.pallas import tpu as pltpu
from jax.experimental.pallas import tpu_sc as plsc

F32 = jnp.float32
BF16 = jnp.bfloat16
MESH = pl.DeviceIdType.MESH

EPS = 1e-6
N_HEADS = 16
NOPE = 128
ROPE = 64
VDIM = 128
QK_DIM = NOPE + ROPE
HEAD_W = 256
CHUNK = 128
GROUP = 128
SGU_CHUNKS_PER_STEP = 4
CONV_W = 31
CONV_PAD = 32
ROPE_THETA = 10000.0
ATTN_SCALE = QK_DIM ** -0.5
LOG2E = 1.4426950408889634
ATTN_SCALE_LOG2E = ATTN_SCALE * LOG2E
ADAM_LR, ADAM_B1, ADAM_B2, ADAM_EPS, ADAM_WD, ADAM_STEP = 0.001, 0.9, 0.999, 1e-08, 0.01, 10
N_CHIPS = 4
N_DEV = 8
VMEM_LIMIT = 56 * 1024 * 1024
WHOLE_K_LIMIT = 4096
SMALL_COLS = 1024
CHIP_RELS = ((1, 0), (0, 1), (1, 1))
GATHER_COLLECTIVE_ID = 0
SWAP_COLLECTIVE_ID = 1
SCATTER_COLLECTIVE_ID = 2


def _pcall(body, **kw):
    return pl.pallas_call(body, **kw)


def _cp(sem=None, **kw):
    return pltpu.CompilerParams(dimension_semantics=sem, vmem_limit_bytes=VMEM_LIMIT, **kw)


def _pick(n, cands):
    for c in cands:
        if n % c == 0:
            return c
    return n


def _matmul(a, b, *, ta=False, tb=False, out_dtype=F32, name, extra=(), epilogue=None, out_dtypes=None):
    if ta:
        K, M = a.shape
    else:
        M, K = a.shape
    if tb:
        N, K2 = b.shape
    else:
        K2, N = b.shape
    assert K == K2, (a.shape, b.shape, ta, tb)
    tm = _pick(M, (1024, 512, 256, 128))
    tn = N if N <= 1536 else _pick(N, (1024, 512, 256, 128))
    tk = _pick(K, (2048, 1024, 512, 256, 128))
    if ta and K <= WHOLE_K_LIMIT and max(a.dtype.itemsize, b.dtype.itemsize) == 2:
        tk = K
    nk = K // tk
    dn = (((0 if ta else 1,), (1 if tb else 0,)), ((), ()))

    n_extra = len(extra)
    single = epilogue is None
    if single:
        out_dtypes = [out_dtype]

    n_out = len(out_dtypes)

    def body(a_ref, b_ref, *rest):
        extra_refs, o_refs = rest[:n_extra], rest[n_extra:n_extra + n_out]

        def dot():
            return lax.dot_general(a_ref[...].astype(BF16), b_ref[...].astype(BF16), dn, preferred_element_type=F32)

        def finish(acc):
            res = (acc,) if single else epilogue(acc, *[r[...] for r in extra_refs])
            for o_ref, val in zip(o_refs, res):
                o_ref[...] = val.astype(o_ref.dtype)

        if nk == 1:
            finish(dot())
            return
        acc_ref = rest[-1]
        k = pl.program_id(2)

        @pl.when(k == 0)
        def _():
            acc_ref[...] = jnp.zeros_like(acc_ref)

        acc_ref[...] += dot()

        @pl.when(k == nk - 1)
        def _():
            finish(acc_ref[...])

    a_spec = pl.BlockSpec((tk, tm), lambda i, j, k: (k, i)) if ta else pl.BlockSpec((tm, tk), lambda i, j, k: (i, k))
    b_spec = pl.BlockSpec((tn, tk), lambda i, j, k: (j, k)) if tb else pl.BlockSpec((tk, tn), lambda i, j, k: (k, j))
    mn_spec = pl.BlockSpec((tm, tn), lambda i, j, k: (i, j))
    outs = _pcall(
        body, name=name, grid=(M // tm, N // tn, nk), in_specs=[a_spec, b_spec] + [mn_spec] * n_extra,
        out_specs=[mn_spec] * len(out_dtypes),
        out_shape=[jax.ShapeDtypeStruct((M, N), d) for d in out_dtypes],
        scratch_shapes=[pltpu.VMEM((tm, tn), F32)] if nk > 1 else [],
        compiler_params=_cp(("parallel", "parallel", "arbitrary")),
    )(a, b, *extra)
    return outs[0] if single else outs


def _linear(name, out_dtype=F32):
    @jax.custom_vjp
    def mm(a, w):
        return _matmul(a, w, out_dtype=out_dtype, name=name + "_fwd")

    def fwd(a, w):
        return mm(a, w), (a, w)

    def bwd(res, dy):
        a, w = res
        da = _matmul(dy, w, tb=True, out_dtype=a.dtype, name=name + "_dgrad")
        dw = _matmul(a, dy, ta=True, out_dtype=w.dtype, name=name + "_wgrad")
        return da, dw

    mm.defvjp(fwd, bwd)
    return mm


def _ada_wgrad_call(c_act, dmod):
    P, D = c_act.shape
    _, L, C = dmod.shape
    tm = _pick(D, (1024, 512, 256, 128))
    tn = _pick(C, (1024, 512, 256, 128))
    nj = C // tn

    def body(a_ref, b_ref, o_ref):
        o_ref[...] = lax.dot_general(a_ref[...].astype(BF16), b_ref[...].astype(BF16), (((0,), (0,)), ((), ())),
                                     preferred_element_type=F32)

    return _pcall(
        body, name="ada_wgrad", grid=(L, D // tm, nj),
        in_specs=[pl.BlockSpec((P, tm), lambda l, i, j: (0, i)), pl.BlockSpec((P, tn), lambda l, i, j: (0, l * nj + j))],
        out_specs=pl.BlockSpec((None, tm, tn), lambda l, i, j: (l, i, j)),
        out_shape=jax.ShapeDtypeStruct((L, D, C), F32), compiler_params=_cp(("parallel", "parallel", "parallel")),
    )(c_act, dmod.reshape(P, L * C))


def _relu2_epilogue(acc):
    return acc, jnp.square(jnp.maximum(acc, 0.0))


def _relu2_grad_epilogue(acc, h1):
    return (acc * (2.0 * jnp.maximum(h1.astype(F32), 0.0)),)


def _mlp(name):
    def run(h, w1, w2):
        h1, act = _matmul(h, w1, name=name + "_up_fwd", epilogue=_relu2_epilogue, out_dtypes=[BF16, BF16])
        return _matmul(act, w2, name=name + "_down_fwd"), h1, act

    @jax.custom_vjp
    def mlp(h, w1, w2):
        return run(h, w1, w2)[0]

    def fwd(h, w1, w2):
        y, h1, act = run(h, w1, w2)
        return y, (h, w1, w2, h1, act)

    def bwd(res, dy):
        h, w1, w2, h1, act = res
        dw2 = _matmul(act, dy, ta=True, out_dtype=w2.dtype, name=name + "_down_wgrad")
        (dh1,) = _matmul(dy, w2, tb=True, name=name + "_down_dgrad", extra=(h1,), epilogue=_relu2_grad_epilogue, out_dtypes=[BF16])
        dw1 = _matmul(h, dh1, ta=True, out_dtype=w1.dtype, name=name + "_up_wgrad")
        dh = _matmul(dh1, w1, tb=True, out_dtype=h.dtype, name=name + "_up_dgrad")
        return dh, dw1, dw2

    mlp.defvjp(fwd, bwd)
    return mlp


def _rowop(name, f, *, n_x, n_nd, n_p, x_w, nd_w, nd_shared, p_per_group, out_w, out_dtypes, tile, groups=1):
    G = groups

    def specs(S):
        t = min(tile, S)
        xs = [pl.BlockSpec((t, w), lambda g, r: (r, g)) for w in x_w]
        nds = [pl.BlockSpec((t, w), (lambda g, r: (r, 0)) if sh else (lambda g, r: (r, g))) for w, sh in zip(nd_w, nd_shared)]
        outs = [pl.BlockSpec((t, w), lambda g, r: (r, g)) for w in out_w]
        return t, xs, nds, outs

    def pspecs(ps):
        return [pl.BlockSpec((None,) + p.shape[1:], (lambda g, r: (g, 0, 0)) if pg else (lambda g, r: (0, 0, 0)))
                for p, pg in zip(ps, p_per_group)]

    def fwd_call(xs, nds, ps):
        S = xs[0].shape[0]
        t, xsp, ndsp, osp = specs(S)

        def body(*refs):
            ins, outs = refs[:n_x + n_nd + n_p], refs[n_x + n_nd + n_p:]
            vals = [r[...].astype(F32) for r in ins]
            res = f(*vals)
            for o, r in zip(res, outs):
                r[...] = o.astype(r.dtype)

        return _pcall(
            body, name=name + "_fwd", grid=(G, S // t), in_specs=xsp + ndsp + pspecs(ps), out_specs=osp,
            out_shape=[jax.ShapeDtypeStruct((S, G * w), d) for w, d in zip(out_w, out_dtypes)],
            compiler_params=_cp(("parallel", "parallel")),
        )(*xs, *nds, *ps)

    def bwd_call(xs, nds, ps, douts):
        S = xs[0].shape[0]
        t, xsp, ndsp, osp = specs(S)
        n_in = n_x + n_nd + n_p + len(out_w)

        def body(*refs):
            ins, outs = refs[:n_in], refs[n_in:]
            xv = [r[...].astype(F32) for r in ins[:n_x]]
            ndv = [r[...].astype(F32) for r in ins[n_x:n_x + n_nd]]
            pv = [r[...].astype(F32) for r in ins[n_x + n_nd:n_x + n_nd + n_p]]
            dov = tuple(r[...].astype(F32) for r in ins[n_x + n_nd + n_p:])
            _, vjp = jax.vjp(lambda *a: tuple(f(*a[:n_x], *ndv, *a[n_x:])), *xv, *pv)
            cts = vjp(dov)
            for i in range(n_x):
                outs[i][...] = cts[i].astype(outs[i].dtype)
            g, r = pl.program_id(0), pl.program_id(1)
            for i in range(n_p):
                first = (r == 0) if p_per_group[i] else jnp.logical_and(g == 0, r == 0)
                ref, ct = outs[n_x + i], cts[n_x + i]

                @pl.when(first)
                def _(ref=ref, ct=ct):
                    ref[...] = ct

                @pl.when(jnp.logical_not(first))
                def _(ref=ref, ct=ct):
                    ref[...] += ct

        return _pcall(
            body, name=name + "_bwd", grid=(G, S // t), in_specs=xsp + ndsp + pspecs(ps) + osp,
            out_specs=xsp + pspecs(ps),
            out_shape=[jax.ShapeDtypeStruct(x.shape, x.dtype) for x in xs] + [jax.ShapeDtypeStruct(p.shape, F32) for p in ps],
            compiler_params=_cp(("arbitrary", "arbitrary")),
        )(*xs, *nds, *ps, *douts)

    @jax.custom_vjp
    def op(xs, nds, ps):
        return tuple(fwd_call(xs, nds, ps))

    def op_fwd(xs, nds, ps):
        return op(xs, nds, ps), (xs, nds, ps)

    def op_bwd(res, douts):
        xs, nds, ps = res
        out = bwd_call(xs, nds, ps, douts)
        return tuple(out[:n_x]), tuple(jnp.zeros_like(n) for n in nds), tuple(out[n_x:])

    op.defvjp(op_fwd, op_bwd)
    return op


def _rms_rows(x):
    return x * lax.rsqrt(jnp.mean(x * x, axis=-1, keepdims=True) + EPS)


def _f_norm_mod(x, g, shift, scale):
    return ((_rms_rows(x) * g) * (1.0 + scale) + shift,)


def _f_resid_norm_mod(x, mix, gate, g, shift, scale):
    xn = x + gate * mix
    return xn, (_rms_rows(xn) * g) * (1.0 + scale) + shift


def _f_resid(x, mix, gate):
    return (x + gate * mix,)


def _f_rms(x, g):
    return (_rms_rows(x) * g,)


def _f_glu(a, g):
    return (a * jax.nn.sigmoid(g),)


def _f_ln_silu(z, g, b):
    mu = jnp.mean(z, axis=-1, keepdims=True)
    zc = z - mu
    var = jnp.mean(zc * zc, axis=-1, keepdims=True)
    y = zc * lax.rsqrt(var + EPS) * g + b
    return (y * jax.nn.sigmoid(y),)


def _f_silu(x):
    return (x * jax.nn.sigmoid(x),)


@jax.custom_vjp
def _bdot_chunks(a, b):
    return lax.dot_general(a.astype(BF16), b.astype(BF16), (((2,), (1,)), ((0,), (0,))), preferred_element_type=F32)


def _bdot_chunks_fwd(a, b):
    return _bdot_chunks(a, b), (a, b)


def _bdot_chunks_bwd(res, ct):
    a, b = res
    c16 = ct.astype(BF16)
    da = lax.dot_general(c16, b.astype(BF16), (((2,), (2,)), ((0,), (0,))), preferred_element_type=F32)
    db = lax.dot_general(a.astype(BF16), c16, (((1,), (1,)), ((0,), (0,))), preferred_element_type=F32)
    return da, db


_bdot_chunks.defvjp(_bdot_chunks_fwd, _bdot_chunks_bwd)


def _f_sgu(u, v, ng, w, bexp):
    n = u.shape[0] // CHUNK
    vn = (_rms_rows(jax.nn.gelu(v)) * ng).reshape(n, CHUNK, GROUP)
    row = lax.broadcasted_iota(jnp.int32, w.shape, 0)
    col = lax.broadcasted_iota(jnp.int32, w.shape, 1)
    wm = jnp.broadcast_to(jnp.where(row >= col, w, 0.0)[None], (n, CHUNK, CHUNK))
    mixed = _bdot_chunks(wm, vn) + bexp[None]
    return (jax.nn.gelu(u) * mixed.reshape(n * CHUNK, GROUP),)


def _lo_mask():
    return lax.broadcasted_iota(jnp.int32, (1, HEAD_W), 1) < NOPE


def _f_qhead(x, tab, g):
    lo = _lo_mask()
    x2 = x * x
    ms_lo = jnp.sum(jnp.where(lo, x2, 0.0), axis=-1, keepdims=True) * (1.0 / NOPE)
    ms_hi = jnp.sum(jnp.where(lo, 0.0, x2), axis=-1, keepdims=True) * (1.0 / (HEAD_W - NOPE))
    r = jnp.where(lo, lax.rsqrt(ms_lo + EPS), lax.rsqrt(ms_hi + EPS))
    return (((x * r) * g) * tab,)


def _f_kvhead(x, g):
    lo = _lo_mask()
    ms = jnp.sum(jnp.where(lo, x * x, 0.0), axis=-1, keepdims=True) * (1.0 / NOPE)
    return (jnp.where(lo, (x * lax.rsqrt(ms + EPS)) * g, x),)


def _f_krope(a, b, ta, tb, ga, gb):
    r = lax.rsqrt(jnp.mean(a * a, axis=-1, keepdims=True) + EPS)
    return (((a * r) * ga) * ta + ((b * r) * gb) * tb,)


def _conv_fwd_call(y, w, b, name):
    S, C = y.shape
    cw = 128
    rt = _pick(S, (128,))
    w = jnp.pad(w, ((0, CONV_PAD - CONV_W), (0, 0)))

    def body(y_ref, w_ref, b_ref, z_ref, pad_ref):
        pad_ref[pl.ds(0, CONV_PAD), :] = jnp.zeros((CONV_PAD, cw), F32)
        pad_ref[pl.ds(CONV_PAD, S), :] = y_ref[...]
        wv = w_ref[...]
        bv = b_ref[...]

        def chunk(ci, carry):
            r0 = pl.multiple_of(ci * rt, rt)
            win = pad_ref[pl.ds(r0, rt + CONV_PAD), :]
            acc = jnp.broadcast_to(bv, (rt, cw))
            for k in range(CONV_W):
                off = CONV_PAD - (CONV_W - 1) + k
                sh = win if off == 0 else pltpu.roll(win, rt + CONV_PAD - off, axis=0)
                acc = acc + wv[k:k + 1, :] * sh[:rt, :]
            z_ref[pl.ds(r0, rt), :] = acc
            return carry

        lax.fori_loop(0, S // rt, chunk, 0)

    return _pcall(
        body, name=name, grid=(C // cw,),
        in_specs=[pl.BlockSpec((S, cw), lambda j: (0, j)), pl.BlockSpec((CONV_PAD, cw), lambda j: (0, j)),
                  pl.BlockSpec((1, cw), lambda j: (0, j))],
        out_specs=pl.BlockSpec((S, cw), lambda j: (0, j)),
        out_shape=jax.ShapeDtypeStruct((S, C), F32),
        scratch_shapes=[pltpu.VMEM((S + CONV_PAD, cw), F32)],
        compiler_params=_cp(("parallel",)),
    )(y, w, b)


def _conv_bwd_call(y, w, dz, name):
    S, C = y.shape
    cw = 128
    rt = _pick(S, (128,))
    w = jnp.pad(w, ((0, CONV_PAD - CONV_W), (0, 0)))

    def body(y_ref, w_ref, dz_ref, dy_ref, dw_ref, db_ref, ypad_ref, zpad_ref):
        ypad_ref[pl.ds(0, CONV_PAD), :] = jnp.zeros((CONV_PAD, cw), F32)
        ypad_ref[pl.ds(CONV_PAD, S), :] = y_ref[...]
        zpad_ref[pl.ds(0, S), :] = dz_ref[...]
        zpad_ref[pl.ds(S, CONV_PAD), :] = jnp.zeros((CONV_PAD, cw), F32)
        dw_ref[...] = jnp.zeros_like(dw_ref)
        wv = w_ref[...]

        def chunk(ci, dbacc):
            r0 = pl.multiple_of(ci * rt, rt)
            ywin = ypad_ref[pl.ds(r0, rt + CONV_PAD), :]
            zwin = zpad_ref[pl.ds(r0, rt + CONV_PAD), :]
            dzc = zwin[:rt, :]
            acc = jnp.zeros((rt, cw), F32)
            for k in range(CONV_W):
                off_z = (CONV_W - 1) - k
                zs = zwin if off_z == 0 else pltpu.roll(zwin, rt + CONV_PAD - off_z, axis=0)
                acc = acc + wv[k:k + 1, :] * zs[:rt, :]
                off_y = CONV_PAD - (CONV_W - 1) + k
                ys = pltpu.roll(ywin, rt + CONV_PAD - off_y, axis=0)
                dw_ref[k:k + 1, :] += jnp.sum(dzc * ys[:rt, :], axis=0, keepdims=True)
            dy_ref[pl.ds(r0, rt), :] = acc
            return dbacc + jnp.sum(dzc, axis=0, keepdims=True)

        db_ref[...] = lax.fori_loop(0, S // rt, chunk, jnp.zeros((1, cw), F32))

    dy, dw, db = _pcall(
        body, name=name, grid=(C // cw,),
        in_specs=[pl.BlockSpec((S, cw), lambda j: (0, j)), pl.BlockSpec((CONV_PAD, cw), lambda j: (0, j)),
                  pl.BlockSpec((S, cw), lambda j: (0, j))],
        out_specs=[pl.BlockSpec((S, cw), lambda j: (0, j)), pl.BlockSpec((CONV_PAD, cw), lambda j: (0, j)),
                   pl.BlockSpec((1, cw), lambda j: (0, j))],
        out_shape=[jax.ShapeDtypeStruct((S, C), F32), jax.ShapeDtypeStruct((CONV_PAD, C), F32),
                   jax.ShapeDtypeStruct((1, C), F32)],
        scratch_shapes=[pltpu.VMEM((S + CONV_PAD, cw), F32), pltpu.VMEM((S + CONV_PAD, cw), F32)],
        compiler_params=_cp(("parallel",)),
    )(y, w, dz)
    return dy, dw[:CONV_W], db


def _conv_op(name):
    @jax.custom_vjp
    def conv(y, w, b):
        return _conv_fwd_call(y, w, b, name + "_fwd")

    def fwd(y, w, b):
        return conv(y, w, b), (y, w)

    def bwd(res, dz):
        y, w = res
        return _conv_bwd_call(y, w, dz, name + "_bwd")

    conv.defvjp(fwd, bwd)
    return conv


def _attn_tile(S):
    return _pick(S, (512, 256, 128))


def _attn_fwd_call(q, kv, kr, name):
    S = q.shape[0]
    t = _attn_tile(S)

    def body(q_ref, kv_ref, kr_ref, o_ref, lse_ref):
        i = pl.program_id(1)
        qv = q_ref[...]

        def update(jb, carry, diagonal):
            m, l, acc = carry
            off = pl.multiple_of(jb * t, t)
            kc = jnp.concatenate([kv_ref[pl.ds(off, t), pl.ds(0, NOPE)], kr_ref[pl.ds(off, t), :]], axis=-1)
            vv = kv_ref[pl.ds(off, t), pl.ds(NOPE, VDIM)]
            s = lax.dot_general(qv, kc, (((1,), (1,)), ((), ())), preferred_element_type=F32)
            if diagonal:
                row = lax.broadcasted_iota(jnp.int32, (t, t), 0)
                col = lax.broadcasted_iota(jnp.int32, (t, t), 1)
                s = jnp.where(col <= row, s, -jnp.inf)
            mn = jnp.maximum(m, jnp.max(s, axis=-1, keepdims=True))
            p = jnp.exp2((s - mn) * ATTN_SCALE_LOG2E)
            al = jnp.exp2((m - mn) * ATTN_SCALE_LOG2E)
            l = al * l + jnp.sum(p, axis=-1, keepdims=True)
            acc = al * acc + jnp.dot(p.astype(BF16), vv, preferred_element_type=F32)
            return mn, l, acc

        init = (jnp.full((t, 1), -jnp.inf, F32), jnp.zeros((t, 1), F32), jnp.zeros((t, VDIM), F32))
        carry = lax.fori_loop(0, i, lambda jb, cr: update(jb, cr, False), init)
        m, l, acc = update(i, carry, True)
        o_ref[...] = (acc / l).astype(o_ref.dtype)
        lse_ref[...] = m * ATTN_SCALE + jnp.log(l)

    return _pcall(
        body, name=name, grid=(N_HEADS, S // t),
        in_specs=[pl.BlockSpec((t, HEAD_W), lambda h, i: (i, h)), pl.BlockSpec((S, HEAD_W), lambda h, i: (0, h)),
                  pl.BlockSpec((S, 128), lambda h, i: (0, 0))],
        out_specs=[pl.BlockSpec((t, VDIM), lambda h, i: (i, h)), pl.BlockSpec((None, t, 1), lambda h, i: (h, i, 0))],
        out_shape=[jax.ShapeDtypeStruct((S, N_HEADS * VDIM), BF16), jax.ShapeDtypeStruct((N_HEADS, S, 1), F32)],
        compiler_params=_cp(("parallel", "parallel")),
    )(q, kv, kr)


def _attn_dd_call(o, do, name):
    S = o.shape[0]
    t = _attn_tile(S)

    def body(o_ref, do_ref, dd_ref):
        head = lax.broadcasted_iota(jnp.int32, (1, N_HEADS), 1)
        acc = jnp.zeros((t, N_HEADS), F32)
        for h in range(N_HEADS):
            cols = pl.ds(h * VDIM, VDIM)
            part = jnp.sum(do_ref[:, cols].astype(F32) * o_ref[:, cols].astype(F32), axis=-1, keepdims=True)
            acc = acc + jnp.where(head == h, part, 0.0)
        dd_ref[...] = acc

    return _pcall(
        body, name=name, grid=(S // t,),
        in_specs=[pl.BlockSpec((t, N_HEADS * VDIM), lambda i: (i, 0)), pl.BlockSpec((t, N_HEADS * VDIM), lambda i: (i, 0))],
        out_specs=pl.BlockSpec((t, N_HEADS), lambda i: (i, 0)),
        out_shape=jax.ShapeDtypeStruct((S, N_HEADS), F32), compiler_params=_cp(("parallel",)),
    )(o, do)


def _attn_bwd_call(q, kv, kr, do, lse, dd, name):
    S = q.shape[0]
    t = _attn_tile(S)
    nq = S // t

    def body(q_ref, kv_ref, kr_ref, do_ref, lse_ref, dd_ref, dq_ref, dkv_ref, dkr_ref, dq_acc):
        j = pl.program_id(1)

        @pl.when(j == 0)
        def _():
            dq_acc[...] = jnp.zeros_like(dq_acc)

        kc = jnp.concatenate([kv_ref[:, pl.ds(0, NOPE)], kr_ref[...]], axis=-1)
        vv = kv_ref[:, pl.ds(NOPE, VDIM)]

        def update(ib, carry, diagonal):
            dkc, dv = carry
            off = pl.multiple_of(ib * t, t)
            qv = q_ref[pl.ds(off, t), :]
            dov = do_ref[pl.ds(off, t), :]
            st = lax.dot_general(kc, qv, (((1,), (1,)), ((), ())), preferred_element_type=F32)
            if diagonal:
                key = lax.broadcasted_iota(jnp.int32, (t, t), 0)
                qry = lax.broadcasted_iota(jnp.int32, (t, t), 1)
                st = jnp.where(key <= qry, st, -jnp.inf)
            pt = jnp.exp2(st * ATTN_SCALE_LOG2E - lse_ref[pl.ds(ib, 1), :])
            dv = dv + jnp.dot(pt.astype(BF16), dov, preferred_element_type=F32)
            dpt = lax.dot_general(vv, dov, (((1,), (1,)), ((), ())), preferred_element_type=F32)
            dst = (pt * (dpt - dd_ref[pl.ds(ib, 1), :]) * ATTN_SCALE).astype(BF16)
            dkc = dkc + jnp.dot(dst, qv, preferred_element_type=F32)
            dq_acc[pl.ds(off, t), :] += lax.dot_general(dst, kc, (((0,), (0,)), ((), ())), preferred_element_type=F32)
            return dkc, dv

        carry = update(j, (jnp.zeros((t, HEAD_W), F32), jnp.zeros((t, VDIM), F32)), True)
        dkc, dv = lax.fori_loop(j + 1, nq, lambda ib, cr: update(ib, cr, False), carry)
        dkv_ref[:, pl.ds(0, NOPE)] = dkc[:, :NOPE].astype(dkv_ref.dtype)
        dkv_ref[:, pl.ds(NOPE, VDIM)] = dv.astype(dkv_ref.dtype)
        dkr_ref[...] = dkc[:, NOPE:]

        @pl.when(j == nq - 1)
        def _():
            dq_ref[...] = dq_acc[...].astype(dq_ref.dtype)

    dq, dkv, dkr_heads = _pcall(
        body, name=name, grid=(N_HEADS, nq),
        in_specs=[pl.BlockSpec((S, HEAD_W), lambda h, j: (0, h)), pl.BlockSpec((t, HEAD_W), lambda h, j: (j, h)),
                  pl.BlockSpec((t, 128), lambda h, j: (j, 0)), pl.BlockSpec((S, VDIM), lambda h, j: (0, h)),
                  pl.BlockSpec((None, nq, t), lambda h, j: (h, 0, 0)), pl.BlockSpec((None, nq, t), lambda h, j: (h, 0, 0))],
        out_specs=[pl.BlockSpec((S, HEAD_W), lambda h, j: (0, h)), pl.BlockSpec((t, HEAD_W), lambda h, j: (j, h)),
                   pl.BlockSpec((None, t, 128), lambda h, j: (h, j, 0))],
        out_shape=[jax.ShapeDtypeStruct(q.shape, q.dtype), jax.ShapeDtypeStruct(kv.shape, kv.dtype),
                   jax.ShapeDtypeStruct((N_HEADS, S, 128), F32)],
        scratch_shapes=[pltpu.VMEM((S, HEAD_W), F32)],
        compiler_params=_cp(("parallel", "arbitrary")),
    )(q, kv, kr, do, (lse * LOG2E).reshape(N_HEADS, nq, t), dd.T.reshape(N_HEADS, nq, t))

    def sum_body(p_ref, o_ref):
        acc = p_ref[0]
        for h in range(1, N_HEADS):
            acc = acc + p_ref[h]
        o_ref[...] = acc.astype(o_ref.dtype)

    dkr = _pcall(
        sum_body, name=name + "_rope_sum", grid=(nq,),
        in_specs=[pl.BlockSpec((N_HEADS, t, 128), lambda i: (0, i, 0))], out_specs=pl.BlockSpec((t, 128), lambda i: (i, 0)),
        out_shape=jax.ShapeDtypeStruct((S, 128), kr.dtype), compiler_params=_cp(("parallel",)),
    )(dkr_heads)
    return dq, dkv, dkr


def _attn_op(name):
    @jax.custom_vjp
    def attn(q, kv, kr):
        return _attn_fwd_call(q, kv, kr, name + "_fwd")[0]

    def fwd(q, kv, kr):
        o, lse = _attn_fwd_call(q, kv, kr, name + "_fwd")
        return o, (q, kv, kr, o, lse)

    def bwd(res, do):
        q, kv, kr, o, lse = res
        dd = _attn_dd_call(o, do, name + "_dd")
        return _attn_bwd_call(q, kv, kr, do, lse, dd, name + "_bwd")

    attn.defvjp(fwd, bwd)
    return attn


def _loss_call(y, target):
    S, D = y.shape
    t = _pick(S, (256, 128))

    def body(y_ref, t_ref, dy_ref, loss_ref):
        @pl.when(pl.program_id(0) == 0)
        def _():
            loss_ref[...] = jnp.zeros_like(loss_ref)

        e = y_ref[...] - t_ref[...]
        dy_ref[...] = e * (1.0 / D)
        loss_ref[...] += 0.5 * jnp.sum(jnp.mean(e * e, axis=-1, keepdims=True), axis=0, keepdims=True)

    return _pcall(
        body, name="loss_head", grid=(S // t,),
        in_specs=[pl.BlockSpec((t, D), lambda i: (i, 0)), pl.BlockSpec((t, D), lambda i: (i, 0))],
        out_specs=[pl.BlockSpec((t, D), lambda i: (i, 0)), pl.BlockSpec((1, 1), lambda i: (0, 0))],
        out_shape=[jax.ShapeDtypeStruct((S, D), F32), jax.ShapeDtypeStruct((1, 1), F32)],
        compiler_params=_cp(("arbitrary",)),
    )(y, target)


def _adamw_call(w, g, m, v, name):
    shape = w.shape
    C = shape[-1]
    R = int(np.prod(shape[:-1]))
    tr = R
    for cand in (512, 256, 128, 64, 32, 16, 8):
        if R % cand == 0 and cand * C * 4 <= 2 * 1024 * 1024:
            tr = cand
            break
    c1 = 1.0 - ADAM_B1 ** ADAM_STEP
    c2 = 1.0 - ADAM_B2 ** ADAM_STEP

    def body(w_ref, g_ref, m_ref, v_ref, d_ref, mo_ref, vo_ref):
        gv = g_ref[...]
        mn = ADAM_B1 * m_ref[...] + (1.0 - ADAM_B1) * gv
        vn = ADAM_B2 * v_ref[...] + (1.0 - ADAM_B2) * (gv * gv)
        d_ref[...] = -ADAM_LR * ((mn / c1) / (jnp.sqrt(vn / c2) + ADAM_EPS) + ADAM_WD * w_ref[...])
        mo_ref[...] = mn
        vo_ref[...] = vn

    spec = pl.BlockSpec((tr, C), lambda i: (i, 0))
    outs = _pcall(
        body, name=name, grid=(R // tr,), in_specs=[spec] * 4, out_specs=[spec] * 3,
        out_shape=[jax.ShapeDtypeStruct((R, C), F32)] * 3, compiler_params=_cp(("parallel",)),
    )(*[a.reshape(R, C) for a in (w, g, m, v)])
    return [o.reshape(shape) for o in outs]


def _gather_small(x2d, *, reduce, name):
    R, C = x2d.shape

    def body(x_ref, out_ref, *scratch):
        if reduce:
            buf_ref, send_sems, recv_sems, local_sem = scratch
        else:
            buf_ref = out_ref
            send_sems, recv_sems, local_sem = scratch
        x, y, c = lax.axis_index("x"), lax.axis_index("y"), lax.axis_index("c")
        me, sibling = (x, y, c), (x, y, 1 - c)
        chips = [(1 - x, y), (x, 1 - y), (1 - x, 1 - y)]

        def rows(px, py, pc):
            return buf_ref.at[pl.ds((4 * px + 2 * py + pc) * R, R), :]

        def copy(k, block, to, src=None):
            return pltpu.make_async_remote_copy(
                src_ref=rows(*block) if src is None else src, dst_ref=rows(*block),
                send_sem=send_sems.at[k], recv_sem=recv_sems.at[k], device_id=to, device_id_type=MESH)

        mine = pltpu.make_async_copy(x_ref, rows(*me), local_sem)
        mine.start()
        first = [copy(0, me, sibling, src=x_ref)]
        first += [copy(1 + j, me, (*chip, c), src=x_ref) for j, chip in enumerate(chips)]
        for cp in first:
            cp.start()
        passed = [copy(4 + j, (*chip, c), sibling) for j, chip in enumerate(chips)]
        for j, chip in enumerate(chips):
            copy(1 + j, (*chip, c), me).wait_recv()
            passed[j].start()
        copy(0, sibling, me).wait_recv()
        for j, chip in enumerate(chips):
            copy(4 + j, (*chip, 1 - c), me).wait_recv()
        for cp in first + passed:
            cp.wait_send()
        mine.wait()
        if reduce:
            acc = buf_ref[pl.ds(0, R), :]
            for d in range(1, N_DEV):
                acc = acc + buf_ref[pl.ds(d * R, R), :]
            out_ref[...] = acc

    scratch = [pltpu.SemaphoreType.DMA((7,)), pltpu.SemaphoreType.DMA((7,)), pltpu.SemaphoreType.DMA]
    if reduce:
        scratch = [pltpu.VMEM((N_DEV * R, C), F32)] + scratch
    return _pcall(
        body, name=name, out_shape=jax.ShapeDtypeStruct((R if reduce else N_DEV * R, C), F32),
        in_specs=[pl.BlockSpec(memory_space=pltpu.VMEM)], out_specs=pl.BlockSpec(memory_space=pltpu.VMEM),
        scratch_shapes=scratch, compiler_params=pltpu.CompilerParams(vmem_limit_bytes=VMEM_LIMIT),
    )(x2d)


def _pack(arrs):
    flat = jnp.concatenate([a.reshape(-1).astype(F32) for a in arrs])
    n = flat.shape[0]
    unit = 8 * SMALL_COLS
    flat = jnp.pad(flat, (0, (-n) % unit))
    return flat.reshape(-1, SMALL_COLS)


def _unpack(flat, shapes):
    out, o = [], 0
    for s in shapes:
        n = int(np.prod(s))
        out.append(flat[o:o + n].reshape(s))
        o += n
    return out


def _all_gather_small(arrs, name):
    p = _pack(arrs)
    g = _gather_small(p, reduce=False, name=name).reshape(N_DEV, -1)
    out, o = [], 0
    for a in arrs:
        n = int(np.prod(a.shape))
        out.append(g[:, o:o + n].reshape((N_DEV,) + a.shape))
        o += n
    return out


def _all_reduce_small(arrs, name):
    p = _pack(arrs)
    return _unpack(_gather_small(p, reduce=True, name=name).reshape(-1), [a.shape for a in arrs])


def _half_rows(shard_shape):
    return shard_shape[0] // 2


def _half_slot(ref, kind, rh, cols, half, slot):
    if kind == "col":
        return ref.at[pl.ds(half * rh, rh), pl.ds(slot * cols, cols)]
    return ref.at[pl.ds((slot * 2 + half) * rh, rh), :]


def _full_shape(kind, shard_shape):
    r, c = shard_shape
    return (r, c * N_CHIPS) if kind == "col" else (r * N_CHIPS, c)


def _mesh_pos():
    x, y, c = lax.axis_index("x"), lax.axis_index("y"), lax.axis_index("c")
    return x, y, c


def _handshake_chip_peers(x, y, c):
    barrier = pltpu.get_barrier_semaphore()
    for peer in [(x, y, 1 - c)] + [(x ^ fx, y ^ fy, c) for fx, fy in CHIP_RELS]:
        pl.semaphore_signal(barrier, inc=1, device_id=peer, device_id_type=MESH)
    pl.semaphore_wait(barrier, 1 + len(CHIP_RELS))


def _gather_big(shards, kinds, name):
    n = len(shards)

    def body(*refs):
        srcs, outs = refs[:n], refs[n:2 * n]
        ici_send, ici_recv, d2d_send, d2d_recv, own_send, own_recv = refs[2 * n:]
        x, y, c = _mesh_pos()
        _handshake_chip_peers(x, y, c)
        my = 2 * x + y
        geo = [(_half_rows(s.shape), s.shape[1]) for s in shards]

        def region(a, half, slot):
            return _half_slot(outs[a], kinds[a], geo[a][0], geo[a][1], half, slot)

        def slot_of(a, slot):
            rh, cols = geo[a]
            if kinds[a] == "col":
                return outs[a].at[:, pl.ds(slot * cols, cols)]
            return outs[a].at[pl.ds(slot * 2 * rh, 2 * rh), :]

        sends = []
        for a in range(n):
            cp = pltpu.make_async_remote_copy(
                src_ref=srcs[a], dst_ref=slot_of(a, my), send_sem=own_send.at[a], recv_sem=own_recv.at[a],
                device_id=(x, y, 1 - c), device_id_type=MESH)
            cp.start()
            sends.append(cp)
        for a in range(n):
            rh = geo[a][0]
            for r, (fx, fy) in enumerate(CHIP_RELS):
                cp = pltpu.make_async_remote_copy(
                    src_ref=srcs[a].at[pl.ds(c * rh, rh), :], dst_ref=region(a, c, my),
                    send_sem=ici_send.at[3 * a + r], recv_sem=ici_recv.at[3 * a + r],
                    device_id=(x ^ fx, y ^ fy, c), device_id_type=MESH)
                cp.start()
                sends.append(cp)
        passed = []
        for a in range(n):
            for r, (fx, fy) in enumerate(CHIP_RELS):
                frm = 2 * (x ^ fx) + (y ^ fy)
                landed = region(a, c, frm)
                pltpu.make_async_remote_copy(
                    src_ref=landed, dst_ref=landed, send_sem=ici_send.at[3 * a + r], recv_sem=ici_recv.at[3 * a + r],
                    device_id=(x, y, c), device_id_type=MESH).wait_recv()
                cp = pltpu.make_async_remote_copy(
                    src_ref=landed, dst_ref=landed, send_sem=d2d_send.at[3 * a + r], recv_sem=d2d_recv.at[3 * a + r],
                    device_id=(x, y, 1 - c), device_id_type=MESH)
                cp.start()
                passed.append(cp)
        for a in range(n):
            for r, (fx, fy) in enumerate(CHIP_RELS):
                frm = 2 * (x ^ fx) + (y ^ fy)
                other = region(a, 1 - c, frm)
                pltpu.make_async_remote_copy(
                    src_ref=other, dst_ref=other, send_sem=d2d_send.at[3 * a + r], recv_sem=d2d_recv.at[3 * a + r],
                    device_id=(x, y, c), device_id_type=MESH).wait_recv()
        for a in range(n):
            pltpu.make_async_remote_copy(
                src_ref=srcs[a], dst_ref=slot_of(a, my), send_sem=own_send.at[a], recv_sem=own_recv.at[a],
                device_id=(x, y, c), device_id_type=MESH).wait_recv()
        for cp in sends + passed:
            cp.wait_send()

    return pl.kernel(
        body, name=name,
        out_type=[jax.ShapeDtypeStruct(_full_shape(k, s.shape), s.dtype) for s, k in zip(shards, kinds)],
        mesh=plsc.ScalarSubcoreMesh(axis_name="sequencer", num_cores=1),
        scratch_types=[pltpu.SemaphoreType.DMA((3 * n,))] * 4 + [pltpu.SemaphoreType.DMA((n,))] * 2,
        compiler_params=pltpu.CompilerParams(collective_id=GATHER_COLLECTIVE_ID),
    )(*shards)


def _swap_halves(fulls, kinds, shard_shapes, name):
    n = len(fulls)
    geo = [(_half_rows(s), s[1]) for s in shard_shapes]

    def body(*refs):
        srcs, outs = refs[:n], refs[n:2 * n]
        send_sems, recv_sems = refs[2 * n:]
        x, y, c = _mesh_pos()
        barrier = pltpu.get_barrier_semaphore()
        pl.semaphore_signal(barrier, inc=1, device_id=(x, y, 1 - c), device_id_type=MESH)
        pl.semaphore_wait(barrier, 1)
        cps = []
        for a in range(n):
            for s in range(N_CHIPS):
                cp = pltpu.make_async_remote_copy(
                    src_ref=_half_slot(srcs[a], kinds[a], geo[a][0], geo[a][1], 1 - c, s), dst_ref=outs[a].at[s],
                    send_sem=send_sems.at[N_CHIPS * a + s], recv_sem=recv_sems.at[N_CHIPS * a + s],
                    device_id=(x, y, 1 - c), device_id_type=MESH)
                cp.start()
                cps.append(cp)
        for cp in cps:
            cp.wait()

    return pl.kernel(
        body, name=name,
        out_type=[jax.ShapeDtypeStruct((N_CHIPS,) + g, f.dtype) for g, f in zip(geo, fulls)],
        mesh=plsc.ScalarSubcoreMesh(axis_name="sequencer", num_cores=1),
        scratch_types=[pltpu.SemaphoreType.DMA((N_CHIPS * n,))] * 2,
        compiler_params=pltpu.CompilerParams(collective_id=SWAP_COLLECTIVE_ID),
    )(*fulls)


def _add_own_half(full, recv, kind, shard_shape, cidx, name):
    rh, cols = _half_rows(shard_shape), shard_shape[1]
    tr = _pick(rh, (256, 128, 64, 32, 16))
    nb = rh // tr

    def body(c_ref, f_ref, r_ref, o_ref):
        o_ref[...] = (f_ref[...].astype(F32) + r_ref[...].astype(F32)).astype(o_ref.dtype)

    if kind == "col":
        f_spec = pl.BlockSpec((tr, cols), lambda s, i, c_ref: (c_ref[0] * nb + i, s))
    else:
        f_spec = pl.BlockSpec((tr, cols), lambda s, i, c_ref: ((s * 2 + c_ref[0]) * nb + i, 0))
    blk = pl.BlockSpec((None, tr, cols), lambda s, i, c_ref: (s, i, 0))
    return _pcall(
        body, name=name,
        grid_spec=pltpu.PrefetchScalarGridSpec(num_scalar_prefetch=1, grid=(N_CHIPS, nb), in_specs=[f_spec, blk], out_specs=blk),
        out_shape=jax.ShapeDtypeStruct((N_CHIPS, rh, cols), BF16),
        compiler_params=_cp(("arbitrary", "arbitrary")),
    )(cidx, full, recv)


def _scatter_partials(partials, name):
    n = len(partials)

    def body(*refs):
        srcs, outs = refs[:n], refs[n:2 * n]
        send_sems, recv_sems = refs[2 * n:]
        x, y, c = _mesh_pos()
        barrier = pltpu.get_barrier_semaphore()
        for fx, fy in CHIP_RELS:
            pl.semaphore_signal(barrier, inc=1, device_id=(x ^ fx, y ^ fy, c), device_id_type=MESH)
        pl.semaphore_wait(barrier, len(CHIP_RELS))
        my = 2 * x + y
        cps = []
        for a in range(n):
            for r, (fx, fy) in enumerate(CHIP_RELS):
                to = 2 * (x ^ fx) + (y ^ fy)
                cp = pltpu.make_async_remote_copy(
                    src_ref=srcs[a].at[to], dst_ref=outs[a].at[my],
                    send_sem=send_sems.at[3 * a + r], recv_sem=recv_sems.at[3 * a + r],
                    device_id=(x ^ fx, y ^ fy, c), device_id_type=MESH)
                cp.start()
                cps.append(cp)
        for a in range(n):
            for r, (fx, fy) in enumerate(CHIP_RELS):
                frm = 2 * (x ^ fx) + (y ^ fy)
                pltpu.make_async_remote_copy(
                    src_ref=outs[a].at[frm], dst_ref=outs[a].at[frm],
                    send_sem=send_sems.at[3 * a + r], recv_sem=recv_sems.at[3 * a + r],
                    device_id=(x, y, c), device_id_type=MESH).wait_recv()
        for cp in cps:
            cp.wait_send()

    return pl.kernel(
        body, name=name,
        out_type=[jax.ShapeDtypeStruct(p.shape, p.dtype) for p in partials],
        mesh=plsc.ScalarSubcoreMesh(axis_name="sequencer", num_cores=1),
        scratch_types=[pltpu.SemaphoreType.DMA((3 * n,))] * 2,
        compiler_params=pltpu.CompilerParams(collective_id=SCATTER_COLLECTIVE_ID),
    )(*partials)


def _sum_chips_into(landed, partial, buf, layer, n_layers, pos, name):
    _, rh, cols = landed.shape
    tr = _pick(rh, (256, 128, 64, 32, 16))
    nb = rh // tr

    def body(pos_ref, l_ref, own_ref, *rest):
        o_ref = rest[-1]
        my = pos_ref[1]
        own = own_ref[...].astype(F32)
        acc = jnp.where(my == 0, own, l_ref[0].astype(F32))
        for j in range(1, N_CHIPS):
            acc = acc + jnp.where(my == j, own, l_ref[j].astype(F32))
        o_ref[...] = acc

    in_specs = [pl.BlockSpec((N_CHIPS, tr, cols), lambda i, pos_ref: (0, i, 0)),
                pl.BlockSpec((None, tr, cols), lambda i, pos_ref: (pos_ref[1], i, 0))]
    args = [pos, landed, partial]
    aliases = {}
    if buf is not None:
        in_specs.append(pl.BlockSpec(memory_space=pl.ANY))
        args.append(buf)
        aliases = {3: 0}
    return _pcall(
        body, name=name,
        grid_spec=pltpu.PrefetchScalarGridSpec(
            num_scalar_prefetch=1, grid=(nb,), in_specs=in_specs,
            out_specs=pl.BlockSpec((None, tr, cols), lambda i, pos_ref: (layer, pos_ref[0] * nb + i, 0))),
        out_shape=jax.ShapeDtypeStruct((n_layers, 2 * rh, cols), F32), input_output_aliases=aliases,
        compiler_params=_cp(("arbitrary",)),
    )(*args)


def _exchange_final(bufs):
    n = len(bufs)
    HBM = pl.BlockSpec(memory_space=pl.ANY)
    n_layers = [b.shape[0] for b in bufs]
    base = np.concatenate([[0], np.cumsum(n_layers)])

    def body(*refs):
        outs = refs[n:2 * n]
        send_sems, recv_sems = refs[2 * n:]
        x, y, c = _mesh_pos()
        cps = []
        for w in range(n):
            rh = bufs[w].shape[1] // 2
            for l in range(n_layers[w]):
                k = int(base[w]) + l
                mine = outs[w].at[l, pl.ds(c * rh, rh), :]
                cp = pltpu.make_async_remote_copy(
                    src_ref=mine, dst_ref=mine, send_sem=send_sems.at[k], recv_sem=recv_sems.at[k],
                    device_id=(x, y, 1 - c), device_id_type=MESH)
                cp.start()
                cps.append(cp)
        for w in range(n):
            rh = bufs[w].shape[1] // 2
            for l in range(n_layers[w]):
                k = int(base[w]) + l
                other = outs[w].at[l, pl.ds((1 - c) * rh, rh), :]
                pltpu.make_async_remote_copy(
                    src_ref=other, dst_ref=other, send_sem=send_sems.at[k], recv_sem=recv_sems.at[k],
                    device_id=(x, y, c), device_id_type=MESH).wait_recv()
        for cp in cps:
            cp.wait_send()

    return _pcall(
        body, name="grad_exchange_final", out_shape=[jax.ShapeDtypeStruct(b.shape, b.dtype) for b in bufs],
        in_specs=[HBM] * n, out_specs=[HBM] * n, input_output_aliases={i: i for i in range(n)},
        scratch_shapes=[pltpu.SemaphoreType.DMA((int(base[-1]),))] * 2,
        compiler_params=pltpu.CompilerParams(vmem_limit_bytes=VMEM_LIMIT),
    )(*bufs)


def _swap_halves_last(z):
    h = z.shape[-1] // 2
    return jnp.concatenate([z[..., h:], z[..., :h]], axis=-1)


def _ext_uq(w):
    lead = w.shape[:-1]
    wh = w.reshape(lead + (-1, QK_DIM))
    rope = wh[..., NOPE:]
    return jnp.concatenate([wh, _swap_halves_last(rope)], axis=-1).reshape(lead + (-1,))


def _fold_uq(d):
    lead = d.shape[:-1]
    dh = d.reshape(lead + (-1, HEAD_W))
    rope = dh[..., NOPE:QK_DIM] + _swap_halves_last(dh[..., QK_DIM:])
    return jnp.concatenate([dh[..., :NOPE], rope], axis=-1).reshape(lead + (-1,))


def _ext_win(w):
    base, kr = w[..., :-ROPE], w[..., -ROPE:]
    ks = _swap_halves_last(kr)
    return jnp.concatenate([base, kr, ks, ks, kr], axis=-1)


def _fold_win(d):
    n = d.shape[-1] - 4 * ROPE
    a, b, c2, e = [d[..., n + i * ROPE:n + (i + 1) * ROPE] for i in range(4)]
    return jnp.concatenate([d[..., :n], a + e + _swap_halves_last(b + c2)], axis=-1)


ROW_TILE = 256


def _stage(st, xcur, mix, big, small, tabs):
    D = xcur.shape[1]
    l = st // 2
    mod = [[small["mod"][k][i].reshape(1, 1, D) for i in range(6)] for k in range(l + 1)]
    sh1, sc1, g1, sh2, sc2, _ = mod[l]
    if st % 2 == 0:
        n1 = small["norm1_g"][l].reshape(1, 1, D)
        if l == 0:
            (h,) = _rowop("norm_mod_0", _f_norm_mod, n_x=1, n_nd=0, n_p=3, x_w=[D], nd_w=[], nd_shared=[], p_per_group=[False] * 3,
                          out_w=[D], out_dtypes=[BF16], tile=ROW_TILE)((xcur,), (), (n1, sh1, sc1))
        else:
            op = _rowop(f"resid_norm_mod_a{l}", _f_resid_norm_mod, n_x=2, n_nd=0, n_p=4, x_w=[D, D], nd_w=[], nd_shared=[],
                        p_per_group=[False] * 4, out_w=[D, D], out_dtypes=[F32, BF16], tile=ROW_TILE)
            xcur, h = op((xcur, mix), (), (mod[l - 1][5], n1, sh1, sc1))
        if l % 2 == 0:
            return xcur, _even_mixer(h, big, small, l // 2, ROW_TILE)
        return xcur, _mla_mixer(h, big, small, l // 2, ROW_TILE, *tabs)
    n2 = small["norm2_g"][l].reshape(1, 1, D)
    op = _rowop(f"resid_norm_mod_b{l}", _f_resid_norm_mod, n_x=2, n_nd=0, n_p=4, x_w=[D, D], nd_w=[], nd_shared=[],
                p_per_group=[False] * 4, out_w=[D, D], out_dtypes=[F32, BF16], tile=ROW_TILE)
    xcur, h = op((xcur, mix), (), (g1, n2, sh2, sc2))
    return xcur, _mlp(f"mlp_{l}")(h, big["mlp_w1"], big["mlp_w2"])


def _last_residual(xcur, mix, small):
    D = xcur.shape[1]
    gate = small["mod"][-1][5].reshape(1, 1, D)
    return _rowop("resid_last", _f_resid, n_x=2, n_nd=0, n_p=1, x_w=[D, D], nd_w=[], nd_shared=[], p_per_group=[False],
                  out_w=[D], out_dtypes=[F32], tile=ROW_TILE)((xcur, mix), (), (gate,))[0]


def _even_mixer(h, big, wts, e, tile):
    proj = _linear(f"ab_in_{e}", BF16)(h, big["ab_w_in"])
    da = proj.shape[1] // 4
    u, v, a, g = [proj[:, i * da:(i + 1) * da] for i in range(4)]
    ng = da // GROUP
    sgu = _rowop(f"sgu_{e}", _f_sgu, n_x=2, n_nd=0, n_p=3, x_w=[GROUP, GROUP], nd_w=[], nd_shared=[], p_per_group=[True] * 3,
                 out_w=[GROUP], out_dtypes=[BF16], tile=SGU_CHUNKS_PER_STEP * CHUNK, groups=ng)
    bexp = jnp.broadcast_to(wts["sgu_b"][e][:, :, None], (ng, CHUNK, GROUP))
    (out_a,) = sgu((u, v), (), (wts["sgu_norm_g"][e].reshape(ng, 1, GROUP), wts["sgu_w"][e], bexp))
    (yglu,) = _rowop(f"glu_{e}", _f_glu, n_x=2, n_nd=0, n_p=0, x_w=[da, da], nd_w=[], nd_shared=[], p_per_group=[],
                     out_w=[da], out_dtypes=[F32], tile=tile)((a, g), (), ())
    z = _conv_op(f"conv_{e}")(yglu, wts["conv_w"][e], wts["conv_b"][e].reshape(1, da))
    (out_b,) = _rowop(f"ln_silu_{e}", _f_ln_silu, n_x=1, n_nd=0, n_p=2, x_w=[da], nd_w=[], nd_shared=[], p_per_group=[False] * 2,
                      out_w=[da], out_dtypes=[BF16], tile=tile)(
        (z,), (), (wts["conv_ln_g"][e].reshape(1, 1, da), wts["conv_ln_b"][e].reshape(1, 1, da)))
    return _linear(f"ab_out_{e}")(jnp.concatenate([out_a, out_b], axis=-1), big["ab_w_out"])


def _mla_mixer(h, big, wts, o, tile, tab_q, tab_ka, tab_kb):
    proj = _linear(f"mla_in_{o}", BF16)(h, big["mla_w_in"])
    rank = (proj.shape[1] - 4 * ROPE) // 2
    c_q, c_kv = proj[:, :rank], proj[:, rank:2 * rank]
    kr_a, kr_b = proj[:, 2 * rank:2 * rank + 2 * ROPE], proj[:, 2 * rank + 2 * ROPE:]

    def rms(name, xx, gg):
        return _rowop(name, _f_rms, n_x=1, n_nd=0, n_p=1, x_w=[rank], nd_w=[], nd_shared=[], p_per_group=[False],
                      out_w=[rank], out_dtypes=[BF16], tile=tile)((xx,), (), (gg.reshape(1, 1, rank),))[0]

    q_raw = _linear(f"mla_uq_{o}", BF16)(rms(f"rms_q_{o}", c_q, wts["mla_q_norm_g"][o]), big["mla_w_uq"])
    kv_raw = _linear(f"mla_ukv_{o}", BF16)(rms(f"rms_kv_{o}", c_kv, wts["mla_kv_norm_g"][o]), big["mla_w_ukv"])
    gq, gk = wts["mla_q_head_g"][o], wts["mla_k_head_g"][o]
    gk_rope = gk[NOPE:]
    gq_ext = jnp.concatenate([gq, _swap_halves_last(gq[NOPE:])]).reshape(1, 1, HEAD_W)
    gk_ext = jnp.concatenate([gk[:NOPE], jnp.ones((HEAD_W - NOPE,), F32)]).reshape(1, 1, HEAD_W)
    gk_a = jnp.concatenate([gk_rope, _swap_halves_last(gk_rope)]).reshape(1, 1, 2 * ROPE)
    gk_b = jnp.concatenate([_swap_halves_last(gk_rope), gk_rope]).reshape(1, 1, 2 * ROPE)
    head_tile = 4 * tile
    (q,) = _rowop(f"q_head_{o}", _f_qhead, n_x=1, n_nd=1, n_p=1, x_w=[HEAD_W], nd_w=[HEAD_W], nd_shared=[True], p_per_group=[False],
                  out_w=[HEAD_W], out_dtypes=[BF16], tile=head_tile, groups=N_HEADS)((q_raw,), (tab_q,), (gq_ext,))
    (kv,) = _rowop(f"kv_head_{o}", _f_kvhead, n_x=1, n_nd=0, n_p=1, x_w=[HEAD_W], nd_w=[], nd_shared=[], p_per_group=[False],
                   out_w=[HEAD_W], out_dtypes=[BF16], tile=head_tile, groups=N_HEADS)((kv_raw,), (), (gk_ext,))
    (kr,) = _rowop(f"k_rope_{o}", _f_krope, n_x=2, n_nd=2, n_p=2, x_w=[2 * ROPE] * 2, nd_w=[2 * ROPE] * 2, nd_shared=[True] * 2,
                   p_per_group=[False] * 2, out_w=[2 * ROPE], out_dtypes=[BF16], tile=tile)((kr_a, kr_b), (tab_ka, tab_kb), (gk_a, gk_b))
    att = _attn_op(f"attn_{o}")(q, kv, kr)
    return _linear(f"mla_out_{o}")(att, big["mla_w_out"])


def _rope_tabs(S):
    pos = jnp.arange(S, dtype=F32)
    inv = ROPE_THETA ** (-jnp.arange(0, ROPE, 2, dtype=F32) / ROPE)
    ang = pos[:, None] * inv[None, :]
    cos, sin = jnp.cos(ang), jnp.sin(ang)
    cc = jnp.concatenate([cos, cos], axis=-1)
    sg = jnp.concatenate([-sin, sin], axis=-1)
    tab_q = jnp.concatenate([jnp.ones((S, NOPE), F32), cc, sg], axis=-1)
    return tab_q, jnp.concatenate([cc, sg], axis=-1), jnp.concatenate([sg, cc], axis=-1)


def kernel(x, c, norm1_g, norm2_g, ada_w, ada_b, mlp_w1, mlp_w2, ab_w_in, sgu_norm_g, sgu_w, sgu_b, conv_w, conv_b, conv_ln_g, conv_ln_b, ab_w_out, mla_w_in, mla_q_norm_g, mla_kv_norm_g, mla_w_uq, mla_w_ukv, mla_q_head_g, mla_k_head_g, mla_w_out, loss_target, m_norm1_g, m_norm2_g, m_ada_w, m_ada_b, m_mlp_w1, m_mlp_w2, m_ab_w_in, m_sgu_norm_g, m_sgu_w, m_sgu_b, m_conv_w, m_conv_b, m_conv_ln_g, m_conv_ln_b, m_ab_w_out, m_mla_w_in, m_mla_q_norm_g, m_mla_kv_norm_g, m_mla_w_uq, m_mla_w_ukv, m_mla_q_head_g, m_mla_k_head_g, m_mla_w_out, v_norm1_g, v_norm2_g, v_ada_w, v_ada_b, v_mlp_w1, v_mlp_w2, v_ab_w_in, v_sgu_norm_g, v_sgu_w, v_sgu_b, v_conv_w, v_conv_b, v_conv_ln_g, v_conv_ln_b, v_ab_w_out, v_mla_w_in, v_mla_q_norm_g, v_mla_kv_norm_g, v_mla_w_uq, v_mla_w_ukv, v_mla_q_head_g, v_mla_k_head_g, v_mla_w_out):
    names = ["norm1_g", "norm2_g", "ada_w", "ada_b", "mlp_w1", "mlp_w2", "ab_w_in", "sgu_norm_g", "sgu_w", "sgu_b", "conv_w",
             "conv_b", "conv_ln_g", "conv_ln_b", "ab_w_out", "mla_w_in", "mla_q_norm_g", "mla_kv_norm_g", "mla_w_uq", "mla_w_ukv",
             "mla_q_head_g", "mla_k_head_g", "mla_w_out"]
    W = dict(zip(names, [norm1_g, norm2_g, ada_w, ada_b, mlp_w1, mlp_w2, ab_w_in, sgu_norm_g, sgu_w, sgu_b, conv_w, conv_b, conv_ln_g,
                         conv_ln_b, ab_w_out, mla_w_in, mla_q_norm_g, mla_kv_norm_g, mla_w_uq, mla_w_ukv, mla_q_head_g, mla_k_head_g,
                         mla_w_out]))
    M = dict(zip(names, [m_norm1_g, m_norm2_g, m_ada_w, m_ada_b, m_mlp_w1, m_mlp_w2, m_ab_w_in, m_sgu_norm_g, m_sgu_w, m_sgu_b, m_conv_w,
                         m_conv_b, m_conv_ln_g, m_conv_ln_b, m_ab_w_out, m_mla_w_in, m_mla_q_norm_g, m_mla_kv_norm_g, m_mla_w_uq,
                         m_mla_w_ukv, m_mla_q_head_g, m_mla_k_head_g, m_mla_w_out]))
    V = dict(zip(names, [v_norm1_g, v_norm2_g, v_ada_w, v_ada_b, v_mlp_w1, v_mlp_w2, v_ab_w_in, v_sgu_norm_g, v_sgu_w, v_sgu_b, v_conv_w,
                         v_conv_b, v_conv_ln_g, v_conv_ln_b, v_ab_w_out, v_mla_w_in, v_mla_q_norm_g, v_mla_kv_norm_g, v_mla_w_uq,
                         v_mla_w_ukv, v_mla_q_head_g, v_mla_k_head_g, v_mla_w_out]))
    xi, yi, ci = lax.axis_index("x"), lax.axis_index("y"), lax.axis_index("c")
    chip = 2 * xi + yi
    dev = 2 * chip + ci
    pos = jnp.stack([ci, chip]).astype(jnp.int32)
    S, D = x.shape[1], x.shape[2]
    depth = norm1_g.shape[0]

    c_all, conv_w_all, qn_all, kvn_all = _all_gather_small([c, conv_w, mla_q_norm_g, mla_kv_norm_g], "gather_small_inputs")
    c_all = c_all.reshape(N_DEV, D)
    by_chip = lambda a: jnp.concatenate([a[2 * j] for j in range(N_CHIPS)], axis=-1)
    conv_w_full, qn_full, kvn_full = by_chip(conv_w_all), by_chip(qn_all), by_chip(kvn_all)

    (c_act,) = _rowop("silu_c", _f_silu, n_x=1, n_nd=0, n_p=0, x_w=[D], nd_w=[], nd_shared=[], p_per_group=[], out_w=[D],
                      out_dtypes=[F32], tile=N_DEV)((c_all,), (), ())
    c_act_pad = jnp.pad(c_act, ((0, 128 - N_DEV), (0, 0)))
    mod_cols = jnp.stack([_matmul(c_act_pad, ada_w[l], name=f"ada_fwd_{l}")[:N_DEV] for l in range(depth)])
    (mod_all,) = _all_gather_small([mod_cols], "gather_mod")
    mod_mine = jnp.concatenate([lax.dynamic_index_in_dim(mod_all[2 * j], dev, axis=1, keepdims=False) for j in range(N_CHIPS)], axis=-1)
    mod_mine = (mod_mine + ada_b).reshape(depth, 6, D)

    big = {"mlp_w1": "col", "mlp_w2": "row", "ab_w_in": "col", "ab_w_out": "row", "mla_w_in": "row", "mla_w_uq": "col",
           "mla_w_ukv": "col", "mla_w_out": "row"}
    src = dict(W)
    src["mla_w_in"] = _ext_win(mla_w_in)
    src["mla_w_uq"] = _ext_uq(mla_w_uq)
    shards, kinds, owner = [], [], []
    for nme, kind in big.items():
        for l in range(src[nme].shape[0]):
            shards.append(src[nme][l].astype(BF16))
            kinds.append(kind)
            owner.append((nme, l))
    def stage_of(nme, l):
        return 2 * l + 1 if nme.startswith("mlp") else (4 * l if nme.startswith("ab") else 4 * l + 2)

    fwd_stages = [[a for a, (nme, l) in enumerate(owner) if stage_of(nme, l) == st] for st in range(2 * depth)]
    fulls = [None] * len(shards)
    for st, idx in enumerate(fwd_stages):
        got = _gather_big([shards[a] for a in idx], [kinds[a] for a in idx], f"gather_weights_{st}")
        for a, f in zip(idx, got):
            fulls[a] = f
    small = dict(norm1_g=norm1_g, norm2_g=norm2_g, mod=mod_mine, sgu_norm_g=sgu_norm_g, sgu_w=sgu_w, sgu_b=sgu_b, conv_w=conv_w_full,
                 conv_b=conv_b, conv_ln_g=conv_ln_g, conv_ln_b=conv_ln_b, mla_q_norm_g=qn_full, mla_kv_norm_g=kvn_full,
                 mla_q_head_g=mla_q_head_g, mla_k_head_g=mla_k_head_g)
    groups, a0 = [], 0
    for nme in big:
        groups.append(list(range(a0, a0 + src[nme].shape[0])))
        a0 += src[nme].shape[0]
    shard_shapes = [s.shape for s in shards]
    where = {a: (wi, l) for wi, idxs in enumerate(groups) for l, a in enumerate(idxs)}

    tabs = _rope_tabs(S)
    n_stages = 2 * depth
    xcur, mix, vjps = x[0], None, []
    for st, idx in enumerate(fwd_stages):
        stage_big = {owner[a][0]: fulls[a] for a in idx}
        if st == 0:
            (xcur, mix), vjp = jax.vjp(lambda xx, bb, ss: _stage(0, xx, None, bb, ss, tabs), xcur, stage_big, small)
        else:
            (xcur, mix), vjp = jax.vjp(lambda xx, mm, bb, ss, st=st: _stage(st, xx, mm, bb, ss, tabs), xcur, mix, stage_big, small)
        vjps.append(vjp)
    y, vjp_last = jax.vjp(_last_residual, xcur, mix, small)
    dy, loss_mine = _loss_call(y, loss_target[0])
    loss = lax.psum(loss_mine[0, 0], ("x", "y", "c"))

    dxc, dmix, dsmall = vjp_last(dy)
    bufs = [None] * len(groups)
    swapped, scattered = {}, {}

    def add_and_scatter(st):
        idx, gr, recv = swapped.pop(st)
        partials = [_add_own_half(gr[a], r, kinds[a], shard_shapes[a], pos, f"grad_add_sibling_{a}") for a, r in zip(idx, recv)]
        scattered[st] = (idx, partials, _scatter_partials(partials, f"grad_scatter_partials_{st}"))

    def sum_chips(st):
        idx, partials, landed = scattered.pop(st)
        for a, part, land in zip(idx, partials, landed):
            wi, l = where[a]
            bufs[wi] = _sum_chips_into(land, part, bufs[wi], l, len(groups[wi]), pos, f"grad_sum_chips_{a}")

    for st in reversed(range(n_stages)):
        idx = fwd_stages[st]
        if st == 0:
            dxc, dbig, ds = vjps[st]((dxc, dmix))
        else:
            dxc, dmix, dbig, ds = vjps[st]((dxc, dmix))
        dsmall = jax.tree.map(jnp.add, dsmall, ds)
        gr = {a: dbig[owner[a][0]] for a in idx}
        swapped[st] = (idx, gr, _swap_halves([gr[a] for a in idx], [kinds[a] for a in idx], [shard_shapes[a] for a in idx],
                                             f"grad_swap_halves_{st}"))
        started = [s for s in ((st + 1, st) if st == 1 else (st + 1,)) if s in swapped]
        for s in started:
            add_and_scatter(s)
        if st + 2 in scattered:
            sum_chips(st + 2)
        if st > 0:
            pinned = {s: scattered[s][1] for s in started}
            (dxc, dmix), pinned, bufs = lax.optimization_barrier(((dxc, dmix), pinned, bufs))
            for s in started:
                scattered[s] = (scattered[s][0], pinned[s], scattered[s][2])
    dx = dxc
    for st in sorted(swapped, reverse=True):
        add_and_scatter(st)
    for st in sorted(scattered, reverse=True):
        sum_chips(st)
    gsh = _exchange_final(bufs)

    dw = dsmall
    dmod = dw["mod"].reshape(depth, 6 * D)
    small_names = ["norm1_g", "norm2_g", "sgu_norm_g", "sgu_w", "sgu_b", "conv_w", "conv_b", "conv_ln_g", "conv_ln_b", "mla_q_norm_g",
                   "mla_kv_norm_g", "mla_q_head_g", "mla_k_head_g"]
    red = _all_reduce_small([dmod] + [dw[nme] for nme in small_names], "reduce_small_grads")
    G = dict(zip(["ada_b"] + small_names, red))
    own_cols = lambda a: lax.dynamic_slice_in_dim(a, chip * (a.shape[-1] // N_CHIPS), a.shape[-1] // N_CHIPS, axis=-1)
    for nme in ("conv_w", "mla_q_norm_g", "mla_kv_norm_g"):
        G[nme] = own_cols(G[nme])

    (dmod_all,) = _all_gather_small([dmod], "gather_dmod")
    dmod_cols = own_cols(dmod_all)
    dmod_pad = jnp.pad(dmod_cols, ((0, 128 - N_DEV), (0, 0), (0, 0)))
    G["ada_w"] = _ada_wgrad_call(c_act_pad, dmod_pad)

    for nme, g in zip(big, gsh):
        G[nme] = g
    G["mla_w_in"] = _fold_win(G["mla_w_in"])
    G["mla_w_uq"] = _fold_uq(G["mla_w_uq"])

    deltas, new_m, new_v = [], [], []
    for nme in names:
        d, mn, vn = _adamw_call(W[nme], G[nme], M[nme], V[nme], f"adamw_{nme}")
        deltas.append(d)
        new_m.append(mn)
        new_v.append(vn)
    return (loss, dx[None], *[G[nme] for nme in names], *deltas, *new_m, *new_v)
```

```python
import functools
import numpy as np
import jax
import jax.numpy as jnp
from jax import lax
from jax.experimental import pallas as pl
from jax.experimental.pallas import tpu as pltpu
from jax.experimental.pallas import tpu_sc as plsc

F32 = jnp.float32
BF16 = jnp.bfloat16
MESH = pl.DeviceIdType.MESH

EPS = 1e-6
N_HEADS = 16
NOPE = 128
ROPE = 64
VDIM = 128
QK_DIM = NOPE + ROPE
HEAD_W = 256
CHUNK = 128
GROUP = 128
SGU_CHUNKS_PER_STEP = 4
CONV_W = 31
CONV_PAD = 32
ROPE_THETA = 10000.0
ATTN_SCALE = QK_DIM ** -0.5
LOG2E = 1.4426950408889634
ATTN_SCALE_LOG2E = ATTN_SCALE * LOG2E
ADAM_LR, ADAM_B1, ADAM_B2, ADAM_EPS, ADAM_WD, ADAM_STEP = 0.001, 0.9, 0.999, 1e-08, 0.01, 10
N_CHIPS = 4
N_DEV = 8
VMEM_LIMIT = 56 * 1024 * 1024
WHOLE_K_LIMIT = 4096
SMALL_COLS = 1024
CHIP_RELS = ((1, 0), (0, 1), (1, 1))
GATHER_COLLECTIVE_ID = 0
SWAP_COLLECTIVE_ID = 1
SCATTER_COLLECTIVE_ID = 2


def _pcall(body, **kw):
    return pl.pallas_call(body, **kw)


def _cp(sem=None, **kw):
    return pltpu.CompilerParams(dimension_semantics=sem, vmem_limit_bytes=VMEM_LIMIT, **kw)


def _pick(n, cands):
    for c in cands:
        if n % c == 0:
            return c
    return n


def _matmul(a, b, *, ta=False, tb=False, out_dtype=F32, name, extra=(), epilogue=None, out_dtypes=None):
    if ta:
        K, M = a.shape
    else:
        M, K = a.shape
    if tb:
        N, K2 = b.shape
    else:
        K2, N = b.shape
    assert K == K2, (a.shape, b.shape, ta, tb)
    tm = _pick(M, (1024, 512, 256, 128))
    tn = N if N <= 1536 else _pick(N, (1024, 512, 256, 128))
    tk = _pick(K, (2048, 1024, 512, 256, 128))
    if ta and K <= WHOLE_K_LIMIT and max(a.dtype.itemsize, b.dtype.itemsize) == 2:
        tk = K
    nk = K // tk
    dn = (((0 if ta else 1,), (1 if tb else 0,)), ((), ()))

    n_extra = len(extra)
    single = epilogue is None
    if single:
        out_dtypes = [out_dtype]

    n_out = len(out_dtypes)

    def body(a_ref, b_ref, *rest):
        extra_refs, o_refs = rest[:n_extra], rest[n_extra:n_extra + n_out]

        def dot():
            return lax.dot_general(a_ref[...].astype(BF16), b_ref[...].astype(BF16), dn, preferred_element_type=F32)

        def finish(acc):
            res = (acc,) if single else epilogue(acc, *[r[...] for r in extra_refs])
            for o_ref, val in zip(o_refs, res):
                o_ref[...] = val.astype(o_ref.dtype)

        if nk == 1:
            finish(dot())
            return
        acc_ref = rest[-1]
        k = pl.program_id(2)

        @pl.when(k == 0)
        def _():
            acc_ref[...] = jnp.zeros_like(acc_ref)

        acc_ref[...] += dot()

        @pl.when(k == nk - 1)
        def _():
            finish(acc_ref[...])

    a_spec = pl.BlockSpec((tk, tm), lambda i, j, k: (k, i)) if ta else pl.BlockSpec((tm, tk), lambda i, j, k: (i, k))
    b_spec = pl.BlockSpec((tn, tk), lambda i, j, k: (j, k)) if tb else pl.BlockSpec((tk, tn), lambda i, j, k: (k, j))
    mn_spec = pl.BlockSpec((tm, tn), lambda i, j, k: (i, j))
    outs = _pcall(
        body, name=name, grid=(M // tm, N // tn, nk), in_specs=[a_spec, b_spec] + [mn_spec] * n_extra,
        out_specs=[mn_spec] * len(out_dtypes),
        out_shape=[jax.ShapeDtypeStruct((M, N), d) for d in out_dtypes],
        scratch_shapes=[pltpu.VMEM((tm, tn), F32)] if nk > 1 else [],
        compiler_params=_cp(("parallel", "parallel", "arbitrary")),
    )(a, b, *extra)
    return outs[0] if single else outs


def _linear(name, out_dtype=F32):
    @jax.custom_vjp
    def mm(a, w):
        return _matmul(a, w, out_dtype=out_dtype, name=name + "_fwd")

    def fwd(a, w):
        return mm(a, w), (a, w)

    def bwd(res, dy):
        a, w = res
        da = _matmul(dy, w, tb=True, out_dtype=a.dtype, name=name + "_dgrad")
        dw = _matmul(a, dy, ta=True, out_dtype=w.dtype, name=name + "_wgrad")
        return da, dw

    mm.defvjp(fwd, bwd)
    return mm


def _ada_wgrad_call(c_act, dmod):
    P, D = c_act.shape
    _, L, C = dmod.shape
    tm = _pick(D, (1024, 512, 256, 128))
    tn = _pick(C, (1024, 512, 256, 128))
    nj = C // tn

    def body(a_ref, b_ref, o_ref):
        o_ref[...] = lax.dot_general(a_ref[...].astype(BF16), b_ref[...].astype(BF16), (((0,), (0,)), ((), ())),
                                     preferred_element_type=F32)

    return _pcall(
        body, name="ada_wgrad", grid=(L, D // tm, nj),
        in_specs=[pl.BlockSpec((P, tm), lambda l, i, j: (0, i)), pl.BlockSpec((P, tn), lambda l, i, j: (0, l * nj + j))],
        out_specs=pl.BlockSpec((None, tm, tn), lambda l, i, j: (l, i, j)),
        out_shape=jax.ShapeDtypeStruct((L, D, C), F32), compiler_params=_cp(("parallel", "parallel", "parallel")),
    )(c_act, dmod.reshape(P, L * C))


def _relu2_epilogue(acc):
    return acc, jnp.square(jnp.maximum(acc, 0.0))


def _relu2_grad_epilogue(acc, h1):
    return (acc * (2.0 * jnp.maximum(h1.astype(F32), 0.0)),)


def _mlp(name):
    def run(h, w1, w2):
        h1, act = _matmul(h, w1, name=name + "_up_fwd", epilogue=_relu2_epilogue, out_dtypes=[BF16, BF16])
        return _matmul(act, w2, out_dtype=BF16, name=name + "_down_fwd"), h1, act

    @jax.custom_vjp
    def mlp(h, w1, w2):
        return run(h, w1, w2)[0]

    def fwd(h, w1, w2):
        y, h1, act = run(h, w1, w2)
        return y, (h, w1, w2, h1, act)

    def bwd(res, dy):
        h, w1, w2, h1, act = res
        dw2 = _matmul(act, dy, ta=True, out_dtype=w2.dtype, name=name + "_down_wgrad")
        (dh1,) = _matmul(dy, w2, tb=True, name=name + "_down_dgrad", extra=(h1,), epilogue=_relu2_grad_epilogue, out_dtypes=[BF16])
        dw1 = _matmul(h, dh1, ta=True, out_dtype=w1.dtype, name=name + "_up_wgrad")
        dh = _matmul(dh1, w1, tb=True, out_dtype=h.dtype, name=name + "_up_dgrad")
        return dh, dw1, dw2

    mlp.defvjp(fwd, bwd)
    return mlp


def _rowop(name, f, *, n_x, n_nd, n_p, x_w, nd_w, nd_shared, p_per_group, out_w, out_dtypes, tile, groups=1):
    G = groups

    def specs(S):
        t = min(tile, S)
        xs = [pl.BlockSpec((t, w), lambda g, r: (r, g)) for w in x_w]
        nds = [pl.BlockSpec((t, w), (lambda g, r: (r, 0)) if sh else (lambda g, r: (r, g))) for w, sh in zip(nd_w, nd_shared)]
        outs = [pl.BlockSpec((t, w), lambda g, r: (r, g)) for w in out_w]
        return t, xs, nds, outs

    def pspecs(ps):
        return [pl.BlockSpec((None,) + p.shape[1:], (lambda g, r: (g, 0, 0)) if pg else (lambda g, r: (0, 0, 0)))
                for p, pg in zip(ps, p_per_group)]

    def fwd_call(xs, nds, ps):
        S = xs[0].shape[0]
        t, xsp, ndsp, osp = specs(S)

        def body(*refs):
            ins, outs = refs[:n_x + n_nd + n_p], refs[n_x + n_nd + n_p:]
            vals = [r[...].astype(F32) for r in ins]
            res = f(*vals)
            for o, r in zip(res, outs):
                r[...] = o.astype(r.dtype)

        return _pcall(
            body, name=name + "_fwd", grid=(G, S // t), in_specs=xsp + ndsp + pspecs(ps), out_specs=osp,
            out_shape=[jax.ShapeDtypeStruct((S, G * w), d) for w, d in zip(out_w, out_dtypes)],
            compiler_params=_cp(("parallel", "parallel")),
        )(*xs, *nds, *ps)

    def bwd_call(xs, nds, ps, douts):
        S = xs[0].shape[0]
        t, xsp, ndsp, osp = specs(S)
        n_in = n_x + n_nd + n_p + len(out_w)

        def body(*refs):
            ins, outs = refs[:n_in], refs[n_in:]
            xv = [r[...].astype(F32) for r in ins[:n_x]]
            ndv = [r[...].astype(F32) for r in ins[n_x:n_x + n_nd]]
            pv = [r[...].astype(F32) for r in ins[n_x + n_nd:n_x + n_nd + n_p]]
            dov = tuple(r[...].astype(F32) for r in ins[n_x + n_nd + n_p:])
            _, vjp = jax.vjp(lambda *a: tuple(f(*a[:n_x], *ndv, *a[n_x:])), *xv, *pv)
            cts = vjp(dov)
            for i in range(n_x):
                outs[i][...] = cts[i].astype(outs[i].dtype)
            g, r = pl.program_id(0), pl.program_id(1)
            for i in range(n_p):
                first = (r == 0) if p_per_group[i] else jnp.logical_and(g == 0, r == 0)
                ref, ct = outs[n_x + i], cts[n_x + i]

                @pl.when(first)
                def _(ref=ref, ct=ct):
                    ref[...] = ct

                @pl.when(jnp.logical_not(first))
                def _(ref=ref, ct=ct):
                    ref[...] += ct

        return _pcall(
            body, name=name + "_bwd", grid=(G, S // t), in_specs=xsp + ndsp + pspecs(ps) + osp,
            out_specs=xsp + pspecs(ps),
            out_shape=[jax.ShapeDtypeStruct(x.shape, x.dtype) for x in xs] + [jax.ShapeDtypeStruct(p.shape, F32) for p in ps],
            compiler_params=_cp(("arbitrary", "arbitrary")),
        )(*xs, *nds, *ps, *douts)

    @jax.custom_vjp
    def op(xs, nds, ps):
        return tuple(fwd_call(xs, nds, ps))

    def op_fwd(xs, nds, ps):
        return op(xs, nds, ps), (xs, nds, ps)

    def op_bwd(res, douts):
        xs, nds, ps = res
        out = bwd_call(xs, nds, ps, douts)
        return tuple(out[:n_x]), tuple(jnp.zeros_like(n) for n in nds), tuple(out[n_x:])

    op.defvjp(op_fwd, op_bwd)
    return op


def _rms_rows(x):
    return x * lax.rsqrt(jnp.mean(x * x, axis=-1, keepdims=True) + EPS)


def _f_norm_mod(x, g, shift, scale):
    return ((_rms_rows(x) * g) * (1.0 + scale) + shift,)


def _f_resid_norm_mod(x, mix, gate, g, shift, scale):
    xn = x + gate * mix
    return xn, (_rms_rows(xn) * g) * (1.0 + scale) + shift


def _f_resid(x, mix, gate):
    return (x + gate * mix,)


def _f_rms(x, g):
    return (_rms_rows(x) * g,)


def _f_glu(a, g):
    return (a * jax.nn.sigmoid(g),)


def _f_ln_silu(z, g, b):
    mu = jnp.mean(z, axis=-1, keepdims=True)
    zc = z - mu
    var = jnp.mean(zc * zc, axis=-1, keepdims=True)
    y = zc * lax.rsqrt(var + EPS) * g + b
    return (y * jax.nn.sigmoid(y),)


def _f_silu(x):
    return (x * jax.nn.sigmoid(x),)


@jax.custom_vjp
def _bdot_chunks(a, b):
    return lax.dot_general(a.astype(BF16), b.astype(BF16), (((2,), (1,)), ((0,), (0,))), preferred_element_type=F32)


def _bdot_chunks_fwd(a, b):
    return _bdot_chunks(a, b), (a, b)


def _bdot_chunks_bwd(res, ct):
    a, b = res
    c16 = ct.astype(BF16)
    da = lax.dot_general(c16, b.astype(BF16), (((2,), (2,)), ((0,), (0,))), preferred_element_type=F32)
    db = lax.dot_general(a.astype(BF16), c16, (((1,), (1,)), ((0,), (0,))), preferred_element_type=F32)
    return da, db


_bdot_chunks.defvjp(_bdot_chunks_fwd, _bdot_chunks_bwd)


def _f_sgu(u, v, ng, w, bexp):
    n = u.shape[0] // CHUNK
    vn = (_rms_rows(jax.nn.gelu(v)) * ng).reshape(n, CHUNK, GROUP)
    row = lax.broadcasted_iota(jnp.int32, w.shape, 0)
    col = lax.broadcasted_iota(jnp.int32, w.shape, 1)
    wm = jnp.broadcast_to(jnp.where(row >= col, w, 0.0)[None], (n, CHUNK, CHUNK))
    mixed = _bdot_chunks(wm, vn) + bexp[None]
    return (jax.nn.gelu(u) * mixed.reshape(n * CHUNK, GROUP),)


def _lo_mask():
    return lax.broadcasted_iota(jnp.int32, (1, HEAD_W), 1) < NOPE


def _f_qhead(x, tab, g):
    lo = _lo_mask()
    x2 = x * x
    ms_lo = jnp.sum(jnp.where(lo, x2, 0.0), axis=-1, keepdims=True) * (1.0 / NOPE)
    ms_hi = jnp.sum(jnp.where(lo, 0.0, x2), axis=-1, keepdims=True) * (1.0 / (HEAD_W - NOPE))
    r = jnp.where(lo, lax.rsqrt(ms_lo + EPS), lax.rsqrt(ms_hi + EPS))
    return (((x * r) * g) * tab,)


def _f_kvhead(x, g):
    lo = _lo_mask()
    ms = jnp.sum(jnp.where(lo, x * x, 0.0), axis=-1, keepdims=True) * (1.0 / NOPE)
    return (jnp.where(lo, (x * lax.rsqrt(ms + EPS)) * g, x),)


def _f_krope(a, b, ta, tb, ga, gb):
    r = lax.rsqrt(jnp.mean(a * a, axis=-1, keepdims=True) + EPS)
    return (((a * r) * ga) * ta + ((b * r) * gb) * tb,)


def _conv_fwd_call(y, w, b, name):
    S, C = y.shape
    cw = 128
    rt = _pick(S, (128,))
    w = jnp.pad(w, ((0, CONV_PAD - CONV_W), (0, 0)))

    def body(y_ref, w_ref, b_ref, z_ref, pad_ref):
        pad_ref[pl.ds(0, CONV_PAD), :] = jnp.zeros((CONV_PAD, cw), F32)
        pad_ref[pl.ds(CONV_PAD, S), :] = y_ref[...]
        wv = w_ref[...]
        bv = b_ref[...]

        def chunk(ci, carry):
            r0 = pl.multiple_of(ci * rt, rt)
            win = pad_ref[pl.ds(r0, rt + CONV_PAD), :]
            acc = jnp.broadcast_to(bv, (rt, cw))
            for k in range(CONV_W):
                off = CONV_PAD - (CONV_W - 1) + k
                sh = win if off == 0 else pltpu.roll(win, rt + CONV_PAD - off, axis=0)
                acc = acc + wv[k:k + 1, :] * sh[:rt, :]
            z_ref[pl.ds(r0, rt), :] = acc
            return carry

        lax.fori_loop(0, S // rt, chunk, 0)

    return _pcall(
        body, name=name, grid=(C // cw,),
        in_specs=[pl.BlockSpec((S, cw), lambda j: (0, j)), pl.BlockSpec((CONV_PAD, cw), lambda j: (0, j)),
                  pl.BlockSpec((1, cw), lambda j: (0, j))],
        out_specs=pl.BlockSpec((S, cw), lambda j: (0, j)),
        out_shape=jax.ShapeDtypeStruct((S, C), F32),
        scratch_shapes=[pltpu.VMEM((S + CONV_PAD, cw), F32)],
        compiler_params=_cp(("parallel",)),
    )(y, w, b)


def _conv_bwd_call(y, w, dz, name):
    S, C = y.shape
    cw = 128
    rt = _pick(S, (128,))
    w = jnp.pad(w, ((0, CONV_PAD - CONV_W), (0, 0)))

    def body(y_ref, w_ref, dz_ref, dy_ref, dw_ref, db_ref, ypad_ref, zpad_ref):
        ypad_ref[pl.ds(0, CONV_PAD), :] = jnp.zeros((CONV_PAD, cw), F32)
        ypad_ref[pl.ds(CONV_PAD, S), :] = y_ref[...]
        zpad_ref[pl.ds(0, S), :] = dz_ref[...]
        zpad_ref[pl.ds(S, CONV_PAD), :] = jnp.zeros((CONV_PAD, cw), F32)
        dw_ref[...] = jnp.zeros_like(dw_ref)
        wv = w_ref[...]

        def chunk(ci, dbacc):
            r0 = pl.multiple_of(ci * rt, rt)
            ywin = ypad_ref[pl.ds(r0, rt + CONV_PAD), :]
            zwin = zpad_ref[pl.ds(r0, rt + CONV_PAD), :]
            dzc = zwin[:rt, :]
            acc = jnp.zeros((rt, cw), F32)
            for k in range(CONV_W):
                off_z = (CONV_W - 1) - k
                zs = zwin if off_z == 0 else pltpu.roll(zwin, rt + CONV_PAD - off_z, axis=0)
                acc = acc + wv[k:k + 1, :] * zs[:rt, :]
                off_y = CONV_PAD - (CONV_W - 1) + k
                ys = pltpu.roll(ywin, rt + CONV_PAD - off_y, axis=0)
                dw_ref[k:k + 1, :] += jnp.sum(dzc * ys[:rt, :], axis=0, keepdims=True)
            dy_ref[pl.ds(r0, rt), :] = acc
            return dbacc + jnp.sum(dzc, axis=0, keepdims=True)

        db_ref[...] = lax.fori_loop(0, S // rt, chunk, jnp.zeros((1, cw), F32))

    dy, dw, db = _pcall(
        body, name=name, grid=(C // cw,),
        in_specs=[pl.BlockSpec((S, cw), lambda j: (0, j)), pl.BlockSpec((CONV_PAD, cw), lambda j: (0, j)),
                  pl.BlockSpec((S, cw), lambda j: (0, j))],
        out_specs=[pl.BlockSpec((S, cw), lambda j: (0, j)), pl.BlockSpec((CONV_PAD, cw), lambda j: (0, j)),
                   pl.BlockSpec((1, cw), lambda j: (0, j))],
        out_shape=[jax.ShapeDtypeStruct((S, C), F32), jax.ShapeDtypeStruct((CONV_PAD, C), F32),
                   jax.ShapeDtypeStruct((1, C), F32)],
        scratch_shapes=[pltpu.VMEM((S + CONV_PAD, cw), F32), pltpu.VMEM((S + CONV_PAD, cw), F32)],
        compiler_params=_cp(("parallel",)),
    )(y, w, dz)
    return dy, dw[:CONV_W], db


def _conv_op(name):
    @jax.custom_vjp
    def conv(y, w, b):
        return _conv_fwd_call(y, w, b, name + "_fwd")

    def fwd(y, w, b):
        return conv(y, w, b), (y, w)

    def bwd(res, dz):
        y, w = res
        return _conv_bwd_call(y, w, dz, name + "_bwd")

    conv.defvjp(fwd, bwd)
    return conv


def _attn_tile(S):
    return _pick(S, (512, 256, 128))


def _attn_fwd_call(q, kv, kr, name):
    S = q.shape[0]
    t = _attn_tile(S)

    def body(q_ref, kv_ref, kr_ref, o_ref, lse_ref):
        i = pl.program_id(1)
        qv = q_ref[...]

        def update(jb, carry, diagonal):
            m, l, acc = carry
            off = pl.multiple_of(jb * t, t)
            kc = jnp.concatenate([kv_ref[pl.ds(off, t), pl.ds(0, NOPE)], kr_ref[pl.ds(off, t), :]], axis=-1)
            vv = kv_ref[pl.ds(off, t), pl.ds(NOPE, VDIM)]
            s = lax.dot_general(qv, kc, (((1,), (1,)), ((), ())), preferred_element_type=F32)
            if diagonal:
                row = lax.broadcasted_iota(jnp.int32, (t, t), 0)
                col = lax.broadcasted_iota(jnp.int32, (t, t), 1)
                s = jnp.where(col <= row, s, -jnp.inf)
            mn = jnp.maximum(m, jnp.max(s, axis=-1, keepdims=True))
            p = jnp.exp2((s - mn) * ATTN_SCALE_LOG2E)
            al = jnp.exp2((m - mn) * ATTN_SCALE_LOG2E)
            l = al * l + jnp.sum(p, axis=-1, keepdims=True)
            acc = al * acc + jnp.dot(p.astype(BF16), vv, preferred_element_type=F32)
            return mn, l, acc

        init = (jnp.full((t, 1), -jnp.inf, F32), jnp.zeros((t, 1), F32), jnp.zeros((t, VDIM), F32))
        carry = lax.fori_loop(0, i, lambda jb, cr: update(jb, cr, False), init)
        m, l, acc = update(i, carry, True)
        o_ref[...] = (acc / l).astype(o_ref.dtype)
        lse_ref[...] = m * ATTN_SCALE + jnp.log(l)

    return _pcall(
        body, name=name, grid=(N_HEADS, S // t),
        in_specs=[pl.BlockSpec((t, HEAD_W), lambda h, i: (i, h)), pl.BlockSpec((S, HEAD_W), lambda h, i: (0, h)),
                  pl.BlockSpec((S, 128), lambda h, i: (0, 0))],
        out_specs=[pl.BlockSpec((t, VDIM), lambda h, i: (i, h)), pl.BlockSpec((None, t, 1), lambda h, i: (h, i, 0))],
        out_shape=[jax.ShapeDtypeStruct((S, N_HEADS * VDIM), BF16), jax.ShapeDtypeStruct((N_HEADS, S, 1), F32)],
        compiler_params=_cp(("parallel", "parallel")),
    )(q, kv, kr)


def _attn_dd_call(o, do, name):
    S = o.shape[0]
    t = _attn_tile(S)

    def body(o_ref, do_ref, dd_ref):
        head = lax.broadcasted_iota(jnp.int32, (1, N_HEADS), 1)
        acc = jnp.zeros((t, N_HEADS), F32)
        for h in range(N_HEADS):
            cols = pl.ds(h * VDIM, VDIM)
            part = jnp.sum(do_ref[:, cols].astype(F32) * o_ref[:, cols].astype(F32), axis=-1, keepdims=True)
            acc = acc + jnp.where(head == h, part, 0.0)
        dd_ref[...] = acc

    return _pcall(
        body, name=name, grid=(S // t,),
        in_specs=[pl.BlockSpec((t, N_HEADS * VDIM), lambda i: (i, 0)), pl.BlockSpec((t, N_HEADS * VDIM), lambda i: (i, 0))],
        out_specs=pl.BlockSpec((t, N_HEADS), lambda i: (i, 0)),
        out_shape=jax.ShapeDtypeStruct((S, N_HEADS), F32), compiler_params=_cp(("parallel",)),
    )(o, do)


def _attn_bwd_call(q, kv, kr, do, lse, dd, name):
    S = q.shape[0]
    t = _attn_tile(S)
    nq = S // t

    def body(q_ref, kv_ref, kr_ref, do_ref, lse_ref, dd_ref, dq_ref, dkv_ref, dkr_ref, dq_acc):
        j = pl.program_id(1)

        @pl.when(j == 0)
        def _():
            dq_acc[...] = jnp.zeros_like(dq_acc)

        kc = jnp.concatenate([kv_ref[:, pl.ds(0, NOPE)], kr_ref[...]], axis=-1)
        vv = kv_ref[:, pl.ds(NOPE, VDIM)]

        def update(ib, carry, diagonal):
            dkc, dv = carry
            off = pl.multiple_of(ib * t, t)
            qv = q_ref[pl.ds(off, t), :]
            dov = do_ref[pl.ds(off, t), :]
            st = lax.dot_general(kc, qv, (((1,), (1,)), ((), ())), preferred_element_type=F32)
            if diagonal:
                key = lax.broadcasted_iota(jnp.int32, (t, t), 0)
                qry = lax.broadcasted_iota(jnp.int32, (t, t), 1)
                st = jnp.where(key <= qry, st, -jnp.inf)
            pt = jnp.exp2(st * ATTN_SCALE_LOG2E - lse_ref[pl.ds(ib, 1), :])
            dv = dv + jnp.dot(pt.astype(BF16), dov, preferred_element_type=F32)
            dpt = lax.dot_general(vv, dov, (((1,), (1,)), ((), ())), preferred_element_type=F32)
            dst = (pt * (dpt - dd_ref[pl.ds(ib, 1), :]) * ATTN_SCALE).astype(BF16)
            dkc = dkc + jnp.dot(dst, qv, preferred_element_type=F32)
            dq_acc[pl.ds(off, t), :] += lax.dot_general(dst, kc, (((0,), (0,)), ((), ())), preferred_element_type=F32)
            return dkc, dv

        carry = update(j, (jnp.zeros((t, HEAD_W), F32), jnp.zeros((t, VDIM), F32)), True)
        dkc, dv = lax.fori_loop(j + 1, nq, lambda ib, cr: update(ib, cr, False), carry)
        dkv_ref[:, pl.ds(0, NOPE)] = dkc[:, :NOPE].astype(dkv_ref.dtype)
        dkv_ref[:, pl.ds(NOPE, VDIM)] = dv.astype(dkv_ref.dtype)
        dkr_ref[...] = dkc[:, NOPE:]

        @pl.when(j == nq - 1)
        def _():
            dq_ref[...] = dq_acc[...].astype(dq_ref.dtype)

    dq, dkv, dkr_heads = _pcall(
        body, name=name, grid=(N_HEADS, nq),
        in_specs=[pl.BlockSpec((S, HEAD_W), lambda h, j: (0, h)), pl.BlockSpec((t, HEAD_W), lambda h, j: (j, h)),
                  pl.BlockSpec((t, 128), lambda h, j: (j, 0)), pl.BlockSpec((S, VDIM), lambda h, j: (0, h)),
                  pl.BlockSpec((None, nq, t), lambda h, j: (h, 0, 0)), pl.BlockSpec((None, nq, t), lambda h, j: (h, 0, 0))],
        out_specs=[pl.BlockSpec((S, HEAD_W), lambda h, j: (0, h)), pl.BlockSpec((t, HEAD_W), lambda h, j: (j, h)),
                   pl.BlockSpec((None, t, 128), lambda h, j: (h, j, 0))],
        out_shape=[jax.ShapeDtypeStruct(q.shape, q.dtype), jax.ShapeDtypeStruct(kv.shape, kv.dtype),
                   jax.ShapeDtypeStruct((N_HEADS, S, 128), F32)],
        scratch_shapes=[pltpu.VMEM((S, HEAD_W), F32)],
        compiler_params=_cp(("parallel", "arbitrary")),
    )(q, kv, kr, do, (lse * LOG2E).reshape(N_HEADS, nq, t), dd.T.reshape(N_HEADS, nq, t))

    def sum_body(p_ref, o_ref):
        acc = p_ref[0]
        for h in range(1, N_HEADS):
            acc = acc + p_ref[h]
        o_ref[...] = acc.astype(o_ref.dtype)

    dkr = _pcall(
        sum_body, name=name + "_rope_sum", grid=(nq,),
        in_specs=[pl.BlockSpec((N_HEADS, t, 128), lambda i: (0, i, 0))], out_specs=pl.BlockSpec((t, 128), lambda i: (i, 0)),
        out_shape=jax.ShapeDtypeStruct((S, 128), kr.dtype), compiler_params=_cp(("parallel",)),
    )(dkr_heads)
    return dq, dkv, dkr


def _attn_op(name):
    @jax.custom_vjp
    def attn(q, kv, kr):
        return _attn_fwd_call(q, kv, kr, name + "_fwd")[0]

    def fwd(q, kv, kr):
        o, lse = _attn_fwd_call(q, kv, kr, name + "_fwd")
        return o, (q, kv, kr, o, lse)

    def bwd(res, do):
        q, kv, kr, o, lse = res
        dd = _attn_dd_call(o, do, name + "_dd")
        return _attn_bwd_call(q, kv, kr, do, lse, dd, name + "_bwd")

    attn.defvjp(fwd, bwd)
    return attn


def _loss_call(y, target):
    S, D = y.shape
    t = _pick(S, (256, 128))

    def body(y_ref, t_ref, dy_ref, loss_ref):
        @pl.when(pl.program_id(0) == 0)
        def _():
            loss_ref[...] = jnp.zeros_like(loss_ref)

        e = y_ref[...] - t_ref[...]
        dy_ref[...] = e * (1.0 / D)
        loss_ref[...] += 0.5 * jnp.sum(jnp.mean(e * e, axis=-1, keepdims=True), axis=0, keepdims=True)

    return _pcall(
        body, name="loss_head", grid=(S // t,),
        in_specs=[pl.BlockSpec((t, D), lambda i: (i, 0)), pl.BlockSpec((t, D), lambda i: (i, 0))],
        out_specs=[pl.BlockSpec((t, D), lambda i: (i, 0)), pl.BlockSpec((1, 1), lambda i: (0, 0))],
        out_shape=[jax.ShapeDtypeStruct((S, D), F32), jax.ShapeDtypeStruct((1, 1), F32)],
        compiler_params=_cp(("arbitrary",)),
    )(y, target)


def _adamw_call(w, g, m, v, name):
    shape = w.shape
    C = shape[-1]
    R = int(np.prod(shape[:-1]))
    tr = R
    for cand in (512, 256, 128, 64, 32, 16, 8):
        if R % cand == 0 and cand * C * 4 <= 2 * 1024 * 1024:
            tr = cand
            break
    c1 = 1.0 - ADAM_B1 ** ADAM_STEP
    c2 = 1.0 - ADAM_B2 ** ADAM_STEP

    def body(w_ref, g_ref, m_ref, v_ref, d_ref, mo_ref, vo_ref):
        gv = g_ref[...]
        mn = ADAM_B1 * m_ref[...] + (1.0 - ADAM_B1) * gv
        vn = ADAM_B2 * v_ref[...] + (1.0 - ADAM_B2) * (gv * gv)
        d_ref[...] = -ADAM_LR * ((mn / c1) / (jnp.sqrt(vn / c2) + ADAM_EPS) + ADAM_WD * w_ref[...])
        mo_ref[...] = mn
        vo_ref[...] = vn

    spec = pl.BlockSpec((tr, C), lambda i: (i, 0))
    outs = _pcall(
        body, name=name, grid=(R // tr,), in_specs=[spec] * 4, out_specs=[spec] * 3,
        out_shape=[jax.ShapeDtypeStruct((R, C), F32)] * 3, compiler_params=_cp(("parallel",)),
    )(*[a.reshape(R, C) for a in (w, g, m, v)])
    return [o.reshape(shape) for o in outs]


def _gather_small(x2d, *, reduce, name):
    R, C = x2d.shape

    def body(x_ref, out_ref, *scratch):
        if reduce:
            buf_ref, send_sems, recv_sems, local_sem = scratch
        else:
            buf_ref = out_ref
            send_sems, recv_sems, local_sem = scratch
        x, y, c = lax.axis_index("x"), lax.axis_index("y"), lax.axis_index("c")
        me, sibling = (x, y, c), (x, y, 1 - c)
        chips = [(1 - x, y), (x, 1 - y), (1 - x, 1 - y)]

        def rows(px, py, pc):
            return buf_ref.at[pl.ds((4 * px + 2 * py + pc) * R, R), :]

        def copy(k, block, to, src=None):
            return pltpu.make_async_remote_copy(
                src_ref=rows(*block) if src is None else src, dst_ref=rows(*block),
                send_sem=send_sems.at[k], recv_sem=recv_sems.at[k], device_id=to, device_id_type=MESH)

        mine = pltpu.make_async_copy(x_ref, rows(*me), local_sem)
        mine.start()
        first = [copy(0, me, sibling, src=x_ref)]
        first += [copy(1 + j, me, (*chip, c), src=x_ref) for j, chip in enumerate(chips)]
        for cp in first:
            cp.start()
        passed = [copy(4 + j, (*chip, c), sibling) for j, chip in enumerate(chips)]
        for j, chip in enumerate(chips):
            copy(1 + j, (*chip, c), me).wait_recv()
            passed[j].start()
        copy(0, sibling, me).wait_recv()
        for j, chip in enumerate(chips):
            copy(4 + j, (*chip, 1 - c), me).wait_recv()
        for cp in first + passed:
            cp.wait_send()
        mine.wait()
        if reduce:
            acc = buf_ref[pl.ds(0, R), :]
            for d in range(1, N_DEV):
                acc = acc + buf_ref[pl.ds(d * R, R), :]
            out_ref[...] = acc

    scratch = [pltpu.SemaphoreType.DMA((7,)), pltpu.SemaphoreType.DMA((7,)), pltpu.SemaphoreType.DMA]
    if reduce:
        scratch = [pltpu.VMEM((N_DEV * R, C), F32)] + scratch
    return _pcall(
        body, name=name, out_shape=jax.ShapeDtypeStruct((R if reduce else N_DEV * R, C), F32),
        in_specs=[pl.BlockSpec(memory_space=pltpu.VMEM)], out_specs=pl.BlockSpec(memory_space=pltpu.VMEM),
        scratch_shapes=scratch, compiler_params=pltpu.CompilerParams(vmem_limit_bytes=VMEM_LIMIT),
    )(x2d)


def _pack(arrs):
    flat = jnp.concatenate([a.reshape(-1).astype(F32) for a in arrs])
    n = flat.shape[0]
    unit = 8 * SMALL_COLS
    flat = jnp.pad(flat, (0, (-n) % unit))
    return flat.reshape(-1, SMALL_COLS)


def _unpack(flat, shapes):
    out, o = [], 0
    for s in shapes:
        n = int(np.prod(s))
        out.append(flat[o:o + n].reshape(s))
        o += n
    return out


def _all_gather_small(arrs, name):
    p = _pack(arrs)
    g = _gather_small(p, reduce=False, name=name).reshape(N_DEV, -1)
    out, o = [], 0
    for a in arrs:
        n = int(np.prod(a.shape))
        out.append(g[:, o:o + n].reshape((N_DEV,) + a.shape))
        o += n
    return out


def _all_reduce_small(arrs, name):
    p = _pack(arrs)
    return _unpack(_gather_small(p, reduce=True, name=name).reshape(-1), [a.shape for a in arrs])


def _half_rows(shard_shape):
    return shard_shape[0] // 2


def _half_slot(ref, kind, rh, cols, half, slot):
    if kind == "col":
        return ref.at[pl.ds(half * rh, rh), pl.ds(slot * cols, cols)]
    return ref.at[pl.ds((slot * 2 + half) * rh, rh), :]


def _full_shape(kind, shard_shape):
    r, c = shard_shape
    return (r, c * N_CHIPS) if kind == "col" else (r * N_CHIPS, c)


def _mesh_pos():
    x, y, c = lax.axis_index("x"), lax.axis_index("y"), lax.axis_index("c")
    return x, y, c


def _handshake_chip_peers(x, y, c):
    barrier = pltpu.get_barrier_semaphore()
    for peer in [(x, y, 1 - c)] + [(x ^ fx, y ^ fy, c) for fx, fy in CHIP_RELS]:
        pl.semaphore_signal(barrier, inc=1, device_id=peer, device_id_type=MESH)
    pl.semaphore_wait(barrier, 1 + len(CHIP_RELS))


def _gather_big(shards, kinds, name):
    n = len(shards)

    def body(*refs):
        srcs, outs = refs[:n], refs[n:2 * n]
        ici_send, ici_recv, d2d_send, d2d_recv, own_send, own_recv = refs[2 * n:]
        x, y, c = _mesh_pos()
        _handshake_chip_peers(x, y, c)
        my = 2 * x + y
        geo = [(_half_rows(s.shape), s.shape[1]) for s in shards]

        def region(a, half, slot):
            return _half_slot(outs[a], kinds[a], geo[a][0], geo[a][1], half, slot)

        def slot_of(a, slot):
            rh, cols = geo[a]
            if kinds[a] == "col":
                return outs[a].at[:, pl.ds(slot * cols, cols)]
            return outs[a].at[pl.ds(slot * 2 * rh, 2 * rh), :]

        sends = []
        for a in range(n):
            cp = pltpu.make_async_remote_copy(
                src_ref=srcs[a], dst_ref=slot_of(a, my), send_sem=own_send.at[a], recv_sem=own_recv.at[a],
                device_id=(x, y, 1 - c), device_id_type=MESH)
            cp.start()
            sends.append(cp)
        for a in range(n):
            rh = geo[a][0]
            for r, (fx, fy) in enumerate(CHIP_RELS):
                cp = pltpu.make_async_remote_copy(
                    src_ref=srcs[a].at[pl.ds(c * rh, rh), :], dst_ref=region(a, c, my),
                    send_sem=ici_send.at[3 * a + r], recv_sem=ici_recv.at[3 * a + r],
                    device_id=(x ^ fx, y ^ fy, c), device_id_type=MESH)
                cp.start()
                sends.append(cp)
        passed = []
        for a in range(n):
            for r, (fx, fy) in enumerate(CHIP_RELS):
                frm = 2 * (x ^ fx) + (y ^ fy)
                landed = region(a, c, frm)
                pltpu.make_async_remote_copy(
                    src_ref=landed, dst_ref=landed, send_sem=ici_send.at[3 * a + r], recv_sem=ici_recv.at[3 * a + r],
                    device_id=(x, y, c), device_id_type=MESH).wait_recv()
                cp = pltpu.make_async_remote_copy(
                    src_ref=landed, dst_ref=landed, send_sem=d2d_send.at[3 * a + r], recv_sem=d2d_recv.at[3 * a + r],
                    device_id=(x, y, 1 - c), device_id_type=MESH)
                cp.start()
                passed.append(cp)
        for a in range(n):
            for r, (fx, fy) in enumerate(CHIP_RELS):
                frm = 2 * (x ^ fx) + (y ^ fy)
                other = region(a, 1 - c, frm)
                pltpu.make_async_remote_copy(
                    src_ref=other, dst_ref=other, send_sem=d2d_send.at[3 * a + r], recv_sem=d2d_recv.at[3 * a + r],
                    device_id=(x, y, c), device_id_type=MESH).wait_recv()
        for a in range(n):
            pltpu.make_async_remote_copy(
                src_ref=srcs[a], dst_ref=slot_of(a, my), send_sem=own_send.at[a], recv_sem=own_recv.at[a],
                device_id=(x, y, c), device_id_type=MESH).wait_recv()
        for cp in sends + passed:
            cp.wait_send()

    return pl.kernel(
        body, name=name,
        out_type=[jax.ShapeDtypeStruct(_full_shape(k, s.shape), s.dtype) for s, k in zip(shards, kinds)],
        mesh=plsc.ScalarSubcoreMesh(axis_name="sequencer", num_cores=1),
        scratch_types=[pltpu.SemaphoreType.DMA((3 * n,))] * 4 + [pltpu.SemaphoreType.DMA((n,))] * 2,
        compiler_params=pltpu.CompilerParams(collective_id=GATHER_COLLECTIVE_ID),
    )(*shards)


def _swap_halves(fulls, kinds, shard_shapes, name):
    n = len(fulls)
    geo = [(_half_rows(s), s[1]) for s in shard_shapes]

    def body(*refs):
        srcs, outs = refs[:n], refs[n:2 * n]
        send_sems, recv_sems = refs[2 * n:]
        x, y, c = _mesh_pos()
        barrier = pltpu.get_barrier_semaphore()
        pl.semaphore_signal(barrier, inc=1, device_id=(x, y, 1 - c), device_id_type=MESH)
        pl.semaphore_wait(barrier, 1)
        cps = []
        for a in range(n):
            for s in range(N_CHIPS):
                cp = pltpu.make_async_remote_copy(
                    src_ref=_half_slot(srcs[a], kinds[a], geo[a][0], geo[a][1], 1 - c, s), dst_ref=outs[a].at[s],
                    send_sem=send_sems.at[N_CHIPS * a + s], recv_sem=recv_sems.at[N_CHIPS * a + s],
                    device_id=(x, y, 1 - c), device_id_type=MESH)
                cp.start()
                cps.append(cp)
        for cp in cps:
            cp.wait()

    return pl.kernel(
        body, name=name,
        out_type=[jax.ShapeDtypeStruct((N_CHIPS,) + g, f.dtype) for g, f in zip(geo, fulls)],
        mesh=plsc.ScalarSubcoreMesh(axis_name="sequencer", num_cores=1),
        scratch_types=[pltpu.SemaphoreType.DMA((N_CHIPS * n,))] * 2,
        compiler_params=pltpu.CompilerParams(collective_id=SWAP_COLLECTIVE_ID),
    )(*fulls)


def _add_own_half(full, recv, kind, shard_shape, cidx, name):
    rh, cols = _half_rows(shard_shape), shard_shape[1]
    tr = _pick(rh, (256, 128, 64, 32, 16))
    nb = rh // tr

    def body(c_ref, f_ref, r_ref, o_ref):
        o_ref[...] = (f_ref[...].astype(F32) + r_ref[...].astype(F32)).astype(o_ref.dtype)

    if kind == "col":
        f_spec = pl.BlockSpec((tr, cols), lambda s, i, c_ref: (c_ref[0] * nb + i, s))
    else:
        f_spec = pl.BlockSpec((tr, cols), lambda s, i, c_ref: ((s * 2 + c_ref[0]) * nb + i, 0))
    blk = pl.BlockSpec((None, tr, cols), lambda s, i, c_ref: (s, i, 0))
    return _pcall(
        body, name=name,
        grid_spec=pltpu.PrefetchScalarGridSpec(num_scalar_prefetch=1, grid=(N_CHIPS, nb), in_specs=[f_spec, blk], out_specs=blk),
        out_shape=jax.ShapeDtypeStruct((N_CHIPS, rh, cols), BF16),
        compiler_params=_cp(("arbitrary", "arbitrary")),
    )(cidx, full, recv)


def _scatter_partials(partials, name):
    n = len(partials)

    def body(*refs):
        srcs, outs = refs[:n], refs[n:2 * n]
        send_sems, recv_sems = refs[2 * n:]
        x, y, c = _mesh_pos()
        barrier = pltpu.get_barrier_semaphore()
        for fx, fy in CHIP_RELS:
            pl.semaphore_signal(barrier, inc=1, device_id=(x ^ fx, y ^ fy, c), device_id_type=MESH)
        pl.semaphore_wait(barrier, len(CHIP_RELS))
        my = 2 * x + y
        cps = []
        for a in range(n):
            for r, (fx, fy) in enumerate(CHIP_RELS):
                to = 2 * (x ^ fx) + (y ^ fy)
                cp = pltpu.make_async_remote_copy(
                    src_ref=srcs[a].at[to], dst_ref=outs[a].at[my],
                    send_sem=send_sems.at[3 * a + r], recv_sem=recv_sems.at[3 * a + r],
                    device_id=(x ^ fx, y ^ fy, c), device_id_type=MESH)
                cp.start()
                cps.append(cp)
        for a in range(n):
            for r, (fx, fy) in enumerate(CHIP_RELS):
                frm = 2 * (x ^ fx) + (y ^ fy)
                pltpu.make_async_remote_copy(
                    src_ref=outs[a].at[frm], dst_ref=outs[a].at[frm],
                    send_sem=send_sems.at[3 * a + r], recv_sem=recv_sems.at[3 * a + r],
                    device_id=(x, y, c), device_id_type=MESH).wait_recv()
        for cp in cps:
            cp.wait_send()

    return pl.kernel(
        body, name=name,
        out_type=[jax.ShapeDtypeStruct(p.shape, p.dtype) for p in partials],
        mesh=plsc.ScalarSubcoreMesh(axis_name="sequencer", num_cores=1),
        scratch_types=[pltpu.SemaphoreType.DMA((3 * n,))] * 2,
        compiler_params=pltpu.CompilerParams(collective_id=SCATTER_COLLECTIVE_ID),
    )(*partials)


def _sum_chips_into(landed, partial, buf, layer, n_layers, pos, name):
    _, rh, cols = landed.shape
    tr = _pick(rh, (256, 128, 64, 32, 16))
    nb = rh // tr

    def body(pos_ref, l_ref, own_ref, *rest):
        o_ref = rest[-1]
        my = pos_ref[1]
        own = own_ref[...].astype(F32)
        acc = jnp.where(my == 0, own, l_ref[0].astype(F32))
        for j in range(1, N_CHIPS):
            acc = acc + jnp.where(my == j, own, l_ref[j].astype(F32))
        o_ref[...] = acc

    in_specs = [pl.BlockSpec((N_CHIPS, tr, cols), lambda i, pos_ref: (0, i, 0)),
                pl.BlockSpec((None, tr, cols), lambda i, pos_ref: (pos_ref[1], i, 0))]
    args = [pos, landed, partial]
    aliases = {}
    if buf is not None:
        in_specs.append(pl.BlockSpec(memory_space=pl.ANY))
        args.append(buf)
        aliases = {3: 0}
    return _pcall(
        body, name=name,
        grid_spec=pltpu.PrefetchScalarGridSpec(
            num_scalar_prefetch=1, grid=(nb,), in_specs=in_specs,
            out_specs=pl.BlockSpec((None, tr, cols), lambda i, pos_ref: (layer, pos_ref[0] * nb + i, 0))),
        out_shape=jax.ShapeDtypeStruct((n_layers, 2 * rh, cols), F32), input_output_aliases=aliases,
        compiler_params=_cp(("arbitrary",)),
    )(*args)


def _exchange_final(bufs):
    n = len(bufs)
    HBM = pl.BlockSpec(memory_space=pl.ANY)
    n_layers = [b.shape[0] for b in bufs]
    base = np.concatenate([[0], np.cumsum(n_layers)])

    def body(*refs):
        outs = refs[n:2 * n]
        send_sems, recv_sems = refs[2 * n:]
        x, y, c = _mesh_pos()
        cps = []
        for w in range(n):
            rh = bufs[w].shape[1] // 2
            for l in range(n_layers[w]):
                k = int(base[w]) + l
                mine = outs[w].at[l, pl.ds(c * rh, rh), :]
                cp = pltpu.make_async_remote_copy(
                    src_ref=mine, dst_ref=mine, send_sem=send_sems.at[k], recv_sem=recv_sems.at[k],
                    device_id=(x, y, 1 - c), device_id_type=MESH)
                cp.start()
                cps.append(cp)
        for w in range(n):
            rh = bufs[w].shape[1] // 2
            for l in range(n_layers[w]):
                k = int(base[w]) + l
                other = outs[w].at[l, pl.ds((1 - c) * rh, rh), :]
                pltpu.make_async_remote_copy(
                    src_ref=other, dst_ref=other, send_sem=send_sems.at[k], recv_sem=recv_sems.at[k],
                    device_id=(x, y, c), device_id_type=MESH).wait_recv()
        for cp in cps:
            cp.wait_send()

    return _pcall(
        body, name="grad_exchange_final", out_shape=[jax.ShapeDtypeStruct(b.shape, b.dtype) for b in bufs],
        in_specs=[HBM] * n, out_specs=[HBM] * n, input_output_aliases={i: i for i in range(n)},
        scratch_shapes=[pltpu.SemaphoreType.DMA((int(base[-1]),))] * 2,
        compiler_params=pltpu.CompilerParams(vmem_limit_bytes=VMEM_LIMIT),
    )(*bufs)


def _swap_halves_last(z):
    h = z.shape[-1] // 2
    return jnp.concatenate([z[..., h:], z[..., :h]], axis=-1)


def _ext_uq(w):
    lead = w.shape[:-1]
    wh = w.reshape(lead + (-1, QK_DIM))
    rope = wh[..., NOPE:]
    return jnp.concatenate([wh, _swap_halves_last(rope)], axis=-1).reshape(lead + (-1,))


def _fold_uq(d):
    lead = d.shape[:-1]
    dh = d.reshape(lead + (-1, HEAD_W))
    rope = dh[..., NOPE:QK_DIM] + _swap_halves_last(dh[..., QK_DIM:])
    return jnp.concatenate([dh[..., :NOPE], rope], axis=-1).reshape(lead + (-1,))


def _ext_win(w):
    base, kr = w[..., :-ROPE], w[..., -ROPE:]
    ks = _swap_halves_last(kr)
    return jnp.concatenate([base, kr, ks, ks, kr], axis=-1)


def _fold_win(d):
    n = d.shape[-1] - 4 * ROPE
    a, b, c2, e = [d[..., n + i * ROPE:n + (i + 1) * ROPE] for i in range(4)]
    return jnp.concatenate([d[..., :n], a + e + _swap_halves_last(b + c2)], axis=-1)


ROW_TILE = 256


def _stage(st, xcur, mix, big, small, tabs):
    D = xcur.shape[1]
    l = st // 2
    mod = [[small["mod"][k][i].reshape(1, 1, D) for i in range(6)] for k in range(l + 1)]
    sh1, sc1, g1, sh2, sc2, _ = mod[l]
    if st % 2 == 0:
        n1 = small["norm1_g"][l].reshape(1, 1, D)
        if l == 0:
            (h,) = _rowop("norm_mod_0", _f_norm_mod, n_x=1, n_nd=0, n_p=3, x_w=[D], nd_w=[], nd_shared=[], p_per_group=[False] * 3,
                          out_w=[D], out_dtypes=[BF16], tile=ROW_TILE)((xcur,), (), (n1, sh1, sc1))
        else:
            op = _rowop(f"resid_norm_mod_a{l}", _f_resid_norm_mod, n_x=2, n_nd=0, n_p=4, x_w=[D, D], nd_w=[], nd_shared=[],
                        p_per_group=[False] * 4, out_w=[D, D], out_dtypes=[F32, BF16], tile=ROW_TILE)
            xcur, h = op((xcur, mix), (), (mod[l - 1][5], n1, sh1, sc1))
        if l % 2 == 0:
            return xcur, _even_mixer(h, big, small, l // 2, ROW_TILE)
        return xcur, _mla_mixer(h, big, small, l // 2, ROW_TILE, *tabs)
    n2 = small["norm2_g"][l].reshape(1, 1, D)
    op = _rowop(f"resid_norm_mod_b{l}", _f_resid_norm_mod, n_x=2, n_nd=0, n_p=4, x_w=[D, D], nd_w=[], nd_shared=[],
                p_per_group=[False] * 4, out_w=[D, D], out_dtypes=[F32, BF16], tile=ROW_TILE)
    xcur, h = op((xcur, mix), (), (g1, n2, sh2, sc2))
    return xcur, _mlp(f"mlp_{l}")(h, big["mlp_w1"], big["mlp_w2"])


def _last_residual(xcur, mix, small):
    D = xcur.shape[1]
    gate = small["mod"][-1][5].reshape(1, 1, D)
    return _rowop("resid_last", _f_resid, n_x=2, n_nd=0, n_p=1, x_w=[D, D], nd_w=[], nd_shared=[], p_per_group=[False],
                  out_w=[D], out_dtypes=[F32], tile=ROW_TILE)((xcur, mix), (), (gate,))[0]


def _even_mixer(h, big, wts, e, tile):
    proj = _linear(f"ab_in_{e}", BF16)(h, big["ab_w_in"])
    da = proj.shape[1] // 4
    u, v, a, g = [proj[:, i * da:(i + 1) * da] for i in range(4)]
    ng = da // GROUP
    sgu = _rowop(f"sgu_{e}", _f_sgu, n_x=2, n_nd=0, n_p=3, x_w=[GROUP, GROUP], nd_w=[], nd_shared=[], p_per_group=[True] * 3,
                 out_w=[GROUP], out_dtypes=[BF16], tile=SGU_CHUNKS_PER_STEP * CHUNK, groups=ng)
    bexp = jnp.broadcast_to(wts["sgu_b"][e][:, :, None], (ng, CHUNK, GROUP))
    (out_a,) = sgu((u, v), (), (wts["sgu_norm_g"][e].reshape(ng, 1, GROUP), wts["sgu_w"][e], bexp))
    (yglu,) = _rowop(f"glu_{e}", _f_glu, n_x=2, n_nd=0, n_p=0, x_w=[da, da], nd_w=[], nd_shared=[], p_per_group=[],
                     out_w=[da], out_dtypes=[F32], tile=tile)((a, g), (), ())
    z = _conv_op(f"conv_{e}")(yglu, wts["conv_w"][e], wts["conv_b"][e].reshape(1, da))
    (out_b,) = _rowop(f"ln_silu_{e}", _f_ln_silu, n_x=1, n_nd=0, n_p=2, x_w=[da], nd_w=[], nd_shared=[], p_per_group=[False] * 2,
                      out_w=[da], out_dtypes=[BF16], tile=tile)(
        (z,), (), (wts["conv_ln_g"][e].reshape(1, 1, da), wts["conv_ln_b"][e].reshape(1, 1, da)))
    return _linear(f"ab_out_{e}", BF16)(jnp.concatenate([out_a, out_b], axis=-1), big["ab_w_out"])


def _mla_mixer(h, big, wts, o, tile, tab_q, tab_ka, tab_kb):
    proj = _linear(f"mla_in_{o}", BF16)(h, big["mla_w_in"])
    rank = (proj.shape[1] - 4 * ROPE) // 2
    c_q, c_kv = proj[:, :rank], proj[:, rank:2 * rank]
    kr_a, kr_b = proj[:, 2 * rank:2 * rank + 2 * ROPE], proj[:, 2 * rank + 2 * ROPE:]

    def rms(name, xx, gg):
        return _rowop(name, _f_rms, n_x=1, n_nd=0, n_p=1, x_w=[rank], nd_w=[], nd_shared=[], p_per_group=[False],
                      out_w=[rank], out_dtypes=[BF16], tile=tile)((xx,), (), (gg.reshape(1, 1, rank),))[0]

    q_raw = _linear(f"mla_uq_{o}", BF16)(rms(f"rms_q_{o}", c_q, wts["mla_q_norm_g"][o]), big["mla_w_uq"])
    kv_raw = _linear(f"mla_ukv_{o}", BF16)(rms(f"rms_kv_{o}", c_kv, wts["mla_kv_norm_g"][o]), big["mla_w_ukv"])
    gq, gk = wts["mla_q_head_g"][o], wts["mla_k_head_g"][o]
    gk_rope = gk[NOPE:]
    gq_ext = jnp.concatenate([gq, _swap_halves_last(gq[NOPE:])]).reshape(1, 1, HEAD_W)
    gk_ext = jnp.concatenate([gk[:NOPE], jnp.ones((HEAD_W - NOPE,), F32)]).reshape(1, 1, HEAD_W)
    gk_a = jnp.concatenate([gk_rope, _swap_halves_last(gk_rope)]).reshape(1, 1, 2 * ROPE)
    gk_b = jnp.concatenate([_swap_halves_last(gk_rope), gk_rope]).reshape(1, 1, 2 * ROPE)
    head_tile = 4 * tile
    (q,) = _rowop(f"q_head_{o}", _f_qhead, n_x=1, n_nd=1, n_p=1, x_w=[HEAD_W], nd_w=[HEAD_W], nd_shared=[True], p_per_group=[False],
                  out_w=[HEAD_W], out_dtypes=[BF16], tile=head_tile, groups=N_HEADS)((q_raw,), (tab_q,), (gq_ext,))
    (kv,) = _rowop(f"kv_head_{o}", _f_kvhead, n_x=1, n_nd=0, n_p=1, x_w=[HEAD_W], nd_w=[], nd_shared=[], p_per_group=[False],
                   out_w=[HEAD_W], out_dtypes=[BF16], tile=head_tile, groups=N_HEADS)((kv_raw,), (), (gk_ext,))
    (kr,) = _rowop(f"k_rope_{o}", _f_krope, n_x=2, n_nd=2, n_p=2, x_w=[2 * ROPE] * 2, nd_w=[2 * ROPE] * 2, nd_shared=[True] * 2,
                   p_per_group=[False] * 2, out_w=[2 * ROPE], out_dtypes=[BF16], tile=tile)((kr_a, kr_b), (tab_ka, tab_kb), (gk_a, gk_b))
    att = _attn_op(f"attn_{o}")(q, kv, kr)
    return _linear(f"mla_out_{o}", BF16)(att, big["mla_w_out"])


def _rope_tabs(S):
    pos = jnp.arange(S, dtype=F32)
    inv = ROPE_THETA ** (-jnp.arange(0, ROPE, 2, dtype=F32) / ROPE)
    ang = pos[:, None] * inv[None, :]
    cos, sin = jnp.cos(ang), jnp.sin(ang)
    cc = jnp.concatenate([cos, cos], axis=-1)
    sg = jnp.concatenate([-sin, sin], axis=-1)
    tab_q = jnp.concatenate([jnp.ones((S, NOPE), F32), cc, sg], axis=-1)
    return tab_q, jnp.concatenate([cc, sg], axis=-1), jnp.concatenate([sg, cc], axis=-1)


def kernel(x, c, norm1_g, norm2_g, ada_w, ada_b, mlp_w1, mlp_w2, ab_w_in, sgu_norm_g, sgu_w, sgu_b, conv_w, conv_b, conv_ln_g, conv_ln_b, ab_w_out, mla_w_in, mla_q_norm_g, mla_kv_norm_g, mla_w_uq, mla_w_ukv, mla_q_head_g, mla_k_head_g, mla_w_out, loss_target, m_norm1_g, m_norm2_g, m_ada_w, m_ada_b, m_mlp_w1, m_mlp_w2, m_ab_w_in, m_sgu_norm_g, m_sgu_w, m_sgu_b, m_conv_w, m_conv_b, m_conv_ln_g, m_conv_ln_b, m_ab_w_out, m_mla_w_in, m_mla_q_norm_g, m_mla_kv_norm_g, m_mla_w_uq, m_mla_w_ukv, m_mla_q_head_g, m_mla_k_head_g, m_mla_w_out, v_norm1_g, v_norm2_g, v_ada_w, v_ada_b, v_mlp_w1, v_mlp_w2, v_ab_w_in, v_sgu_norm_g, v_sgu_w, v_sgu_b, v_conv_w, v_conv_b, v_conv_ln_g, v_conv_ln_b, v_ab_w_out, v_mla_w_in, v_mla_q_norm_g, v_mla_kv_norm_g, v_mla_w_uq, v_mla_w_ukv, v_mla_q_head_g, v_mla_k_head_g, v_mla_w_out):
    names = ["norm1_g", "norm2_g", "ada_w", "ada_b", "mlp_w1", "mlp_w2", "ab_w_in", "sgu_norm_g", "sgu_w", "sgu_b", "conv_w",
             "conv_b", "conv_ln_g", "conv_ln_b", "ab_w_out", "mla_w_in", "mla_q_norm_g", "mla_kv_norm_g", "mla_w_uq", "mla_w_ukv",
             "mla_q_head_g", "mla_k_head_g", "mla_w_out"]
    W = dict(zip(names, [norm1_g, norm2_g, ada_w, ada_b, mlp_w1, mlp_w2, ab_w_in, sgu_norm_g, sgu_w, sgu_b, conv_w, conv_b, conv_ln_g,
                         conv_ln_b, ab_w_out, mla_w_in, mla_q_norm_g, mla_kv_norm_g, mla_w_uq, mla_w_ukv, mla_q_head_g, mla_k_head_g,
                         mla_w_out]))
    M = dict(zip(names, [m_norm1_g, m_norm2_g, m_ada_w, m_ada_b, m_mlp_w1, m_mlp_w2, m_ab_w_in, m_sgu_norm_g, m_sgu_w, m_sgu_b, m_conv_w,
                         m_conv_b, m_conv_ln_g, m_conv_ln_b, m_ab_w_out, m_mla_w_in, m_mla_q_norm_g, m_mla_kv_norm_g, m_mla_w_uq,
                         m_mla_w_ukv, m_mla_q_head_g, m_mla_k_head_g, m_mla_w_out]))
    V = dict(zip(names, [v_norm1_g, v_norm2_g, v_ada_w, v_ada_b, v_mlp_w1, v_mlp_w2, v_ab_w_in, v_sgu_norm_g, v_sgu_w, v_sgu_b, v_conv_w,
                         v_conv_b, v_conv_ln_g, v_conv_ln_b, v_ab_w_out, v_mla_w_in, v_mla_q_norm_g, v_mla_kv_norm_g, v_mla_w_uq,
                         v_mla_w_ukv, v_mla_q_head_g, v_mla_k_head_g, v_mla_w_out]))
    xi, yi, ci = lax.axis_index("x"), lax.axis_index("y"), lax.axis_index("c")
    chip = 2 * xi + yi
    dev = 2 * chip + ci
    pos = jnp.stack([ci, chip]).astype(jnp.int32)
    S, D = x.shape[1], x.shape[2]
    depth = norm1_g.shape[0]

    c_all, conv_w_all, qn_all, kvn_all = _all_gather_small([c, conv_w, mla_q_norm_g, mla_kv_norm_g], "gather_small_inputs")
    c_all = c_all.reshape(N_DEV, D)
    by_chip = lambda a: jnp.concatenate([a[2 * j] for j in range(N_CHIPS)], axis=-1)
    conv_w_full, qn_full, kvn_full = by_chip(conv_w_all), by_chip(qn_all), by_chip(kvn_all)

    (c_act,) = _rowop("silu_c", _f_silu, n_x=1, n_nd=0, n_p=0, x_w=[D], nd_w=[], nd_shared=[], p_per_group=[], out_w=[D],
                      out_dtypes=[F32], tile=N_DEV)((c_all,), (), ())
    c_act_pad = jnp.pad(c_act, ((0, 128 - N_DEV), (0, 0)))
    mod_cols = jnp.stack([_matmul(c_act_pad, ada_w[l], name=f"ada_fwd_{l}")[:N_DEV] for l in range(depth)])
    (mod_all,) = _all_gather_small([mod_cols], "gather_mod")
    mod_mine = jnp.concatenate([lax.dynamic_index_in_dim(mod_all[2 * j], dev, axis=1, keepdims=False) for j in range(N_CHIPS)], axis=-1)
    mod_mine = (mod_mine + ada_b).reshape(depth, 6, D)

    big = {"mlp_w1": "col", "mlp_w2": "row", "ab_w_in": "col", "ab_w_out": "row", "mla_w_in": "row", "mla_w_uq": "col",
           "mla_w_ukv": "col", "mla_w_out": "row"}
    src = dict(W)
    src["mla_w_in"] = _ext_win(mla_w_in)
    src["mla_w_uq"] = _ext_uq(mla_w_uq)
    shards, kinds, owner = [], [], []
    for nme, kind in big.items():
        for l in range(src[nme].shape[0]):
            shards.append(src[nme][l].astype(BF16))
            kinds.append(kind)
            owner.append((nme, l))
    def stage_of(nme, l):
        return 2 * l + 1 if nme.startswith("mlp") else (4 * l if nme.startswith("ab") else 4 * l + 2)

    fwd_stages = [[a for a, (nme, l) in enumerate(owner) if stage_of(nme, l) == st] for st in range(2 * depth)]
    fulls = [None] * len(shards)
    for st, idx in enumerate(fwd_stages):
        got = _gather_big([shards[a] for a in idx], [kinds[a] for a in idx], f"gather_weights_{st}")
        for a, f in zip(idx, got):
            fulls[a] = f
    small = dict(norm1_g=norm1_g, norm2_g=norm2_g, mod=mod_mine, sgu_norm_g=sgu_norm_g, sgu_w=sgu_w, sgu_b=sgu_b, conv_w=conv_w_full,
                 conv_b=conv_b, conv_ln_g=conv_ln_g, conv_ln_b=conv_ln_b, mla_q_norm_g=qn_full, mla_kv_norm_g=kvn_full,
                 mla_q_head_g=mla_q_head_g, mla_k_head_g=mla_k_head_g)
    groups, a0 = [], 0
    for nme in big:
        groups.append(list(range(a0, a0 + src[nme].shape[0])))
        a0 += src[nme].shape[0]
    shard_shapes = [s.shape for s in shards]
    where = {a: (wi, l) for wi, idxs in enumerate(groups) for l, a in enumerate(idxs)}

    tabs = _rope_tabs(S)
    n_stages = 2 * depth
    xcur, mix, vjps = x[0], None, []
    for st, idx in enumerate(fwd_stages):
        stage_big = {owner[a][0]: fulls[a] for a in idx}
        if st == 0:
            (xcur, mix), vjp = jax.vjp(lambda xx, bb, ss: _stage(0, xx, None, bb, ss, tabs), xcur, stage_big, small)
        else:
            (xcur, mix), vjp = jax.vjp(lambda xx, mm, bb, ss, st=st: _stage(st, xx, mm, bb, ss, tabs), xcur, mix, stage_big, small)
        vjps.append(vjp)
    y, vjp_last = jax.vjp(_last_residual, xcur, mix, small)
    dy, loss_mine = _loss_call(y, loss_target[0])
    loss = lax.psum(loss_mine[0, 0], ("x", "y", "c"))

    dxc, dmix, dsmall = vjp_last(dy)
    bufs = [None] * len(groups)
    swapped, scattered = {}, {}

    def add_and_scatter(st):
        idx, gr, recv = swapped.pop(st)
        partials = [_add_own_half(gr[a], r, kinds[a], shard_shapes[a], pos, f"grad_add_sibling_{a}") for a, r in zip(idx, recv)]
        scattered[st] = (idx, partials, _scatter_partials(partials, f"grad_scatter_partials_{st}"))

    def sum_chips(st):
        idx, partials, landed = scattered.pop(st)
        for a, part, land in zip(idx, partials, landed):
            wi, l = where[a]
            bufs[wi] = _sum_chips_into(land, part, bufs[wi], l, len(groups[wi]), pos, f"grad_sum_chips_{a}")

    for st in reversed(range(n_stages)):
        idx = fwd_stages[st]
        if st == 0:
            dxc, dbig, ds = vjps[st]((dxc, dmix))
        else:
            dxc, dmix, dbig, ds = vjps[st]((dxc, dmix))
        dsmall = jax.tree.map(jnp.add, dsmall, ds)
        gr = {a: dbig[owner[a][0]] for a in idx}
        swapped[st] = (idx, gr, _swap_halves([gr[a] for a in idx], [kinds[a] for a in idx], [shard_shapes[a] for a in idx],
                                             f"grad_swap_halves_{st}"))
        started = [s for s in ((st + 1, st) if st == 1 else (st + 1,)) if s in swapped]
        for s in started:
            add_and_scatter(s)
        if st + 2 in scattered:
            sum_chips(st + 2)
        if st > 0:
            pinned = {s: scattered[s][1] for s in started}
            (dxc, dmix), pinned, bufs = lax.optimization_barrier(((dxc, dmix), pinned, bufs))
            for s in started:
                scattered[s] = (scattered[s][0], pinned[s], scattered[s][2])
    dx = dxc
    for st in sorted(swapped, reverse=True):
        add_and_scatter(st)
    for st in sorted(scattered, reverse=True):
        sum_chips(st)
    gsh = _exchange_final(bufs)

    dw = dsmall
    dmod = dw["mod"].reshape(depth, 6 * D)
    small_names = ["norm1_g", "norm2_g", "sgu_norm_g", "sgu_w", "sgu_b", "conv_w", "conv_b", "conv_ln_g", "conv_ln_b", "mla_q_norm_g",
                   "mla_kv_norm_g", "mla_q_head_g", "mla_k_head_g"]
    red = _all_reduce_small([dmod] + [dw[nme] for nme in small_names], "reduce_small_grads")
    G = dict(zip(["ada_b"] + small_names, red))
    own_cols = lambda a: lax.dynamic_slice_in_dim(a, chip * (a.shape[-1] // N_CHIPS), a.shape[-1] // N_CHIPS, axis=-1)
    for nme in ("conv_w", "mla_q_norm_g", "mla_kv_norm_g"):
        G[nme] = own_cols(G[nme])

    (dmod_all,) = _all_gather_small([dmod], "gather_dmod")
    dmod_cols = own_cols(dmod_all)
    dmod_pad = jnp.pad(dmod_cols, ((0, 128 - N_DEV), (0, 0), (0, 0)))
    G["ada_w"] = _ada_wgrad_call(c_act_pad, dmod_pad)

    for nme, g in zip(big, gsh):
        G[nme] = g
    G["mla_w_in"] = _fold_win(G["mla_w_in"])
    G["mla_w_uq"] = _fold_uq(G["mla_w_uq"])

    deltas, new_m, new_v = [], [], []
    for nme in names:
        d, mn, vn = _adamw_call(W[nme], G[nme], M[nme], V[nme], f"adamw_{nme}")
        deltas.append(d)
        new_m.append(mn)
        new_v.append(vn)
    return (loss, dx[None], *[G[nme] for nme in names], *deltas, *new_m, *new_v)
```
